```python
import jax, jax.numpy as jnp
from jax import lax
import numpy as np

D_MODEL = 1024
BATCH = 8
SEQ = 2048
DEPTH = 1
DEC_BATCH = 128
DEC_SEQ = 4
PAST_LEN = 16384
PAGE_SIZE = 128

D_MIX = D_MODEL
D_POOL = D_MIX // 2
D_CONV = D_MIX - D_POOL
POOL_WINDOWS = (2, 4, 8, 16)
N_POOL_GROUPS = len(POOL_WINDOWS)
POOL_GC = D_POOL // N_POOL_GROUPS
POOL_BUF = max(POOL_WINDOWS) - 1
N_CONV_HEADS = 8
CONV_HEAD_DIM = D_CONV // N_CONV_HEADS
CONV_W = 3
D_FF = 2816
D_IN_PROJ = D_POOL + 3 * D_CONV
RMS_EPS = 1e-6

kernel_name = 'hybrid_pool_shortconv_convffn_step'


def rmsnorm(x, g):
    xf = x.astype(jnp.float32)
    r = xf * lax.rsqrt(jnp.mean(xf * xf, axis=-1, keepdims=True) + RMS_EPS)
    return (r * g.astype(jnp.float32)).astype(x.dtype)


def causal_dwconv(p, w):
    T = p.shape[1] - (CONV_W - 1)
    out = p[:, 0:T] * w[0]
    for k in range(1, CONV_W):
        out = out + p[:, k:k + T] * w[k]
    return out


def pool_mixer(p, pos0, pool_w, pool_scale):
    T = p.shape[1] - POOL_BUF
    pf = p.astype(jnp.float32)
    cs = jnp.concatenate([jnp.zeros_like(pf[:, :1]), jnp.cumsum(pf, axis=1)], axis=1)
    cur = pf[:, POOL_BUF:]
    t = jnp.arange(T, dtype=jnp.int32)
    outs = []
    for g, w in enumerate(POOL_WINDOWS):
        sl = slice(g * POOL_GC, (g + 1) * POOL_GC)
        s = cs[:, POOL_BUF + 1:POOL_BUF + 1 + T, sl] - cs[:, POOL_BUF + 1 - w:POOL_BUF + 1 - w + T, sl]
        cnt = jnp.minimum(w, pos0 + t + 1).astype(jnp.float32)[None, :, None]
        outs.append(s / cnt - cur[..., sl])
    d = jnp.stack(outs, axis=2).astype(p.dtype)
    y = jnp.einsum('btgc,gcd->btgd', d, pool_w)
    return y.reshape(y.shape[0], T, D_POOL) * pool_scale


def trunk_layer(x, c, pool_buf, conv_buf, ffn_buf, pos0, w_ada, b_ada, g_pre_mix, g_post_mix,
                g_pre_ffn, g_post_ffn, w_in, pool_w, pool_scale, conv_w, w_out,
                ffn_w_up, ffn_conv_w, ffn_w_down):
    mod = (jax.nn.silu(c) @ w_ada + b_ada)[:, None, :]
    sh1, sc1, gt1, sh2, sc2, gt2 = jnp.split(mod, 6, axis=-1)
    h = rmsnorm(x, g_pre_mix) * (1 + sc1) + sh1
    proj = h @ w_in
    v_pool, x_conv, gate_b, gate_c = jnp.split(
        proj, [D_POOL, D_POOL + D_CONV, D_POOL + 2 * D_CONV], axis=-1)
    pool_in = jnp.concatenate([pool_buf, v_pool], axis=1)
    y_pool = pool_mixer(pool_in, pos0, pool_w, pool_scale)
    conv_in = jnp.concatenate([conv_buf, gate_c * x_conv], axis=1)
    y_conv = gate_b * causal_dwconv(conv_in, conv_w)
    mix = jnp.concatenate([y_pool, y_conv], axis=-1) @ w_out
    x = x + gt1 * rmsnorm(mix, g_post_mix)
    h2 = rmsnorm(x, g_pre_ffn) * (1 + sc2) + sh2
    up = h2 @ ffn_w_up
    ffn_in = jnp.concatenate([ffn_buf, up], axis=1)
    a, b = jnp.split(causal_dwconv(ffn_in, ffn_conv_w), 2, axis=-1)
    f = (jax.nn.silu(a) * b) @ ffn_w_down
    x = x + gt2 * rmsnorm(f, g_post_ffn)
    return x, pool_in[:, -POOL_BUF:], conv_in[:, -(CONV_W - 1):], ffn_in[:, -(CONV_W - 1):]


def setup_inputs(seed: int = 0) -> dict:
    key = jax.random.key(seed)
    ks = jax.random.split(key, 24)
    f32 = jnp.float32
    nrm = lambda k, s, sc: jax.random.normal(k, s, f32) * sc
    return {
        'x_prompt': nrm(ks[0], (BATCH, SEQ, D_MODEL), 1.0),
        'x_sample': nrm(ks[1], (DEC_BATCH, DEC_SEQ, D_MODEL), 1.0),
        'state_pool': nrm(ks[2], (DEPTH, DEC_BATCH, POOL_BUF, D_POOL), 1.0),
        'state_conv': nrm(ks[3], (DEPTH, DEC_BATCH, CONV_W - 1, D_CONV), 1.0),
        'state_ffn': nrm(ks[4], (DEPTH, DEC_BATCH, CONV_W - 1, 2 * D_FF), 1.0),
        'c_prompt': nrm(ks[5], (BATCH, D_MODEL), 1.0),
        'c_sample': nrm(ks[6], (DEC_BATCH, D_MODEL), 1.0),
        'w_ada': nrm(ks[7], (DEPTH, D_MODEL, 6 * D_MODEL), 0.02),
        'b_ada': nrm(ks[8], (DEPTH, 6 * D_MODEL), 0.02),
        'g_pre_mix': 1.0 + nrm(ks[9], (DEPTH, D_MODEL), 0.05),
        'g_post_mix': 1.0 + nrm(ks[10], (DEPTH, D_MODEL), 0.05),
        'g_pre_ffn': 1.0 + nrm(ks[11], (DEPTH, D_MODEL), 0.05),
        'g_post_ffn': 1.0 + nrm(ks[12], (DEPTH, D_MODEL), 0.05),
        'w_in': nrm(ks[13], (DEPTH, D_MODEL, D_IN_PROJ), D_MODEL ** -0.5),
        'pool_w': nrm(ks[14], (DEPTH, N_POOL_GROUPS, POOL_GC, POOL_GC), POOL_GC ** -0.5),
        'pool_scale': 1.0 + nrm(ks[15], (DEPTH, D_POOL), 0.1),
        'conv_w': nrm(ks[16], (DEPTH, CONV_W, D_CONV), CONV_W ** -0.5),
        'w_out': nrm(ks[17], (DEPTH, D_MIX, D_MODEL), D_MIX ** -0.5),
        'ffn_w_up': nrm(ks[18], (DEPTH, D_MODEL, 2 * D_FF), D_MODEL ** -0.5),
        'ffn_conv_w': nrm(ks[19], (DEPTH, CONV_W, 2 * D_FF), CONV_W ** -0.5),
        'ffn_w_down': nrm(ks[20], (DEPTH, D_FF, D_MODEL), D_FF ** -0.5),
    }


def reference(x_prompt, x_sample, state_pool, state_conv, state_ffn, c_prompt, c_sample,
              w_ada, b_ada, g_pre_mix, g_post_mix, g_pre_ffn, g_post_ffn, w_in, pool_w,
              pool_scale, conv_w, w_out, ffn_w_up, ffn_conv_w, ffn_w_down):
    xp, xs = x_prompt, x_sample
    npp, ncp, nfp, nps, ncs, nfs = [], [], [], [], [], []
    for l in range(DEPTH):
        wl = (w_ada[l], b_ada[l], g_pre_mix[l], g_post_mix[l], g_pre_ffn[l], g_post_ffn[l],
              w_in[l], pool_w[l], pool_scale[l], conv_w[l], w_out[l],
              ffn_w_up[l], ffn_conv_w[l], ffn_w_down[l])
        zp = jnp.zeros((xp.shape[0], POOL_BUF, D_POOL), xp.dtype)
        zc = jnp.zeros((xp.shape[0], CONV_W - 1, D_CONV), xp.dtype)
        zf = jnp.zeros((xp.shape[0], CONV_W - 1, 2 * D_FF), xp.dtype)
        xp, sp, sc, sf = trunk_layer(xp, c_prompt, zp, zc, zf, 0, *wl)
        npp.append(sp); ncp.append(sc); nfp.append(sf)
        xs, sp2, sc2, sf2 = trunk_layer(xs, c_sample, state_pool[l], state_conv[l], state_ffn[l],
                                        PAST_LEN, *wl)
        nps.append(sp2); ncs.append(sc2); nfs.append(sf2)
    return (xp, xs, jnp.stack(npp), jnp.stack(ncp), jnp.stack(nfp),
            jnp.stack(nps), jnp.stack(ncs), jnp.stack(nfs))
```

```python
import functools

import jax
import jax.numpy as jnp
from jax import lax
from jax.experimental import pallas as pl
from jax.experimental.pallas import tpu as pltpu

D_MODEL = 1024
D_POOL = 512
D_CONV = 512
POOL_WINDOWS = (2, 4, 8, 16)
POOL_GC = 128
POOL_BUF = 15
CONV_W = 3
D_FF = 2816
D_IN_PROJ = D_POOL + 3 * D_CONV
RMS_EPS = 1e-6
PAST_LEN = 16384

V7X_VMEM_LIMIT_BYTES = 56 * 1024 * 1024
SUBLANES = 8
FF_CHUNK = 256
N_FF_CHUNKS = D_FF // FF_CHUNK
PROMPT_TM = 512
POOL_HALO = 16
CONV_HALO = SUBLANES

_bf16 = jnp.bfloat16
_f32 = jnp.float32


def _dot(a, b):
    return jnp.dot(a, b, preferred_element_type=_f32)


def _rms(x, g):
    ms = jnp.mean(x * x, axis=-1, keepdims=True)
    return x * lax.rsqrt(ms + RMS_EPS) * g


def _silu(a):
    return a * jax.nn.sigmoid(a)


def _ada_kernel(cp_ref, cs_ref, w_ref, b_ref, mp_ref, ms_ref):
    w = w_ref[...].astype(_bf16)
    b = b_ref[...]
    mp_ref[...] = _dot(_silu(cp_ref[...]).astype(_bf16), w) + b
    ms_ref[...] = _dot(_silu(cs_ref[...]).astype(_bf16), w) + b


def _ada(c_prompt, c_sample, w_ada, b_ada):
    n = w_ada.shape[1]
    tn = 1024
    bp, bs = c_prompt.shape[0], c_sample.shape[0]
    return pl.pallas_call(
        _ada_kernel,
        grid=(n // tn,),
        in_specs=[
            pl.BlockSpec((bp, D_MODEL), lambda i: (0, 0)),
            pl.BlockSpec((bs, D_MODEL), lambda i: (0, 0)),
            pl.BlockSpec((D_MODEL, tn), lambda i: (0, i)),
            pl.BlockSpec((1, tn), lambda i: (0, i)),
        ],
        out_specs=[
            pl.BlockSpec((bp, tn), lambda i: (0, i)),
            pl.BlockSpec((bs, tn), lambda i: (0, i)),
        ],
        out_shape=[
            jax.ShapeDtypeStruct((bp, n), _f32),
            jax.ShapeDtypeStruct((bs, n), _f32),
        ],
        compiler_params=pltpu.CompilerParams(
            dimension_semantics=("arbitrary",),
            vmem_limit_bytes=V7X_VMEM_LIMIT_BYTES),
        name="ada_mod",
    )(c_prompt, c_sample, w_ada, b_ada)


def _shift_rows(ext, k, halo):
    return pltpu.roll(ext, k, axis=0)[halo:]


def _prompt_kernel(x_ref, mod_ref, g_ref, w_in_ref, pool_w_ref, pool_scale_ref, conv_w_ref,
                   w_out_ref, w_up_ref, ffn_conv_w_ref, w_down_ref,
                   y_ref, pool_out_ref, conv_out_ref, ffn_out_ref,
                   pool_carry, conv_carry, ffn_carry, h2_ref, f_ref):
    b = pl.program_id(0)
    j = pl.program_id(1)
    tm = x_ref.shape[1]

    @pl.when(j == 0)
    def _():
        pool_carry[...] = jnp.zeros_like(pool_carry)
        conv_carry[...] = jnp.zeros_like(conv_carry)
        ffn_carry[...] = jnp.zeros_like(ffn_carry)

    x = x_ref[0]
    mod = mod_ref[pl.ds(b, 1), :]
    sh1, sc1, gt1, sh2, sc2, gt2 = [mod[:, i * D_MODEL:(i + 1) * D_MODEL] for i in range(6)]
    g = g_ref[...]
    g_pre_mix, g_post_mix, g_pre_ffn, g_post_ffn = [g[i:i + 1] for i in range(4)]

    h = _rms(x, g_pre_mix) * (1.0 + sc1) + sh1
    proj = _dot(h.astype(_bf16), w_in_ref[...])
    v_pool = proj[:, :D_POOL]
    x_conv = proj[:, D_POOL:D_POOL + D_CONV]
    gate_b = proj[:, D_POOL + D_CONV:D_POOL + 2 * D_CONV]
    gate_c = proj[:, D_POOL + 2 * D_CONV:]

    pool_ext = jnp.concatenate([pool_carry[...], v_pool], axis=0)
    pos = j * tm + lax.broadcasted_iota(jnp.int32, (tm, POOL_GC), 0)
    y_pool = []
    for gi, w in enumerate(POOL_WINDOWS):
        sl = slice(gi * POOL_GC, (gi + 1) * POOL_GC)
        s = pool_ext[:, sl]
        step = 1
        while step < w:
            s = s + pltpu.roll(s, step, axis=0)
            step *= 2
        cnt = jnp.minimum(w, pos + 1).astype(_f32)
        d = s[POOL_HALO:] / cnt - v_pool[:, sl]
        y_pool.append(_dot(d.astype(_bf16), pool_w_ref[gi]))
    y_pool = jnp.concatenate(y_pool, axis=-1) * pool_scale_ref[...]
    pool_carry[...] = v_pool[tm - POOL_HALO:]
    pool_out_ref[0, 0] = v_pool[tm - POOL_BUF:]

    cx = gate_c * x_conv
    conv_ext = jnp.concatenate([conv_carry[...], cx], axis=0)
    cw = conv_w_ref[...]
    conv = (_shift_rows(conv_ext, 2, CONV_HALO) * cw[0:1]
            + _shift_rows(conv_ext, 1, CONV_HALO) * cw[1:2]
            + cx * cw[2:3])
    y_conv = gate_b * conv
    conv_carry[...] = cx[tm - CONV_HALO:]
    conv_out_ref[0, 0] = cx[tm - (CONV_W - 1):]

    mix = _dot(jnp.concatenate([y_pool, y_conv], axis=-1).astype(_bf16), w_out_ref[...])
    x1 = x + gt1 * _rms(mix, g_post_mix)

    h2_ref[...] = (_rms(x1, g_pre_ffn) * (1.0 + sc2) + sh2).astype(_bf16)
    for c in range(N_FF_CHUNKS):
        halves = []
        for base in (0, D_FF):
            cols = slice(base + c * FF_CHUNK, base + (c + 1) * FF_CHUNK)
            up = _dot(h2_ref[...], w_up_ref[:, cols])
            up_ext = jnp.concatenate([ffn_carry[:, cols], up], axis=0)
            fw = ffn_conv_w_ref[:, cols]
            halves.append(_shift_rows(up_ext, 2, CONV_HALO) * fw[0:1]
                          + _shift_rows(up_ext, 1, CONV_HALO) * fw[1:2]
                          + up * fw[2:3])
            ffn_carry[:, cols] = up[tm - CONV_HALO:]
            ffn_out_ref[0, 0, :, cols] = up[tm - (CONV_W - 1):]
        act = (_silu(halves[0]) * halves[1]).astype(_bf16)
        contrib = _dot(act, w_down_ref[c * FF_CHUNK:(c + 1) * FF_CHUNK, :])
        if c == 0:
            f_ref[...] = contrib
        else:
            f_ref[...] += contrib

    y_ref[0] = x1 + gt2 * _rms(f_ref[...], g_post_ffn)


def _const_spec(shape):
    nd = len(shape)
    return pl.BlockSpec(shape, lambda b, j: (0,) * nd, pipeline_mode=pl.Buffered(1))


def _prompt(x, mod, g, w_in, pool_w, pool_scale, conv_w, w_out, w_up, ffn_conv_w, w_down):
    nb, seq, _ = x.shape
    tm = PROMPT_TM
    nt = seq // tm
    consts = (mod, g, w_in, pool_w, pool_scale, conv_w, w_out, w_up, ffn_conv_w, w_down)
    return pl.pallas_call(
        _prompt_kernel,
        grid=(nb, nt),
        in_specs=[pl.BlockSpec((1, tm, D_MODEL), lambda b, j: (b, j, 0))]
        + [_const_spec(a.shape) for a in consts],
        out_specs=[
            pl.BlockSpec((1, tm, D_MODEL), lambda b, j: (b, j, 0)),
            pl.BlockSpec((1, 1, POOL_BUF, D_POOL), lambda b, j: (0, b, 0, 0)),
            pl.BlockSpec((1, 1, CONV_W - 1, D_CONV), lambda b, j: (0, b, 0, 0)),
            pl.BlockSpec((1, 1, CONV_W - 1, 2 * D_FF), lambda b, j: (0, b, 0, 0)),
        ],
        out_shape=[
            jax.ShapeDtypeStruct((nb, seq, D_MODEL), _f32),
            jax.ShapeDtypeStruct((1, nb, POOL_BUF, D_POOL), _f32),
            jax.ShapeDtypeStruct((1, nb, CONV_W - 1, D_CONV), _f32),
            jax.ShapeDtypeStruct((1, nb, CONV_W - 1, 2 * D_FF), _f32),
        ],
        scratch_shapes=[
            pltpu.VMEM((POOL_HALO, D_POOL), _f32),
            pltpu.VMEM((CONV_HALO, D_CONV), _f32),
            pltpu.VMEM((CONV_HALO, 2 * D_FF), _f32),
            pltpu.VMEM((tm, D_MODEL), _bf16),
            pltpu.VMEM((tm, D_MODEL), _f32),
        ],
        compiler_params=pltpu.CompilerParams(
            dimension_semantics=("arbitrary", "arbitrary"),
            vmem_limit_bytes=V7X_VMEM_LIMIT_BYTES),
        name="prompt_trunk",
    )(x, *consts)


def _sample_kernel(x_ref, sp_ref, sc_ref, sf_ref, mod_ref, g_ref, w_in_ref, pool_w_ref,
                   pool_scale_ref, conv_w_ref, w_out_ref, w_up_ref, ffn_conv_w_ref, w_down_ref,
                   y_ref, pool_out_ref, conv_out_ref, ffn_out_ref, *, nb, nt):
    def tile_t(a):
        return jnp.concatenate([a] * nt, axis=0)

    def slabs(ref, n, width):
        return [ref[:, k * width:(k + 1) * width] for k in range(n)]

    x = jnp.concatenate(slabs(x_ref, nt, D_MODEL), axis=0)
    mod = mod_ref[...]
    sh1, sc1, gt1, sh2, sc2, gt2 = [tile_t(mod[:, i * D_MODEL:(i + 1) * D_MODEL])
                                    for i in range(6)]
    g = g_ref[...]
    g_pre_mix, g_post_mix, g_pre_ffn, g_post_ffn = [g[i:i + 1] for i in range(4)]
    m = nt * nb

    h = _rms(x, g_pre_mix) * (1.0 + sc1) + sh1
    proj = _dot(h.astype(_bf16), w_in_ref[...])
    v_pool = proj[:, :D_POOL]
    x_conv = proj[:, D_POOL:D_POOL + D_CONV]
    gate_b = proj[:, D_POOL + D_CONV:D_POOL + 2 * D_CONV]
    gate_c = proj[:, D_POOL + 2 * D_CONV:]

    pool_rows = slabs(sp_ref, POOL_BUF, D_POOL) + [v_pool[t * nb:(t + 1) * nb] for t in range(nt)]
    pool_ext = jnp.concatenate(pool_rows, axis=0)
    y_pool = []
    for gi, w in enumerate(POOL_WINDOWS):
        sl = slice(gi * POOL_GC, (gi + 1) * POOL_GC)
        s = v_pool[:, sl]
        for k in range(1, w):
            s = s + pool_ext[(POOL_BUF - k) * nb:(POOL_BUF - k) * nb + m, sl]
        cnt = float(min(w, PAST_LEN + 1))
        d = s / cnt - v_pool[:, sl]
        y_pool.append(_dot(d.astype(_bf16), pool_w_ref[gi]))
    y_pool = jnp.concatenate(y_pool, axis=-1) * pool_scale_ref[...]
    for k in range(POOL_BUF):
        pool_out_ref[:, k * D_POOL:(k + 1) * D_POOL] = pool_rows[k + nt]

    cx = gate_c * x_conv
    hist = CONV_W - 1
    conv_rows = slabs(sc_ref, hist, D_CONV) + [cx[t * nb:(t + 1) * nb] for t in range(nt)]
    conv_ext = jnp.concatenate(conv_rows, axis=0)
    cw = conv_w_ref[...]
    conv = conv_ext[0:m] * cw[0:1] + conv_ext[nb:nb + m] * cw[1:2] + cx * cw[2:3]
    y_conv = gate_b * conv
    for k in range(hist):
        conv_out_ref[:, k * D_CONV:(k + 1) * D_CONV] = conv_rows[k + nt]

    mix = _dot(jnp.concatenate([y_pool, y_conv], axis=-1).astype(_bf16), w_out_ref[...])
    x1 = x + gt1 * _rms(mix, g_post_mix)

    h2 = (_rms(x1, g_pre_ffn) * (1.0 + sc2) + sh2).astype(_bf16)
    f = None
    for c in range(N_FF_CHUNKS):
        halves = []
        for base in (0, D_FF):
            cols = slice(base + c * FF_CHUNK, base + (c + 1) * FF_CHUNK)
            up = _dot(h2, w_up_ref[:, cols])
            up_rows = ([sf_ref[:, k * 2 * D_FF + cols.start:k * 2 * D_FF + cols.stop]
                        for k in range(hist)]
                       + [up[t * nb:(t + 1) * nb] for t in range(nt)])
            up_ext = jnp.concatenate(up_rows, axis=0)
            fw = ffn_conv_w_ref[:, cols]
            halves.append(up_ext[0:m] * fw[0:1] + up_ext[nb:nb + m] * fw[1:2] + up * fw[2:3])
            for k in range(hist):
                ffn_out_ref[:, k * 2 * D_FF + cols.start:k * 2 * D_FF + cols.stop] = up_rows[k + nt]
        act = (_silu(halves[0]) * halves[1]).astype(_bf16)
        contrib = _dot(act, w_down_ref[c * FF_CHUNK:(c + 1) * FF_CHUNK, :])
        f = contrib if f is None else f + contrib

    y = x1 + gt2 * _rms(f, g_post_ffn)
    for t in range(nt):
        y_ref[:, t * D_MODEL:(t + 1) * D_MODEL] = y[t * nb:(t + 1) * nb]


def _sample(x, sp, sc, sf, mod, g, w_in, pool_w, pool_scale, conv_w, w_out, w_up, ffn_conv_w,
            w_down, *, nt):
    nb = x.shape[0]
    args = (x, sp, sc, sf, mod, g, w_in, pool_w, pool_scale, conv_w, w_out, w_up, ffn_conv_w, w_down)
    return pl.pallas_call(
        functools.partial(_sample_kernel, nb=nb, nt=nt),
        out_shape=[
            jax.ShapeDtypeStruct(x.shape, _f32),
            jax.ShapeDtypeStruct(sp.shape, _f32),
            jax.ShapeDtypeStruct(sc.shape, _f32),
            jax.ShapeDtypeStruct(sf.shape, _f32),
        ],
        compiler_params=pltpu.CompilerParams(vmem_limit_bytes=V7X_VMEM_LIMIT_BYTES),
        name="sample_trunk",
    )(*args)


def kernel(x_prompt, x_sample, state_pool, state_conv, state_ffn, c_prompt, c_sample, w_ada, b_ada,
           g_pre_mix, g_post_mix, g_pre_ffn, g_post_ffn, w_in, pool_w, pool_scale, conv_w, w_out,
           ffn_w_up, ffn_conv_w, ffn_w_down):
    assert w_ada.shape[0] == 1, "single trunk layer"
    nbs, nts, _ = x_sample.shape
    mod_p, mod_s = _ada(c_prompt, c_sample, w_ada[0], b_ada)
    g = jnp.concatenate([g_pre_mix, g_post_mix, g_pre_ffn, g_post_ffn], axis=0)
    weights = (g, w_in[0].astype(_bf16), pool_w[0].astype(_bf16), pool_scale, conv_w[0],
               w_out[0].astype(_bf16), ffn_w_up[0].astype(_bf16), ffn_conv_w[0],
               ffn_w_down[0].astype(_bf16))

    y_p, pool_p, conv_p, ffn_p = _prompt(x_prompt, mod_p, *weights)

    y_s, pool_s, conv_s, ffn_s = _sample(
        x_sample.reshape(nbs, nts * D_MODEL),
        state_pool[0].reshape(nbs, POOL_BUF * D_POOL),
        state_conv[0].reshape(nbs, (CONV_W - 1) * D_CONV),
        state_ffn[0].reshape(nbs, (CONV_W - 1) * 2 * D_FF),
        mod_s, *weights, nt=nts)

    return (y_p, y_s.reshape(nbs, nts, D_MODEL), pool_p, conv_p, ffn_p,
            pool_s.reshape(1, nbs, POOL_BUF, D_POOL),
            conv_s.reshape(1, nbs, CONV_W - 1, D_CONV),
            ffn_s.reshape(1, nbs, CONV_W - 1, 2 * D_FF))
```

```python
import functools

import jax
import jax.numpy as jnp
from jax import lax
from jax.experimental import pallas as pl
from jax.experimental.pallas import tpu as pltpu

D_MODEL = 1024
D_POOL = 512
D_CONV = 512
POOL_WINDOWS = (2, 4, 8, 16)
POOL_GC = 128
POOL_BUF = 15
CONV_W = 3
D_FF = 2816
D_IN_PROJ = D_POOL + 3 * D_CONV
RMS_EPS = 1e-6
PAST_LEN = 16384

V7X_VMEM_LIMIT_BYTES = 56 * 1024 * 1024
SUBLANES = 8
FF_CHUNK = 256
N_FF_CHUNKS = D_FF // FF_CHUNK
PROMPT_TM = 512
POOL_HALO = 16
CONV_HALO = SUBLANES

_bf16 = jnp.bfloat16
_f32 = jnp.float32


def _dot(a, b):
    return jnp.dot(a, b, preferred_element_type=_f32)


def _rms(x, g):
    ms = jnp.mean(x * x, axis=-1, keepdims=True)
    return x * lax.rsqrt(ms + RMS_EPS) * g


def _silu(a):
    return a * jax.nn.sigmoid(a)


def _ada_kernel(cp_ref, cs_ref, w_ref, b_ref, mp_ref, ms_ref):
    w = w_ref[...].astype(_bf16)
    b = b_ref[...]
    mp_ref[...] = _dot(_silu(cp_ref[...]).astype(_bf16), w) + b
    ms_ref[...] = _dot(_silu(cs_ref[...]).astype(_bf16), w) + b


def _ada(c_prompt, c_sample, w_ada, b_ada):
    n = w_ada.shape[1]
    tn = 1024
    bp, bs = c_prompt.shape[0], c_sample.shape[0]
    return pl.pallas_call(
        _ada_kernel,
        grid=(n // tn,),
        in_specs=[
            pl.BlockSpec((bp, D_MODEL), lambda i: (0, 0)),
            pl.BlockSpec((bs, D_MODEL), lambda i: (0, 0)),
            pl.BlockSpec((D_MODEL, tn), lambda i: (0, i)),
            pl.BlockSpec((1, tn), lambda i: (0, i)),
        ],
        out_specs=[
            pl.BlockSpec((bp, tn), lambda i: (0, i)),
            pl.BlockSpec((bs, tn), lambda i: (0, i)),
        ],
        out_shape=[
            jax.ShapeDtypeStruct((bp, n), _f32),
            jax.ShapeDtypeStruct((bs, n), _f32),
        ],
        compiler_params=pltpu.CompilerParams(
            dimension_semantics=("arbitrary",),
            vmem_limit_bytes=V7X_VMEM_LIMIT_BYTES),
        name="ada_mod",
    )(c_prompt, c_sample, w_ada, b_ada)


def _shift_rows(ext, k, halo):
    return pltpu.roll(ext, k, axis=0)[halo:]


def _prompt_kernel(x_ref, mod_ref, g_ref, w_in_ref, pool_w_ref, pool_scale_ref, conv_w_ref,
                   w_out_ref, w_up_ref, ffn_conv_w_ref, w_down_ref,
                   y_ref, pool_out_ref, conv_out_ref, ffn_out_ref,
                   pool_carry, conv_carry, ffn_carry, h2_ref, f_ref):
    b = pl.program_id(0)
    j = pl.program_id(1)
    tm = x_ref.shape[1]

    @pl.when(j == 0)
    def _():
        pool_carry[...] = jnp.zeros_like(pool_carry)
        conv_carry[...] = jnp.zeros_like(conv_carry)
        ffn_carry[...] = jnp.zeros_like(ffn_carry)

    x = x_ref[0]
    mod = mod_ref[pl.ds(b, 1), :]
    sh1, sc1, gt1, sh2, sc2, gt2 = [mod[:, i * D_MODEL:(i + 1) * D_MODEL] for i in range(6)]
    g = g_ref[...]
    g_pre_mix, g_post_mix, g_pre_ffn, g_post_ffn = [g[i:i + 1] for i in range(4)]

    h = _rms(x, g_pre_mix) * (1.0 + sc1) + sh1
    proj = _dot(h.astype(_bf16), w_in_ref[...])
    v_pool = proj[:, :D_POOL]
    x_conv = proj[:, D_POOL:D_POOL + D_CONV]
    gate_b = proj[:, D_POOL + D_CONV:D_POOL + 2 * D_CONV]
    gate_c = proj[:, D_POOL + 2 * D_CONV:]

    pool_ext = jnp.concatenate([pool_carry[...], v_pool], axis=0)
    pos = j * tm + lax.broadcasted_iota(jnp.int32, (tm, POOL_GC), 0)
    y_pool = []
    for gi, w in enumerate(POOL_WINDOWS):
        sl = slice(gi * POOL_GC, (gi + 1) * POOL_GC)
        s = pool_ext[:, sl]
        step = 1
        while step < w:
            s = s + pltpu.roll(s, step, axis=0)
            step *= 2
        cnt = jnp.minimum(w, pos + 1).astype(_f32)
        d = s[POOL_HALO:] / cnt - v_pool[:, sl]
        y_pool.append(_dot(d.astype(_bf16), pool_w_ref[gi]))
    y_pool = jnp.concatenate(y_pool, axis=-1) * pool_scale_ref[...]
    pool_carry[...] = v_pool[tm - POOL_HALO:]
    pool_out_ref[0, 0] = v_pool[tm - POOL_BUF:]

    cx = gate_c * x_conv
    conv_ext = jnp.concatenate([conv_carry[...], cx], axis=0)
    cw = conv_w_ref[...]
    conv = (_shift_rows(conv_ext, 2, CONV_HALO) * cw[0:1]
            + _shift_rows(conv_ext, 1, CONV_HALO) * cw[1:2]
            + cx * cw[2:3])
    y_conv = gate_b * conv
    conv_carry[...] = cx[tm - CONV_HALO:]
    conv_out_ref[0, 0] = cx[tm - (CONV_W - 1):]

    mix = _dot(jnp.concatenate([y_pool, y_conv], axis=-1).astype(_bf16), w_out_ref[...])
    x1 = x + gt1 * _rms(mix, g_post_mix)

    h2_ref[...] = (_rms(x1, g_pre_ffn) * (1.0 + sc2) + sh2).astype(_bf16)

    def ffn_cols(c):
        return [slice(base + c * FF_CHUNK, base + (c + 1) * FF_CHUNK) for base in (0, D_FF)]

    def up_proj(c):
        return [_dot(h2_ref[...], w_up_ref[:, cols]) for cols in ffn_cols(c)]

    ups = up_proj(0)
    for c in range(N_FF_CHUNKS):
        ups_next = up_proj(c + 1) if c + 1 < N_FF_CHUNKS else None
        halves = []
        for cols, up in zip(ffn_cols(c), ups):
            up_ext = jnp.concatenate([ffn_carry[:, cols], up], axis=0)
            fw = ffn_conv_w_ref[:, cols]
            halves.append(_shift_rows(up_ext, 2, CONV_HALO) * fw[0:1]
                          + _shift_rows(up_ext, 1, CONV_HALO) * fw[1:2]
                          + up * fw[2:3])
            ffn_carry[:, cols] = up[tm - CONV_HALO:]
            ffn_out_ref[0, 0, :, cols] = up[tm - (CONV_W - 1):]
        act = (_silu(halves[0]) * halves[1]).astype(_bf16)
        contrib = _dot(act, w_down_ref[c * FF_CHUNK:(c + 1) * FF_CHUNK, :])
        if c == 0:
            f_ref[...] = contrib
        else:
            f_ref[...] += contrib
        ups = ups_next

    y_ref[0] = x1 + gt2 * _rms(f_ref[...], g_post_ffn)


def _const_spec(shape):
    nd = len(shape)
    return pl.BlockSpec(shape, lambda b, j: (0,) * nd, pipeline_mode=pl.Buffered(1))


def _prompt(x, mod, g, w_in, pool_w, pool_scale, conv_w, w_out, w_up, ffn_conv_w, w_down):
    nb, seq, _ = x.shape
    tm = PROMPT_TM
    nt = seq // tm
    consts = (mod, g, w_in, pool_w, pool_scale, conv_w, w_out, w_up, ffn_conv_w, w_down)
    return pl.pallas_call(
        _prompt_kernel,
        grid=(nb, nt),
        in_specs=[pl.BlockSpec((1, tm, D_MODEL), lambda b, j: (b, j, 0))]
        + [_const_spec(a.shape) for a in consts],
        out_specs=[
            pl.BlockSpec((1, tm, D_MODEL), lambda b, j: (b, j, 0)),
            pl.BlockSpec((1, 1, POOL_BUF, D_POOL), lambda b, j: (0, b, 0, 0)),
            pl.BlockSpec((1, 1, CONV_W - 1, D_CONV), lambda b, j: (0, b, 0, 0)),
            pl.BlockSpec((1, 1, CONV_W - 1, 2 * D_FF), lambda b, j: (0, b, 0, 0)),
        ],
        out_shape=[
            jax.ShapeDtypeStruct((nb, seq, D_MODEL), _f32),
            jax.ShapeDtypeStruct((1, nb, POOL_BUF, D_POOL), _f32),
            jax.ShapeDtypeStruct((1, nb, CONV_W - 1, D_CONV), _f32),
            jax.ShapeDtypeStruct((1, nb, CONV_W - 1, 2 * D_FF), _f32),
        ],
        scratch_shapes=[
            pltpu.VMEM((POOL_HALO, D_POOL), _f32),
            pltpu.VMEM((CONV_HALO, D_CONV), _f32),
            pltpu.VMEM((CONV_HALO, 2 * D_FF), _f32),
            pltpu.VMEM((tm, D_MODEL), _bf16),
            pltpu.VMEM((tm, D_MODEL), _f32),
        ],
        compiler_params=pltpu.CompilerParams(
            dimension_semantics=("arbitrary", "arbitrary"),
            vmem_limit_bytes=V7X_VMEM_LIMIT_BYTES),
        name="prompt_trunk",
    )(x, *consts)


def _sample_kernel(x_ref, sp_ref, sc_ref, sf_ref, mod_ref, g_ref, w_in_ref, pool_w_ref,
                   pool_scale_ref, conv_w_ref, w_out_ref, w_up_ref, ffn_conv_w_ref, w_down_ref,
                   y_ref, pool_out_ref, conv_out_ref, ffn_out_ref, *, nb, nt):
    def tile_t(a):
        return jnp.concatenate([a] * nt, axis=0)

    def slabs(ref, n, width):
        return [ref[:, k * width:(k + 1) * width] for k in range(n)]

    x = jnp.concatenate(slabs(x_ref, nt, D_MODEL), axis=0)
    mod = mod_ref[...]
    sh1, sc1, gt1, sh2, sc2, gt2 = [tile_t(mod[:, i * D_MODEL:(i + 1) * D_MODEL])
                                    for i in range(6)]
    g = g_ref[...]
    g_pre_mix, g_post_mix, g_pre_ffn, g_post_ffn = [g[i:i + 1] for i in range(4)]
    m = nt * nb

    h = _rms(x, g_pre_mix) * (1.0 + sc1) + sh1
    proj = _dot(h.astype(_bf16), w_in_ref[...])
    v_pool = proj[:, :D_POOL]
    x_conv = proj[:, D_POOL:D_POOL + D_CONV]
    gate_b = proj[:, D_POOL + D_CONV:D_POOL + 2 * D_CONV]
    gate_c = proj[:, D_POOL + 2 * D_CONV:]

    pool_rows = slabs(sp_ref, POOL_BUF, D_POOL) + [v_pool[t * nb:(t + 1) * nb] for t in range(nt)]
    pool_ext = jnp.concatenate(pool_rows, axis=0)
    y_pool = []
    for gi, w in enumerate(POOL_WINDOWS):
        sl = slice(gi * POOL_GC, (gi + 1) * POOL_GC)
        s = v_pool[:, sl]
        for k in range(1, w):
            s = s + pool_ext[(POOL_BUF - k) * nb:(POOL_BUF - k) * nb + m, sl]
        cnt = float(min(w, PAST_LEN + 1))
        d = s / cnt - v_pool[:, sl]
        y_pool.append(_dot(d.astype(_bf16), pool_w_ref[gi]))
    y_pool = jnp.concatenate(y_pool, axis=-1) * pool_scale_ref[...]
    for k in range(POOL_BUF):
        pool_out_ref[:, k * D_POOL:(k + 1) * D_POOL] = pool_rows[k + nt]

    cx = gate_c * x_conv
    hist = CONV_W - 1
    conv_rows = slabs(sc_ref, hist, D_CONV) + [cx[t * nb:(t + 1) * nb] for t in range(nt)]
    conv_ext = jnp.concatenate(conv_rows, axis=0)
    cw = conv_w_ref[...]
    conv = conv_ext[0:m] * cw[0:1] + conv_ext[nb:nb + m] * cw[1:2] + cx * cw[2:3]
    y_conv = gate_b * conv
    for k in range(hist):
        conv_out_ref[:, k * D_CONV:(k + 1) * D_CONV] = conv_rows[k + nt]

    mix = _dot(jnp.concatenate([y_pool, y_conv], axis=-1).astype(_bf16), w_out_ref[...])
    x1 = x + gt1 * _rms(mix, g_post_mix)

    h2 = (_rms(x1, g_pre_ffn) * (1.0 + sc2) + sh2).astype(_bf16)
    f = None
    for c in range(N_FF_CHUNKS):
        halves = []
        for base in (0, D_FF):
            cols = slice(base + c * FF_CHUNK, base + (c + 1) * FF_CHUNK)
            up = _dot(h2, w_up_ref[:, cols])
            up_rows = ([sf_ref[:, k * 2 * D_FF + cols.start:k * 2 * D_FF + cols.stop]
                        for k in range(hist)]
                       + [up[t * nb:(t + 1) * nb] for t in range(nt)])
            up_ext = jnp.concatenate(up_rows, axis=0)
            fw = ffn_conv_w_ref[:, cols]
            halves.append(up_ext[0:m] * fw[0:1] + up_ext[nb:nb + m] * fw[1:2] + up * fw[2:3])
            for k in range(hist):
                ffn_out_ref[:, k * 2 * D_FF + cols.start:k * 2 * D_FF + cols.stop] = up_rows[k + nt]
        act = (_silu(halves[0]) * halves[1]).astype(_bf16)
        contrib = _dot(act, w_down_ref[c * FF_CHUNK:(c + 1) * FF_CHUNK, :])
        f = contrib if f is None else f + contrib

    y = x1 + gt2 * _rms(f, g_post_ffn)
    for t in range(nt):
        y_ref[:, t * D_MODEL:(t + 1) * D_MODEL] = y[t * nb:(t + 1) * nb]


def _sample(x, sp, sc, sf, mod, g, w_in, pool_w, pool_scale, conv_w, w_out, w_up, ffn_conv_w,
            w_down, *, nt):
    nb = x.shape[0]
    args = (x, sp, sc, sf, mod, g, w_in, pool_w, pool_scale, conv_w, w_out, w_up, ffn_conv_w, w_down)
    return pl.pallas_call(
        functools.partial(_sample_kernel, nb=nb, nt=nt),
        out_shape=[
            jax.ShapeDtypeStruct(x.shape, _f32),
            jax.ShapeDtypeStruct(sp.shape, _f32),
            jax.ShapeDtypeStruct(sc.shape, _f32),
            jax.ShapeDtypeStruct(sf.shape, _f32),
        ],
        compiler_params=pltpu.CompilerParams(vmem_limit_bytes=V7X_VMEM_LIMIT_BYTES),
        name="sample_trunk",
    )(*args)


def kernel(x_prompt, x_sample, state_pool, state_conv, state_ffn, c_prompt, c_sample, w_ada, b_ada,
           g_pre_mix, g_post_mix, g_pre_ffn, g_post_ffn, w_in, pool_w, pool_scale, conv_w, w_out,
           ffn_w_up, ffn_conv_w, ffn_w_down):
    assert w_ada.shape[0] == 1, "single trunk layer"
    nbs, nts, _ = x_sample.shape
    mod_p, mod_s = _ada(c_prompt, c_sample, w_ada[0], b_ada)
    g = jnp.concatenate([g_pre_mix, g_post_mix, g_pre_ffn, g_post_ffn], axis=0)
    weights = (g, w_in[0].astype(_bf16), pool_w[0].astype(_bf16), pool_scale, conv_w[0],
               w_out[0].astype(_bf16), ffn_w_up[0].astype(_bf16), ffn_conv_w[0],
               ffn_w_down[0].astype(_bf16))

    y_p, pool_p, conv_p, ffn_p = _prompt(x_prompt, mod_p, *weights)

    y_s, pool_s, conv_s, ffn_s = _sample(
        x_sample.reshape(nbs, nts * D_MODEL),
        state_pool[0].reshape(nbs, POOL_BUF * D_POOL),
        state_conv[0].reshape(nbs, (CONV_W - 1) * D_CONV),
        state_ffn[0].reshape(nbs, (CONV_W - 1) * 2 * D_FF),
        mod_s, *weights, nt=nts)

    return (y_p, y_s.reshape(nbs, nts, D_MODEL), pool_p, conv_p, ffn_p,
            pool_s.reshape(1, nbs, POOL_BUF, D_POOL),
            conv_s.reshape(1, nbs, CONV_W - 1, D_CONV),
            ffn_s.reshape(1, nbs, CONV_W - 1, 2 * D_FF))
```

```python
import functools

import jax
import jax.numpy as jnp
from jax import lax
from jax.experimental import pallas as pl
from jax.experimental.pallas import tpu as pltpu

D_MODEL = 1024
D_POOL = 512
D_CONV = 512
POOL_WINDOWS = (2, 4, 8, 16)
POOL_GC = 128
POOL_BUF = 15
CONV_W = 3
CONV_HIST = CONV_W - 1
D_FF = 2816
D_IN_PROJ = D_POOL + 3 * D_CONV
RMS_EPS = 1e-6
PAST_LEN = 16384

V7X_VMEM_LIMIT_BYTES = 56 * 1024 * 1024
SUBLANES = 8
LANES = 128
FF_CHUNK = 256
N_FF_CHUNKS = D_FF // FF_CHUNK
PROMPT_TT = 64
PROMPT_PITCH = PROMPT_TT + SUBLANES
POOL_HALO = 16

_bf16 = jnp.bfloat16
_f32 = jnp.float32


def _dot(a, b):
    return jnp.dot(a, b, preferred_element_type=_f32)


def _rms(x, g):
    ms = jnp.mean(x * x, axis=-1, keepdims=True)
    return x * lax.rsqrt(ms + RMS_EPS) * g


def _silu(a):
    return a * jax.nn.sigmoid(a)


def _lane_cat(parts):
    return jnp.concatenate(parts, axis=1)


def _row_cat(parts):
    return jnp.concatenate(parts, axis=0)


def _ada_kernel(cp_ref, cs_ref, w_ref, b_ref, mp_ref, ms_ref):
    w = w_ref[...].astype(_bf16)
    b = b_ref[...]
    mp_ref[...] = _dot(_silu(cp_ref[...]).astype(_bf16), w) + b
    ms_ref[...] = _dot(_silu(cs_ref[...]).astype(_bf16), w) + b


def _ada(c_prompt, c_sample, w_ada, b_ada):
    n = w_ada.shape[1]
    tn = 1024
    bp, bs = c_prompt.shape[0], c_sample.shape[0]
    return pl.pallas_call(
        _ada_kernel,
        grid=(n // tn,),
        in_specs=[
            pl.BlockSpec((bp, D_MODEL), lambda i: (0, 0)),
            pl.BlockSpec((bs, D_MODEL), lambda i: (0, 0)),
            pl.BlockSpec((D_MODEL, tn), lambda i: (0, i)),
            pl.BlockSpec((1, tn), lambda i: (0, i)),
        ],
        out_specs=[
            pl.BlockSpec((bp, tn), lambda i: (0, i)),
            pl.BlockSpec((bs, tn), lambda i: (0, i)),
        ],
        out_shape=[
            jax.ShapeDtypeStruct((bp, n), _f32),
            jax.ShapeDtypeStruct((bs, n), _f32),
        ],
        compiler_params=pltpu.CompilerParams(
            dimension_semantics=("arbitrary",),
            vmem_limit_bytes=V7X_VMEM_LIMIT_BYTES),
        name="ada_mod",
    )(c_prompt, c_sample, w_ada, b_ada)


def _trunk(x, mods, w, h2_ref, f_ref, *, nb, cnt_fn, hist_pool, hist_conv, hist_ffn,
           sink_pool, sink_conv, sink_ffn):
    (g_ref, w_in_ref, pool_w_ref, pool_scale_ref, conv_w_ref, w_out_ref, w_up_ref,
     ffn_conv_w_ref, w_down_ref) = w
    sh1, sc1, gt1, sh2, sc2, gt2 = mods
    m = x.shape[0]
    g = g_ref[...]
    g_pre_mix, g_post_mix, g_pre_ffn, g_post_ffn = [g[i:i + 1] for i in range(4)]

    h = _rms(x, g_pre_mix) * (1.0 + sc1) + sh1
    proj = _dot(h.astype(_bf16), w_in_ref[...])
    v_pool = proj[:, :D_POOL]
    x_conv = proj[:, D_POOL:D_POOL + D_CONV]
    gate_b = proj[:, D_POOL + D_CONV:D_POOL + 2 * D_CONV]
    gate_c = proj[:, D_POOL + 2 * D_CONV:]

    pool_ext = _row_cat([hist_pool(), v_pool])
    halo = (pool_ext.shape[0] - m) // nb
    y_pool = []
    for gi, win in enumerate(POOL_WINDOWS):
        sl = slice(gi * POOL_GC, (gi + 1) * POOL_GC)
        s = pool_ext[:, sl]
        step = 1
        while step < win:
            n = s.shape[0]
            s = s[step * nb:] + s[:n - step * nb]
            step *= 2
        first = (halo - (win - 1)) * nb
        d = s[first:first + m] / cnt_fn(win) - v_pool[:, sl]
        y_pool.append(_dot(d.astype(_bf16), pool_w_ref[gi]))
    y_pool = _lane_cat(y_pool) * pool_scale_ref[...]
    sink_pool(pool_ext)

    cx = gate_c * x_conv
    conv_ext = _row_cat([hist_conv(), cx])
    cw = conv_w_ref[...]
    conv = conv_ext[0:m] * cw[0:1] + conv_ext[nb:nb + m] * cw[1:2] + cx * cw[2:3]
    y_conv = gate_b * conv
    sink_conv(conv_ext)

    mix = _dot(_lane_cat([y_pool, y_conv]).astype(_bf16), w_out_ref[...])
    x1 = x + gt1 * _rms(mix, g_post_mix)

    h2_ref[...] = (_rms(x1, g_pre_ffn) * (1.0 + sc2) + sh2).astype(_bf16)

    def ffn_cols(c):
        return [slice(base + c * FF_CHUNK, base + (c + 1) * FF_CHUNK) for base in (0, D_FF)]

    def up_proj(c):
        return [_dot(h2_ref[...], w_up_ref[:, cols]) for cols in ffn_cols(c)]

    ups = up_proj(0)
    for c in range(N_FF_CHUNKS):
        ups_next = up_proj(c + 1) if c + 1 < N_FF_CHUNKS else None
        halves = []
        for cols, up in zip(ffn_cols(c), ups):
            up_ext = _row_cat([hist_ffn(cols), up])
            fw = ffn_conv_w_ref[:, cols]
            halves.append(up_ext[0:m] * fw[0:1] + up_ext[nb:nb + m] * fw[1:2] + up * fw[2:3])
            sink_ffn(cols, up_ext)
        act = (_silu(halves[0]) * halves[1]).astype(_bf16)
        contrib = _dot(act, w_down_ref[c * FF_CHUNK:(c + 1) * FF_CHUNK, :])
        if c == 0:
            f_ref[...] = contrib
        else:
            f_ref[...] += contrib
        ups = ups_next

    return x1 + gt2 * _rms(f_ref[...], g_post_ffn)


def _prompt_kernel(x_ref, mod_ref, *rest, nb, tt, n_steps):
    w = rest[:9]
    y_ref, pool_out_ref, conv_out_ref, ffn_out_ref = rest[9:13]
    xs, ys, pool_carry, conv_carry, ffn_carry, h2_ref, f_ref = rest[13:]
    j = pl.program_id(0)
    pitch = xs.shape[1] // nb
    n_slabs = D_MODEL // LANES
    m = tt * nb

    @pl.when(j == 0)
    def _():
        pool_carry[...] = jnp.zeros_like(pool_carry)
        conv_carry[...] = jnp.zeros_like(conv_carry)
        ffn_carry[...] = jnp.zeros_like(ffn_carry)

    for b in range(nb):
        for c in range(n_slabs):
            xs[c, b * pitch:b * pitch + tt, :] = x_ref[b, :, c * LANES:(c + 1) * LANES]
    x = _row_cat([_lane_cat([xs[c, pl.ds(i, nb, stride=pitch), :] for c in range(n_slabs)])
                  for i in range(tt)])

    mod = mod_ref[...]
    mods = [_row_cat([mod[:, i * D_MODEL:(i + 1) * D_MODEL]] * tt) for i in range(6)]

    t_idx = j * tt + lax.shift_right_logical(
        lax.broadcasted_iota(jnp.int32, (m, POOL_GC), 0), nb.bit_length() - 1)

    def cnt_fn(win):
        return jnp.minimum(win, t_idx + 1).astype(_f32)

    def carry_rows(ref, first, n):
        return _lane_cat([ref[first + q] for q in range(n)])

    def keep_rows(ref, first, ext):
        rows = ref.shape[1]
        for q in range(ext.shape[1] // LANES):
            ref[first + q] = ext[ext.shape[0] - rows:, q * LANES:(q + 1) * LANES]

    y = _trunk(
        x, mods, w, h2_ref, f_ref, nb=nb, cnt_fn=cnt_fn,
        hist_pool=lambda: carry_rows(pool_carry, 0, D_POOL // LANES),
        hist_conv=lambda: carry_rows(conv_carry, 0, D_CONV // LANES),
        hist_ffn=lambda cols: carry_rows(ffn_carry, cols.start // LANES, FF_CHUNK // LANES),
        sink_pool=lambda ext: keep_rows(pool_carry, 0, ext),
        sink_conv=lambda ext: keep_rows(conv_carry, 0, ext),
        sink_ffn=lambda cols, ext: keep_rows(ffn_carry, cols.start // LANES, ext))

    for i in range(tt):
        for c in range(n_slabs):
            ys[c, pl.ds(i, nb, stride=pitch), :] = y[i * nb:(i + 1) * nb, c * LANES:(c + 1) * LANES]
    for b in range(nb):
        for c in range(n_slabs):
            y_ref[b, :, c * LANES:(c + 1) * LANES] = ys[c, b * pitch:b * pitch + tt, :]

    @pl.when(j == n_steps - 1)
    def _():
        def emit(out_ref, carry, n):
            first = carry.shape[1] // nb - n
            for b in range(nb):
                for q in range(carry.shape[0]):
                    out_ref[0, b, :, q * LANES:(q + 1) * LANES] = (
                        carry[q, pl.ds(first * nb + b, n, stride=nb), :])
        emit(pool_out_ref, pool_carry, POOL_BUF)
        emit(conv_out_ref, conv_carry, CONV_HIST)
        emit(ffn_out_ref, ffn_carry, CONV_HIST)


def _prompt(x, mod, *weights):
    nb, seq, _ = x.shape
    assert nb == SUBLANES, "one vreg row per time step"
    tt = PROMPT_TT
    n_steps = seq // tt
    m = tt * nb
    n_slabs = D_MODEL // LANES

    def const_spec(a):
        nd = a.ndim
        return pl.BlockSpec(a.shape, lambda j: (0,) * nd, pipeline_mode=pl.Buffered(1))

    return pl.pallas_call(
        functools.partial(_prompt_kernel, nb=nb, tt=tt, n_steps=n_steps),
        grid=(n_steps,),
        in_specs=[pl.BlockSpec((nb, tt, D_MODEL), lambda j: (0, j, 0)), const_spec(mod)]
        + [const_spec(a) for a in weights],
        out_specs=[
            pl.BlockSpec((nb, tt, D_MODEL), lambda j: (0, j, 0)),
            pl.BlockSpec((1, nb, POOL_BUF, D_POOL), lambda j: (0, 0, 0, 0)),
            pl.BlockSpec((1, nb, CONV_HIST, D_CONV), lambda j: (0, 0, 0, 0)),
            pl.BlockSpec((1, nb, CONV_HIST, 2 * D_FF), lambda j: (0, 0, 0, 0)),
        ],
        out_shape=[
            jax.ShapeDtypeStruct((nb, seq, D_MODEL), _f32),
            jax.ShapeDtypeStruct((1, nb, POOL_BUF, D_POOL), _f32),
            jax.ShapeDtypeStruct((1, nb, CONV_HIST, D_CONV), _f32),
            jax.ShapeDtypeStruct((1, nb, CONV_HIST, 2 * D_FF), _f32),
        ],
        scratch_shapes=[
            pltpu.VMEM((n_slabs, nb * PROMPT_PITCH, LANES), _f32),
            pltpu.VMEM((n_slabs, nb * PROMPT_PITCH, LANES), _f32),
            pltpu.VMEM((D_POOL // LANES, POOL_HALO * nb, LANES), _f32),
            pltpu.VMEM((D_CONV // LANES, CONV_HIST * nb, LANES), _f32),
            pltpu.VMEM((2 * D_FF // LANES, CONV_HIST * nb, LANES), _f32),
            pltpu.VMEM((m, D_MODEL), _bf16),
            pltpu.VMEM((m, D_MODEL), _f32),
        ],
        compiler_params=pltpu.CompilerParams(
            dimension_semantics=("arbitrary",),
            vmem_limit_bytes=V7X_VMEM_LIMIT_BYTES),
        name="prompt_trunk",
    )(x, mod, *weights)


def _sample_kernel(x_ref, sp_ref, sc_ref, sf_ref, mod_ref, *rest, nb, nt):
    w = rest[:9]
    y_ref, pool_out_ref, conv_out_ref, ffn_out_ref, h2_ref, f_ref = rest[9:]

    def hist(ref, n, pitch, cols):
        return _row_cat([ref[:, k * pitch + cols.start:k * pitch + cols.stop] for k in range(n)])

    def keep(out_ref, n, pitch, cols, ext):
        first = ext.shape[0] - n * nb
        for k in range(n):
            out_ref[:, k * pitch + cols.start:k * pitch + cols.stop] = (
                ext[first + k * nb:first + (k + 1) * nb])

    all_pool, all_conv = slice(0, D_POOL), slice(0, D_CONV)
    x = hist(x_ref, nt, D_MODEL, slice(0, D_MODEL))
    mod = mod_ref[...]
    mods = [_row_cat([mod[:, i * D_MODEL:(i + 1) * D_MODEL]] * nt) for i in range(6)]

    y = _trunk(
        x, mods, w, h2_ref, f_ref, nb=nb,
        cnt_fn=lambda win: float(min(win, PAST_LEN + 1)),
        hist_pool=lambda: hist(sp_ref, POOL_BUF, D_POOL, all_pool),
        hist_conv=lambda: hist(sc_ref, CONV_HIST, D_CONV, all_conv),
        hist_ffn=lambda cols: hist(sf_ref, CONV_HIST, 2 * D_FF, cols),
        sink_pool=lambda ext: keep(pool_out_ref, POOL_BUF, D_POOL, all_pool, ext),
        sink_conv=lambda ext: keep(conv_out_ref, CONV_HIST, D_CONV, all_conv, ext),
        sink_ffn=lambda cols, ext: keep(ffn_out_ref, CONV_HIST, 2 * D_FF, cols, ext))

    for t in range(nt):
        y_ref[:, t * D_MODEL:(t + 1) * D_MODEL] = y[t * nb:(t + 1) * nb]


def _sample(x, sp, sc, sf, mod, *weights, nt):
    nb = x.shape[0]
    return pl.pallas_call(
        functools.partial(_sample_kernel, nb=nb, nt=nt),
        out_shape=[
            jax.ShapeDtypeStruct(x.shape, _f32),
            jax.ShapeDtypeStruct(sp.shape, _f32),
            jax.ShapeDtypeStruct(sc.shape, _f32),
            jax.ShapeDtypeStruct(sf.shape, _f32),
        ],
        scratch_shapes=[
            pltpu.VMEM((nt * nb, D_MODEL), _bf16),
            pltpu.VMEM((nt * nb, D_MODEL), _f32),
        ],
        compiler_params=pltpu.CompilerParams(vmem_limit_bytes=V7X_VMEM_LIMIT_BYTES),
        name="sample_trunk",
    )(x, sp, sc, sf, mod, *weights)


def kernel(x_prompt, x_sample, state_pool, state_conv, state_ffn, c_prompt, c_sample, w_ada, b_ada,
           g_pre_mix, g_post_mix, g_pre_ffn, g_post_ffn, w_in, pool_w, pool_scale, conv_w, w_out,
           ffn_w_up, ffn_conv_w, ffn_w_down):
    assert w_ada.shape[0] == 1, "single trunk layer"
    nbs, nts, _ = x_sample.shape
    mod_p, mod_s = _ada(c_prompt, c_sample, w_ada[0], b_ada)
    g = jnp.concatenate([g_pre_mix, g_post_mix, g_pre_ffn, g_post_ffn], axis=0)
    weights = (g, w_in[0].astype(_bf16), pool_w[0].astype(_bf16), pool_scale, conv_w[0],
               w_out[0].astype(_bf16), ffn_w_up[0].astype(_bf16), ffn_conv_w[0],
               ffn_w_down[0].astype(_bf16))

    y_p, pool_p, conv_p, ffn_p = _prompt(x_prompt, mod_p, *weights)

    y_s, pool_s, conv_s, ffn_s = _sample(
        x_sample.reshape(nbs, nts * D_MODEL),
        state_pool[0].reshape(nbs, POOL_BUF * D_POOL),
        state_conv[0].reshape(nbs, CONV_HIST * D_CONV),
        state_ffn[0].reshape(nbs, CONV_HIST * 2 * D_FF),
        mod_s, *weights, nt=nts)

    return (y_p, y_s.reshape(nbs, nts, D_MODEL), pool_p, conv_p, ffn_p,
            pool_s.reshape(1, nbs, POOL_BUF, D_POOL),
            conv_s.reshape(1, nbs, CONV_HIST, D_CONV),
            ffn_s.reshape(1, nbs, CONV_HIST, 2 * D_FF))
```

```python
import functools

import jax
import jax.numpy as jnp
from jax import lax
from jax.experimental import pallas as pl
from jax.experimental.pallas import tpu as pltpu

D_MODEL = 1024
D_POOL = 512
D_CONV = 512
POOL_WINDOWS = (2, 4, 8, 16)
POOL_GC = 128
POOL_BUF = 15
CONV_W = 3
CONV_HIST = CONV_W - 1
D_FF = 2816
D_IN_PROJ = D_POOL + 3 * D_CONV
RMS_EPS = 1e-6
PAST_LEN = 16384

V7X_VMEM_LIMIT_BYTES = 56 * 1024 * 1024
SUBLANES = 8
LANES = 128
FF_CHUNK = 256
N_FF_CHUNKS = D_FF // FF_CHUNK
FF_DOWN_GROUP = 3
PROMPT_TT = 64
PROMPT_PITCH = PROMPT_TT + SUBLANES
POOL_HALO = 16

_bf16 = jnp.bfloat16
_f32 = jnp.float32


def _dot(a, b):
    return jnp.dot(a, b, preferred_element_type=_f32)


def _rms(x, g):
    ms = jnp.mean(x * x, axis=-1, keepdims=True)
    return x * lax.rsqrt(ms + RMS_EPS) * g


def _silu(a):
    return a * jax.nn.sigmoid(a)


def _lane_cat(parts):
    return jnp.concatenate(parts, axis=1)


def _row_cat(parts):
    return jnp.concatenate(parts, axis=0)


def _ada_kernel(cp_ref, cs_ref, w_ref, b_ref, mp_ref, ms_ref):
    w = w_ref[...].astype(_bf16)
    b = b_ref[...]
    mp_ref[...] = _dot(_silu(cp_ref[...]).astype(_bf16), w) + b
    ms_ref[...] = _dot(_silu(cs_ref[...]).astype(_bf16), w) + b


def _ada(c_prompt, c_sample, w_ada, b_ada):
    n = w_ada.shape[1]
    tn = 1024
    bp, bs = c_prompt.shape[0], c_sample.shape[0]
    return pl.pallas_call(
        _ada_kernel,
        grid=(n // tn,),
        in_specs=[
            pl.BlockSpec((bp, D_MODEL), lambda i: (0, 0)),
            pl.BlockSpec((bs, D_MODEL), lambda i: (0, 0)),
            pl.BlockSpec((D_MODEL, tn), lambda i: (0, i)),
            pl.BlockSpec((1, tn), lambda i: (0, i)),
        ],
        out_specs=[
            pl.BlockSpec((bp, tn), lambda i: (0, i)),
            pl.BlockSpec((bs, tn), lambda i: (0, i)),
        ],
        out_shape=[
            jax.ShapeDtypeStruct((bp, n), _f32),
            jax.ShapeDtypeStruct((bs, n), _f32),
        ],
        compiler_params=pltpu.CompilerParams(
            dimension_semantics=("arbitrary",),
            vmem_limit_bytes=V7X_VMEM_LIMIT_BYTES),
        name="ada_mod",
    )(c_prompt, c_sample, w_ada, b_ada)


def _trunk(x, mods, w, h2_ref, f_ref, act_ref, *, nb, cnt_fn, hist_pool, hist_conv, hist_ffn,
           sink_pool, sink_conv, sink_ffn):
    (g_ref, w_in_ref, pool_w_ref, pool_scale_ref, conv_w_ref, w_out_ref, w_up_ref,
     ffn_conv_w_ref, w_down_ref) = w
    sh1, sc1, gt1, sh2, sc2, gt2 = mods
    m = x.shape[0]
    g = g_ref[...]
    g_pre_mix, g_post_mix, g_pre_ffn, g_post_ffn = [g[i:i + 1] for i in range(4)]

    h = _rms(x, g_pre_mix) * (1.0 + sc1) + sh1
    proj = _dot(h.astype(_bf16), w_in_ref[...])
    v_pool = proj[:, :D_POOL]
    x_conv = proj[:, D_POOL:D_POOL + D_CONV]
    gate_b = proj[:, D_POOL + D_CONV:D_POOL + 2 * D_CONV]
    gate_c = proj[:, D_POOL + 2 * D_CONV:]

    pool_ext = _row_cat([hist_pool(), v_pool])
    halo = (pool_ext.shape[0] - m) // nb
    y_pool = []
    for gi, win in enumerate(POOL_WINDOWS):
        sl = slice(gi * POOL_GC, (gi + 1) * POOL_GC)
        s = pool_ext[:, sl]
        step = 1
        while step < win:
            n = s.shape[0]
            s = s[step * nb:] + s[:n - step * nb]
            step *= 2
        first = (halo - (win - 1)) * nb
        d = s[first:first + m] / cnt_fn(win) - v_pool[:, sl]
        y_pool.append(_dot(d.astype(_bf16), pool_w_ref[gi]))
    y_pool = _lane_cat(y_pool) * pool_scale_ref[...]
    sink_pool(pool_ext)

    cx = gate_c * x_conv
    conv_ext = _row_cat([hist_conv(), cx])
    cw = conv_w_ref[...]
    conv = conv_ext[0:m] * cw[0:1] + conv_ext[nb:nb + m] * cw[1:2] + cx * cw[2:3]
    y_conv = gate_b * conv
    sink_conv(conv_ext)

    mix = _dot(_lane_cat([y_pool, y_conv]).astype(_bf16), w_out_ref[...])
    x1 = x + gt1 * _rms(mix, g_post_mix)

    h2_ref[...] = (_rms(x1, g_pre_ffn) * (1.0 + sc2) + sh2).astype(_bf16)

    def ffn_cols(c):
        return [slice(base + c * FF_CHUNK, base + (c + 1) * FF_CHUNK) for base in (0, D_FF)]

    def up_proj(c):
        return [_dot(h2_ref[...], w_up_ref[:, cols]) for cols in ffn_cols(c)]

    ups = up_proj(0)
    group_start = 0
    for c in range(N_FF_CHUNKS):
        ups_next = up_proj(c + 1) if c + 1 < N_FF_CHUNKS else None
        halves = []
        for cols, up in zip(ffn_cols(c), ups):
            up_ext = _row_cat([hist_ffn(cols), up])
            fw = ffn_conv_w_ref[:, cols]
            halves.append(up_ext[0:m] * fw[0:1] + up_ext[nb:nb + m] * fw[1:2] + up * fw[2:3])
            sink_ffn(cols, up_ext)
        act_ref[:, c * FF_CHUNK:(c + 1) * FF_CHUNK] = (_silu(halves[0]) * halves[1]).astype(_bf16)
        if (c + 1) % FF_DOWN_GROUP == 0 or c + 1 == N_FF_CHUNKS:
            rows = slice(group_start * FF_CHUNK, (c + 1) * FF_CHUNK)
            contrib = _dot(act_ref[:, rows], w_down_ref[rows, :])
            if group_start == 0:
                f_ref[...] = contrib
            else:
                f_ref[...] += contrib
            group_start = c + 1
        ups = ups_next

    return x1 + gt2 * _rms(f_ref[...], g_post_ffn)


def _prompt_kernel(x_ref, mod_ref, *rest, nb, tt, n_steps):
    w = rest[:9]
    y_ref, pool_out_ref, conv_out_ref, ffn_out_ref = rest[9:13]
    xs, ys, pool_carry, conv_carry, ffn_carry, h2_ref, f_ref, act_ref = rest[13:]
    j = pl.program_id(0)
    pitch = xs.shape[1] // nb
    n_slabs = D_MODEL // LANES
    m = tt * nb

    @pl.when(j == 0)
    def _():
        pool_carry[...] = jnp.zeros_like(pool_carry)
        conv_carry[...] = jnp.zeros_like(conv_carry)
        ffn_carry[...] = jnp.zeros_like(ffn_carry)

    for b in range(nb):
        for c in range(n_slabs):
            xs[c, b * pitch:b * pitch + tt, :] = x_ref[b, :, c * LANES:(c + 1) * LANES]
    x = _row_cat([_lane_cat([xs[c, pl.ds(i, nb, stride=pitch), :] for c in range(n_slabs)])
                  for i in range(tt)])

    mod = mod_ref[...]
    mods = [_row_cat([mod[:, i * D_MODEL:(i + 1) * D_MODEL]] * tt) for i in range(6)]

    t_idx = j * tt + lax.shift_right_logical(
        lax.broadcasted_iota(jnp.int32, (m, POOL_GC), 0), nb.bit_length() - 1)

    def cnt_fn(win):
        return jnp.minimum(win, t_idx + 1).astype(_f32)

    def carry_rows(ref, first, n):
        return _lane_cat([ref[first + q] for q in range(n)])

    def keep_rows(ref, first, ext):
        rows = ref.shape[1]
        for q in range(ext.shape[1] // LANES):
            ref[first + q] = ext[ext.shape[0] - rows:, q * LANES:(q + 1) * LANES]

    y = _trunk(
        x, mods, w, h2_ref, f_ref, act_ref, nb=nb, cnt_fn=cnt_fn,
        hist_pool=lambda: carry_rows(pool_carry, 0, D_POOL // LANES),
        hist_conv=lambda: carry_rows(conv_carry, 0, D_CONV // LANES),
        hist_ffn=lambda cols: carry_rows(ffn_carry, cols.start // LANES, FF_CHUNK // LANES),
        sink_pool=lambda ext: keep_rows(pool_carry, 0, ext),
        sink_conv=lambda ext: keep_rows(conv_carry, 0, ext),
        sink_ffn=lambda cols, ext: keep_rows(ffn_carry, cols.start // LANES, ext))

    for i in range(tt):
        for c in range(n_slabs):
            ys[c, pl.ds(i, nb, stride=pitch), :] = y[i * nb:(i + 1) * nb, c * LANES:(c + 1) * LANES]
    for b in range(nb):
        for c in range(n_slabs):
            y_ref[b, :, c * LANES:(c + 1) * LANES] = ys[c, b * pitch:b * pitch + tt, :]

    @pl.when(j == n_steps - 1)
    def _():
        def emit(out_ref, carry, n):
            first = carry.shape[1] // nb - n
            for b in range(nb):
                for q in range(carry.shape[0]):
                    out_ref[0, b, :, q * LANES:(q + 1) * LANES] = (
                        carry[q, pl.ds(first * nb + b, n, stride=nb), :])
        emit(pool_out_ref, pool_carry, POOL_BUF)
        emit(conv_out_ref, conv_carry, CONV_HIST)
        emit(ffn_out_ref, ffn_carry, CONV_HIST)


def _prompt(x, mod, *weights):
    nb, seq, _ = x.shape
    assert nb == SUBLANES, "one vreg row per time step"
    tt = PROMPT_TT
    n_steps = seq // tt
    m = tt * nb
    n_slabs = D_MODEL // LANES

    def const_spec(a):
        nd = a.ndim
        return pl.BlockSpec(a.shape, lambda j: (0,) * nd, pipeline_mode=pl.Buffered(1))

    return pl.pallas_call(
        functools.partial(_prompt_kernel, nb=nb, tt=tt, n_steps=n_steps),
        grid=(n_steps,),
        in_specs=[pl.BlockSpec((nb, tt, D_MODEL), lambda j: (0, j, 0)), const_spec(mod)]
        + [const_spec(a) for a in weights],
        out_specs=[
            pl.BlockSpec((nb, tt, D_MODEL), lambda j: (0, j, 0)),
            pl.BlockSpec((1, nb, POOL_BUF, D_POOL), lambda j: (0, 0, 0, 0)),
            pl.BlockSpec((1, nb, CONV_HIST, D_CONV), lambda j: (0, 0, 0, 0)),
            pl.BlockSpec((1, nb, CONV_HIST, 2 * D_FF), lambda j: (0, 0, 0, 0)),
        ],
        out_shape=[
            jax.ShapeDtypeStruct((nb, seq, D_MODEL), _f32),
            jax.ShapeDtypeStruct((1, nb, POOL_BUF, D_POOL), _f32),
            jax.ShapeDtypeStruct((1, nb, CONV_HIST, D_CONV), _f32),
            jax.ShapeDtypeStruct((1, nb, CONV_HIST, 2 * D_FF), _f32),
        ],
        scratch_shapes=[
            pltpu.VMEM((n_slabs, nb * PROMPT_PITCH, LANES), _f32),
            pltpu.VMEM((n_slabs, nb * PROMPT_PITCH, LANES), _f32),
            pltpu.VMEM((D_POOL // LANES, POOL_HALO * nb, LANES), _f32),
            pltpu.VMEM((D_CONV // LANES, CONV_HIST * nb, LANES), _f32),
            pltpu.VMEM((2 * D_FF // LANES, CONV_HIST * nb, LANES), _f32),
            pltpu.VMEM((m, D_MODEL), _bf16),
            pltpu.VMEM((m, D_MODEL), _f32),
            pltpu.VMEM((m, D_FF), _bf16),
        ],
        compiler_params=pltpu.CompilerParams(
            dimension_semantics=("arbitrary",),
            vmem_limit_bytes=V7X_VMEM_LIMIT_BYTES),
        name="prompt_trunk",
    )(x, mod, *weights)


def _sample_kernel(x_ref, sp_ref, sc_ref, sf_ref, mod_ref, *rest, nb, nt):
    w = rest[:9]
    y_ref, pool_out_ref, conv_out_ref, ffn_out_ref, h2_ref, f_ref, act_ref = rest[9:]

    def hist(ref, n, pitch, cols):
        return _row_cat([ref[:, k * pitch + cols.start:k * pitch + cols.stop] for k in range(n)])

    def keep(out_ref, n, pitch, cols, ext):
        first = ext.shape[0] - n * nb
        for k in range(n):
            out_ref[:, k * pitch + cols.start:k * pitch + cols.stop] = (
                ext[first + k * nb:first + (k + 1) * nb])

    all_pool, all_conv = slice(0, D_POOL), slice(0, D_CONV)
    x = hist(x_ref, nt, D_MODEL, slice(0, D_MODEL))
    mod = mod_ref[...]
    mods = [_row_cat([mod[:, i * D_MODEL:(i + 1) * D_MODEL]] * nt) for i in range(6)]

    y = _trunk(
        x, mods, w, h2_ref, f_ref, act_ref, nb=nb,
        cnt_fn=lambda win: float(min(win, PAST_LEN + 1)),
        hist_pool=lambda: hist(sp_ref, POOL_BUF, D_POOL, all_pool),
        hist_conv=lambda: hist(sc_ref, CONV_HIST, D_CONV, all_conv),
        hist_ffn=lambda cols: hist(sf_ref, CONV_HIST, 2 * D_FF, cols),
        sink_pool=lambda ext: keep(pool_out_ref, POOL_BUF, D_POOL, all_pool, ext),
        sink_conv=lambda ext: keep(conv_out_ref, CONV_HIST, D_CONV, all_conv, ext),
        sink_ffn=lambda cols, ext: keep(ffn_out_ref, CONV_HIST, 2 * D_FF, cols, ext))

    for t in range(nt):
        y_ref[:, t * D_MODEL:(t + 1) * D_MODEL] = y[t * nb:(t + 1) * nb]


def _sample(x, sp, sc, sf, mod, *weights, nt):
    nb = x.shape[0]
    return pl.pallas_call(
        functools.partial(_sample_kernel, nb=nb, nt=nt),
        out_shape=[
            jax.ShapeDtypeStruct(x.shape, _f32),
            jax.ShapeDtypeStruct(sp.shape, _f32),
            jax.ShapeDtypeStruct(sc.shape, _f32),
            jax.ShapeDtypeStruct(sf.shape, _f32),
        ],
        scratch_shapes=[
            pltpu.VMEM((nt * nb, D_MODEL), _bf16),
            pltpu.VMEM((nt * nb, D_MODEL), _f32),
            pltpu.VMEM((nt * nb, D_FF), _bf16),
        ],
        compiler_params=pltpu.CompilerParams(vmem_limit_bytes=V7X_VMEM_LIMIT_BYTES),
        name="sample_trunk",
    )(x, sp, sc, sf, mod, *weights)


def kernel(x_prompt, x_sample, state_pool, state_conv, state_ffn, c_prompt, c_sample, w_ada, b_ada,
           g_pre_mix, g_post_mix, g_pre_ffn, g_post_ffn, w_in, pool_w, pool_scale, conv_w, w_out,
           ffn_w_up, ffn_conv_w, ffn_w_down):
    assert w_ada.shape[0] == 1, "single trunk layer"
    nbs, nts, _ = x_sample.shape
    mod_p, mod_s = _ada(c_prompt, c_sample, w_ada[0], b_ada)
    g = jnp.concatenate([g_pre_mix, g_post_mix, g_pre_ffn, g_post_ffn], axis=0)
    weights = (g, w_in[0].astype(_bf16), pool_w[0].astype(_bf16), pool_scale, conv_w[0],
               w_out[0].astype(_bf16), ffn_w_up[0].astype(_bf16), ffn_conv_w[0],
               ffn_w_down[0].astype(_bf16))

    y_p, pool_p, conv_p, ffn_p = _prompt(x_prompt, mod_p, *weights)

    y_s, pool_s, conv_s, ffn_s = _sample(
        x_sample.reshape(nbs, nts * D_MODEL),
        state_pool[0].reshape(nbs, POOL_BUF * D_POOL),
        state_conv[0].reshape(nbs, CONV_HIST * D_CONV),
        state_ffn[0].reshape(nbs, CONV_HIST * 2 * D_FF),
        mod_s, *weights, nt=nts)

    return (y_p, y_s.reshape(nbs, nts, D_MODEL), pool_p, conv_p, ffn_p,
            pool_s.reshape(1, nbs, POOL_BUF, D_POOL),
            conv_s.reshape(1, nbs, CONV_HIST, D_CONV),
            ffn_s.reshape(1, nbs, CONV_HIST, 2 * D_FF))
```

```python
import functools

import jax
import jax.numpy as jnp
from jax import lax
from jax.experimental import pallas as pl
from jax.experimental.pallas import tpu as pltpu

D_MODEL = 1024
D_POOL = 512
D_CONV = 512
POOL_WINDOWS = (2, 4, 8, 16)
POOL_GC = 128
POOL_BUF = 15
CONV_W = 3
CONV_HIST = CONV_W - 1
D_FF = 2816
D_IN_PROJ = D_POOL + 3 * D_CONV
RMS_EPS = 1e-6
PAST_LEN = 16384

V7X_VMEM_LIMIT_BYTES = 56 * 1024 * 1024
SUBLANES = 8
LANES = 128
FF_CHUNK = 256
N_FF_CHUNKS = D_FF // FF_CHUNK
FF_DOWN_GROUP = 3
FFN_LO_BLOCKS = 40
PROMPT_TT = 64
PROMPT_PITCH = PROMPT_TT + SUBLANES
POOL_HALO = 16

_bf16 = jnp.bfloat16
_f32 = jnp.float32


def _dot(a, b):
    return jnp.dot(a, b, preferred_element_type=_f32)


def _rms(x, g):
    ms = jnp.mean(x * x, axis=-1, keepdims=True)
    return x * lax.rsqrt(ms + RMS_EPS) * g


def _silu(a):
    return a * jax.nn.sigmoid(a)


def _lane_cat(parts):
    return jnp.concatenate(parts, axis=1)


def _row_cat(parts):
    return jnp.concatenate(parts, axis=0)


def _ada_kernel(cp_ref, cs_ref, w_ref, b_ref, mp_ref, ms_ref):
    w = w_ref[...].astype(_bf16)
    b = b_ref[...]
    mp_ref[...] = _dot(_silu(cp_ref[...]).astype(_bf16), w) + b
    ms_ref[...] = _dot(_silu(cs_ref[...]).astype(_bf16), w) + b


def _ada(c_prompt, c_sample, w_ada, b_ada):
    n = w_ada.shape[1]
    tn = 1024
    bp, bs = c_prompt.shape[0], c_sample.shape[0]
    return pl.pallas_call(
        _ada_kernel,
        grid=(n // tn,),
        in_specs=[
            pl.BlockSpec((bp, D_MODEL), lambda i: (0, 0)),
            pl.BlockSpec((bs, D_MODEL), lambda i: (0, 0)),
            pl.BlockSpec((D_MODEL, tn), lambda i: (0, i)),
            pl.BlockSpec((1, tn), lambda i: (0, i)),
        ],
        out_specs=[
            pl.BlockSpec((bp, tn), lambda i: (0, i)),
            pl.BlockSpec((bs, tn), lambda i: (0, i)),
        ],
        out_shape=[
            jax.ShapeDtypeStruct((bp, n), _f32),
            jax.ShapeDtypeStruct((bs, n), _f32),
        ],
        compiler_params=pltpu.CompilerParams(
            dimension_semantics=("arbitrary",),
            vmem_limit_bytes=V7X_VMEM_LIMIT_BYTES),
        name="ada_mod",
    )(c_prompt, c_sample, w_ada, b_ada)


def _trunk(x, mods, w, h2_ref, f_ref, act_ref, *, nb, cnt_fn, hist_pool, hist_conv, hist_ffn,
           sink_pool, sink_conv, sink_ffn):
    (g_ref, w_in_ref, pool_w_ref, pool_scale_ref, conv_w_ref, w_out_ref, w_up_ref,
     ffn_conv_w_ref, w_down_ref) = w
    sh1, sc1, gt1, sh2, sc2, gt2 = mods
    m = x.shape[0]
    g = g_ref[...]
    g_pre_mix, g_post_mix, g_pre_ffn, g_post_ffn = [g[i:i + 1] for i in range(4)]

    h = _rms(x, g_pre_mix) * (1.0 + sc1) + sh1
    proj = _dot(h.astype(_bf16), w_in_ref[...])
    v_pool = proj[:, :D_POOL]
    x_conv = proj[:, D_POOL:D_POOL + D_CONV]
    gate_b = proj[:, D_POOL + D_CONV:D_POOL + 2 * D_CONV]
    gate_c = proj[:, D_POOL + 2 * D_CONV:]

    pool_ext = _row_cat([hist_pool(), v_pool])
    halo = (pool_ext.shape[0] - m) // nb
    y_pool = []
    for gi, win in enumerate(POOL_WINDOWS):
        sl = slice(gi * POOL_GC, (gi + 1) * POOL_GC)
        s = pool_ext[:, sl]
        step = 1
        while step < win:
            n = s.shape[0]
            s = s[step * nb:] + s[:n - step * nb]
            step *= 2
        first = (halo - (win - 1)) * nb
        d = s[first:first + m] / cnt_fn(win) - v_pool[:, sl]
        y_pool.append(_dot(d.astype(_bf16), pool_w_ref[gi]))
    y_pool = _lane_cat(y_pool) * pool_scale_ref[...]
    sink_pool(pool_ext)

    cx = gate_c * x_conv
    conv_ext = _row_cat([hist_conv(), cx])
    cw = conv_w_ref[...]
    conv = conv_ext[0:m] * cw[0:1] + conv_ext[nb:nb + m] * cw[1:2] + cx * cw[2:3]
    y_conv = gate_b * conv
    sink_conv(conv_ext)

    mix = _dot(_lane_cat([y_pool, y_conv]).astype(_bf16), w_out_ref[...])
    x1 = x + gt1 * _rms(mix, g_post_mix)

    h2_ref[...] = (_rms(x1, g_pre_ffn) * (1.0 + sc2) + sh2).astype(_bf16)

    def ffn_cols(c):
        return [slice(base + c * FF_CHUNK, base + (c + 1) * FF_CHUNK) for base in (0, D_FF)]

    def up_proj(c):
        return [_dot(h2_ref[...], w_up_ref[:, cols]) for cols in ffn_cols(c)]

    ups = up_proj(0)
    group_start = 0
    for c in range(N_FF_CHUNKS):
        ups_next = up_proj(c + 1) if c + 1 < N_FF_CHUNKS else None
        halves = []
        for cols, up in zip(ffn_cols(c), ups):
            up_ext = _row_cat([hist_ffn(cols), up])
            fw = ffn_conv_w_ref[:, cols]
            halves.append(up_ext[0:m] * fw[0:1] + up_ext[nb:nb + m] * fw[1:2] + up * fw[2:3])
            sink_ffn(cols, up_ext)
        act_ref[:, c * FF_CHUNK:(c + 1) * FF_CHUNK] = (_silu(halves[0]) * halves[1]).astype(_bf16)
        if (c + 1) % FF_DOWN_GROUP == 0 or c + 1 == N_FF_CHUNKS:
            rows = slice(group_start * FF_CHUNK, (c + 1) * FF_CHUNK)
            contrib = _dot(act_ref[:, rows], w_down_ref[rows, :])
            if group_start == 0:
                f_ref[...] = contrib
            else:
                f_ref[...] += contrib
            group_start = c + 1
        ups = ups_next

    return x1 + gt2 * _rms(f_ref[...], g_post_ffn)


def _prompt_kernel(x_ref, mod_ref, *rest, nb, tt, n_steps):
    w = rest[:9]
    y_ref, pool_out_ref, conv_out_ref, ffn_out_ref = rest[9:13]
    xs, ys, pool_carry, conv_carry, ffn_carry, h2_ref, f_ref, act_ref = rest[13:]
    j = pl.program_id(0)
    pitch = xs.shape[1] // nb
    n_slabs = D_MODEL // LANES
    m = tt * nb

    @pl.when(j == 0)
    def _():
        pool_carry[...] = jnp.zeros_like(pool_carry)
        conv_carry[...] = jnp.zeros_like(conv_carry)
        ffn_carry[...] = jnp.zeros_like(ffn_carry)

    for b in range(nb):
        for c in range(n_slabs):
            xs[c, b * pitch:b * pitch + tt, :] = x_ref[b, :, c * LANES:(c + 1) * LANES]
    x = _row_cat([_lane_cat([xs[c, pl.ds(i, nb, stride=pitch), :] for c in range(n_slabs)])
                  for i in range(tt)])

    mod = mod_ref[...]
    mods = [_row_cat([mod[:, i * D_MODEL:(i + 1) * D_MODEL]] * tt) for i in range(6)]

    t_idx = j * tt + lax.shift_right_logical(
        lax.broadcasted_iota(jnp.int32, (m, POOL_GC), 0), nb.bit_length() - 1)

    def cnt_fn(win):
        return jnp.minimum(win, t_idx + 1).astype(_f32)

    def carry_rows(ref, first, n):
        return _lane_cat([ref[first + q] for q in range(n)])

    def keep_rows(ref, first, ext):
        rows = ref.shape[1]
        for q in range(ext.shape[1] // LANES):
            ref[first + q] = ext[ext.shape[0] - rows:, q * LANES:(q + 1) * LANES]

    y = _trunk(
        x, mods, w, h2_ref, f_ref, act_ref, nb=nb, cnt_fn=cnt_fn,
        hist_pool=lambda: carry_rows(pool_carry, 0, D_POOL // LANES),
        hist_conv=lambda: carry_rows(conv_carry, 0, D_CONV // LANES),
        hist_ffn=lambda cols: carry_rows(ffn_carry, cols.start // LANES, FF_CHUNK // LANES),
        sink_pool=lambda ext: keep_rows(pool_carry, 0, ext),
        sink_conv=lambda ext: keep_rows(conv_carry, 0, ext),
        sink_ffn=lambda cols, ext: keep_rows(ffn_carry, cols.start // LANES, ext))

    for i in range(tt):
        for c in range(n_slabs):
            ys[c, pl.ds(i, nb, stride=pitch), :] = y[i * nb:(i + 1) * nb, c * LANES:(c + 1) * LANES]
    for b in range(nb):
        for c in range(n_slabs):
            y_ref[b, :, c * LANES:(c + 1) * LANES] = ys[c, b * pitch:b * pitch + tt, :]

    @pl.when(j == n_steps - 1)
    def _():
        def emit(out_ref, carry, n):
            first = carry.shape[1] // nb - n
            for b in range(nb):
                for q in range(carry.shape[0]):
                    out_ref[0, b, :, q * LANES:(q + 1) * LANES] = (
                        carry[q, pl.ds(first * nb + b, n, stride=nb), :])
        pool_rows = carry_rows(pool_carry, 0, D_POOL // LANES)
        for k in range(POOL_BUF):
            first = (POOL_HALO - POOL_BUF + k) * nb
            pool_out_ref[k] = pool_rows[first:first + nb]
        emit(conv_out_ref, conv_carry, CONV_HIST)
        emit(ffn_out_ref, ffn_carry, CONV_HIST)


def _prompt(x, mod, *weights):
    nb, seq, _ = x.shape
    assert nb == SUBLANES, "one vreg row per time step"
    tt = PROMPT_TT
    n_steps = seq // tt
    m = tt * nb
    n_slabs = D_MODEL // LANES

    def const_spec(a):
        nd = a.ndim
        return pl.BlockSpec(a.shape, lambda j: (0,) * nd, pipeline_mode=pl.Buffered(1))

    return pl.pallas_call(
        functools.partial(_prompt_kernel, nb=nb, tt=tt, n_steps=n_steps),
        grid=(n_steps,),
        in_specs=[pl.BlockSpec((nb, tt, D_MODEL), lambda j: (0, j, 0)), const_spec(mod)]
        + [const_spec(a) for a in weights],
        out_specs=[
            pl.BlockSpec((nb, tt, D_MODEL), lambda j: (0, j, 0)),
            pl.BlockSpec((POOL_BUF, nb, D_POOL), lambda j: (0, 0, 0)),
            pl.BlockSpec((1, nb, CONV_HIST, D_CONV), lambda j: (0, 0, 0, 0)),
            pl.BlockSpec((1, nb, CONV_HIST, 2 * D_FF), lambda j: (0, 0, 0, 0)),
        ],
        out_shape=[
            jax.ShapeDtypeStruct((nb, seq, D_MODEL), _f32),
            jax.ShapeDtypeStruct((POOL_BUF, nb, D_POOL), _f32),
            jax.ShapeDtypeStruct((1, nb, CONV_HIST, D_CONV), _f32),
            jax.ShapeDtypeStruct((1, nb, CONV_HIST, 2 * D_FF), _f32),
        ],
        scratch_shapes=[
            pltpu.VMEM((n_slabs, nb * PROMPT_PITCH, LANES), _f32),
            pltpu.VMEM((n_slabs, nb * PROMPT_PITCH, LANES), _f32),
            pltpu.VMEM((D_POOL // LANES, POOL_HALO * nb, LANES), _f32),
            pltpu.VMEM((D_CONV // LANES, CONV_HIST * nb, LANES), _f32),
            pltpu.VMEM((2 * D_FF // LANES, CONV_HIST * nb, LANES), _f32),
            pltpu.VMEM((m, D_MODEL), _bf16),
            pltpu.VMEM((m, D_MODEL), _f32),
            pltpu.VMEM((m, D_FF), _bf16),
        ],
        compiler_params=pltpu.CompilerParams(
            dimension_semantics=("arbitrary",),
            vmem_limit_bytes=V7X_VMEM_LIMIT_BYTES),
        name="prompt_trunk",
    )(x, mod, *weights)


def _step_rows(n_blocks, n_steps, k, c, nb):
    return pl.ds(n_steps * c + k, nb, stride=n_steps * n_blocks)


def _sample_kernel(x_ref, sp_ref, sc_ref, sf_ref, mod_ref, *rest, nb, nt):
    w = rest[:9]
    (y_ref, pool_out_ref, conv_out_ref, ffn_lo_ref, ffn_hi_ref,
     h2_ref, f_ref, act_ref) = rest[9:]
    d_blocks, c_blocks = D_MODEL // LANES, D_CONV // LANES

    def read_steps(ref, n_blocks, n_steps):
        return _row_cat([_lane_cat([ref[_step_rows(n_blocks, n_steps, k, c, nb), :]
                                    for c in range(n_blocks)]) for k in range(n_steps)])

    def write_steps(ref, n_blocks, n_steps, rows):
        for k in range(n_steps):
            for c in range(n_blocks):
                ref[_step_rows(n_blocks, n_steps, k, c, nb), :] = (
                    rows[k * nb:(k + 1) * nb, c * LANES:(c + 1) * LANES])

    def hist_ffn(cols):
        return _row_cat([sf_ref[:, k * 2 * D_FF + cols.start:k * 2 * D_FF + cols.stop]
                         for k in range(CONV_HIST)])

    def sink_pool(ext):
        first = ext.shape[0] - POOL_BUF * nb
        for k in range(POOL_BUF):
            pool_out_ref[k] = ext[first + k * nb:first + (k + 1) * nb]

    def sink_conv(ext):
        write_steps(conv_out_ref, c_blocks, CONV_HIST, ext[ext.shape[0] - CONV_HIST * nb:])

    def sink_ffn(cols, ext):
        last = ext[ext.shape[0] - CONV_HIST * nb:]
        hi_blocks = 2 * D_FF // LANES - FFN_LO_BLOCKS
        for q in range(FF_CHUNK // LANES):
            block = cols.start // LANES + q
            if block < FFN_LO_BLOCKS:
                ref, n_blocks = ffn_lo_ref, FFN_LO_BLOCKS
            else:
                ref, n_blocks, block = ffn_hi_ref, hi_blocks, block - FFN_LO_BLOCKS
            for k in range(CONV_HIST):
                ref[_step_rows(n_blocks, CONV_HIST, k, block, nb), :] = (
                    last[k * nb:(k + 1) * nb, q * LANES:(q + 1) * LANES])

    x = read_steps(x_ref, d_blocks, nt)
    mod = mod_ref[...]
    mods = [_row_cat([mod[:, i * D_MODEL:(i + 1) * D_MODEL]] * nt) for i in range(6)]

    y = _trunk(
        x, mods, w, h2_ref, f_ref, act_ref, nb=nb,
        cnt_fn=lambda win: float(min(win, PAST_LEN + 1)),
        hist_pool=lambda: _row_cat([sp_ref[k] for k in range(POOL_BUF)]),
        hist_conv=lambda: read_steps(sc_ref, c_blocks, CONV_HIST),
        hist_ffn=hist_ffn, sink_pool=sink_pool, sink_conv=sink_conv, sink_ffn=sink_ffn)

    write_steps(y_ref, d_blocks, nt, y)


def _sample(x, sp, sc, sf, mod, *weights, nb, nt):
    rows_per_block = nb * CONV_HIST
    hi_blocks = 2 * D_FF // LANES - FFN_LO_BLOCKS
    return pl.pallas_call(
        functools.partial(_sample_kernel, nb=nb, nt=nt),
        out_shape=[
            jax.ShapeDtypeStruct(x.shape, _f32),
            jax.ShapeDtypeStruct(sp.shape, _f32),
            jax.ShapeDtypeStruct(sc.shape, _f32),
            jax.ShapeDtypeStruct((FFN_LO_BLOCKS * rows_per_block, LANES), _f32),
            jax.ShapeDtypeStruct((hi_blocks * rows_per_block, LANES), _f32),
        ],
        scratch_shapes=[
            pltpu.VMEM((nt * nb, D_MODEL), _bf16),
            pltpu.VMEM((nt * nb, D_MODEL), _f32),
            pltpu.VMEM((nt * nb, D_FF), _bf16),
        ],
        compiler_params=pltpu.CompilerParams(vmem_limit_bytes=V7X_VMEM_LIMIT_BYTES),
        name="sample_trunk",
    )(x, sp, sc, sf, mod, *weights)


def _to_block_major(a, n_steps):
    nb, _, width = a.shape
    return (a.reshape(nb, n_steps, width // LANES, LANES).transpose(0, 2, 1, 3)
            .reshape(nb * (width // LANES) * n_steps, LANES))


def _from_block_major(a, nb, n_steps):
    blocks = a.shape[0] // (nb * n_steps)
    return (a.reshape(nb, blocks, n_steps, LANES).transpose(0, 2, 1, 3)
            .reshape(nb, n_steps, blocks * LANES))


def kernel(x_prompt, x_sample, state_pool, state_conv, state_ffn, c_prompt, c_sample, w_ada, b_ada,
           g_pre_mix, g_post_mix, g_pre_ffn, g_post_ffn, w_in, pool_w, pool_scale, conv_w, w_out,
           ffn_w_up, ffn_conv_w, ffn_w_down):
    assert w_ada.shape[0] == 1, "single trunk layer"
    nbs, nts, _ = x_sample.shape
    mod_p, mod_s = _ada(c_prompt, c_sample, w_ada[0], b_ada)
    g = jnp.concatenate([g_pre_mix, g_post_mix, g_pre_ffn, g_post_ffn], axis=0)
    weights = (g, w_in[0].astype(_bf16), pool_w[0].astype(_bf16), pool_scale, conv_w[0],
               w_out[0].astype(_bf16), ffn_w_up[0].astype(_bf16), ffn_conv_w[0],
               ffn_w_down[0].astype(_bf16))

    y_p, pool_p, conv_p, ffn_p = _prompt(x_prompt, mod_p, *weights)

    y_s, pool_s, conv_s, ffn_lo, ffn_hi = _sample(
        _to_block_major(x_sample, nts),
        state_pool[0].transpose(1, 0, 2),
        _to_block_major(state_conv[0], CONV_HIST),
        state_ffn[0].reshape(nbs, CONV_HIST * 2 * D_FF),
        mod_s, *weights, nb=nbs, nt=nts)
    ffn_s = jnp.concatenate([_from_block_major(ffn_lo, nbs, CONV_HIST),
                             _from_block_major(ffn_hi, nbs, CONV_HIST)], axis=-1)

    return (y_p, _from_block_major(y_s, nbs, nts), pool_p.transpose(1, 0, 2)[None], conv_p, ffn_p,
            pool_s.transpose(1, 0, 2)[None],
            _from_block_major(conv_s, nbs, CONV_HIST)[None],
            ffn_s[None])
```

```python
import functools

import jax
import jax.numpy as jnp
from jax import lax
from jax.experimental import pallas as pl
from jax.experimental.pallas import tpu as pltpu

D_MODEL = 1024
D_POOL = 512
D_CONV = 512
POOL_WINDOWS = (2, 4, 8, 16)
POOL_GC = 128
POOL_BUF = 15
CONV_W = 3
CONV_HIST = CONV_W - 1
D_FF = 2816
D_IN_PROJ = D_POOL + 3 * D_CONV
RMS_EPS = 1e-6
PAST_LEN = 16384

V7X_VMEM_LIMIT_BYTES = 56 * 1024 * 1024
SUBLANES = 8
LANES = 128
FF_CHUNK = 256
N_FF_CHUNKS = D_FF // FF_CHUNK
FF_DOWN_GROUP = 3
FFN_LO_BLOCKS = 40
PROMPT_TT = 64
PROMPT_PITCH = PROMPT_TT + SUBLANES
ADA_TN = 2048
POOL_HALO = 16

_bf16 = jnp.bfloat16
_f32 = jnp.float32


def _dot(a, b):
    return jnp.dot(a, b, preferred_element_type=_f32)


def _rms_scaled(x, scale):
    ms = jnp.mean(x * x, axis=-1, keepdims=True)
    return x * lax.rsqrt(ms + RMS_EPS) * scale


def _silu(a):
    return a * jax.nn.sigmoid(a)


def _lane_cat(parts):
    return jnp.concatenate(parts, axis=1)


def _row_cat(parts):
    return jnp.concatenate(parts, axis=0)


def _ada_kernel(cp_ref, cs_ref, w_ref, b_ref, mp_ref, ms_ref):
    c = _row_cat([cp_ref[...], cs_ref[...]])
    mod = _dot(_silu(c).astype(_bf16), w_ref[...].astype(_bf16)) + b_ref[...]
    bp = cp_ref.shape[0]
    mp_ref[...] = mod[:bp]
    ms_ref[...] = mod[bp:]


def _ada(c_prompt, c_sample, w_ada, b_ada):
    n = w_ada.shape[1]
    tn = ADA_TN
    bp, bs = c_prompt.shape[0], c_sample.shape[0]
    return pl.pallas_call(
        _ada_kernel,
        grid=(n // tn,),
        in_specs=[
            pl.BlockSpec((bp, D_MODEL), lambda i: (0, 0)),
            pl.BlockSpec((bs, D_MODEL), lambda i: (0, 0)),
            pl.BlockSpec((D_MODEL, tn), lambda i: (0, i)),
            pl.BlockSpec((1, tn), lambda i: (0, i)),
        ],
        out_specs=[
            pl.BlockSpec((bp, tn), lambda i: (0, i)),
            pl.BlockSpec((bs, tn), lambda i: (0, i)),
        ],
        out_shape=[
            jax.ShapeDtypeStruct((bp, n), _f32),
            jax.ShapeDtypeStruct((bs, n), _f32),
        ],
        compiler_params=pltpu.CompilerParams(
            dimension_semantics=("arbitrary",),
            vmem_limit_bytes=V7X_VMEM_LIMIT_BYTES),
        name="ada_mod",
    )(c_prompt, c_sample, w_ada, b_ada)


def _front_stages(x, mods, w, out, *, nb, cnt_fn, hist_pool, hist_conv, sink_pool, sink_conv):
    (_, w_in_ref, pool_w_ref, pool_scale_ref, conv_w_ref, w_out_ref) = w[:6]
    sh1, scale1, gate1, sh2, scale2, _ = mods
    m = x.shape[0]

    h = (_rms_scaled(x, scale1) + sh1).astype(_bf16)
    yield
    proj = _dot(h, w_in_ref[...])
    yield
    v_pool = proj[:, :D_POOL]
    x_conv = proj[:, D_POOL:D_POOL + D_CONV]
    gate_b = proj[:, D_POOL + D_CONV:D_POOL + 2 * D_CONV]
    gate_c = proj[:, D_POOL + 2 * D_CONV:]

    pool_ext = _row_cat([hist_pool(), v_pool])
    halo = (pool_ext.shape[0] - m) // nb
    y_pool = []
    for gi, win in enumerate(POOL_WINDOWS):
        sl = slice(gi * POOL_GC, (gi + 1) * POOL_GC)
        s = pool_ext[:, sl]
        step = 1
        while step < win:
            n = s.shape[0]
            s = s[step * nb:] + s[:n - step * nb]
            step *= 2
        first = (halo - (win - 1)) * nb
        d = s[first:first + m] / cnt_fn(win) - v_pool[:, sl]
        y_pool.append(_dot(d.astype(_bf16), pool_w_ref[gi]))
    y_pool = _lane_cat(y_pool) * pool_scale_ref[...]
    sink_pool(pool_ext)

    cx = gate_c * x_conv
    conv_ext = _row_cat([hist_conv(), cx])
    cw = conv_w_ref[...]
    conv = conv_ext[0:m] * cw[0:1] + conv_ext[nb:nb + m] * cw[1:2] + cx * cw[2:3]
    y_conv = gate_b * conv
    sink_conv(conv_ext)
    mixed = _lane_cat([y_pool, y_conv]).astype(_bf16)
    yield
    mix = _dot(mixed, w_out_ref[...])
    yield
    x1 = x + _rms_scaled(mix, gate1)
    out["x1"] = x1
    out["h2"] = (_rms_scaled(x1, scale2) + sh2).astype(_bf16)
    yield


def _ffn_stages(w, h2_ref, f_ref, act_ref, *, nb, hist_ffn, sink_ffn):
    w_up_ref, ffn_conv_w_ref, w_down_ref = w[6:]
    m = h2_ref.shape[0]

    def ffn_cols(c):
        return [slice(base + c * FF_CHUNK, base + (c + 1) * FF_CHUNK) for base in (0, D_FF)]

    def up_proj(c):
        return [_dot(h2_ref[...], w_up_ref[:, cols]) for cols in ffn_cols(c)]

    ups = up_proj(0)
    group_start = 0
    for c in range(N_FF_CHUNKS):
        ups_next = up_proj(c + 1) if c + 1 < N_FF_CHUNKS else None
        halves = []
        for cols, up in zip(ffn_cols(c), ups):
            up_ext = _row_cat([hist_ffn(cols), up])
            fw = ffn_conv_w_ref[:, cols]
            halves.append(up_ext[0:m] * fw[0:1] + up_ext[nb:nb + m] * fw[1:2] + up * fw[2:3])
            sink_ffn(cols, up_ext)
        act_ref[:, c * FF_CHUNK:(c + 1) * FF_CHUNK] = (_silu(halves[0]) * halves[1]).astype(_bf16)
        if (c + 1) % FF_DOWN_GROUP == 0 or c + 1 == N_FF_CHUNKS:
            rows = slice(group_start * FF_CHUNK, (c + 1) * FF_CHUNK)
            contrib = _dot(act_ref[:, rows], w_down_ref[rows, :])
            if group_start == 0:
                f_ref[...] = contrib
            else:
                f_ref[...] += contrib
            group_start = c + 1
        ups = ups_next
        yield


def _finish(x1, f_ref, gate2):
    return x1 + _rms_scaled(f_ref[...], gate2)


def _fold_mods(mod, g_ref, n_steps):
    g = g_ref[...]
    sh1, sc1, gt1, sh2, sc2, gt2 = [mod[:, i * D_MODEL:(i + 1) * D_MODEL] for i in range(6)]
    folded = [sh1, g[0:1] * (1.0 + sc1), gt1 * g[1:2], sh2, g[2:3] * (1.0 + sc2), gt2 * g[3:4]]
    return [_row_cat([a] * n_steps) for a in folded]


def _trunk(x, mods, w, h2_ref, f_ref, act_ref, *, nb, cnt_fn, hist_pool, hist_conv, hist_ffn,
           sink_pool, sink_conv, sink_ffn):
    out = {}
    for _ in _front_stages(x, mods, w, out, nb=nb, cnt_fn=cnt_fn, hist_pool=hist_pool,
                           hist_conv=hist_conv, sink_pool=sink_pool, sink_conv=sink_conv):
        pass
    h2_ref[...] = out["h2"]
    for _ in _ffn_stages(w, h2_ref, f_ref, act_ref, nb=nb, hist_ffn=hist_ffn, sink_ffn=sink_ffn):
        pass
    return _finish(out["x1"], f_ref, mods[5])


def _prompt_kernel(x_ref, mod_ref, *rest, nb, tt, n_steps):
    w = rest[:9]
    y_ref, pool_out_ref, conv_out_ref, ffn_out_ref = rest[9:13]
    xs, ys, pool_carry, conv_carry, ffn_carry, h2_ref, f_ref, act_ref = rest[13:]
    j = pl.program_id(0)
    pitch = xs.shape[1] // nb
    n_slabs = D_MODEL // LANES
    m = tt * nb

    @pl.when(j == 0)
    def _():
        pool_carry[...] = jnp.zeros_like(pool_carry)
        conv_carry[...] = jnp.zeros_like(conv_carry)
        ffn_carry[...] = jnp.zeros_like(ffn_carry)

    for b in range(nb):
        for c in range(n_slabs):
            xs[c, b * pitch:b * pitch + tt, :] = x_ref[b, :, c * LANES:(c + 1) * LANES]
    x = _row_cat([_lane_cat([xs[c, pl.ds(i, nb, stride=pitch), :] for c in range(n_slabs)])
                  for i in range(tt)])

    mods = _fold_mods(mod_ref[...], w[0], tt)

    t_idx = j * tt + lax.shift_right_logical(
        lax.broadcasted_iota(jnp.int32, (m, POOL_GC), 0), nb.bit_length() - 1)

    def cnt_fn(win):
        return jnp.minimum(win, t_idx + 1).astype(_f32)

    def carry_rows(ref, first, n):
        return _lane_cat([ref[first + q] for q in range(n)])

    def keep_rows(ref, first, ext):
        rows = ref.shape[1]
        for q in range(ext.shape[1] // LANES):
            ref[first + q] = ext[ext.shape[0] - rows:, q * LANES:(q + 1) * LANES]

    y = _trunk(
        x, mods, w, h2_ref, f_ref, act_ref, nb=nb, cnt_fn=cnt_fn,
        hist_pool=lambda: carry_rows(pool_carry, 0, D_POOL // LANES),
        hist_conv=lambda: carry_rows(conv_carry, 0, D_CONV // LANES),
        hist_ffn=lambda cols: carry_rows(ffn_carry, cols.start // LANES, FF_CHUNK // LANES),
        sink_pool=lambda ext: keep_rows(pool_carry, 0, ext),
        sink_conv=lambda ext: keep_rows(conv_carry, 0, ext),
        sink_ffn=lambda cols, ext: keep_rows(ffn_carry, cols.start // LANES, ext))

    for i in range(tt):
        for c in range(n_slabs):
            ys[c, pl.ds(i, nb, stride=pitch), :] = y[i * nb:(i + 1) * nb, c * LANES:(c + 1) * LANES]
    for b in range(nb):
        for c in range(n_slabs):
            y_ref[b, :, c * LANES:(c + 1) * LANES] = ys[c, b * pitch:b * pitch + tt, :]

    @pl.when(j == n_steps - 1)
    def _():
        def emit(out_ref, carry, n):
            first = carry.shape[1] // nb - n
            for b in range(nb):
                for q in range(carry.shape[0]):
                    out_ref[0, b, :, q * LANES:(q + 1) * LANES] = (
                        carry[q, pl.ds(first * nb + b, n, stride=nb), :])
        pool_rows = carry_rows(pool_carry, 0, D_POOL // LANES)
        for k in range(POOL_BUF):
            first = (POOL_HALO - POOL_BUF + k) * nb
            pool_out_ref[k] = pool_rows[first:first + nb]
        emit(conv_out_ref, conv_carry, CONV_HIST)
        emit(ffn_out_ref, ffn_carry, CONV_HIST)


def _prompt(x, mod, *weights):
    nb, seq, _ = x.shape
    assert nb == SUBLANES, "one vreg row per time step"
    tt = PROMPT_TT
    n_steps = seq // tt
    m = tt * nb
    n_slabs = D_MODEL // LANES

    def const_spec(a):
        nd = a.ndim
        return pl.BlockSpec(a.shape, lambda j: (0,) * nd, pipeline_mode=pl.Buffered(1))

    return pl.pallas_call(
        functools.partial(_prompt_kernel, nb=nb, tt=tt, n_steps=n_steps),
        grid=(n_steps,),
        in_specs=[pl.BlockSpec((nb, tt, D_MODEL), lambda j: (0, j, 0)), const_spec(mod)]
        + [const_spec(a) for a in weights],
        out_specs=[
            pl.BlockSpec((nb, tt, D_MODEL), lambda j: (0, j, 0)),
            pl.BlockSpec((POOL_BUF, nb, D_POOL), lambda j: (0, 0, 0)),
            pl.BlockSpec((1, nb, CONV_HIST, D_CONV), lambda j: (0, 0, 0, 0)),
            pl.BlockSpec((1, nb, CONV_HIST, 2 * D_FF), lambda j: (0, 0, 0, 0)),
        ],
        out_shape=[
            jax.ShapeDtypeStruct((nb, seq, D_MODEL), _f32),
            jax.ShapeDtypeStruct((POOL_BUF, nb, D_POOL), _f32),
            jax.ShapeDtypeStruct((1, nb, CONV_HIST, D_CONV), _f32),
            jax.ShapeDtypeStruct((1, nb, CONV_HIST, 2 * D_FF), _f32),
        ],
        scratch_shapes=[
            pltpu.VMEM((n_slabs, nb * PROMPT_PITCH, LANES), _f32),
            pltpu.VMEM((n_slabs, nb * PROMPT_PITCH, LANES), _f32),
            pltpu.VMEM((D_POOL // LANES, POOL_HALO * nb, LANES), _f32),
            pltpu.VMEM((D_CONV // LANES, CONV_HIST * nb, LANES), _f32),
            pltpu.VMEM((2 * D_FF // LANES, CONV_HIST * nb, LANES), _f32),
            pltpu.VMEM((m, D_MODEL), _bf16),
            pltpu.VMEM((m, D_MODEL), _f32),
            pltpu.VMEM((m, D_FF), _bf16),
        ],
        compiler_params=pltpu.CompilerParams(
            dimension_semantics=("arbitrary",),
            vmem_limit_bytes=V7X_VMEM_LIMIT_BYTES),
        name="prompt_trunk",
    )(x, mod, *weights)


def _step_rows(n_blocks, n_steps, k, c, nb):
    return pl.ds(n_steps * c + k, nb, stride=n_steps * n_blocks)


def _sample_kernel(x_ref, sp_ref, sc_ref, sf_ref, mod_ref, *rest, nb, nt):
    w = rest[:9]
    (y_ref, pool_out_ref, conv_out_ref, ffn_lo_ref, ffn_hi_ref,
     h2_ref, f_ref, act_ref) = rest[9:]
    d_blocks, c_blocks = D_MODEL // LANES, D_CONV // LANES

    def read_steps(ref, n_blocks, n_steps):
        return _row_cat([_lane_cat([ref[_step_rows(n_blocks, n_steps, k, c, nb), :]
                                    for c in range(n_blocks)]) for k in range(n_steps)])

    def write_steps(ref, n_blocks, n_steps, rows):
        for k in range(n_steps):
            for c in range(n_blocks):
                ref[_step_rows(n_blocks, n_steps, k, c, nb), :] = (
                    rows[k * nb:(k + 1) * nb, c * LANES:(c + 1) * LANES])

    def hist_ffn(cols):
        return _row_cat([sf_ref[:, k * 2 * D_FF + cols.start:k * 2 * D_FF + cols.stop]
                         for k in range(CONV_HIST)])

    def sink_pool(ext):
        first = ext.shape[0] - POOL_BUF * nb
        for k in range(POOL_BUF):
            pool_out_ref[k] = ext[first + k * nb:first + (k + 1) * nb]

    def sink_conv(ext):
        write_steps(conv_out_ref, c_blocks, CONV_HIST, ext[ext.shape[0] - CONV_HIST * nb:])

    def sink_ffn(cols, ext):
        last = ext[ext.shape[0] - CONV_HIST * nb:]
        hi_blocks = 2 * D_FF // LANES - FFN_LO_BLOCKS
        for q in range(FF_CHUNK // LANES):
            block = cols.start // LANES + q
            if block < FFN_LO_BLOCKS:
                ref, n_blocks = ffn_lo_ref, FFN_LO_BLOCKS
            else:
                ref, n_blocks, block = ffn_hi_ref, hi_blocks, block - FFN_LO_BLOCKS
            for k in range(CONV_HIST):
                ref[_step_rows(n_blocks, CONV_HIST, k, block, nb), :] = (
                    last[k * nb:(k + 1) * nb, q * LANES:(q + 1) * LANES])

    x = read_steps(x_ref, d_blocks, nt)
    mods = _fold_mods(mod_ref[...], w[0], nt)

    y = _trunk(
        x, mods, w, h2_ref, f_ref, act_ref, nb=nb,
        cnt_fn=lambda win: float(min(win, PAST_LEN + 1)),
        hist_pool=lambda: _row_cat([sp_ref[k] for k in range(POOL_BUF)]),
        hist_conv=lambda: read_steps(sc_ref, c_blocks, CONV_HIST),
        hist_ffn=hist_ffn, sink_pool=sink_pool, sink_conv=sink_conv, sink_ffn=sink_ffn)

    write_steps(y_ref, d_blocks, nt, y)


def _sample(x, sp, sc, sf, mod, *weights, nb, nt):
    rows_per_block = nb * CONV_HIST
    hi_blocks = 2 * D_FF // LANES - FFN_LO_BLOCKS
    return pl.pallas_call(
        functools.partial(_sample_kernel, nb=nb, nt=nt),
        out_shape=[
            jax.ShapeDtypeStruct(x.shape, _f32),
            jax.ShapeDtypeStruct(sp.shape, _f32),
            jax.ShapeDtypeStruct(sc.shape, _f32),
            jax.ShapeDtypeStruct((FFN_LO_BLOCKS * rows_per_block, LANES), _f32),
            jax.ShapeDtypeStruct((hi_blocks * rows_per_block, LANES), _f32),
        ],
        scratch_shapes=[
            pltpu.VMEM((nt * nb, D_MODEL), _bf16),
            pltpu.VMEM((nt * nb, D_MODEL), _f32),
            pltpu.VMEM((nt * nb, D_FF), _bf16),
        ],
        compiler_params=pltpu.CompilerParams(vmem_limit_bytes=V7X_VMEM_LIMIT_BYTES),
        name="sample_trunk",
    )(x, sp, sc, sf, mod, *weights)


def _to_block_major(a, n_steps):
    nb, _, width = a.shape
    return (a.reshape(nb, n_steps, width // LANES, LANES).transpose(0, 2, 1, 3)
            .reshape(nb * (width // LANES) * n_steps, LANES))


def _from_block_major(a, nb, n_steps):
    blocks = a.shape[0] // (nb * n_steps)
    return (a.reshape(nb, blocks, n_steps, LANES).transpose(0, 2, 1, 3)
            .reshape(nb, n_steps, blocks * LANES))


def kernel(x_prompt, x_sample, state_pool, state_conv, state_ffn, c_prompt, c_sample, w_ada, b_ada,
           g_pre_mix, g_post_mix, g_pre_ffn, g_post_ffn, w_in, pool_w, pool_scale, conv_w, w_out,
           ffn_w_up, ffn_conv_w, ffn_w_down):
    assert w_ada.shape[0] == 1, "single trunk layer"
    nbs, nts, _ = x_sample.shape
    mod_p, mod_s = _ada(c_prompt, c_sample, w_ada[0], b_ada)
    g = jnp.concatenate([g_pre_mix, g_post_mix, g_pre_ffn, g_post_ffn], axis=0)
    weights = (g, w_in[0].astype(_bf16), pool_w[0].astype(_bf16), pool_scale, conv_w[0],
               w_out[0].astype(_bf16), ffn_w_up[0].astype(_bf16), ffn_conv_w[0],
               ffn_w_down[0].astype(_bf16))

    y_p, pool_p, conv_p, ffn_p = _prompt(x_prompt, mod_p, *weights)

    y_s, pool_s, conv_s, ffn_lo, ffn_hi = _sample(
        _to_block_major(x_sample, nts),
        state_pool[0].transpose(1, 0, 2),
        _to_block_major(state_conv[0], CONV_HIST),
        state_ffn[0].reshape(nbs, CONV_HIST * 2 * D_FF),
        mod_s, *weights, nb=nbs, nt=nts)
    ffn_s = jnp.concatenate([_from_block_major(ffn_lo, nbs, CONV_HIST),
                             _from_block_major(ffn_hi, nbs, CONV_HIST)], axis=-1)

    return (y_p, _from_block_major(y_s, nbs, nts), pool_p.transpose(1, 0, 2)[None], conv_p, ffn_p,
            pool_s.transpose(1, 0, 2)[None],
            _from_block_major(conv_s, nbs, CONV_HIST)[None],
            ffn_s[None])
```

```python
import functools

import jax
import jax.numpy as jnp
from jax import lax
from jax.experimental import pallas as pl
from jax.experimental.pallas import tpu as pltpu

D_MODEL = 1024
D_POOL = 512
D_CONV = 512
POOL_WINDOWS = (2, 4, 8, 16)
POOL_GC = 128
POOL_BUF = 15
CONV_W = 3
CONV_HIST = CONV_W - 1
D_FF = 2816
D_IN_PROJ = D_POOL + 3 * D_CONV
RMS_EPS = 1e-6
PAST_LEN = 16384

V7X_VMEM_LIMIT_BYTES = 58 * 1024 * 1024
SUBLANES = 8
LANES = 128
FF_CHUNK = 256
N_FF_CHUNKS = D_FF // FF_CHUNK
FF_DOWN_GROUP = 3
FFN_LO_BLOCKS = 40
PROMPT_TT = 64
PROMPT_PITCH = PROMPT_TT + SUBLANES
WEIGHT_STAGE_ROWS = {2048: 128, 1024: 256, 5632: 64}
ADA_TN = 2048
POOL_HALO = 16

_bf16 = jnp.bfloat16
_f32 = jnp.float32


def _dot(a, b):
    return jnp.dot(a, b, preferred_element_type=_f32)


def _rms_scaled(x, scale):
    ms = jnp.mean(x * x, axis=-1, keepdims=True)
    return x * lax.rsqrt(ms + RMS_EPS) * scale


def _silu(a):
    return a * jax.nn.sigmoid(a)


def _lane_cat(parts):
    return jnp.concatenate(parts, axis=1)


def _row_cat(parts):
    return jnp.concatenate(parts, axis=0)


def _ada_kernel(cp_ref, cs_ref, w_ref, b_ref, mp_ref, ms_ref):
    c = _row_cat([cp_ref[...], cs_ref[...]])
    mod = _dot(_silu(c).astype(_bf16), w_ref[...].astype(_bf16)) + b_ref[...]
    bp = cp_ref.shape[0]
    mp_ref[...] = mod[:bp]
    ms_ref[...] = mod[bp:]


def _ada(c_prompt, c_sample, w_ada, b_ada):
    n = w_ada.shape[1]
    tn = ADA_TN
    bp, bs = c_prompt.shape[0], c_sample.shape[0]
    return pl.pallas_call(
        _ada_kernel,
        grid=(n // tn,),
        in_specs=[
            pl.BlockSpec((bp, D_MODEL), lambda i: (0, 0)),
            pl.BlockSpec((bs, D_MODEL), lambda i: (0, 0)),
            pl.BlockSpec((D_MODEL, tn), lambda i: (0, i)),
            pl.BlockSpec((1, tn), lambda i: (0, i)),
        ],
        out_specs=[
            pl.BlockSpec((bp, tn), lambda i: (0, i)),
            pl.BlockSpec((bs, tn), lambda i: (0, i)),
        ],
        out_shape=[
            jax.ShapeDtypeStruct((bp, n), _f32),
            jax.ShapeDtypeStruct((bs, n), _f32),
        ],
        compiler_params=pltpu.CompilerParams(
            dimension_semantics=("arbitrary",),
            vmem_limit_bytes=V7X_VMEM_LIMIT_BYTES),
        name="ada_mod",
    )(c_prompt, c_sample, w_ada, b_ada)


def _front_stages(x, mods, w, out, *, nb, cnt_fn, hist_pool, hist_conv, sink_pool, sink_conv):
    (_, w_in_ref, pool_w_ref, pool_scale_ref, conv_w_ref, w_out_ref) = w[:6]
    sh1, scale1, gate1, sh2, scale2, _ = mods
    m = x.shape[0]

    h = (_rms_scaled(x, scale1) + sh1).astype(_bf16)
    yield
    proj = _dot(h, w_in_ref[...])
    yield
    v_pool = proj[:, :D_POOL]
    x_conv = proj[:, D_POOL:D_POOL + D_CONV]
    gate_b = proj[:, D_POOL + D_CONV:D_POOL + 2 * D_CONV]
    gate_c = proj[:, D_POOL + 2 * D_CONV:]

    pool_ext = _row_cat([hist_pool(), v_pool])
    halo = (pool_ext.shape[0] - m) // nb
    y_pool = []
    for gi, win in enumerate(POOL_WINDOWS):
        sl = slice(gi * POOL_GC, (gi + 1) * POOL_GC)
        s = pool_ext[:, sl]
        step = 1
        while step < win:
            n = s.shape[0]
            s = s[step * nb:] + s[:n - step * nb]
            step *= 2
        first = (halo - (win - 1)) * nb
        d = s[first:first + m] / cnt_fn(win) - v_pool[:, sl]
        y_pool.append(_dot(d.astype(_bf16), pool_w_ref[gi].astype(_bf16)))
    y_pool = _lane_cat(y_pool) * pool_scale_ref[...]
    sink_pool(pool_ext)

    cx = gate_c * x_conv
    conv_ext = _row_cat([hist_conv(), cx])
    cw = conv_w_ref[...]
    conv = conv_ext[0:m] * cw[0:1] + conv_ext[nb:nb + m] * cw[1:2] + cx * cw[2:3]
    y_conv = gate_b * conv
    sink_conv(conv_ext)
    mixed = _lane_cat([y_pool, y_conv]).astype(_bf16)
    yield
    mix = _dot(mixed, w_out_ref[...])
    yield
    x1 = x + _rms_scaled(mix, gate1)
    out["x1"] = x1
    out["h2"] = (_rms_scaled(x1, scale2) + sh2).astype(_bf16)
    yield


def _ffn_stages(w, h2_ref, f_ref, act_ref, *, nb, hist_ffn, sink_ffn):
    w_up_ref, ffn_conv_w_ref, w_down_ref = w[6:]
    m = h2_ref.shape[0]

    def ffn_cols(c):
        return [slice(base + c * FF_CHUNK, base + (c + 1) * FF_CHUNK) for base in (0, D_FF)]

    def up_proj(c):
        return [_dot(h2_ref[...], w_up_ref[:, cols]) for cols in ffn_cols(c)]

    ups = up_proj(0)
    group_start = 0
    for c in range(N_FF_CHUNKS):
        ups_next = up_proj(c + 1) if c + 1 < N_FF_CHUNKS else None
        halves = []
        for cols, up in zip(ffn_cols(c), ups):
            up_ext = _row_cat([hist_ffn(cols), up])
            fw = ffn_conv_w_ref[:, cols]
            halves.append(up_ext[0:m] * fw[0:1] + up_ext[nb:nb + m] * fw[1:2] + up * fw[2:3])
            sink_ffn(cols, up_ext)
        act_ref[:, c * FF_CHUNK:(c + 1) * FF_CHUNK] = (_silu(halves[0]) * halves[1]).astype(_bf16)
        if (c + 1) % FF_DOWN_GROUP == 0 or c + 1 == N_FF_CHUNKS:
            rows = slice(group_start * FF_CHUNK, (c + 1) * FF_CHUNK)
            contrib = _dot(act_ref[:, rows], w_down_ref[rows, :])
            if group_start == 0:
                f_ref[...] = contrib
            else:
                f_ref[...] += contrib
            group_start = c + 1
        ups = ups_next
        yield


def _finish(x1, f_ref, gate2):
    return x1 + _rms_scaled(f_ref[...], gate2)


def _fold_mods(mod, g_ref, n_steps):
    g = g_ref[...]
    sh1, sc1, gt1, sh2, sc2, gt2 = [mod[:, i * D_MODEL:(i + 1) * D_MODEL] for i in range(6)]
    folded = [sh1, g[0:1] * (1.0 + sc1), gt1 * g[1:2], sh2, g[2:3] * (1.0 + sc2), gt2 * g[3:4]]
    return [_row_cat([a] * n_steps) for a in folded]


def _trunk(x, mods, w, h2_ref, f_ref, act_ref, *, nb, cnt_fn, hist_pool, hist_conv, hist_ffn,
           sink_pool, sink_conv, sink_ffn):
    out = {}
    for _ in _front_stages(x, mods, w, out, nb=nb, cnt_fn=cnt_fn, hist_pool=hist_pool,
                           hist_conv=hist_conv, sink_pool=sink_pool, sink_conv=sink_conv):
        pass
    h2_ref[...] = out["h2"]
    for _ in _ffn_stages(w, h2_ref, f_ref, act_ref, nb=nb, hist_ffn=hist_ffn, sink_ffn=sink_ffn):
        pass
    return _finish(out["x1"], f_ref, mods[5])


def _stream_cast(src, dst, stage, sems):
    rows = stage.shape[1]
    n = src.shape[0] // rows

    def chunk(i, slot):
        return pltpu.make_async_copy(src.at[pl.ds(i * rows, rows), :], stage.at[slot], sems.at[slot])

    chunk(0, 0).start()

    def body(i, carry):
        slot = lax.rem(i, 2)

        @pl.when(i + 1 < n)
        def _():
            chunk(i + 1, 1 - slot).start()

        chunk(i, slot).wait()
        dst[pl.ds(pl.multiple_of(i * rows, rows), rows), :] = stage[slot].astype(_bf16)
        return carry

    lax.fori_loop(0, n, body, 0)


def _prompt_kernel(x_ref, mod_ref, *rest, nb, tt, n_steps):
    (g_ref, w_in_hbm, pool_w_ref, pool_scale_ref, conv_w_ref, w_out_hbm, w_up_hbm, ffn_conv_w_ref,
     w_down_hbm) = rest[:9]
    y_ref, pool_out_ref, conv_out_ref, ffn_out_ref = rest[9:13]
    bf_out = rest[13:17]
    (xs, ys, pool_carry, conv_carry, ffn_carry, h2_ref, f_ref, act_ref,
     w_in_bf, w_out_bf, w_up_bf, w_down_bf, stage_in, stage_sq, stage_up,
     stage_sems, out_sems) = rest[17:]
    bf_vmem = (w_in_bf, w_out_bf, w_up_bf, w_down_bf)
    w = (g_ref, w_in_bf, pool_w_ref, pool_scale_ref, conv_w_ref, w_out_bf, w_up_bf, ffn_conv_w_ref,
         w_down_bf)
    j = pl.program_id(0)

    def bf_writeback(k):
        return pltpu.make_async_copy(bf_vmem[k], bf_out[k], out_sems.at[k])

    pitch = xs.shape[1] // nb
    n_slabs = D_MODEL // LANES
    m = tt * nb

    @pl.when(j == 0)
    def _():
        pool_carry[...] = jnp.zeros_like(pool_carry)
        conv_carry[...] = jnp.zeros_like(conv_carry)
        ffn_carry[...] = jnp.zeros_like(ffn_carry)
        _stream_cast(w_in_hbm.at[0], w_in_bf, stage_in, stage_sems.at[0])
        _stream_cast(w_out_hbm.at[0], w_out_bf, stage_sq, stage_sems.at[1])
        _stream_cast(w_up_hbm.at[0], w_up_bf, stage_up, stage_sems.at[2])
        _stream_cast(w_down_hbm.at[0], w_down_bf, stage_sq, stage_sems.at[1])
        for k in range(len(bf_vmem)):
            bf_writeback(k).start()

    for b in range(nb):
        for c in range(n_slabs):
            xs[c, b * pitch:b * pitch + tt, :] = x_ref[b, :, c * LANES:(c + 1) * LANES]
    x = _row_cat([_lane_cat([xs[c, pl.ds(i, nb, stride=pitch), :] for c in range(n_slabs)])
                  for i in range(tt)])

    mods = _fold_mods(mod_ref[...], w[0], tt)

    t_idx = j * tt + lax.shift_right_logical(
        lax.broadcasted_iota(jnp.int32, (m, POOL_GC), 0), nb.bit_length() - 1)

    def cnt_fn(win):
        return jnp.minimum(win, t_idx + 1).astype(_f32)

    def carry_rows(ref, first, n):
        return _lane_cat([ref[first + q] for q in range(n)])

    def keep_rows(ref, first, ext):
        rows = ref.shape[1]
        for q in range(ext.shape[1] // LANES):
            ref[first + q] = ext[ext.shape[0] - rows:, q * LANES:(q + 1) * LANES]

    y = _trunk(
        x, mods, w, h2_ref, f_ref, act_ref, nb=nb, cnt_fn=cnt_fn,
        hist_pool=lambda: carry_rows(pool_carry, 0, D_POOL // LANES),
        hist_conv=lambda: carry_rows(conv_carry, 0, D_CONV // LANES),
        hist_ffn=lambda cols: carry_rows(ffn_carry, cols.start // LANES, FF_CHUNK // LANES),
        sink_pool=lambda ext: keep_rows(pool_carry, 0, ext),
        sink_conv=lambda ext: keep_rows(conv_carry, 0, ext),
        sink_ffn=lambda cols, ext: keep_rows(ffn_carry, cols.start // LANES, ext))

    for i in range(tt):
        for c in range(n_slabs):
            ys[c, pl.ds(i, nb, stride=pitch), :] = y[i * nb:(i + 1) * nb, c * LANES:(c + 1) * LANES]
    for b in range(nb):
        for c in range(n_slabs):
            y_ref[b, :, c * LANES:(c + 1) * LANES] = ys[c, b * pitch:b * pitch + tt, :]

    @pl.when(j == n_steps - 1)
    def _():
        def emit(out_ref, carry, n):
            first = carry.shape[1] // nb - n
            for b in range(nb):
                for q in range(carry.shape[0]):
                    out_ref[0, b, :, q * LANES:(q + 1) * LANES] = (
                        carry[q, pl.ds(first * nb + b, n, stride=nb), :])
        pool_rows = carry_rows(pool_carry, 0, D_POOL // LANES)
        for k in range(POOL_BUF):
            first = (POOL_HALO - POOL_BUF + k) * nb
            pool_out_ref[k] = pool_rows[first:first + nb]
        emit(conv_out_ref, conv_carry, CONV_HIST)
        emit(ffn_out_ref, ffn_carry, CONV_HIST)
        for k in range(len(bf_vmem)):
            bf_writeback(k).wait()


def _prompt(x, mod, g, w_in, pool_w, pool_scale, conv_w, w_out, w_up, ffn_conv_w, w_down):
    nb, seq, _ = x.shape
    assert nb == SUBLANES, "one vreg row per time step"
    tt = PROMPT_TT
    n_steps = seq // tt
    m = tt * nb
    n_slabs = D_MODEL // LANES
    big = (w_in, w_out, w_up, w_down)
    stage_widths = (D_IN_PROJ, D_MODEL, 2 * D_FF)
    in_hbm = pl.BlockSpec(memory_space=pl.ANY)

    def const_spec(a):
        nd = a.ndim
        return pl.BlockSpec(a.shape, lambda j: (0,) * nd, pipeline_mode=pl.Buffered(1))

    return pl.pallas_call(
        functools.partial(_prompt_kernel, nb=nb, tt=tt, n_steps=n_steps),
        grid=(n_steps,),
        in_specs=[pl.BlockSpec((nb, tt, D_MODEL), lambda j: (0, j, 0)), const_spec(mod),
                  const_spec(g), in_hbm, const_spec(pool_w), const_spec(pool_scale),
                  const_spec(conv_w), in_hbm, in_hbm, const_spec(ffn_conv_w), in_hbm],
        out_specs=[
            pl.BlockSpec((nb, tt, D_MODEL), lambda j: (0, j, 0)),
            pl.BlockSpec((POOL_BUF, nb, D_POOL), lambda j: (0, 0, 0)),
            pl.BlockSpec((1, nb, CONV_HIST, D_CONV), lambda j: (0, 0, 0, 0)),
            pl.BlockSpec((1, nb, CONV_HIST, 2 * D_FF), lambda j: (0, 0, 0, 0)),
        ] + [pl.BlockSpec(memory_space=pl.ANY) for _ in big],
        out_shape=[
            jax.ShapeDtypeStruct((nb, seq, D_MODEL), _f32),
            jax.ShapeDtypeStruct((POOL_BUF, nb, D_POOL), _f32),
            jax.ShapeDtypeStruct((1, nb, CONV_HIST, D_CONV), _f32),
            jax.ShapeDtypeStruct((1, nb, CONV_HIST, 2 * D_FF), _f32),
        ] + [jax.ShapeDtypeStruct(a.shape[1:], _bf16) for a in big],
        scratch_shapes=[
            pltpu.VMEM((n_slabs, nb * PROMPT_PITCH, LANES), _f32),
            pltpu.VMEM((n_slabs, nb * PROMPT_PITCH, LANES), _f32),
            pltpu.VMEM((D_POOL // LANES, POOL_HALO * nb, LANES), _f32),
            pltpu.VMEM((D_CONV // LANES, CONV_HIST * nb, LANES), _f32),
            pltpu.VMEM((2 * D_FF // LANES, CONV_HIST * nb, LANES), _f32),
            pltpu.VMEM((m, D_MODEL), _bf16),
            pltpu.VMEM((m, D_MODEL), _f32),
            pltpu.VMEM((m, D_FF), _bf16),
        ] + [pltpu.VMEM(a.shape[1:], _bf16) for a in big]
        + [pltpu.VMEM((2, WEIGHT_STAGE_ROWS[c], c), _f32) for c in stage_widths]
        + [pltpu.SemaphoreType.DMA((len(stage_widths), 2)), pltpu.SemaphoreType.DMA((len(big),))],
        compiler_params=pltpu.CompilerParams(
            dimension_semantics=("arbitrary",),
            vmem_limit_bytes=V7X_VMEM_LIMIT_BYTES),
        name="prompt_trunk",
    )(x, mod, g, w_in, pool_w, pool_scale, conv_w, w_out, w_up, ffn_conv_w, w_down)


def _step_rows(n_blocks, n_steps, k, c, nb):
    return pl.ds(n_steps * c + k, nb, stride=n_steps * n_blocks)


def _sample_kernel(x_ref, sp_ref, sc_ref, sf_ref, mod_ref, *rest, nb, nt):
    w = rest[:9]
    (y_ref, pool_out_ref, conv_out_ref, ffn_lo_ref, ffn_hi_ref,
     h2_ref, f_ref, act_ref) = rest[9:]
    d_blocks, c_blocks = D_MODEL // LANES, D_CONV // LANES

    def read_steps(ref, n_blocks, n_steps):
        return _row_cat([_lane_cat([ref[_step_rows(n_blocks, n_steps, k, c, nb), :]
                                    for c in range(n_blocks)]) for k in range(n_steps)])

    def write_steps(ref, n_blocks, n_steps, rows):
        for k in range(n_steps):
            for c in range(n_blocks):
                ref[_step_rows(n_blocks, n_steps, k, c, nb), :] = (
                    rows[k * nb:(k + 1) * nb, c * LANES:(c + 1) * LANES])

    def hist_ffn(cols):
        return _row_cat([sf_ref[:, k * 2 * D_FF + cols.start:k * 2 * D_FF + cols.stop]
                         for k in range(CONV_HIST)])

    def sink_pool(ext):
        first = ext.shape[0] - POOL_BUF * nb
        for k in range(POOL_BUF):
            pool_out_ref[k] = ext[first + k * nb:first + (k + 1) * nb]

    def sink_conv(ext):
        write_steps(conv_out_ref, c_blocks, CONV_HIST, ext[ext.shape[0] - CONV_HIST * nb:])

    def sink_ffn(cols, ext):
        last = ext[ext.shape[0] - CONV_HIST * nb:]
        hi_blocks = 2 * D_FF // LANES - FFN_LO_BLOCKS
        for q in range(FF_CHUNK // LANES):
            block = cols.start // LANES + q
            if block < FFN_LO_BLOCKS:
                ref, n_blocks = ffn_lo_ref, FFN_LO_BLOCKS
            else:
                ref, n_blocks, block = ffn_hi_ref, hi_blocks, block - FFN_LO_BLOCKS
            for k in range(CONV_HIST):
                ref[_step_rows(n_blocks, CONV_HIST, k, block, nb), :] = (
                    last[k * nb:(k + 1) * nb, q * LANES:(q + 1) * LANES])

    x = read_steps(x_ref, d_blocks, nt)
    mods = _fold_mods(mod_ref[...], w[0], nt)

    y = _trunk(
        x, mods, w, h2_ref, f_ref, act_ref, nb=nb,
        cnt_fn=lambda win: float(min(win, PAST_LEN + 1)),
        hist_pool=lambda: _row_cat([sp_ref[k] for k in range(POOL_BUF)]),
        hist_conv=lambda: read_steps(sc_ref, c_blocks, CONV_HIST),
        hist_ffn=hist_ffn, sink_pool=sink_pool, sink_conv=sink_conv, sink_ffn=sink_ffn)

    write_steps(y_ref, d_blocks, nt, y)


def _sample(x, sp, sc, sf, mod, *weights, nb, nt):
    rows_per_block = nb * CONV_HIST
    hi_blocks = 2 * D_FF // LANES - FFN_LO_BLOCKS
    return pl.pallas_call(
        functools.partial(_sample_kernel, nb=nb, nt=nt),
        out_shape=[
            jax.ShapeDtypeStruct(x.shape, _f32),
            jax.ShapeDtypeStruct(sp.shape, _f32),
            jax.ShapeDtypeStruct(sc.shape, _f32),
            jax.ShapeDtypeStruct((FFN_LO_BLOCKS * rows_per_block, LANES), _f32),
            jax.ShapeDtypeStruct((hi_blocks * rows_per_block, LANES), _f32),
        ],
        scratch_shapes=[
            pltpu.VMEM((nt * nb, D_MODEL), _bf16),
            pltpu.VMEM((nt * nb, D_MODEL), _f32),
            pltpu.VMEM((nt * nb, D_FF), _bf16),
        ],
        compiler_params=pltpu.CompilerParams(vmem_limit_bytes=V7X_VMEM_LIMIT_BYTES),
        name="sample_trunk",
    )(x, sp, sc, sf, mod, *weights)


def _to_block_major(a, n_steps):
    nb, _, width = a.shape
    return (a.reshape(nb, n_steps, width // LANES, LANES).transpose(0, 2, 1, 3)
            .reshape(nb * (width // LANES) * n_steps, LANES))


def _from_block_major(a, nb, n_steps):
    blocks = a.shape[0] // (nb * n_steps)
    return (a.reshape(nb, blocks, n_steps, LANES).transpose(0, 2, 1, 3)
            .reshape(nb, n_steps, blocks * LANES))


def kernel(x_prompt, x_sample, state_pool, state_conv, state_ffn, c_prompt, c_sample, w_ada, b_ada,
           g_pre_mix, g_post_mix, g_pre_ffn, g_post_ffn, w_in, pool_w, pool_scale, conv_w, w_out,
           ffn_w_up, ffn_conv_w, ffn_w_down):
    assert w_ada.shape[0] == 1, "single trunk layer"
    nbs, nts, _ = x_sample.shape
    mod_p, mod_s = _ada(c_prompt, c_sample, w_ada[0], b_ada)
    g = jnp.concatenate([g_pre_mix, g_post_mix, g_pre_ffn, g_post_ffn], axis=0)
    y_p, pool_p, conv_p, ffn_p, w_in_bf, w_out_bf, w_up_bf, w_down_bf = _prompt(
        x_prompt, mod_p, g, w_in, pool_w[0], pool_scale, conv_w[0], w_out, ffn_w_up, ffn_conv_w[0],
        ffn_w_down)
    weights = (g, w_in_bf, pool_w[0], pool_scale, conv_w[0], w_out_bf, w_up_bf, ffn_conv_w[0],
               w_down_bf)

    y_s, pool_s, conv_s, ffn_lo, ffn_hi = _sample(
        _to_block_major(x_sample, nts),
        state_pool[0].transpose(1, 0, 2),
        _to_block_major(state_conv[0], CONV_HIST),
        state_ffn[0].reshape(nbs, CONV_HIST * 2 * D_FF),
        mod_s, *weights, nb=nbs, nt=nts)
    ffn_s = jnp.concatenate([_from_block_major(ffn_lo, nbs, CONV_HIST),
                             _from_block_major(ffn_hi, nbs, CONV_HIST)], axis=-1)

    return (y_p, _from_block_major(y_s, nbs, nts), pool_p.transpose(1, 0, 2)[None], conv_p, ffn_p,
            pool_s.transpose(1, 0, 2)[None],
            _from_block_major(conv_s, nbs, CONV_HIST)[None],
            ffn_s[None])
```

```python
import functools

import jax
import jax.numpy as jnp
from jax import lax
from jax.experimental import pallas as pl
from jax.experimental.pallas import tpu as pltpu

D_MODEL = 1024
D_POOL = 512
D_CONV = 512
POOL_WINDOWS = (2, 4, 8, 16)
POOL_GC = 128
POOL_BUF = 15
CONV_W = 3
CONV_HIST = CONV_W - 1
D_FF = 2816
D_IN_PROJ = D_POOL + 3 * D_CONV
RMS_EPS = 1e-6
PAST_LEN = 16384

V7X_VMEM_LIMIT_BYTES = 58 * 1024 * 1024
SUBLANES = 8
LANES = 128
FF_CHUNK = 256
N_FF_CHUNKS = D_FF // FF_CHUNK
FF_DOWN_GROUP = 3
FFN_LO_BLOCKS = 40
PROMPT_TT = 64
PROMPT_PITCH = PROMPT_TT + SUBLANES
WEIGHT_STAGE_ROWS = {2048: 128, 1024: 256, 5632: 64}
WEIGHT_STAGE_SLOTS = 4
ADA_TN = 2048
POOL_HALO = 16

_bf16 = jnp.bfloat16
_f32 = jnp.float32


def _dot(a, b):
    return jnp.dot(a, b, preferred_element_type=_f32)


def _rms_scaled(x, scale):
    ms = jnp.mean(x * x, axis=-1, keepdims=True)
    return x * lax.rsqrt(ms + RMS_EPS) * scale


def _silu(a):
    return a * jax.nn.sigmoid(a)


def _lane_cat(parts):
    return jnp.concatenate(parts, axis=1)


def _row_cat(parts):
    return jnp.concatenate(parts, axis=0)


def _ada_kernel(cp_ref, cs_ref, w_ref, b_ref, mp_ref, ms_ref):
    c = _row_cat([cp_ref[...], cs_ref[...]])
    mod = _dot(_silu(c).astype(_bf16), w_ref[...].astype(_bf16)) + b_ref[...]
    bp = cp_ref.shape[0]
    mp_ref[...] = mod[:bp]
    ms_ref[...] = mod[bp:]


def _ada(c_prompt, c_sample, w_ada, b_ada):
    n = w_ada.shape[1]
    tn = ADA_TN
    bp, bs = c_prompt.shape[0], c_sample.shape[0]
    return pl.pallas_call(
        _ada_kernel,
        grid=(n // tn,),
        in_specs=[
            pl.BlockSpec((bp, D_MODEL), lambda i: (0, 0)),
            pl.BlockSpec((bs, D_MODEL), lambda i: (0, 0)),
            pl.BlockSpec((D_MODEL, tn), lambda i: (0, i)),
            pl.BlockSpec((1, tn), lambda i: (0, i)),
        ],
        out_specs=[
            pl.BlockSpec((bp, tn), lambda i: (0, i)),
            pl.BlockSpec((bs, tn), lambda i: (0, i)),
        ],
        out_shape=[
            jax.ShapeDtypeStruct((bp, n), _f32),
            jax.ShapeDtypeStruct((bs, n), _f32),
        ],
        compiler_params=pltpu.CompilerParams(
            dimension_semantics=("arbitrary",),
            vmem_limit_bytes=V7X_VMEM_LIMIT_BYTES),
        name="ada_mod",
    )(c_prompt, c_sample, w_ada, b_ada)


def _front_stages(x, mods, w, out, *, nb, cnt_fn, hist_pool, hist_conv, sink_pool, sink_conv):
    (_, w_in_ref, pool_w_ref, pool_scale_ref, conv_w_ref, w_out_ref) = w[:6]
    sh1, scale1, gate1, sh2, scale2, _ = mods
    m = x.shape[0]

    h = (_rms_scaled(x, scale1) + sh1).astype(_bf16)
    yield
    proj = _dot(h, w_in_ref[...])
    yield
    v_pool = proj[:, :D_POOL]
    x_conv = proj[:, D_POOL:D_POOL + D_CONV]
    gate_b = proj[:, D_POOL + D_CONV:D_POOL + 2 * D_CONV]
    gate_c = proj[:, D_POOL + 2 * D_CONV:]

    pool_ext = _row_cat([hist_pool(), v_pool])
    halo = (pool_ext.shape[0] - m) // nb
    y_pool = []
    for gi, win in enumerate(POOL_WINDOWS):
        sl = slice(gi * POOL_GC, (gi + 1) * POOL_GC)
        s = pool_ext[:, sl]
        step = 1
        while step < win:
            n = s.shape[0]
            s = s[step * nb:] + s[:n - step * nb]
            step *= 2
        first = (halo - (win - 1)) * nb
        d = s[first:first + m] / cnt_fn(win) - v_pool[:, sl]
        y_pool.append(_dot(d.astype(_bf16), pool_w_ref[gi].astype(_bf16)))
    y_pool = _lane_cat(y_pool) * pool_scale_ref[...]
    sink_pool(pool_ext)

    cx = gate_c * x_conv
    conv_ext = _row_cat([hist_conv(), cx])
    cw = conv_w_ref[...]
    conv = conv_ext[0:m] * cw[0:1] + conv_ext[nb:nb + m] * cw[1:2] + cx * cw[2:3]
    y_conv = gate_b * conv
    sink_conv(conv_ext)
    mixed = _lane_cat([y_pool, y_conv]).astype(_bf16)
    yield
    mix = _dot(mixed, w_out_ref[...])
    yield
    x1 = x + _rms_scaled(mix, gate1)
    out["x1"] = x1
    out["h2"] = (_rms_scaled(x1, scale2) + sh2).astype(_bf16)
    yield


def _ffn_stages(w, h2_ref, f_ref, act_ref, *, nb, hist_ffn, sink_ffn):
    w_up_ref, ffn_conv_w_ref, w_down_ref = w[6:]
    m = h2_ref.shape[0]

    def ffn_cols(c):
        return [slice(base + c * FF_CHUNK, base + (c + 1) * FF_CHUNK) for base in (0, D_FF)]

    def up_proj(c):
        return [_dot(h2_ref[...], w_up_ref[:, cols]) for cols in ffn_cols(c)]

    ups = up_proj(0)
    group_start = 0
    for c in range(N_FF_CHUNKS):
        ups_next = up_proj(c + 1) if c + 1 < N_FF_CHUNKS else None
        halves = []
        for cols, up in zip(ffn_cols(c), ups):
            up_ext = _row_cat([hist_ffn(cols), up])
            fw = ffn_conv_w_ref[:, cols]
            halves.append(up_ext[0:m] * fw[0:1] + up_ext[nb:nb + m] * fw[1:2] + up * fw[2:3])
            sink_ffn(cols, up_ext)
        act_ref[:, c * FF_CHUNK:(c + 1) * FF_CHUNK] = (_silu(halves[0]) * halves[1]).astype(_bf16)
        if (c + 1) % FF_DOWN_GROUP == 0 or c + 1 == N_FF_CHUNKS:
            rows = slice(group_start * FF_CHUNK, (c + 1) * FF_CHUNK)
            contrib = _dot(act_ref[:, rows], w_down_ref[rows, :])
            if group_start == 0:
                f_ref[...] = contrib
            else:
                f_ref[...] += contrib
            group_start = c + 1
        ups = ups_next
        yield


def _finish(x1, f_ref, gate2):
    return x1 + _rms_scaled(f_ref[...], gate2)


def _fold_mods(mod, g_ref, n_steps):
    g = g_ref[...]
    sh1, sc1, gt1, sh2, sc2, gt2 = [mod[:, i * D_MODEL:(i + 1) * D_MODEL] for i in range(6)]
    folded = [sh1, g[0:1] * (1.0 + sc1), gt1 * g[1:2], sh2, g[2:3] * (1.0 + sc2), gt2 * g[3:4]]
    return [_row_cat([a] * n_steps) for a in folded]


def _trunk(x, mods, w, h2_ref, f_ref, act_ref, *, nb, cnt_fn, hist_pool, hist_conv, hist_ffn,
           sink_pool, sink_conv, sink_ffn):
    out = {}
    for _ in _front_stages(x, mods, w, out, nb=nb, cnt_fn=cnt_fn, hist_pool=hist_pool,
                           hist_conv=hist_conv, sink_pool=sink_pool, sink_conv=sink_conv):
        pass
    h2_ref[...] = out["h2"]
    for _ in _ffn_stages(w, h2_ref, f_ref, act_ref, nb=nb, hist_ffn=hist_ffn, sink_ffn=sink_ffn):
        pass
    return _finish(out["x1"], f_ref, mods[5])


def _stream_cast(src, dst, stage, sems):
    slots, rows = stage.shape[0], stage.shape[1]
    n = src.shape[0] // rows
    ahead = slots - 1

    def chunk(i, slot):
        return pltpu.make_async_copy(src.at[pl.ds(i * rows, rows), :], stage.at[slot], sems.at[slot])

    for i in range(min(ahead, n)):
        chunk(i, i).start()

    def body(i, carry):
        slot = lax.rem(i, slots)

        @pl.when(i + ahead < n)
        def _():
            chunk(i + ahead, lax.rem(i + ahead, slots)).start()

        chunk(i, slot).wait()
        dst[pl.ds(pl.multiple_of(i * rows, rows), rows), :] = stage[slot].astype(_bf16)
        return carry

    lax.fori_loop(0, n, body, 0)


def _prompt_kernel(x_ref, mod_ref, *rest, nb, tt, n_steps):
    (g_ref, w_in_hbm, pool_w_ref, pool_scale_ref, conv_w_ref, w_out_hbm, w_up_hbm, ffn_conv_w_ref,
     w_down_hbm) = rest[:9]
    y_ref, pool_out_ref, conv_out_ref, ffn_out_ref = rest[9:13]
    bf_out = rest[13:17]
    (xs, pool_carry, conv_carry, ffn_carry, h2_ref, f_ref, act_ref,
     w_in_bf, w_out_bf, w_up_bf, w_down_bf, stage_in, stage_sq, stage_up,
     stage_sems, out_sems) = rest[17:]
    bf_vmem = (w_in_bf, w_out_bf, w_up_bf, w_down_bf)
    w = (g_ref, w_in_bf, pool_w_ref, pool_scale_ref, conv_w_ref, w_out_bf, w_up_bf, ffn_conv_w_ref,
         w_down_bf)
    j = pl.program_id(0)

    def bf_writeback(k):
        return pltpu.make_async_copy(bf_vmem[k], bf_out[k], out_sems.at[k])

    pitch = xs.shape[1] // nb
    n_slabs = D_MODEL // LANES
    m = tt * nb

    @pl.when(j == 0)
    def _():
        pool_carry[...] = jnp.zeros_like(pool_carry)
        conv_carry[...] = jnp.zeros_like(conv_carry)
        ffn_carry[...] = jnp.zeros_like(ffn_carry)
        _stream_cast(w_in_hbm.at[0], w_in_bf, stage_in, stage_sems.at[0])
        _stream_cast(w_out_hbm.at[0], w_out_bf, stage_sq, stage_sems.at[1])
        _stream_cast(w_up_hbm.at[0], w_up_bf, stage_up, stage_sems.at[2])
        _stream_cast(w_down_hbm.at[0], w_down_bf, stage_sq, stage_sems.at[1])
        for k in range(len(bf_vmem)):
            bf_writeback(k).start()

    for b in range(nb):
        for c in range(n_slabs):
            xs[c, b * pitch:b * pitch + tt, :] = x_ref[b, :, c * LANES:(c + 1) * LANES]
    x = _row_cat([_lane_cat([xs[c, pl.ds(i, nb, stride=pitch), :] for c in range(n_slabs)])
                  for i in range(tt)])

    mods = _fold_mods(mod_ref[...], w[0], tt)

    t_idx = j * tt + lax.shift_right_logical(
        lax.broadcasted_iota(jnp.int32, (m, POOL_GC), 0), nb.bit_length() - 1)

    def cnt_fn(win):
        return jnp.minimum(win, t_idx + 1).astype(_f32)

    def carry_rows(ref, first, n):
        return _lane_cat([ref[first + q] for q in range(n)])

    def keep_rows(ref, first, ext):
        rows = ref.shape[1]
        for q in range(ext.shape[1] // LANES):
            ref[first + q] = ext[ext.shape[0] - rows:, q * LANES:(q + 1) * LANES]

    y = _trunk(
        x, mods, w, h2_ref, f_ref, act_ref, nb=nb, cnt_fn=cnt_fn,
        hist_pool=lambda: carry_rows(pool_carry, 0, D_POOL // LANES),
        hist_conv=lambda: carry_rows(conv_carry, 0, D_CONV // LANES),
        hist_ffn=lambda cols: carry_rows(ffn_carry, cols.start // LANES, FF_CHUNK // LANES),
        sink_pool=lambda ext: keep_rows(pool_carry, 0, ext),
        sink_conv=lambda ext: keep_rows(conv_carry, 0, ext),
        sink_ffn=lambda cols, ext: keep_rows(ffn_carry, cols.start // LANES, ext))

    for i in range(tt):
        for c in range(n_slabs):
            xs[c, pl.ds(i, nb, stride=pitch), :] = y[i * nb:(i + 1) * nb, c * LANES:(c + 1) * LANES]
    for b in range(nb):
        for c in range(n_slabs):
            y_ref[b, :, c * LANES:(c + 1) * LANES] = xs[c, b * pitch:b * pitch + tt, :]

    @pl.when(j == n_steps - 1)
    def _():
        def emit(out_ref, carry, n):
            first = carry.shape[1] // nb - n
            for b in range(nb):
                for q in range(carry.shape[0]):
                    out_ref[0, b, :, q * LANES:(q + 1) * LANES] = (
                        carry[q, pl.ds(first * nb + b, n, stride=nb), :])
        pool_rows = carry_rows(pool_carry, 0, D_POOL // LANES)
        for k in range(POOL_BUF):
            first = (POOL_HALO - POOL_BUF + k) * nb
            pool_out_ref[k] = pool_rows[first:first + nb]
        emit(conv_out_ref, conv_carry, CONV_HIST)
        emit(ffn_out_ref, ffn_carry, CONV_HIST)
        for k in range(len(bf_vmem)):
            bf_writeback(k).wait()


def _prompt(x, mod, g, w_in, pool_w, pool_scale, conv_w, w_out, w_up, ffn_conv_w, w_down):
    nb, seq, _ = x.shape
    assert nb == SUBLANES, "one vreg row per time step"
    tt = PROMPT_TT
    n_steps = seq // tt
    m = tt * nb
    n_slabs = D_MODEL // LANES
    big = (w_in, w_out, w_up, w_down)
    stage_widths = (D_IN_PROJ, D_MODEL, 2 * D_FF)
    in_hbm = pl.BlockSpec(memory_space=pl.ANY)

    def const_spec(a):
        nd = a.ndim
        return pl.BlockSpec(a.shape, lambda j: (0,) * nd, pipeline_mode=pl.Buffered(1))

    return pl.pallas_call(
        functools.partial(_prompt_kernel, nb=nb, tt=tt, n_steps=n_steps),
        grid=(n_steps,),
        in_specs=[pl.BlockSpec((nb, tt, D_MODEL), lambda j: (0, j, 0)), const_spec(mod),
                  const_spec(g), in_hbm, const_spec(pool_w), const_spec(pool_scale),
                  const_spec(conv_w), in_hbm, in_hbm, const_spec(ffn_conv_w), in_hbm],
        out_specs=[
            pl.BlockSpec((nb, tt, D_MODEL), lambda j: (0, j, 0)),
            pl.BlockSpec((POOL_BUF, nb, D_POOL), lambda j: (0, 0, 0)),
            pl.BlockSpec((1, nb, CONV_HIST, D_CONV), lambda j: (0, 0, 0, 0)),
            pl.BlockSpec((1, nb, CONV_HIST, 2 * D_FF), lambda j: (0, 0, 0, 0)),
        ] + [pl.BlockSpec(memory_space=pl.ANY) for _ in big],
        out_shape=[
            jax.ShapeDtypeStruct((nb, seq, D_MODEL), _f32),
            jax.ShapeDtypeStruct((POOL_BUF, nb, D_POOL), _f32),
            jax.ShapeDtypeStruct((1, nb, CONV_HIST, D_CONV), _f32),
            jax.ShapeDtypeStruct((1, nb, CONV_HIST, 2 * D_FF), _f32),
        ] + [jax.ShapeDtypeStruct(a.shape[1:], _bf16) for a in big],
        scratch_shapes=[
            pltpu.VMEM((n_slabs, nb * PROMPT_PITCH, LANES), _f32),
            pltpu.VMEM((D_POOL // LANES, POOL_HALO * nb, LANES), _f32),
            pltpu.VMEM((D_CONV // LANES, CONV_HIST * nb, LANES), _f32),
            pltpu.VMEM((2 * D_FF // LANES, CONV_HIST * nb, LANES), _f32),
            pltpu.VMEM((m, D_MODEL), _bf16),
            pltpu.VMEM((m, D_MODEL), _f32),
            pltpu.VMEM((m, D_FF), _bf16),
        ] + [pltpu.VMEM(a.shape[1:], _bf16) for a in big]
        + [pltpu.VMEM((WEIGHT_STAGE_SLOTS, WEIGHT_STAGE_ROWS[c], c), _f32) for c in stage_widths]
        + [pltpu.SemaphoreType.DMA((len(stage_widths), WEIGHT_STAGE_SLOTS)),
           pltpu.SemaphoreType.DMA((len(big),))],
        compiler_params=pltpu.CompilerParams(
            dimension_semantics=("arbitrary",),
            vmem_limit_bytes=V7X_VMEM_LIMIT_BYTES),
        name="prompt_trunk",
    )(x, mod, g, w_in, pool_w, pool_scale, conv_w, w_out, w_up, ffn_conv_w, w_down)


def _step_rows(n_blocks, n_steps, k, c, nb):
    return pl.ds(n_steps * c + k, nb, stride=n_steps * n_blocks)


def _sample_kernel(x_ref, sp_ref, sc_ref, sf_ref, mod_ref, *rest, nb, nt):
    w = rest[:9]
    (y_ref, pool_out_ref, conv_out_ref, ffn_lo_ref, ffn_hi_ref,
     h2_ref, f_ref, act_ref) = rest[9:]
    d_blocks, c_blocks = D_MODEL // LANES, D_CONV // LANES

    def read_steps(ref, n_blocks, n_steps):
        return _row_cat([_lane_cat([ref[_step_rows(n_blocks, n_steps, k, c, nb), :]
                                    for c in range(n_blocks)]) for k in range(n_steps)])

    def write_steps(ref, n_blocks, n_steps, rows):
        for k in range(n_steps):
            for c in range(n_blocks):
                ref[_step_rows(n_blocks, n_steps, k, c, nb), :] = (
                    rows[k * nb:(k + 1) * nb, c * LANES:(c + 1) * LANES])

    def hist_ffn(cols):
        return _row_cat([sf_ref[:, k * 2 * D_FF + cols.start:k * 2 * D_FF + cols.stop]
                         for k in range(CONV_HIST)])

    def sink_pool(ext):
        first = ext.shape[0] - POOL_BUF * nb
        for k in range(POOL_BUF):
            pool_out_ref[k] = ext[first + k * nb:first + (k + 1) * nb]

    def sink_conv(ext):
        write_steps(conv_out_ref, c_blocks, CONV_HIST, ext[ext.shape[0] - CONV_HIST * nb:])

    def sink_ffn(cols, ext):
        last = ext[ext.shape[0] - CONV_HIST * nb:]
        hi_blocks = 2 * D_FF // LANES - FFN_LO_BLOCKS
        for q in range(FF_CHUNK // LANES):
            block = cols.start // LANES + q
            if block < FFN_LO_BLOCKS:
                ref, n_blocks = ffn_lo_ref, FFN_LO_BLOCKS
            else:
                ref, n_blocks, block = ffn_hi_ref, hi_blocks, block - FFN_LO_BLOCKS
            for k in range(CONV_HIST):
                ref[_step_rows(n_blocks, CONV_HIST, k, block, nb), :] = (
                    last[k * nb:(k + 1) * nb, q * LANES:(q + 1) * LANES])

    x = read_steps(x_ref, d_blocks, nt)
    mods = _fold_mods(mod_ref[...], w[0], nt)

    y = _trunk(
        x, mods, w, h2_ref, f_ref, act_ref, nb=nb,
        cnt_fn=lambda win: float(min(win, PAST_LEN + 1)),
        hist_pool=lambda: _row_cat([sp_ref[k] for k in range(POOL_BUF)]),
        hist_conv=lambda: read_steps(sc_ref, c_blocks, CONV_HIST),
        hist_ffn=hist_ffn, sink_pool=sink_pool, sink_conv=sink_conv, sink_ffn=sink_ffn)

    write_steps(y_ref, d_blocks, nt, y)


def _sample(x, sp, sc, sf, mod, *weights, nb, nt):
    rows_per_block = nb * CONV_HIST
    hi_blocks = 2 * D_FF // LANES - FFN_LO_BLOCKS
    return pl.pallas_call(
        functools.partial(_sample_kernel, nb=nb, nt=nt),
        out_shape=[
            jax.ShapeDtypeStruct(x.shape, _f32),
            jax.ShapeDtypeStruct(sp.shape, _f32),
            jax.ShapeDtypeStruct(sc.shape, _f32),
            jax.ShapeDtypeStruct((FFN_LO_BLOCKS * rows_per_block, LANES), _f32),
            jax.ShapeDtypeStruct((hi_blocks * rows_per_block, LANES), _f32),
        ],
        scratch_shapes=[
            pltpu.VMEM((nt * nb, D_MODEL), _bf16),
            pltpu.VMEM((nt * nb, D_MODEL), _f32),
            pltpu.VMEM((nt * nb, D_FF), _bf16),
        ],
        compiler_params=pltpu.CompilerParams(vmem_limit_bytes=V7X_VMEM_LIMIT_BYTES),
        name="sample_trunk",
    )(x, sp, sc, sf, mod, *weights)


def _to_block_major(a, n_steps):
    nb, _, width = a.shape
    return (a.reshape(nb, n_steps, width // LANES, LANES).transpose(0, 2, 1, 3)
            .reshape(nb * (width // LANES) * n_steps, LANES))


def _from_block_major(a, nb, n_steps):
    blocks = a.shape[0] // (nb * n_steps)
    return (a.reshape(nb, blocks, n_steps, LANES).transpose(0, 2, 1, 3)
            .reshape(nb, n_steps, blocks * LANES))


def kernel(x_prompt, x_sample, state_pool, state_conv, state_ffn, c_prompt, c_sample, w_ada, b_ada,
           g_pre_mix, g_post_mix, g_pre_ffn, g_post_ffn, w_in, pool_w, pool_scale, conv_w, w_out,
           ffn_w_up, ffn_conv_w, ffn_w_down):
    assert w_ada.shape[0] == 1, "single trunk layer"
    nbs, nts, _ = x_sample.shape
    mod_p, mod_s = _ada(c_prompt, c_sample, w_ada[0], b_ada)
    g = jnp.concatenate([g_pre_mix, g_post_mix, g_pre_ffn, g_post_ffn], axis=0)
    y_p, pool_p, conv_p, ffn_p, w_in_bf, w_out_bf, w_up_bf, w_down_bf = _prompt(
        x_prompt, mod_p, g, w_in, pool_w[0], pool_scale, conv_w[0], w_out, ffn_w_up, ffn_conv_w[0],
        ffn_w_down)
    weights = (g, w_in_bf, pool_w[0], pool_scale, conv_w[0], w_out_bf, w_up_bf, ffn_conv_w[0],
               w_down_bf)

    y_s, pool_s, conv_s, ffn_lo, ffn_hi = _sample(
        _to_block_major(x_sample, nts),
        state_pool[0].transpose(1, 0, 2),
        _to_block_major(state_conv[0], CONV_HIST),
        state_ffn[0].reshape(nbs, CONV_HIST * 2 * D_FF),
        mod_s, *weights, nb=nbs, nt=nts)
    ffn_s = jnp.concatenate([_from_block_major(ffn_lo, nbs, CONV_HIST),
                             _from_block_major(ffn_hi, nbs, CONV_HIST)], axis=-1)

    return (y_p, _from_block_major(y_s, nbs, nts), pool_p.transpose(1, 0, 2)[None], conv_p, ffn_p,
            pool_s.transpose(1, 0, 2)[None],
            _from_block_major(conv_s, nbs, CONV_HIST)[None],
            ffn_s[None])
```

```python
import functools

import jax
import jax.numpy as jnp
from jax import lax
from jax.experimental import pallas as pl
from jax.experimental.pallas import tpu as pltpu

D_MODEL = 1024
D_POOL = 512
D_CONV = 512
POOL_WINDOWS = (2, 4, 8, 16)
POOL_GC = 128
POOL_BUF = 15
CONV_W = 3
CONV_HIST = CONV_W - 1
D_FF = 2816
D_IN_PROJ = D_POOL + 3 * D_CONV
RMS_EPS = 1e-6
PAST_LEN = 16384

V7X_VMEM_LIMIT_BYTES = 58 * 1024 * 1024
SUBLANES = 8
LANES = 128
FF_CHUNK = 256
N_FF_CHUNKS = D_FF // FF_CHUNK
FF_DOWN_GROUP = 3
FFN_LO_BLOCKS = 40
PROMPT_TT = 64
PROMPT_PITCH = PROMPT_TT + SUBLANES
WEIGHT_STAGE_ROWS = {2048: 128, 1024: 256, 5632: 64}
WEIGHT_STAGE_SLOTS = 4
ADA_ROW_PARTS = 4
ADA_TN = 2048
POOL_HALO = 16

_bf16 = jnp.bfloat16
_f32 = jnp.float32


def _dot(a, b):
    return jnp.dot(a, b, preferred_element_type=_f32)


def _rms_scaled(x, scale):
    ms = jnp.mean(x * x, axis=-1, keepdims=True)
    return x * lax.rsqrt(ms + RMS_EPS) * scale


def _silu(a):
    return a * jax.nn.sigmoid(a)


def _lane_cat(parts):
    return jnp.concatenate(parts, axis=1)


def _row_cat(parts):
    return jnp.concatenate(parts, axis=0)


def _ada_kernel(cp_ref, cs_ref, *rest):
    w_refs, (b_ref, mp_ref, ms_ref) = rest[:ADA_ROW_PARTS], rest[ADA_ROW_PARTS:]
    c = _row_cat([cp_ref[...], cs_ref[...]])
    w = _row_cat([w_ref[...] for w_ref in w_refs]).astype(_bf16)
    mod = _dot(_silu(c).astype(_bf16), w) + b_ref[...]
    bp = cp_ref.shape[0]
    mp_ref[...] = mod[:bp]
    ms_ref[...] = mod[bp:]


def _ada(c_prompt, c_sample, w_ada, b_ada):
    n = w_ada.shape[1]
    tn = ADA_TN
    bp, bs = c_prompt.shape[0], c_sample.shape[0]
    part = D_MODEL // ADA_ROW_PARTS
    w_specs = [pl.BlockSpec((part, tn), functools.partial(lambda i, p: (p, i), p=p))
               for p in range(ADA_ROW_PARTS)]
    return pl.pallas_call(
        _ada_kernel,
        grid=(n // tn,),
        in_specs=[
            pl.BlockSpec((bp, D_MODEL), lambda i: (0, 0)),
            pl.BlockSpec((bs, D_MODEL), lambda i: (0, 0)),
        ] + w_specs + [
            pl.BlockSpec((1, tn), lambda i: (0, i)),
        ],
        out_specs=[
            pl.BlockSpec((bp, tn), lambda i: (0, i)),
            pl.BlockSpec((bs, tn), lambda i: (0, i)),
        ],
        out_shape=[
            jax.ShapeDtypeStruct((bp, n), _f32),
            jax.ShapeDtypeStruct((bs, n), _f32),
        ],
        compiler_params=pltpu.CompilerParams(
            dimension_semantics=("arbitrary",),
            vmem_limit_bytes=V7X_VMEM_LIMIT_BYTES),
        name="ada_mod",
    )(c_prompt, c_sample, *([w_ada] * ADA_ROW_PARTS), b_ada)


def _front_stages(x, mods, w, out, *, nb, cnt_fn, hist_pool, hist_conv, sink_pool, sink_conv):
    (_, w_in_ref, pool_w_ref, pool_scale_ref, conv_w_ref, w_out_ref) = w[:6]
    sh1, scale1, gate1, sh2, scale2, _ = mods
    m = x.shape[0]

    h = (_rms_scaled(x, scale1) + sh1).astype(_bf16)
    yield
    proj = _dot(h, w_in_ref[...])
    yield
    v_pool = proj[:, :D_POOL]
    x_conv = proj[:, D_POOL:D_POOL + D_CONV]
    gate_b = proj[:, D_POOL + D_CONV:D_POOL + 2 * D_CONV]
    gate_c = proj[:, D_POOL + 2 * D_CONV:]

    pool_ext = _row_cat([hist_pool(), v_pool])
    halo = (pool_ext.shape[0] - m) // nb
    y_pool = []
    for gi, win in enumerate(POOL_WINDOWS):
        sl = slice(gi * POOL_GC, (gi + 1) * POOL_GC)
        s = pool_ext[:, sl]
        step = 1
        while step < win:
            n = s.shape[0]
            s = s[step * nb:] + s[:n - step * nb]
            step *= 2
        first = (halo - (win - 1)) * nb
        d = s[first:first + m] / cnt_fn(win) - v_pool[:, sl]
        y_pool.append(_dot(d.astype(_bf16), pool_w_ref[gi].astype(_bf16)))
    y_pool = _lane_cat(y_pool) * pool_scale_ref[...]
    sink_pool(pool_ext)

    cx = gate_c * x_conv
    conv_ext = _row_cat([hist_conv(), cx])
    cw = conv_w_ref[...]
    conv = conv_ext[0:m] * cw[0:1] + conv_ext[nb:nb + m] * cw[1:2] + cx * cw[2:3]
    y_conv = gate_b * conv
    sink_conv(conv_ext)
    mixed = _lane_cat([y_pool, y_conv]).astype(_bf16)
    yield
    mix = _dot(mixed, w_out_ref[...])
    yield
    x1 = x + _rms_scaled(mix, gate1)
    out["x1"] = x1
    out["h2"] = (_rms_scaled(x1, scale2) + sh2).astype(_bf16)
    yield


def _ffn_stages(w, h2_ref, f_ref, act_ref, *, nb, hist_ffn, sink_ffn):
    w_up_ref, ffn_conv_w_ref, w_down_ref = w[6:]
    m = h2_ref.shape[0]

    def ffn_cols(c):
        return [slice(base + c * FF_CHUNK, base + (c + 1) * FF_CHUNK) for base in (0, D_FF)]

    def up_proj(c):
        return [_dot(h2_ref[...], w_up_ref[:, cols]) for cols in ffn_cols(c)]

    ups = up_proj(0)
    group_start = 0
    for c in range(N_FF_CHUNKS):
        ups_next = up_proj(c + 1) if c + 1 < N_FF_CHUNKS else None
        halves = []
        for cols, up in zip(ffn_cols(c), ups):
            up_ext = _row_cat([hist_ffn(cols), up])
            fw = ffn_conv_w_ref[:, cols]
            halves.append(up_ext[0:m] * fw[0:1] + up_ext[nb:nb + m] * fw[1:2] + up * fw[2:3])
            sink_ffn(cols, up_ext)
        act_ref[:, c * FF_CHUNK:(c + 1) * FF_CHUNK] = (_silu(halves[0]) * halves[1]).astype(_bf16)
        if (c + 1) % FF_DOWN_GROUP == 0 or c + 1 == N_FF_CHUNKS:
            rows = slice(group_start * FF_CHUNK, (c + 1) * FF_CHUNK)
            contrib = _dot(act_ref[:, rows], w_down_ref[rows, :])
            if group_start == 0:
                f_ref[...] = contrib
            else:
                f_ref[...] += contrib
            group_start = c + 1
        ups = ups_next
        yield


def _finish(x1, f_ref, gate2):
    return x1 + _rms_scaled(f_ref[...], gate2)


def _fold_mods(mod, g_ref, n_steps):
    g = g_ref[...]
    sh1, sc1, gt1, sh2, sc2, gt2 = [mod[:, i * D_MODEL:(i + 1) * D_MODEL] for i in range(6)]
    folded = [sh1, g[0:1] * (1.0 + sc1), gt1 * g[1:2], sh2, g[2:3] * (1.0 + sc2), gt2 * g[3:4]]
    return [_row_cat([a] * n_steps) for a in folded]


def _trunk(x, mods, w, h2_ref, f_ref, act_ref, *, nb, cnt_fn, hist_pool, hist_conv, hist_ffn,
           sink_pool, sink_conv, sink_ffn):
    out = {}
    for _ in _front_stages(x, mods, w, out, nb=nb, cnt_fn=cnt_fn, hist_pool=hist_pool,
                           hist_conv=hist_conv, sink_pool=sink_pool, sink_conv=sink_conv):
        pass
    h2_ref[...] = out["h2"]
    for _ in _ffn_stages(w, h2_ref, f_ref, act_ref, nb=nb, hist_ffn=hist_ffn, sink_ffn=sink_ffn):
        pass
    return _finish(out["x1"], f_ref, mods[5])


def _stream_cast(src, dst, stage, sems, sem_row):
    slots, rows = stage.shape[0], stage.shape[1]
    n = src.shape[0] // rows
    ahead = slots - 1

    def chunk(i, slot):
        return pltpu.make_async_copy(src.at[pl.ds(i * rows, rows), :], stage.at[slot],
                                     sems.at[sem_row, slot])

    for i in range(min(ahead, n)):
        chunk(i, i).start()

    def body(i, carry):
        slot = lax.rem(i, slots)

        @pl.when(i + ahead < n)
        def _():
            chunk(i + ahead, lax.rem(i + ahead, slots)).start()

        chunk(i, slot).wait()
        dst[pl.ds(pl.multiple_of(i * rows, rows), rows), :] = stage[slot].astype(_bf16)
        return carry

    lax.fori_loop(0, n, body, 0)


def _prompt_kernel(x_ref, mod_ref, *rest, nb, tt, n_steps):
    (g_ref, w_in_hbm, pool_w_ref, pool_scale_ref, conv_w_ref, w_out_hbm, w_up_hbm, ffn_conv_w_ref,
     w_down_hbm) = rest[:9]
    y_ref, pool_out_ref, conv_out_ref, ffn_out_ref = rest[9:13]
    bf_out = rest[13:17]
    (xs, pool_carry, conv_carry, ffn_carry, h2_ref, f_ref, act_ref,
     w_in_bf, w_out_bf, w_up_bf, w_down_bf, stage_in, stage_sq, stage_up,
     stage_sems, out_sems) = rest[17:]
    bf_vmem = (w_in_bf, w_out_bf, w_up_bf, w_down_bf)
    w = (g_ref, w_in_bf, pool_w_ref, pool_scale_ref, conv_w_ref, w_out_bf, w_up_bf, ffn_conv_w_ref,
         w_down_bf)
    j = pl.program_id(0)

    def bf_writeback(k):
        return pltpu.make_async_copy(bf_vmem[k], bf_out[k], out_sems.at[k])

    pitch = xs.shape[1] // nb
    n_slabs = D_MODEL // LANES
    m = tt * nb

    @pl.when(j == 0)
    def _():
        pool_carry[...] = jnp.zeros_like(pool_carry)
        conv_carry[...] = jnp.zeros_like(conv_carry)
        ffn_carry[...] = jnp.zeros_like(ffn_carry)
        _stream_cast(w_in_hbm.at[0], w_in_bf, stage_in, stage_sems, 0)
        _stream_cast(w_out_hbm.at[0], w_out_bf, stage_sq, stage_sems, 1)
        _stream_cast(w_up_hbm.at[0], w_up_bf, stage_up, stage_sems, 2)
        _stream_cast(w_down_hbm.at[0], w_down_bf, stage_sq, stage_sems, 1)
        for k in range(len(bf_vmem)):
            bf_writeback(k).start()

    for b in range(nb):
        for c in range(n_slabs):
            xs[c, b * pitch:b * pitch + tt, :] = x_ref[b, :, c * LANES:(c + 1) * LANES]
    x = _row_cat([_lane_cat([xs[c, pl.ds(i, nb, stride=pitch), :] for c in range(n_slabs)])
                  for i in range(tt)])

    mods = _fold_mods(mod_ref[...], w[0], tt)

    t_idx = j * tt + lax.shift_right_logical(
        lax.broadcasted_iota(jnp.int32, (m, POOL_GC), 0), nb.bit_length() - 1)

    def cnt_fn(win):
        return jnp.minimum(win, t_idx + 1).astype(_f32)

    def carry_rows(ref, first, n):
        return _lane_cat([ref[first + q] for q in range(n)])

    def keep_rows(ref, first, ext):
        rows = ref.shape[1]
        for q in range(ext.shape[1] // LANES):
            ref[first + q] = ext[ext.shape[0] - rows:, q * LANES:(q + 1) * LANES]

    y = _trunk(
        x, mods, w, h2_ref, f_ref, act_ref, nb=nb, cnt_fn=cnt_fn,
        hist_pool=lambda: carry_rows(pool_carry, 0, D_POOL // LANES),
        hist_conv=lambda: carry_rows(conv_carry, 0, D_CONV // LANES),
        hist_ffn=lambda cols: carry_rows(ffn_carry, cols.start // LANES, FF_CHUNK // LANES),
        sink_pool=lambda ext: keep_rows(pool_carry, 0, ext),
        sink_conv=lambda ext: keep_rows(conv_carry, 0, ext),
        sink_ffn=lambda cols, ext: keep_rows(ffn_carry, cols.start // LANES, ext))

    for i in range(tt):
        for c in range(n_slabs):
            xs[c, pl.ds(i, nb, stride=pitch), :] = y[i * nb:(i + 1) * nb, c * LANES:(c + 1) * LANES]
    for b in range(nb):
        for c in range(n_slabs):
            y_ref[b, :, c * LANES:(c + 1) * LANES] = xs[c, b * pitch:b * pitch + tt, :]

    @pl.when(j == n_steps - 1)
    def _():
        def emit(out_ref, carry, n):
            first = carry.shape[1] // nb - n
            for b in range(nb):
                for q in range(carry.shape[0]):
                    out_ref[0, b, :, q * LANES:(q + 1) * LANES] = (
                        carry[q, pl.ds(first * nb + b, n, stride=nb), :])
        pool_rows = carry_rows(pool_carry, 0, D_POOL // LANES)
        for k in range(POOL_BUF):
            first = (POOL_HALO - POOL_BUF + k) * nb
            pool_out_ref[k] = pool_rows[first:first + nb]
        emit(conv_out_ref, conv_carry, CONV_HIST)
        emit(ffn_out_ref, ffn_carry, CONV_HIST)
        for k in range(len(bf_vmem)):
            bf_writeback(k).wait()


def _prompt(x, mod, g, w_in, pool_w, pool_scale, conv_w, w_out, w_up, ffn_conv_w, w_down):
    nb, seq, _ = x.shape
    assert nb == SUBLANES, "one vreg row per time step"
    tt = PROMPT_TT
    n_steps = seq // tt
    m = tt * nb
    n_slabs = D_MODEL // LANES
    big = (w_in, w_out, w_up, w_down)
    stage_widths = (D_IN_PROJ, D_MODEL, 2 * D_FF)
    in_hbm = pl.BlockSpec(memory_space=pl.ANY)

    def const_spec(a):
        nd = a.ndim
        return pl.BlockSpec(a.shape, lambda j: (0,) * nd, pipeline_mode=pl.Buffered(1))

    return pl.pallas_call(
        functools.partial(_prompt_kernel, nb=nb, tt=tt, n_steps=n_steps),
        grid=(n_steps,),
        in_specs=[pl.BlockSpec((nb, tt, D_MODEL), lambda j: (0, j, 0)), const_spec(mod),
                  const_spec(g), in_hbm, const_spec(pool_w), const_spec(pool_scale),
                  const_spec(conv_w), in_hbm, in_hbm, const_spec(ffn_conv_w), in_hbm],
        out_specs=[
            pl.BlockSpec((nb, tt, D_MODEL), lambda j: (0, j, 0)),
            pl.BlockSpec((POOL_BUF, nb, D_POOL), lambda j: (0, 0, 0)),
            pl.BlockSpec((1, nb, CONV_HIST, D_CONV), lambda j: (0, 0, 0, 0)),
            pl.BlockSpec((1, nb, CONV_HIST, 2 * D_FF), lambda j: (0, 0, 0, 0)),
        ] + [pl.BlockSpec(memory_space=pl.ANY) for _ in big],
        out_shape=[
            jax.ShapeDtypeStruct((nb, seq, D_MODEL), _f32),
            jax.ShapeDtypeStruct((POOL_BUF, nb, D_POOL), _f32),
            jax.ShapeDtypeStruct((1, nb, CONV_HIST, D_CONV), _f32),
            jax.ShapeDtypeStruct((1, nb, CONV_HIST, 2 * D_FF), _f32),
        ] + [jax.ShapeDtypeStruct(a.shape[1:], _bf16) for a in big],
        scratch_shapes=[
            pltpu.VMEM((n_slabs, nb * PROMPT_PITCH, LANES), _f32),
            pltpu.VMEM((D_POOL // LANES, POOL_HALO * nb, LANES), _f32),
            pltpu.VMEM((D_CONV // LANES, CONV_HIST * nb, LANES), _f32),
            pltpu.VMEM((2 * D_FF // LANES, CONV_HIST * nb, LANES), _f32),
            pltpu.VMEM((m, D_MODEL), _bf16),
            pltpu.VMEM((m, D_MODEL), _f32),
            pltpu.VMEM((m, D_FF), _bf16),
        ] + [pltpu.VMEM(a.shape[1:], _bf16) for a in big]
        + [pltpu.VMEM((WEIGHT_STAGE_SLOTS, WEIGHT_STAGE_ROWS[c], c), _f32) for c in stage_widths]
        + [pltpu.SemaphoreType.DMA((len(stage_widths), WEIGHT_STAGE_SLOTS)),
           pltpu.SemaphoreType.DMA((len(big),))],
        compiler_params=pltpu.CompilerParams(
            dimension_semantics=("arbitrary",),
            vmem_limit_bytes=V7X_VMEM_LIMIT_BYTES),
        name="prompt_trunk",
    )(x, mod, g, w_in, pool_w, pool_scale, conv_w, w_out, w_up, ffn_conv_w, w_down)


def _step_rows(n_blocks, n_steps, k, c, nb):
    return pl.ds(n_steps * c + k, nb, stride=n_steps * n_blocks)


def _sample_kernel(x_ref, sp_ref, sc_ref, sf_ref, mod_ref, *rest, nb, nt):
    w = rest[:9]
    (y_ref, pool_out_ref, conv_out_ref, ffn_lo_ref, ffn_hi_ref,
     h2_ref, f_ref, act_ref) = rest[9:]
    d_blocks, c_blocks = D_MODEL // LANES, D_CONV // LANES

    def read_steps(ref, n_blocks, n_steps):
        return _row_cat([_lane_cat([ref[_step_rows(n_blocks, n_steps, k, c, nb), :]
                                    for c in range(n_blocks)]) for k in range(n_steps)])

    def write_steps(ref, n_blocks, n_steps, rows):
        for k in range(n_steps):
            for c in range(n_blocks):
                ref[_step_rows(n_blocks, n_steps, k, c, nb), :] = (
                    rows[k * nb:(k + 1) * nb, c * LANES:(c + 1) * LANES])

    def hist_ffn(cols):
        return _row_cat([sf_ref[:, k * 2 * D_FF + cols.start:k * 2 * D_FF + cols.stop]
                         for k in range(CONV_HIST)])

    def sink_pool(ext):
        first = ext.shape[0] - POOL_BUF * nb
        for k in range(POOL_BUF):
            pool_out_ref[k] = ext[first + k * nb:first + (k + 1) * nb]

    def sink_conv(ext):
        write_steps(conv_out_ref, c_blocks, CONV_HIST, ext[ext.shape[0] - CONV_HIST * nb:])

    def sink_ffn(cols, ext):
        last = ext[ext.shape[0] - CONV_HIST * nb:]
        hi_blocks = 2 * D_FF // LANES - FFN_LO_BLOCKS
        for q in range(FF_CHUNK // LANES):
            block = cols.start // LANES + q
            if block < FFN_LO_BLOCKS:
                ref, n_blocks = ffn_lo_ref, FFN_LO_BLOCKS
            else:
                ref, n_blocks, block = ffn_hi_ref, hi_blocks, block - FFN_LO_BLOCKS
            for k in range(CONV_HIST):
                ref[_step_rows(n_blocks, CONV_HIST, k, block, nb), :] = (
                    last[k * nb:(k + 1) * nb, q * LANES:(q + 1) * LANES])

    x = read_steps(x_ref, d_blocks, nt)
    mods = _fold_mods(mod_ref[...], w[0], nt)

    y = _trunk(
        x, mods, w, h2_ref, f_ref, act_ref, nb=nb,
        cnt_fn=lambda win: float(min(win, PAST_LEN + 1)),
        hist_pool=lambda: _row_cat([sp_ref[k] for k in range(POOL_BUF)]),
        hist_conv=lambda: read_steps(sc_ref, c_blocks, CONV_HIST),
        hist_ffn=hist_ffn, sink_pool=sink_pool, sink_conv=sink_conv, sink_ffn=sink_ffn)

    write_steps(y_ref, d_blocks, nt, y)


def _sample(x, sp, sc, sf, mod, *weights, nb, nt):
    rows_per_block = nb * CONV_HIST
    hi_blocks = 2 * D_FF // LANES - FFN_LO_BLOCKS
    return pl.pallas_call(
        functools.partial(_sample_kernel, nb=nb, nt=nt),
        out_shape=[
            jax.ShapeDtypeStruct(x.shape, _f32),
            jax.ShapeDtypeStruct(sp.shape, _f32),
            jax.ShapeDtypeStruct(sc.shape, _f32),
            jax.ShapeDtypeStruct((FFN_LO_BLOCKS * rows_per_block, LANES), _f32),
            jax.ShapeDtypeStruct((hi_blocks * rows_per_block, LANES), _f32),
        ],
        scratch_shapes=[
            pltpu.VMEM((nt * nb, D_MODEL), _bf16),
            pltpu.VMEM((nt * nb, D_MODEL), _f32),
            pltpu.VMEM((nt * nb, D_FF), _bf16),
        ],
        compiler_params=pltpu.CompilerParams(vmem_limit_bytes=V7X_VMEM_LIMIT_BYTES),
        name="sample_trunk",
    )(x, sp, sc, sf, mod, *weights)


def _to_block_major(a, n_steps):
    nb, _, width = a.shape
    return (a.reshape(nb, n_steps, width // LANES, LANES).transpose(0, 2, 1, 3)
            .reshape(nb * (width // LANES) * n_steps, LANES))


def _from_block_major(a, nb, n_steps):
    blocks = a.shape[0] // (nb * n_steps)
    return (a.reshape(nb, blocks, n_steps, LANES).transpose(0, 2, 1, 3)
            .reshape(nb, n_steps, blocks * LANES))


def kernel(x_prompt, x_sample, state_pool, state_conv, state_ffn, c_prompt, c_sample, w_ada, b_ada,
           g_pre_mix, g_post_mix, g_pre_ffn, g_post_ffn, w_in, pool_w, pool_scale, conv_w, w_out,
           ffn_w_up, ffn_conv_w, ffn_w_down):
    assert w_ada.shape[0] == 1, "single trunk layer"
    nbs, nts, _ = x_sample.shape
    mod_p, mod_s = _ada(c_prompt, c_sample, w_ada[0], b_ada)
    g = jnp.concatenate([g_pre_mix, g_post_mix, g_pre_ffn, g_post_ffn], axis=0)
    y_p, pool_p, conv_p, ffn_p, w_in_bf, w_out_bf, w_up_bf, w_down_bf = _prompt(
        x_prompt, mod_p, g, w_in, pool_w[0], pool_scale, conv_w[0], w_out, ffn_w_up, ffn_conv_w[0],
        ffn_w_down)
    weights = (g, w_in_bf, pool_w[0], pool_scale, conv_w[0], w_out_bf, w_up_bf, ffn_conv_w[0],
               w_down_bf)

    y_s, pool_s, conv_s, ffn_lo, ffn_hi = _sample(
        _to_block_major(x_sample, nts),
        state_pool[0].transpose(1, 0, 2),
        _to_block_major(state_conv[0], CONV_HIST),
        state_ffn[0].reshape(nbs, CONV_HIST * 2 * D_FF),
        mod_s, *weights, nb=nbs, nt=nts)
    ffn_s = jnp.concatenate([_from_block_major(ffn_lo, nbs, CONV_HIST),
                             _from_block_major(ffn_hi, nbs, CONV_HIST)], axis=-1)

    return (y_p, _from_block_major(y_s, nbs, nts), pool_p.transpose(1, 0, 2)[None], conv_p, ffn_p,
            pool_s.transpose(1, 0, 2)[None],
            _from_block_major(conv_s, nbs, CONV_HIST)[None],
            ffn_s[None])
```

```python
import functools

import jax
import jax.numpy as jnp
from jax import lax
from jax.experimental import pallas as pl
from jax.experimental.pallas import tpu as pltpu

D_MODEL = 1024
D_POOL = 512
D_CONV = 512
POOL_WINDOWS = (2, 4, 8, 16)
POOL_GC = 128
POOL_BUF = 15
CONV_W = 3
CONV_HIST = CONV_W - 1
D_FF = 2816
D_IN_PROJ = D_POOL + 3 * D_CONV
RMS_EPS = 1e-6
PAST_LEN = 16384

V7X_VMEM_LIMIT_BYTES = 58 * 1024 * 1024
SUBLANES = 8
LANES = 128
FF_CHUNK = 256
N_FF_CHUNKS = D_FF // FF_CHUNK
FFN_UPS_AHEAD = 1
FF_DOWN_GROUP = 3
FFN_LO_BLOCKS = 40
PROMPT_TT = 64
PROMPT_PITCH = PROMPT_TT + SUBLANES
WEIGHT_STAGE_ROWS = {2048: 128, 1024: 256, 5632: 64}
WEIGHT_STAGE_SLOTS = 4
ADA_ROW_PARTS = 4
ADA_TN = 2048
POOL_HALO = 16

_bf16 = jnp.bfloat16
_f32 = jnp.float32


def _dot(a, b):
    return jnp.dot(a, b, preferred_element_type=_f32)


def _rms_scaled(x, scale):
    ms = jnp.mean(x * x, axis=-1, keepdims=True)
    return x * lax.rsqrt(ms + RMS_EPS) * scale


def _silu(a):
    return a * jax.nn.sigmoid(a)


def _lane_cat(parts):
    return jnp.concatenate(parts, axis=1)


def _row_cat(parts):
    return jnp.concatenate(parts, axis=0)


def _ada_kernel(cp_ref, cs_ref, *rest):
    w_refs, (b_ref, mp_ref, ms_ref) = rest[:ADA_ROW_PARTS], rest[ADA_ROW_PARTS:]
    c = _row_cat([cp_ref[...], cs_ref[...]])
    w = _row_cat([w_ref[...] for w_ref in w_refs]).astype(_bf16)
    mod = _dot(_silu(c).astype(_bf16), w) + b_ref[...]
    bp = cp_ref.shape[0]
    mp_ref[...] = mod[:bp]
    ms_ref[...] = mod[bp:]


def _ada(c_prompt, c_sample, w_ada, b_ada):
    n = w_ada.shape[1]
    tn = ADA_TN
    bp, bs = c_prompt.shape[0], c_sample.shape[0]
    part = D_MODEL // ADA_ROW_PARTS
    w_specs = [pl.BlockSpec((part, tn), functools.partial(lambda i, p: (p, i), p=p))
               for p in range(ADA_ROW_PARTS)]
    return pl.pallas_call(
        _ada_kernel,
        grid=(n // tn,),
        in_specs=[
            pl.BlockSpec((bp, D_MODEL), lambda i: (0, 0)),
            pl.BlockSpec((bs, D_MODEL), lambda i: (0, 0)),
        ] + w_specs + [
            pl.BlockSpec((1, tn), lambda i: (0, i)),
        ],
        out_specs=[
            pl.BlockSpec((bp, tn), lambda i: (0, i)),
            pl.BlockSpec((bs, tn), lambda i: (0, i)),
        ],
        out_shape=[
            jax.ShapeDtypeStruct((bp, n), _f32),
            jax.ShapeDtypeStruct((bs, n), _f32),
        ],
        compiler_params=pltpu.CompilerParams(
            dimension_semantics=("arbitrary",),
            vmem_limit_bytes=V7X_VMEM_LIMIT_BYTES),
        name="ada_mod",
    )(c_prompt, c_sample, *([w_ada] * ADA_ROW_PARTS), b_ada)


def _front_head(x, mods):
    return (_rms_scaled(x, mods[1]) + mods[0]).astype(_bf16)


def _front_stages(x, h, mods, w, out, *, nb, cnt_fn, hist_pool, hist_conv, sink_pool, sink_conv):
    (_, w_in_ref, pool_w_ref, pool_scale_ref, conv_w_ref, w_out_ref) = w[:6]
    _, _, gate1, sh2, scale2, _ = mods
    m = x.shape[0]

    proj = _dot(h[...], w_in_ref[...])
    yield
    v_pool = proj[:, :D_POOL]
    x_conv = proj[:, D_POOL:D_POOL + D_CONV]
    gate_b = proj[:, D_POOL + D_CONV:D_POOL + 2 * D_CONV]
    gate_c = proj[:, D_POOL + 2 * D_CONV:]

    pool_ext = _row_cat([hist_pool(), v_pool])
    halo = (pool_ext.shape[0] - m) // nb
    y_pool = []
    for gi, win in enumerate(POOL_WINDOWS):
        sl = slice(gi * POOL_GC, (gi + 1) * POOL_GC)
        s = pool_ext[:, sl]
        step = 1
        while step < win:
            n = s.shape[0]
            s = s[step * nb:] + s[:n - step * nb]
            step *= 2
        first = (halo - (win - 1)) * nb
        d = s[first:first + m] / cnt_fn(win) - v_pool[:, sl]
        y_pool.append(_dot(d.astype(_bf16), pool_w_ref[gi].astype(_bf16)))
    y_pool = _lane_cat(y_pool) * pool_scale_ref[...]
    sink_pool(pool_ext)

    cx = gate_c * x_conv
    conv_ext = _row_cat([hist_conv(), cx])
    cw = conv_w_ref[...]
    conv = conv_ext[0:m] * cw[0:1] + conv_ext[nb:nb + m] * cw[1:2] + cx * cw[2:3]
    y_conv = gate_b * conv
    sink_conv(conv_ext)
    mixed = _lane_cat([y_pool, y_conv]).astype(_bf16)
    yield
    mix = _dot(mixed, w_out_ref[...])
    yield
    x1 = x + _rms_scaled(mix, gate1)
    out["x1"] = x1
    out["h2"] = (_rms_scaled(x1, scale2) + sh2).astype(_bf16)
    yield


class _UpValues:
    def __init__(self):
        self._v = {}

    def put(self, c, half, val):
        self._v[c, half] = val

    def get(self, c, half):
        return self._v.pop((c, half))


def _ffn_cols(c):
    return [slice(base + c * FF_CHUNK, base + (c + 1) * FF_CHUNK) for base in (0, D_FF)]


def _ffn_issue(w, h2_ref, ups, c):
    for half, cols in enumerate(_ffn_cols(c)):
        ups.put(c, half, _dot(h2_ref[...], w[6][:, cols]))


def _ffn_stages(w, h2_ref, f_ref, act_ref, ups, *, nb, hist_ffn, sink_ffn):
    _, ffn_conv_w_ref, w_down_ref = w[6:]
    m = h2_ref.shape[0]
    group_start = 0
    for c in range(N_FF_CHUNKS):
        if c + FFN_UPS_AHEAD < N_FF_CHUNKS:
            _ffn_issue(w, h2_ref, ups, c + FFN_UPS_AHEAD)
        halves = []
        for half, cols in enumerate(_ffn_cols(c)):
            up = ups.get(c, half)
            up_ext = _row_cat([hist_ffn(cols), up])
            fw = ffn_conv_w_ref[:, cols]
            halves.append(up_ext[0:m] * fw[0:1] + up_ext[nb:nb + m] * fw[1:2] + up * fw[2:3])
            sink_ffn(cols, up_ext)
        act_ref[:, c * FF_CHUNK:(c + 1) * FF_CHUNK] = (_silu(halves[0]) * halves[1]).astype(_bf16)
        if (c + 1) % FF_DOWN_GROUP == 0 or c + 1 == N_FF_CHUNKS:
            rows = slice(group_start * FF_CHUNK, (c + 1) * FF_CHUNK)
            contrib = _dot(act_ref[:, rows], w_down_ref[rows, :])
            if group_start == 0:
                f_ref[...] = contrib
            else:
                f_ref[...] += contrib
            group_start = c + 1
        yield


def _finish(x1, f_ref, gate2):
    return x1 + _rms_scaled(f_ref[...], gate2)


def _fold_mods(mod, g_ref, n_steps):
    g = g_ref[...]
    sh1, sc1, gt1, sh2, sc2, gt2 = [mod[:, i * D_MODEL:(i + 1) * D_MODEL] for i in range(6)]
    folded = [sh1, g[0:1] * (1.0 + sc1), gt1 * g[1:2], sh2, g[2:3] * (1.0 + sc2), gt2 * g[3:4]]
    return [_row_cat([a] * n_steps) for a in folded]


def _trunk(x, mods, w, h2_ref, f_ref, act_ref, *, nb, cnt_fn, hist_pool, hist_conv, hist_ffn,
           sink_pool, sink_conv, sink_ffn):
    out = {}
    for _ in _front_stages(x, _front_head(x, mods), mods, w, out, nb=nb, cnt_fn=cnt_fn,
                           hist_pool=hist_pool, hist_conv=hist_conv, sink_pool=sink_pool,
                           sink_conv=sink_conv):
        pass
    h2_ref[...] = out["h2"]
    ups = _UpValues()
    for c in range(FFN_UPS_AHEAD):
        _ffn_issue(w, h2_ref, ups, c)
    for _ in _ffn_stages(w, h2_ref, f_ref, act_ref, ups, nb=nb, hist_ffn=hist_ffn,
                         sink_ffn=sink_ffn):
        pass
    return _finish(out["x1"], f_ref, mods[5])


def _stream_cast(src, dst, stage, sems, sem_row):
    slots, rows = stage.shape[0], stage.shape[1]
    n = src.shape[0] // rows
    ahead = slots - 1

    def chunk(i, slot):
        return pltpu.make_async_copy(src.at[pl.ds(i * rows, rows), :], stage.at[slot],
                                     sems.at[sem_row, slot])

    for i in range(min(ahead, n)):
        chunk(i, i).start()

    def body(i, carry):
        slot = lax.rem(i, slots)

        @pl.when(i + ahead < n)
        def _():
            chunk(i + ahead, lax.rem(i + ahead, slots)).start()

        chunk(i, slot).wait()
        dst[pl.ds(pl.multiple_of(i * rows, rows), rows), :] = stage[slot].astype(_bf16)
        return carry

    lax.fori_loop(0, n, body, 0)


def _prompt_kernel(x_ref, mod_ref, *rest, nb, tt, n_steps):
    (g_ref, w_in_hbm, pool_w_ref, pool_scale_ref, conv_w_ref, w_out_hbm, w_up_hbm, ffn_conv_w_ref,
     w_down_hbm) = rest[:9]
    y_ref, pool_out_ref, conv_out_ref, ffn_out_ref = rest[9:13]
    bf_out = rest[13:17]
    (xs, pool_carry, conv_carry, ffn_carry, h2_ref, f_ref, act_ref,
     w_in_bf, w_out_bf, w_up_bf, w_down_bf, stage_in, stage_sq, stage_up,
     stage_sems, out_sems) = rest[17:]
    bf_vmem = (w_in_bf, w_out_bf, w_up_bf, w_down_bf)
    w = (g_ref, w_in_bf, pool_w_ref, pool_scale_ref, conv_w_ref, w_out_bf, w_up_bf, ffn_conv_w_ref,
         w_down_bf)
    j = pl.program_id(0)

    def bf_writeback(k):
        return pltpu.make_async_copy(bf_vmem[k], bf_out[k], out_sems.at[k])

    pitch = xs.shape[1] // nb
    n_slabs = D_MODEL // LANES
    m = tt * nb

    @pl.when(j == 0)
    def _():
        pool_carry[...] = jnp.zeros_like(pool_carry)
        conv_carry[...] = jnp.zeros_like(conv_carry)
        ffn_carry[...] = jnp.zeros_like(ffn_carry)
        _stream_cast(w_in_hbm.at[0], w_in_bf, stage_in, stage_sems, 0)
        _stream_cast(w_out_hbm.at[0], w_out_bf, stage_sq, stage_sems, 1)
        _stream_cast(w_up_hbm.at[0], w_up_bf, stage_up, stage_sems, 2)
        _stream_cast(w_down_hbm.at[0], w_down_bf, stage_sq, stage_sems, 1)
        for k in range(len(bf_vmem)):
            bf_writeback(k).start()

    for b in range(nb):
        for c in range(n_slabs):
            xs[c, b * pitch:b * pitch + tt, :] = x_ref[b, :, c * LANES:(c + 1) * LANES]
    x = _row_cat([_lane_cat([xs[c, pl.ds(i, nb, stride=pitch), :] for c in range(n_slabs)])
                  for i in range(tt)])

    mods = _fold_mods(mod_ref[...], w[0], tt)

    t_idx = j * tt + lax.shift_right_logical(
        lax.broadcasted_iota(jnp.int32, (m, POOL_GC), 0), nb.bit_length() - 1)

    def cnt_fn(win):
        return jnp.minimum(win, t_idx + 1).astype(_f32)

    def carry_rows(ref, first, n):
        return _lane_cat([ref[first + q] for q in range(n)])

    def keep_rows(ref, first, ext):
        rows = ref.shape[1]
        for q in range(ext.shape[1] // LANES):
            ref[first + q] = ext[ext.shape[0] - rows:, q * LANES:(q + 1) * LANES]

    y = _trunk(
        x, mods, w, h2_ref, f_ref, act_ref, nb=nb, cnt_fn=cnt_fn,
        hist_pool=lambda: carry_rows(pool_carry, 0, D_POOL // LANES),
        hist_conv=lambda: carry_rows(conv_carry, 0, D_CONV // LANES),
        hist_ffn=lambda cols: carry_rows(ffn_carry, cols.start // LANES, FF_CHUNK // LANES),
        sink_pool=lambda ext: keep_rows(pool_carry, 0, ext),
        sink_conv=lambda ext: keep_rows(conv_carry, 0, ext),
        sink_ffn=lambda cols, ext: keep_rows(ffn_carry, cols.start // LANES, ext))

    for i in range(tt):
        for c in range(n_slabs):
            xs[c, pl.ds(i, nb, stride=pitch), :] = y[i * nb:(i + 1) * nb, c * LANES:(c + 1) * LANES]
    for b in range(nb):
        for c in range(n_slabs):
            y_ref[b, :, c * LANES:(c + 1) * LANES] = xs[c, b * pitch:b * pitch + tt, :]

    @pl.when(j == n_steps - 1)
    def _():
        def emit(out_ref, carry, n):
            first = carry.shape[1] // nb - n
            for b in range(nb):
                for q in range(carry.shape[0]):
                    out_ref[0, b, :, q * LANES:(q + 1) * LANES] = (
                        carry[q, pl.ds(first * nb + b, n, stride=nb), :])
        pool_rows = carry_rows(pool_carry, 0, D_POOL // LANES)
        for k in range(POOL_BUF):
            first = (POOL_HALO - POOL_BUF + k) * nb
            pool_out_ref[k] = pool_rows[first:first + nb]
        emit(conv_out_ref, conv_carry, CONV_HIST)
        emit(ffn_out_ref, ffn_carry, CONV_HIST)
        for k in range(len(bf_vmem)):
            bf_writeback(k).wait()


def _prompt(x, mod, g, w_in, pool_w, pool_scale, conv_w, w_out, w_up, ffn_conv_w, w_down):
    nb, seq, _ = x.shape
    assert nb == SUBLANES, "one vreg row per time step"
    tt = PROMPT_TT
    n_steps = seq // tt
    m = tt * nb
    n_slabs = D_MODEL // LANES
    big = (w_in, w_out, w_up, w_down)
    stage_widths = (D_IN_PROJ, D_MODEL, 2 * D_FF)
    in_hbm = pl.BlockSpec(memory_space=pl.ANY)

    def const_spec(a):
        nd = a.ndim
        return pl.BlockSpec(a.shape, lambda j: (0,) * nd, pipeline_mode=pl.Buffered(1))

    return pl.pallas_call(
        functools.partial(_prompt_kernel, nb=nb, tt=tt, n_steps=n_steps),
        grid=(n_steps,),
        in_specs=[pl.BlockSpec((nb, tt, D_MODEL), lambda j: (0, j, 0)), const_spec(mod),
                  const_spec(g), in_hbm, const_spec(pool_w), const_spec(pool_scale),
                  const_spec(conv_w), in_hbm, in_hbm, const_spec(ffn_conv_w), in_hbm],
        out_specs=[
            pl.BlockSpec((nb, tt, D_MODEL), lambda j: (0, j, 0)),
            pl.BlockSpec((POOL_BUF, nb, D_POOL), lambda j: (0, 0, 0)),
            pl.BlockSpec((1, nb, CONV_HIST, D_CONV), lambda j: (0, 0, 0, 0)),
            pl.BlockSpec((1, nb, CONV_HIST, 2 * D_FF), lambda j: (0, 0, 0, 0)),
        ] + [pl.BlockSpec(memory_space=pl.ANY) for _ in big],
        out_shape=[
            jax.ShapeDtypeStruct((nb, seq, D_MODEL), _f32),
            jax.ShapeDtypeStruct((POOL_BUF, nb, D_POOL), _f32),
            jax.ShapeDtypeStruct((1, nb, CONV_HIST, D_CONV), _f32),
            jax.ShapeDtypeStruct((1, nb, CONV_HIST, 2 * D_FF), _f32),
        ] + [jax.ShapeDtypeStruct(a.shape[1:], _bf16) for a in big],
        scratch_shapes=[
            pltpu.VMEM((n_slabs, nb * PROMPT_PITCH, LANES), _f32),
            pltpu.VMEM((D_POOL // LANES, POOL_HALO * nb, LANES), _f32),
            pltpu.VMEM((D_CONV // LANES, CONV_HIST * nb, LANES), _f32),
            pltpu.VMEM((2 * D_FF // LANES, CONV_HIST * nb, LANES), _f32),
            pltpu.VMEM((m, D_MODEL), _bf16),
            pltpu.VMEM((m, D_MODEL), _f32),
            pltpu.VMEM((m, D_FF), _bf16),
        ] + [pltpu.VMEM(a.shape[1:], _bf16) for a in big]
        + [pltpu.VMEM((WEIGHT_STAGE_SLOTS, WEIGHT_STAGE_ROWS[c], c), _f32) for c in stage_widths]
        + [pltpu.SemaphoreType.DMA((len(stage_widths), WEIGHT_STAGE_SLOTS)),
           pltpu.SemaphoreType.DMA((len(big),))],
        compiler_params=pltpu.CompilerParams(
            dimension_semantics=("arbitrary",),
            vmem_limit_bytes=V7X_VMEM_LIMIT_BYTES),
        name="prompt_trunk",
    )(x, mod, g, w_in, pool_w, pool_scale, conv_w, w_out, w_up, ffn_conv_w, w_down)


def _step_rows(n_blocks, n_steps, k, c, nb):
    return pl.ds(n_steps * c + k, nb, stride=n_steps * n_blocks)


def _sample_kernel(*refs, nb, nt):
    (x_hbm, sp_hbm, sc_hbm, sf_hbm, mod_hbm, g_ref, w_in_hbm, pool_w_ref, pool_scale_ref,
     conv_w_ref, w_out_hbm, w_up_hbm, ffn_conv_w_ref, w_down_hbm) = refs[:14]
    out_hbm = refs[14:19]
    (x_ref, sp_ref, sc_ref, sf_ref, mod_ref, w_in_ref, w_out_ref, w_up_ref, w_down_ref) = refs[19:28]
    out_vmem = refs[28:33]
    y_ref, pool_out_ref, conv_out_ref, ffn_lo_ref, ffn_hi_ref = out_vmem
    h2_ref, f_ref, act_ref, in_sems, out_sems = refs[33:]
    w = (g_ref, w_in_ref, pool_w_ref, pool_scale_ref, conv_w_ref, w_out_ref, w_up_ref,
         ffn_conv_w_ref, w_down_ref)
    d_blocks, c_blocks = D_MODEL // LANES, D_CONV // LANES

    in_order = (("x", x_hbm, x_ref), ("mod", mod_hbm, mod_ref), ("w_in", w_in_hbm, w_in_ref),
                ("pool", sp_hbm, sp_ref), ("conv", sc_hbm, sc_ref), ("w_out", w_out_hbm, w_out_ref),
                ("w_up", w_up_hbm, w_up_ref), ("ffn", sf_hbm, sf_ref), ("w_down", w_down_hbm, w_down_ref))
    fetch = {name: pltpu.make_async_copy(src, dst, in_sems.at[k])
             for k, (name, src, dst) in enumerate(in_order)}
    out_names = ("y", "pool", "conv", "ffn_lo", "ffn_hi")
    send = {name: pltpu.make_async_copy(src, dst, out_sems.at[k])
            for k, (name, src, dst) in enumerate(zip(out_names, out_vmem, out_hbm))}
    for name, _, _ in in_order:
        fetch[name].start()

    def read_steps(ref, n_blocks, n_steps):
        return _row_cat([_lane_cat([ref[_step_rows(n_blocks, n_steps, k, c, nb), :]
                                    for c in range(n_blocks)]) for k in range(n_steps)])

    def write_steps(ref, n_blocks, n_steps, rows):
        for k in range(n_steps):
            for c in range(n_blocks):
                ref[_step_rows(n_blocks, n_steps, k, c, nb), :] = (
                    rows[k * nb:(k + 1) * nb, c * LANES:(c + 1) * LANES])

    def hist_ffn(cols):
        return _row_cat([sf_ref[:, k * 2 * D_FF + cols.start:k * 2 * D_FF + cols.stop]
                         for k in range(CONV_HIST)])

    def sink_pool(ext):
        first = ext.shape[0] - POOL_BUF * nb
        for k in range(POOL_BUF):
            pool_out_ref[k] = ext[first + k * nb:first + (k + 1) * nb]

    def sink_conv(ext):
        write_steps(conv_out_ref, c_blocks, CONV_HIST, ext[ext.shape[0] - CONV_HIST * nb:])

    def sink_ffn(cols, ext):
        last = ext[ext.shape[0] - CONV_HIST * nb:]
        hi_blocks = 2 * D_FF // LANES - FFN_LO_BLOCKS
        for q in range(FF_CHUNK // LANES):
            block = cols.start // LANES + q
            if block < FFN_LO_BLOCKS:
                ref, n_blocks = ffn_lo_ref, FFN_LO_BLOCKS
            else:
                ref, n_blocks, block = ffn_hi_ref, hi_blocks, block - FFN_LO_BLOCKS
            for k in range(CONV_HIST):
                ref[_step_rows(n_blocks, CONV_HIST, k, block, nb), :] = (
                    last[k * nb:(k + 1) * nb, q * LANES:(q + 1) * LANES])

    fetch["x"].wait()
    fetch["mod"].wait()
    x = read_steps(x_ref, d_blocks, nt)
    mods = _fold_mods(mod_ref[...], g_ref, nt)
    h = _front_head(x, mods)
    mixed = {}
    front = _front_stages(
        x, h, mods, w, mixed, nb=nb,
        cnt_fn=lambda win: float(min(win, PAST_LEN + 1)),
        hist_pool=lambda: _row_cat([sp_ref[k] for k in range(POOL_BUF)]),
        hist_conv=lambda: read_steps(sc_ref, c_blocks, CONV_HIST),
        sink_pool=sink_pool, sink_conv=sink_conv)
    fetch["w_in"].wait()
    next(front)
    fetch["pool"].wait()
    fetch["conv"].wait()
    next(front)
    send["pool"].start()
    send["conv"].start()
    fetch["w_out"].wait()
    next(front)
    next(front)
    h2_ref[...] = mixed["h2"]

    fetch["w_up"].wait()
    fetch["ffn"].wait()
    fetch["w_down"].wait()
    ups = _UpValues()
    for c in range(FFN_UPS_AHEAD):
        _ffn_issue(w, h2_ref, ups, c)
    for _ in _ffn_stages(w, h2_ref, f_ref, act_ref, ups, nb=nb, hist_ffn=hist_ffn,
                         sink_ffn=sink_ffn):
        pass
    send["ffn_lo"].start()
    send["ffn_hi"].start()
    write_steps(y_ref, d_blocks, nt, _finish(mixed["x1"], f_ref, mods[5]))
    send["y"].start()
    for name in out_names:
        send[name].wait()


def _sample(x, sp, sc, sf, mod, g, w_in, pool_w, pool_scale, conv_w, w_out, w_up, ffn_conv_w, w_down,
            *, nb, nt):
    rows_per_block = nb * CONV_HIST
    hi_blocks = 2 * D_FF // LANES - FFN_LO_BLOCKS
    hbm = pl.BlockSpec(memory_space=pl.ANY)
    vmem = pl.BlockSpec(memory_space=pltpu.VMEM)
    copied_in = (x, sp, sc, sf, mod, w_in, w_out, w_up, w_down)
    out_shape = [
        jax.ShapeDtypeStruct(x.shape, _f32),
        jax.ShapeDtypeStruct(sp.shape, _f32),
        jax.ShapeDtypeStruct(sc.shape, _f32),
        jax.ShapeDtypeStruct((FFN_LO_BLOCKS * rows_per_block, LANES), _f32),
        jax.ShapeDtypeStruct((hi_blocks * rows_per_block, LANES), _f32),
    ]
    return pl.pallas_call(
        functools.partial(_sample_kernel, nb=nb, nt=nt),
        in_specs=[hbm, hbm, hbm, hbm, hbm, vmem, hbm, vmem, vmem, vmem, hbm, hbm, vmem, hbm],
        out_specs=[hbm] * len(out_shape),
        out_shape=out_shape,
        scratch_shapes=[pltpu.VMEM(a.shape, a.dtype) for a in copied_in]
        + [pltpu.VMEM(o.shape, o.dtype) for o in out_shape]
        + [
            pltpu.VMEM((nt * nb, D_MODEL), _bf16),
            pltpu.VMEM((nt * nb, D_MODEL), _f32),
            pltpu.VMEM((nt * nb, D_FF), _bf16),
            pltpu.SemaphoreType.DMA((len(copied_in),)),
            pltpu.SemaphoreType.DMA((len(out_shape),)),
        ],
        compiler_params=pltpu.CompilerParams(vmem_limit_bytes=V7X_VMEM_LIMIT_BYTES),
        name="sample_trunk",
    )(x, sp, sc, sf, mod, g, w_in, pool_w, pool_scale, conv_w, w_out, w_up, ffn_conv_w, w_down)


def _to_block_major(a, n_steps):
    nb, _, width = a.shape
    return (a.reshape(nb, n_steps, width // LANES, LANES).transpose(0, 2, 1, 3)
            .reshape(nb * (width // LANES) * n_steps, LANES))


def _from_block_major(a, nb, n_steps):
    blocks = a.shape[0] // (nb * n_steps)
    return (a.reshape(nb, blocks, n_steps, LANES).transpose(0, 2, 1, 3)
            .reshape(nb, n_steps, blocks * LANES))


def kernel(x_prompt, x_sample, state_pool, state_conv, state_ffn, c_prompt, c_sample, w_ada, b_ada,
           g_pre_mix, g_post_mix, g_pre_ffn, g_post_ffn, w_in, pool_w, pool_scale, conv_w, w_out,
           ffn_w_up, ffn_conv_w, ffn_w_down):
    assert w_ada.shape[0] == 1, "single trunk layer"
    nbs, nts, _ = x_sample.shape
    mod_p, mod_s = _ada(c_prompt, c_sample, w_ada[0], b_ada)
    g = jnp.concatenate([g_pre_mix, g_post_mix, g_pre_ffn, g_post_ffn], axis=0)
    y_p, pool_p, conv_p, ffn_p, w_in_bf, w_out_bf, w_up_bf, w_down_bf = _prompt(
        x_prompt, mod_p, g, w_in, pool_w[0], pool_scale, conv_w[0], w_out, ffn_w_up, ffn_conv_w[0],
        ffn_w_down)
    weights = (g, w_in_bf, pool_w[0], pool_scale, conv_w[0], w_out_bf, w_up_bf, ffn_conv_w[0],
               w_down_bf)

    y_s, pool_s, conv_s, ffn_lo, ffn_hi = _sample(
        _to_block_major(x_sample, nts),
        state_pool[0].transpose(1, 0, 2),
        _to_block_major(state_conv[0], CONV_HIST),
        state_ffn[0].reshape(nbs, CONV_HIST * 2 * D_FF),
        mod_s, *weights, nb=nbs, nt=nts)
    ffn_s = jnp.concatenate([_from_block_major(ffn_lo, nbs, CONV_HIST),
                             _from_block_major(ffn_hi, nbs, CONV_HIST)], axis=-1)

    return (y_p, _from_block_major(y_s, nbs, nts), pool_p.transpose(1, 0, 2)[None], conv_p, ffn_p,
            pool_s.transpose(1, 0, 2)[None],
            _from_block_major(conv_s, nbs, CONV_HIST)[None],
            ffn_s[None])
```

```python
import functools

import jax
import jax.numpy as jnp
from jax import lax
from jax.experimental import pallas as pl
from jax.experimental.pallas import tpu as pltpu

D_MODEL = 1024
D_POOL = 512
D_CONV = 512
POOL_WINDOWS = (2, 4, 8, 16)
POOL_GC = 128
POOL_BUF = 15
CONV_W = 3
CONV_HIST = CONV_W - 1
D_FF = 2816
D_IN_PROJ = D_POOL + 3 * D_CONV
RMS_EPS = 1e-6
PAST_LEN = 16384

V7X_VMEM_LIMIT_BYTES = 58 * 1024 * 1024
SUBLANES = 8
LANES = 128
FF_CHUNK = 256
N_FF_CHUNKS = D_FF // FF_CHUNK
FFN_UPS_AHEAD = 1
FF_DOWN_GROUP = 3
SAMPLE_FETCHES = 7 + 3 * -(-N_FF_CHUNKS // FF_DOWN_GROUP)
FFN_LO_BLOCKS = 40
PROMPT_TT = 64
PROMPT_PITCH = PROMPT_TT + SUBLANES
WEIGHT_STAGE_ROWS = {2048: 128, 1024: 256, 5632: 64}
WEIGHT_STAGE_SLOTS = 4
ADA_ROW_PARTS = 4
ADA_TN = 2048
POOL_HALO = 16

_bf16 = jnp.bfloat16
_f32 = jnp.float32


def _dot(a, b):
    return jnp.dot(a, b, preferred_element_type=_f32)


def _rms_scaled(x, scale):
    ms = jnp.mean(x * x, axis=-1, keepdims=True)
    return x * lax.rsqrt(ms + RMS_EPS) * scale


def _silu(a):
    return a * jax.nn.sigmoid(a)


def _lane_cat(parts):
    return jnp.concatenate(parts, axis=1)


def _row_cat(parts):
    return jnp.concatenate(parts, axis=0)


def _ada_kernel(cp_ref, cs_ref, *rest):
    w_refs, (b_ref, mp_ref, ms_ref) = rest[:ADA_ROW_PARTS], rest[ADA_ROW_PARTS:]
    c = _row_cat([cp_ref[...], cs_ref[...]])
    w = _row_cat([w_ref[...] for w_ref in w_refs]).astype(_bf16)
    mod = _dot(_silu(c).astype(_bf16), w) + b_ref[...]
    bp = cp_ref.shape[0]
    mp_ref[...] = mod[:bp]
    ms_ref[...] = mod[bp:]


def _ada(c_prompt, c_sample, w_ada, b_ada):
    n = w_ada.shape[1]
    tn = ADA_TN
    bp, bs = c_prompt.shape[0], c_sample.shape[0]
    part = D_MODEL // ADA_ROW_PARTS
    w_specs = [pl.BlockSpec((part, tn), functools.partial(lambda i, p: (p, i), p=p))
               for p in range(ADA_ROW_PARTS)]
    return pl.pallas_call(
        _ada_kernel,
        grid=(n // tn,),
        in_specs=[
            pl.BlockSpec((bp, D_MODEL), lambda i: (0, 0)),
            pl.BlockSpec((bs, D_MODEL), lambda i: (0, 0)),
        ] + w_specs + [
            pl.BlockSpec((1, tn), lambda i: (0, i)),
        ],
        out_specs=[
            pl.BlockSpec((bp, tn), lambda i: (0, i)),
            pl.BlockSpec((bs, tn), lambda i: (0, i)),
        ],
        out_shape=[
            jax.ShapeDtypeStruct((bp, n), _f32),
            jax.ShapeDtypeStruct((bs, n), _f32),
        ],
        compiler_params=pltpu.CompilerParams(
            dimension_semantics=("arbitrary",),
            vmem_limit_bytes=V7X_VMEM_LIMIT_BYTES),
        name="ada_mod",
    )(c_prompt, c_sample, *([w_ada] * ADA_ROW_PARTS), b_ada)


def _front_head(x, mods):
    return (_rms_scaled(x, mods[1]) + mods[0]).astype(_bf16)


def _front_stages(x, h, mods, w, out, *, nb, cnt_fn, hist_pool, hist_conv, sink_pool, sink_conv):
    (_, w_in_ref, pool_w_ref, pool_scale_ref, conv_w_ref, w_out_ref) = w[:6]
    _, _, gate1, sh2, scale2, _ = mods
    m = x.shape[0]

    proj = _dot(h[...], w_in_ref[...])
    yield
    v_pool = proj[:, :D_POOL]
    x_conv = proj[:, D_POOL:D_POOL + D_CONV]
    gate_b = proj[:, D_POOL + D_CONV:D_POOL + 2 * D_CONV]
    gate_c = proj[:, D_POOL + 2 * D_CONV:]

    pool_ext = _row_cat([hist_pool(), v_pool])
    halo = (pool_ext.shape[0] - m) // nb
    y_pool = []
    for gi, win in enumerate(POOL_WINDOWS):
        sl = slice(gi * POOL_GC, (gi + 1) * POOL_GC)
        s = pool_ext[:, sl]
        step = 1
        while step < win:
            n = s.shape[0]
            s = s[step * nb:] + s[:n - step * nb]
            step *= 2
        first = (halo - (win - 1)) * nb
        d = s[first:first + m] / cnt_fn(win) - v_pool[:, sl]
        y_pool.append(_dot(d.astype(_bf16), pool_w_ref[gi].astype(_bf16)))
    y_pool = _lane_cat(y_pool) * pool_scale_ref[...]
    sink_pool(pool_ext)

    cx = gate_c * x_conv
    conv_ext = _row_cat([hist_conv(), cx])
    cw = conv_w_ref[...]
    conv = conv_ext[0:m] * cw[0:1] + conv_ext[nb:nb + m] * cw[1:2] + cx * cw[2:3]
    y_conv = gate_b * conv
    sink_conv(conv_ext)
    mixed = _lane_cat([y_pool, y_conv]).astype(_bf16)
    yield
    mix = _dot(mixed, w_out_ref[...])
    yield
    x1 = x + _rms_scaled(mix, gate1)
    out["x1"] = x1
    out["h2"] = (_rms_scaled(x1, scale2) + sh2).astype(_bf16)
    yield


class _UpValues:
    def __init__(self):
        self._v = {}

    def put(self, c, half, val):
        self._v[c, half] = val

    def get(self, c, half):
        return self._v.pop((c, half))


def _ffn_cols(c):
    return [slice(base + c * FF_CHUNK, base + (c + 1) * FF_CHUNK) for base in (0, D_FF)]


def _ffn_issue(w, h2_ref, ups, c):
    for half, cols in enumerate(_ffn_cols(c)):
        ups.put(c, half, _dot(h2_ref[...], w[6][:, cols]))


def _ffn_stages(w, h2_ref, f_ref, act_ref, ups, *, nb, hist_ffn, sink_ffn):
    _, ffn_conv_w_ref, w_down_ref = w[6:]
    m = h2_ref.shape[0]
    group_start = 0
    for c in range(N_FF_CHUNKS):
        if c + FFN_UPS_AHEAD < N_FF_CHUNKS:
            _ffn_issue(w, h2_ref, ups, c + FFN_UPS_AHEAD)
        halves = []
        for half, cols in enumerate(_ffn_cols(c)):
            up = ups.get(c, half)
            up_ext = _row_cat([hist_ffn(cols), up])
            fw = ffn_conv_w_ref[:, cols]
            halves.append(up_ext[0:m] * fw[0:1] + up_ext[nb:nb + m] * fw[1:2] + up * fw[2:3])
            sink_ffn(cols, up_ext)
        act_ref[:, c * FF_CHUNK:(c + 1) * FF_CHUNK] = (_silu(halves[0]) * halves[1]).astype(_bf16)
        if (c + 1) % FF_DOWN_GROUP == 0 or c + 1 == N_FF_CHUNKS:
            rows = slice(group_start * FF_CHUNK, (c + 1) * FF_CHUNK)
            contrib = _dot(act_ref[:, rows], w_down_ref[rows, :])
            if group_start == 0:
                f_ref[...] = contrib
            else:
                f_ref[...] += contrib
            group_start = c + 1
        yield


def _finish(x1, f_ref, gate2):
    return x1 + _rms_scaled(f_ref[...], gate2)


def _fold_mods(mod, g_ref, n_steps):
    g = g_ref[...]
    sh1, sc1, gt1, sh2, sc2, gt2 = [mod[:, i * D_MODEL:(i + 1) * D_MODEL] for i in range(6)]
    folded = [sh1, g[0:1] * (1.0 + sc1), gt1 * g[1:2], sh2, g[2:3] * (1.0 + sc2), gt2 * g[3:4]]
    return [_row_cat([a] * n_steps) for a in folded]


def _trunk(x, mods, w, h2_ref, f_ref, act_ref, *, nb, cnt_fn, hist_pool, hist_conv, hist_ffn,
           sink_pool, sink_conv, sink_ffn):
    out = {}
    for _ in _front_stages(x, _front_head(x, mods), mods, w, out, nb=nb, cnt_fn=cnt_fn,
                           hist_pool=hist_pool, hist_conv=hist_conv, sink_pool=sink_pool,
                           sink_conv=sink_conv):
        pass
    h2_ref[...] = out["h2"]
    ups = _UpValues()
    for c in range(FFN_UPS_AHEAD):
        _ffn_issue(w, h2_ref, ups, c)
    for _ in _ffn_stages(w, h2_ref, f_ref, act_ref, ups, nb=nb, hist_ffn=hist_ffn,
                         sink_ffn=sink_ffn):
        pass
    return _finish(out["x1"], f_ref, mods[5])


def _stream_cast(src, dst, stage, sems, sem_row):
    slots, rows = stage.shape[0], stage.shape[1]
    n = src.shape[0] // rows
    ahead = slots - 1

    def chunk(i, slot):
        return pltpu.make_async_copy(src.at[pl.ds(i * rows, rows), :], stage.at[slot],
                                     sems.at[sem_row, slot])

    for i in range(min(ahead, n)):
        chunk(i, i).start()

    def body(i, carry):
        slot = lax.rem(i, slots)

        @pl.when(i + ahead < n)
        def _():
            chunk(i + ahead, lax.rem(i + ahead, slots)).start()

        chunk(i, slot).wait()
        dst[pl.ds(pl.multiple_of(i * rows, rows), rows), :] = stage[slot].astype(_bf16)
        return carry

    lax.fori_loop(0, n, body, 0)


def _prompt_kernel(x_ref, mod_ref, *rest, nb, tt, n_steps):
    (g_ref, w_in_hbm, pool_w_ref, pool_scale_ref, conv_w_ref, w_out_hbm, w_up_hbm, ffn_conv_w_ref,
     w_down_hbm) = rest[:9]
    y_ref, pool_out_ref, conv_out_ref, ffn_out_ref = rest[9:13]
    bf_out = rest[13:17]
    (xs, pool_carry, conv_carry, ffn_carry, h2_ref, f_ref, act_ref,
     w_in_bf, w_out_bf, w_up_bf, w_down_bf, stage_in, stage_sq, stage_up,
     stage_sems, out_sems) = rest[17:]
    bf_vmem = (w_in_bf, w_out_bf, w_up_bf, w_down_bf)
    w = (g_ref, w_in_bf, pool_w_ref, pool_scale_ref, conv_w_ref, w_out_bf, w_up_bf, ffn_conv_w_ref,
         w_down_bf)
    j = pl.program_id(0)

    def bf_writeback(k):
        return pltpu.make_async_copy(bf_vmem[k], bf_out[k], out_sems.at[k])

    pitch = xs.shape[1] // nb
    n_slabs = D_MODEL // LANES
    m = tt * nb

    @pl.when(j == 0)
    def _():
        pool_carry[...] = jnp.zeros_like(pool_carry)
        conv_carry[...] = jnp.zeros_like(conv_carry)
        ffn_carry[...] = jnp.zeros_like(ffn_carry)
        _stream_cast(w_in_hbm.at[0], w_in_bf, stage_in, stage_sems, 0)
        _stream_cast(w_out_hbm.at[0], w_out_bf, stage_sq, stage_sems, 1)
        _stream_cast(w_up_hbm.at[0], w_up_bf, stage_up, stage_sems, 2)
        _stream_cast(w_down_hbm.at[0], w_down_bf, stage_sq, stage_sems, 1)
        for k in range(len(bf_vmem)):
            bf_writeback(k).start()

    for b in range(nb):
        for c in range(n_slabs):
            xs[c, b * pitch:b * pitch + tt, :] = x_ref[b, :, c * LANES:(c + 1) * LANES]
    x = _row_cat([_lane_cat([xs[c, pl.ds(i, nb, stride=pitch), :] for c in range(n_slabs)])
                  for i in range(tt)])

    mods = _fold_mods(mod_ref[...], w[0], tt)

    t_idx = j * tt + lax.shift_right_logical(
        lax.broadcasted_iota(jnp.int32, (m, POOL_GC), 0), nb.bit_length() - 1)

    def cnt_fn(win):
        return jnp.minimum(win, t_idx + 1).astype(_f32)

    def carry_rows(ref, first, n):
        return _lane_cat([ref[first + q] for q in range(n)])

    def keep_rows(ref, first, ext):
        rows = ref.shape[1]
        for q in range(ext.shape[1] // LANES):
            ref[first + q] = ext[ext.shape[0] - rows:, q * LANES:(q + 1) * LANES]

    y = _trunk(
        x, mods, w, h2_ref, f_ref, act_ref, nb=nb, cnt_fn=cnt_fn,
        hist_pool=lambda: carry_rows(pool_carry, 0, D_POOL // LANES),
        hist_conv=lambda: carry_rows(conv_carry, 0, D_CONV // LANES),
        hist_ffn=lambda cols: carry_rows(ffn_carry, cols.start // LANES, FF_CHUNK // LANES),
        sink_pool=lambda ext: keep_rows(pool_carry, 0, ext),
        sink_conv=lambda ext: keep_rows(conv_carry, 0, ext),
        sink_ffn=lambda cols, ext: keep_rows(ffn_carry, cols.start // LANES, ext))

    for i in range(tt):
        for c in range(n_slabs):
            xs[c, pl.ds(i, nb, stride=pitch), :] = y[i * nb:(i + 1) * nb, c * LANES:(c + 1) * LANES]
    for b in range(nb):
        for c in range(n_slabs):
            y_ref[b, :, c * LANES:(c + 1) * LANES] = xs[c, b * pitch:b * pitch + tt, :]

    @pl.when(j == n_steps - 1)
    def _():
        def emit(out_ref, carry, n):
            first = carry.shape[1] // nb - n
            for b in range(nb):
                for q in range(carry.shape[0]):
                    out_ref[0, b, :, q * LANES:(q + 1) * LANES] = (
                        carry[q, pl.ds(first * nb + b, n, stride=nb), :])
        pool_rows = carry_rows(pool_carry, 0, D_POOL // LANES)
        for k in range(POOL_BUF):
            first = (POOL_HALO - POOL_BUF + k) * nb
            pool_out_ref[k] = pool_rows[first:first + nb]
        emit(conv_out_ref, conv_carry, CONV_HIST)
        emit(ffn_out_ref, ffn_carry, CONV_HIST)
        for k in range(len(bf_vmem)):
            bf_writeback(k).wait()


def _prompt(x, mod, g, w_in, pool_w, pool_scale, conv_w, w_out, w_up, ffn_conv_w, w_down):
    nb, seq, _ = x.shape
    assert nb == SUBLANES, "one vreg row per time step"
    tt = PROMPT_TT
    n_steps = seq // tt
    m = tt * nb
    n_slabs = D_MODEL // LANES
    big = (w_in, w_out, w_up, w_down)
    stage_widths = (D_IN_PROJ, D_MODEL, 2 * D_FF)
    in_hbm = pl.BlockSpec(memory_space=pl.ANY)

    def const_spec(a):
        nd = a.ndim
        return pl.BlockSpec(a.shape, lambda j: (0,) * nd, pipeline_mode=pl.Buffered(1))

    return pl.pallas_call(
        functools.partial(_prompt_kernel, nb=nb, tt=tt, n_steps=n_steps),
        grid=(n_steps,),
        in_specs=[pl.BlockSpec((nb, tt, D_MODEL), lambda j: (0, j, 0)), const_spec(mod),
                  const_spec(g), in_hbm, const_spec(pool_w), const_spec(pool_scale),
                  const_spec(conv_w), in_hbm, in_hbm, const_spec(ffn_conv_w), in_hbm],
        out_specs=[
            pl.BlockSpec((nb, tt, D_MODEL), lambda j: (0, j, 0)),
            pl.BlockSpec((POOL_BUF, nb, D_POOL), lambda j: (0, 0, 0)),
            pl.BlockSpec((1, nb, CONV_HIST, D_CONV), lambda j: (0, 0, 0, 0)),
            pl.BlockSpec((1, nb, CONV_HIST, 2 * D_FF), lambda j: (0, 0, 0, 0)),
        ] + [pl.BlockSpec(memory_space=pl.ANY) for _ in big],
        out_shape=[
            jax.ShapeDtypeStruct((nb, seq, D_MODEL), _f32),
            jax.ShapeDtypeStruct((POOL_BUF, nb, D_POOL), _f32),
            jax.ShapeDtypeStruct((1, nb, CONV_HIST, D_CONV), _f32),
            jax.ShapeDtypeStruct((1, nb, CONV_HIST, 2 * D_FF), _f32),
        ] + [jax.ShapeDtypeStruct(a.shape[1:], _bf16) for a in big],
        scratch_shapes=[
            pltpu.VMEM((n_slabs, nb * PROMPT_PITCH, LANES), _f32),
            pltpu.VMEM((D_POOL // LANES, POOL_HALO * nb, LANES), _f32),
            pltpu.VMEM((D_CONV // LANES, CONV_HIST * nb, LANES), _f32),
            pltpu.VMEM((2 * D_FF // LANES, CONV_HIST * nb, LANES), _f32),
            pltpu.VMEM((m, D_MODEL), _bf16),
            pltpu.VMEM((m, D_MODEL), _f32),
            pltpu.VMEM((m, D_FF), _bf16),
        ] + [pltpu.VMEM(a.shape[1:], _bf16) for a in big]
        + [pltpu.VMEM((WEIGHT_STAGE_SLOTS, WEIGHT_STAGE_ROWS[c], c), _f32) for c in stage_widths]
        + [pltpu.SemaphoreType.DMA((len(stage_widths), WEIGHT_STAGE_SLOTS)),
           pltpu.SemaphoreType.DMA((len(big),))],
        compiler_params=pltpu.CompilerParams(
            dimension_semantics=("arbitrary",),
            vmem_limit_bytes=V7X_VMEM_LIMIT_BYTES),
        name="prompt_trunk",
    )(x, mod, g, w_in, pool_w, pool_scale, conv_w, w_out, w_up, ffn_conv_w, w_down)


def _step_rows(n_blocks, n_steps, k, c, nb):
    return pl.ds(n_steps * c + k, nb, stride=n_steps * n_blocks)


def _sample_kernel(*refs, nb, nt):
    (x_hbm, sp_hbm, sc_hbm, sf_hbm, mod_hbm, g_ref, w_in_hbm, pool_w_ref, pool_scale_ref,
     conv_w_ref, w_out_hbm, w_up_hbm, ffn_conv_w_ref, w_down_hbm) = refs[:14]
    out_hbm = refs[14:19]
    (x_ref, sp_ref, sc_ref, sf_ref, mod_ref, w_in_ref, w_out_ref, w_up_ref, w_down_ref) = refs[19:28]
    out_vmem = refs[28:33]
    y_ref, pool_out_ref, conv_out_ref, ffn_lo_ref, ffn_hi_ref = out_vmem
    h2_ref, f_ref, act_ref, in_sems, out_sems = refs[33:]
    w = (g_ref, w_in_ref, pool_w_ref, pool_scale_ref, conv_w_ref, w_out_ref, w_up_ref,
         ffn_conv_w_ref, w_down_ref)
    d_blocks, c_blocks = D_MODEL // LANES, D_CONV // LANES

    n_groups = -(-N_FF_CHUNKS // FF_DOWN_GROUP)

    def group_cols(gi, base):
        first = gi * FF_DOWN_GROUP
        n = min(FF_DOWN_GROUP, N_FF_CHUNKS - first)
        return pl.ds(base + first * FF_CHUNK, n * FF_CHUNK)

    in_order = [("x", x_hbm, x_ref), ("mod", mod_hbm, mod_ref), ("w_in", w_in_hbm, w_in_ref),
                ("pool", sp_hbm, sp_ref), ("conv", sc_hbm, sc_ref), ("w_out", w_out_hbm, w_out_ref)]
    for gi in range(n_groups):
        for half, base in enumerate((0, D_FF)):
            cols = group_cols(gi, base)
            in_order.append((f"w_up{gi}{half}", w_up_hbm.at[:, cols], w_up_ref.at[:, cols]))
        if gi == 0:
            in_order.append(("ffn", sf_hbm, sf_ref))
        rows = group_cols(gi, 0)
        in_order.append((f"w_down{gi}", w_down_hbm.at[rows, :], w_down_ref.at[rows, :]))
    fetch = {name: pltpu.make_async_copy(src, dst, in_sems.at[k])
             for k, (name, src, dst) in enumerate(in_order)}

    def wait_ffn_group(gi):
        for name in (f"w_up{gi}0", f"w_up{gi}1", f"w_down{gi}"):
            fetch[name].wait()

    out_names = ("y", "pool", "conv", "ffn_lo", "ffn_hi")
    send = {name: pltpu.make_async_copy(src, dst, out_sems.at[k])
            for k, (name, src, dst) in enumerate(zip(out_names, out_vmem, out_hbm))}
    for name, _, _ in in_order:
        fetch[name].start()

    def read_steps(ref, n_blocks, n_steps):
        return _row_cat([_lane_cat([ref[_step_rows(n_blocks, n_steps, k, c, nb), :]
                                    for c in range(n_blocks)]) for k in range(n_steps)])

    def write_steps(ref, n_blocks, n_steps, rows):
        for k in range(n_steps):
            for c in range(n_blocks):
                ref[_step_rows(n_blocks, n_steps, k, c, nb), :] = (
                    rows[k * nb:(k + 1) * nb, c * LANES:(c + 1) * LANES])

    def hist_ffn(cols):
        return _row_cat([sf_ref[:, k * 2 * D_FF + cols.start:k * 2 * D_FF + cols.stop]
                         for k in range(CONV_HIST)])

    def sink_pool(ext):
        first = ext.shape[0] - POOL_BUF * nb
        for k in range(POOL_BUF):
            pool_out_ref[k] = ext[first + k * nb:first + (k + 1) * nb]

    def sink_conv(ext):
        write_steps(conv_out_ref, c_blocks, CONV_HIST, ext[ext.shape[0] - CONV_HIST * nb:])

    def sink_ffn(cols, ext):
        last = ext[ext.shape[0] - CONV_HIST * nb:]
        hi_blocks = 2 * D_FF // LANES - FFN_LO_BLOCKS
        for q in range(FF_CHUNK // LANES):
            block = cols.start // LANES + q
            if block < FFN_LO_BLOCKS:
                ref, n_blocks = ffn_lo_ref, FFN_LO_BLOCKS
            else:
                ref, n_blocks, block = ffn_hi_ref, hi_blocks, block - FFN_LO_BLOCKS
            for k in range(CONV_HIST):
                ref[_step_rows(n_blocks, CONV_HIST, k, block, nb), :] = (
                    last[k * nb:(k + 1) * nb, q * LANES:(q + 1) * LANES])

    fetch["x"].wait()
    fetch["mod"].wait()
    x = read_steps(x_ref, d_blocks, nt)
    mods = _fold_mods(mod_ref[...], g_ref, nt)
    h = _front_head(x, mods)
    mixed = {}
    front = _front_stages(
        x, h, mods, w, mixed, nb=nb,
        cnt_fn=lambda win: float(min(win, PAST_LEN + 1)),
        hist_pool=lambda: _row_cat([sp_ref[k] for k in range(POOL_BUF)]),
        hist_conv=lambda: read_steps(sc_ref, c_blocks, CONV_HIST),
        sink_pool=sink_pool, sink_conv=sink_conv)
    fetch["w_in"].wait()
    next(front)
    fetch["pool"].wait()
    fetch["conv"].wait()
    next(front)
    send["pool"].start()
    send["conv"].start()
    fetch["w_out"].wait()
    next(front)
    next(front)
    h2_ref[...] = mixed["h2"]

    fetch["ffn"].wait()
    wait_ffn_group(0)
    ups = _UpValues()
    for c in range(FFN_UPS_AHEAD):
        _ffn_issue(w, h2_ref, ups, c)
    ffn = _ffn_stages(w, h2_ref, f_ref, act_ref, ups, nb=nb, hist_ffn=hist_ffn, sink_ffn=sink_ffn)
    for c in range(N_FF_CHUNKS):
        ahead = c + FFN_UPS_AHEAD
        if ahead < N_FF_CHUNKS and ahead % FF_DOWN_GROUP == 0:
            wait_ffn_group(ahead // FF_DOWN_GROUP)
        next(ffn)
    send["ffn_lo"].start()
    send["ffn_hi"].start()
    write_steps(y_ref, d_blocks, nt, _finish(mixed["x1"], f_ref, mods[5]))
    send["y"].start()
    for name in out_names:
        send[name].wait()


def _sample(x, sp, sc, sf, mod, g, w_in, pool_w, pool_scale, conv_w, w_out, w_up, ffn_conv_w, w_down,
            *, nb, nt):
    rows_per_block = nb * CONV_HIST
    hi_blocks = 2 * D_FF // LANES - FFN_LO_BLOCKS
    hbm = pl.BlockSpec(memory_space=pl.ANY)
    vmem = pl.BlockSpec(memory_space=pltpu.VMEM)
    copied_in = (x, sp, sc, sf, mod, w_in, w_out, w_up, w_down)
    out_shape = [
        jax.ShapeDtypeStruct(x.shape, _f32),
        jax.ShapeDtypeStruct(sp.shape, _f32),
        jax.ShapeDtypeStruct(sc.shape, _f32),
        jax.ShapeDtypeStruct((FFN_LO_BLOCKS * rows_per_block, LANES), _f32),
        jax.ShapeDtypeStruct((hi_blocks * rows_per_block, LANES), _f32),
    ]
    return pl.pallas_call(
        functools.partial(_sample_kernel, nb=nb, nt=nt),
        in_specs=[hbm, hbm, hbm, hbm, hbm, vmem, hbm, vmem, vmem, vmem, hbm, hbm, vmem, hbm],
        out_specs=[hbm] * len(out_shape),
        out_shape=out_shape,
        scratch_shapes=[pltpu.VMEM(a.shape, a.dtype) for a in copied_in]
        + [pltpu.VMEM(o.shape, o.dtype) for o in out_shape]
        + [
            pltpu.VMEM((nt * nb, D_MODEL), _bf16),
            pltpu.VMEM((nt * nb, D_MODEL), _f32),
            pltpu.VMEM((nt * nb, D_FF), _bf16),
            pltpu.SemaphoreType.DMA((SAMPLE_FETCHES,)),
            pltpu.SemaphoreType.DMA((len(out_shape),)),
        ],
        compiler_params=pltpu.CompilerParams(vmem_limit_bytes=V7X_VMEM_LIMIT_BYTES),
        name="sample_trunk",
    )(x, sp, sc, sf, mod, g, w_in, pool_w, pool_scale, conv_w, w_out, w_up, ffn_conv_w, w_down)


def _to_block_major(a, n_steps):
    nb, _, width = a.shape
    return (a.reshape(nb, n_steps, width // LANES, LANES).transpose(0, 2, 1, 3)
            .reshape(nb * (width // LANES) * n_steps, LANES))


def _from_block_major(a, nb, n_steps):
    blocks = a.shape[0] // (nb * n_steps)
    return (a.reshape(nb, blocks, n_steps, LANES).transpose(0, 2, 1, 3)
            .reshape(nb, n_steps, blocks * LANES))


def kernel(x_prompt, x_sample, state_pool, state_conv, state_ffn, c_prompt, c_sample, w_ada, b_ada,
           g_pre_mix, g_post_mix, g_pre_ffn, g_post_ffn, w_in, pool_w, pool_scale, conv_w, w_out,
           ffn_w_up, ffn_conv_w, ffn_w_down):
    assert w_ada.shape[0] == 1, "single trunk layer"
    nbs, nts, _ = x_sample.shape
    mod_p, mod_s = _ada(c_prompt, c_sample, w_ada[0], b_ada)
    g = jnp.concatenate([g_pre_mix, g_post_mix, g_pre_ffn, g_post_ffn], axis=0)
    y_p, pool_p, conv_p, ffn_p, w_in_bf, w_out_bf, w_up_bf, w_down_bf = _prompt(
        x_prompt, mod_p, g, w_in, pool_w[0], pool_scale, conv_w[0], w_out, ffn_w_up, ffn_conv_w[0],
        ffn_w_down)
    weights = (g, w_in_bf, pool_w[0], pool_scale, conv_w[0], w_out_bf, w_up_bf, ffn_conv_w[0],
               w_down_bf)

    y_s, pool_s, conv_s, ffn_lo, ffn_hi = _sample(
        _to_block_major(x_sample, nts),
        state_pool[0].transpose(1, 0, 2),
        _to_block_major(state_conv[0], CONV_HIST),
        state_ffn[0].reshape(nbs, CONV_HIST * 2 * D_FF),
        mod_s, *weights, nb=nbs, nt=nts)
    ffn_s = jnp.concatenate([_from_block_major(ffn_lo, nbs, CONV_HIST),
                             _from_block_major(ffn_hi, nbs, CONV_HIST)], axis=-1)

    return (y_p, _from_block_major(y_s, nbs, nts), pool_p.transpose(1, 0, 2)[None], conv_p, ffn_p,
            pool_s.transpose(1, 0, 2)[None],
            _from_block_major(conv_s, nbs, CONV_HIST)[None],
            ffn_s[None])
```

```python
import functools

import jax
import jax.numpy as jnp
from jax import lax
from jax.experimental import pallas as pl
from jax.experimental.pallas import tpu as pltpu

D_MODEL = 1024
D_POOL = 512
D_CONV = 512
POOL_WINDOWS = (2, 4, 8, 16)
POOL_GC = 128
POOL_BUF = 15
CONV_W = 3
CONV_HIST = CONV_W - 1
D_FF = 2816
D_IN_PROJ = D_POOL + 3 * D_CONV
RMS_EPS = 1e-6
PAST_LEN = 16384

V7X_VMEM_LIMIT_BYTES = 58 * 1024 * 1024
SUBLANES = 8
LANES = 128
FF_CHUNK = 256
N_FF_CHUNKS = D_FF // FF_CHUNK
FFN_UPS_AHEAD = 1
FF_DOWN_GROUP = 3
SAMPLE_FETCHES = 7 + 3 * -(-N_FF_CHUNKS // FF_DOWN_GROUP)
SAMPLE_FFN_OUT_SLOTS = 4
SAMPLE_SENDS = 3 + SAMPLE_FFN_OUT_SLOTS
PROMPT_TT = 64
PROMPT_PITCH = PROMPT_TT + SUBLANES
WEIGHT_STAGE_ROWS = {2048: 128, 1024: 256, 5632: 64}
WEIGHT_STAGE_SLOTS = 4
ADA_TN = 1024
POOL_HALO = 16

_bf16 = jnp.bfloat16
_f32 = jnp.float32


def _dot(a, b):
    return jnp.dot(a, b, preferred_element_type=_f32)


def _rms_scaled(x, scale):
    ms = jnp.mean(x * x, axis=-1, keepdims=True)
    return x * lax.rsqrt(ms + RMS_EPS) * scale


def _silu(a):
    return a * jax.nn.sigmoid(a)


def _lane_cat(parts):
    return jnp.concatenate(parts, axis=1)


def _row_cat(parts):
    return jnp.concatenate(parts, axis=0)


def _ada_kernel(cp_ref, cs_ref, w_hbm, b_ref, mp_ref, ms_ref, w_buf, sems):
    n_blocks = w_buf.shape[0]
    tn = w_buf.shape[2]

    def fetch(k):
        return pltpu.make_async_copy(w_hbm.at[:, pl.ds(k * tn, tn)], w_buf.at[k], sems.at[k])

    for k in range(n_blocks):
        fetch(k).start()
    c = _silu(_row_cat([cp_ref[...], cs_ref[...]])).astype(_bf16)
    bp = cp_ref.shape[0]
    for k in range(n_blocks):
        cols = slice(k * tn, (k + 1) * tn)
        fetch(k).wait()
        mod = _dot(c, w_buf[k].astype(_bf16)) + b_ref[:, cols]
        mp_ref[:, cols] = mod[:bp]
        ms_ref[:, cols] = mod[bp:]


def _ada(c_prompt, c_sample, w_ada, b_ada):
    n = w_ada.shape[1]
    bp, bs = c_prompt.shape[0], c_sample.shape[0]
    vmem = pl.BlockSpec(memory_space=pltpu.VMEM)
    return pl.pallas_call(
        _ada_kernel,
        in_specs=[vmem, vmem, pl.BlockSpec(memory_space=pl.ANY), vmem],
        out_specs=[vmem, vmem],
        out_shape=[
            jax.ShapeDtypeStruct((bp, n), _f32),
            jax.ShapeDtypeStruct((bs, n), _f32),
        ],
        scratch_shapes=[
            pltpu.VMEM((n // ADA_TN, D_MODEL, ADA_TN), _f32),
            pltpu.SemaphoreType.DMA((n // ADA_TN,)),
        ],
        compiler_params=pltpu.CompilerParams(vmem_limit_bytes=V7X_VMEM_LIMIT_BYTES),
        name="ada_mod",
    )(c_prompt, c_sample, w_ada, b_ada)


def _front_head(x, mods):
    return (_rms_scaled(x, mods[1]) + mods[0]).astype(_bf16)


def _front_stages(x, h, mods, w, out, *, nb, cnt_fn, hist_pool, hist_conv, sink_pool, sink_conv):
    (_, w_in_ref, pool_w_ref, pool_scale_ref, conv_w_ref, w_out_ref) = w[:6]
    _, _, gate1, sh2, scale2, _ = mods
    m = x.shape[0]

    proj = _dot(h[...], w_in_ref[...])
    yield
    v_pool = proj[:, :D_POOL]
    x_conv = proj[:, D_POOL:D_POOL + D_CONV]
    gate_b = proj[:, D_POOL + D_CONV:D_POOL + 2 * D_CONV]
    gate_c = proj[:, D_POOL + 2 * D_CONV:]

    pool_ext = _row_cat([hist_pool(), v_pool])
    halo = (pool_ext.shape[0] - m) // nb
    y_pool = []
    for gi, win in enumerate(POOL_WINDOWS):
        sl = slice(gi * POOL_GC, (gi + 1) * POOL_GC)
        s = pool_ext[:, sl]
        step = 1
        while step < win:
            n = s.shape[0]
            s = s[step * nb:] + s[:n - step * nb]
            step *= 2
        first = (halo - (win - 1)) * nb
        d = s[first:first + m] / cnt_fn(win) - v_pool[:, sl]
        y_pool.append(_dot(d.astype(_bf16), pool_w_ref[gi].astype(_bf16)))
    y_pool = _lane_cat(y_pool) * pool_scale_ref[...]
    sink_pool(pool_ext)

    cx = gate_c * x_conv
    conv_ext = _row_cat([hist_conv(), cx])
    cw = conv_w_ref[...]
    conv = conv_ext[0:m] * cw[0:1] + conv_ext[nb:nb + m] * cw[1:2] + cx * cw[2:3]
    y_conv = gate_b * conv
    sink_conv(conv_ext)
    mixed = _lane_cat([y_pool, y_conv]).astype(_bf16)
    yield
    mix = _dot(mixed, w_out_ref[...])
    yield
    x1 = x + _rms_scaled(mix, gate1)
    out["x1"] = x1
    out["h2"] = (_rms_scaled(x1, scale2) + sh2).astype(_bf16)
    yield


class _UpValues:
    def __init__(self):
        self._v = {}

    def put(self, c, half, val):
        self._v[c, half] = val

    def get(self, c, half):
        return self._v.pop((c, half))


def _ffn_cols(c):
    return [slice(base + c * FF_CHUNK, base + (c + 1) * FF_CHUNK) for base in (0, D_FF)]


def _ffn_issue(w, h2_ref, ups, c):
    for half, cols in enumerate(_ffn_cols(c)):
        ups.put(c, half, _dot(h2_ref[...], w[6][:, cols]))


def _ffn_stages(w, h2_ref, f_ref, act_ref, ups, *, nb, hist_ffn, sink_ffn):
    _, ffn_conv_w_ref, w_down_ref = w[6:]
    m = h2_ref.shape[0]
    group_start = 0
    for c in range(N_FF_CHUNKS):
        if c + FFN_UPS_AHEAD < N_FF_CHUNKS:
            _ffn_issue(w, h2_ref, ups, c + FFN_UPS_AHEAD)
        halves = []
        for half, cols in enumerate(_ffn_cols(c)):
            up = ups.get(c, half)
            up_ext = _row_cat([hist_ffn(cols), up])
            fw = ffn_conv_w_ref[:, cols]
            halves.append(up_ext[0:m] * fw[0:1] + up_ext[nb:nb + m] * fw[1:2] + up * fw[2:3])
            sink_ffn(cols, up_ext)
        act_ref[:, c * FF_CHUNK:(c + 1) * FF_CHUNK] = (_silu(halves[0]) * halves[1]).astype(_bf16)
        if (c + 1) % FF_DOWN_GROUP == 0 or c + 1 == N_FF_CHUNKS:
            rows = slice(group_start * FF_CHUNK, (c + 1) * FF_CHUNK)
            contrib = _dot(act_ref[:, rows], w_down_ref[rows, :])
            if group_start == 0:
                f_ref[...] = contrib
            else:
                f_ref[...] += contrib
            group_start = c + 1
        yield


def _finish(x1, f_ref, gate2):
    return x1 + _rms_scaled(f_ref[...], gate2)


def _fold_mods(mod, g_ref, n_steps):
    g = g_ref[...]
    sh1, sc1, gt1, sh2, sc2, gt2 = [mod[:, i * D_MODEL:(i + 1) * D_MODEL] for i in range(6)]
    folded = [sh1, g[0:1] * (1.0 + sc1), gt1 * g[1:2], sh2, g[2:3] * (1.0 + sc2), gt2 * g[3:4]]
    return [_row_cat([a] * n_steps) for a in folded]


def _trunk(x, mods, w, h2_ref, f_ref, act_ref, *, nb, cnt_fn, hist_pool, hist_conv, hist_ffn,
           sink_pool, sink_conv, sink_ffn):
    out = {}
    for _ in _front_stages(x, _front_head(x, mods), mods, w, out, nb=nb, cnt_fn=cnt_fn,
                           hist_pool=hist_pool, hist_conv=hist_conv, sink_pool=sink_pool,
                           sink_conv=sink_conv):
        pass
    h2_ref[...] = out["h2"]
    ups = _UpValues()
    for c in range(FFN_UPS_AHEAD):
        _ffn_issue(w, h2_ref, ups, c)
    for _ in _ffn_stages(w, h2_ref, f_ref, act_ref, ups, nb=nb, hist_ffn=hist_ffn,
                         sink_ffn=sink_ffn):
        pass
    return _finish(out["x1"], f_ref, mods[5])


def _stream_cast(src, dst, stage, sems, sem_row):
    slots, rows = stage.shape[0], stage.shape[1]
    n = src.shape[0] // rows
    ahead = slots - 1

    def chunk(i, slot):
        return pltpu.make_async_copy(src.at[pl.ds(i * rows, rows), :], stage.at[slot],
                                     sems.at[sem_row, slot])

    for i in range(min(ahead, n)):
        chunk(i, i).start()

    def body(i, carry):
        slot = lax.rem(i, slots)

        @pl.when(i + ahead < n)
        def _():
            chunk(i + ahead, lax.rem(i + ahead, slots)).start()

        chunk(i, slot).wait()
        dst[pl.ds(pl.multiple_of(i * rows, rows), rows), :] = stage[slot].astype(_bf16)
        return carry

    lax.fori_loop(0, n, body, 0)


def _prompt_kernel(x_ref, mod_ref, *rest, nb, tt, n_steps):
    (g_ref, w_in_hbm, pool_w_ref, pool_scale_ref, conv_w_ref, w_out_hbm, w_up_hbm, ffn_conv_w_ref,
     w_down_hbm) = rest[:9]
    y_ref, pool_out_ref, conv_out_ref, ffn_out_ref = rest[9:13]
    bf_out = rest[13:17]
    (xs, pool_carry, conv_carry, ffn_carry, h2_ref, f_ref, act_ref,
     w_in_bf, w_out_bf, w_up_bf, w_down_bf, stage_in, stage_sq, stage_up,
     stage_sems, out_sems) = rest[17:]
    bf_vmem = (w_in_bf, w_out_bf, w_up_bf, w_down_bf)
    w = (g_ref, w_in_bf, pool_w_ref, pool_scale_ref, conv_w_ref, w_out_bf, w_up_bf, ffn_conv_w_ref,
         w_down_bf)
    j = pl.program_id(0)

    def bf_writeback(k):
        return pltpu.make_async_copy(bf_vmem[k], bf_out[k], out_sems.at[k])

    pitch = xs.shape[1] // nb
    n_slabs = D_MODEL // LANES
    m = tt * nb

    @pl.when(j == 0)
    def _():
        pool_carry[...] = jnp.zeros_like(pool_carry)
        conv_carry[...] = jnp.zeros_like(conv_carry)
        ffn_carry[...] = jnp.zeros_like(ffn_carry)
        _stream_cast(w_in_hbm.at[0], w_in_bf, stage_in, stage_sems, 0)
        _stream_cast(w_out_hbm.at[0], w_out_bf, stage_sq, stage_sems, 1)
        _stream_cast(w_up_hbm.at[0], w_up_bf, stage_up, stage_sems, 2)
        _stream_cast(w_down_hbm.at[0], w_down_bf, stage_sq, stage_sems, 1)
        for k in range(len(bf_vmem)):
            bf_writeback(k).start()

    for b in range(nb):
        for c in range(n_slabs):
            xs[c, b * pitch:b * pitch + tt, :] = x_ref[b, :, c * LANES:(c + 1) * LANES]
    x = _row_cat([_lane_cat([xs[c, pl.ds(i, nb, stride=pitch), :] for c in range(n_slabs)])
                  for i in range(tt)])

    mods = _fold_mods(mod_ref[...], w[0], tt)

    t_idx = j * tt + lax.shift_right_logical(
        lax.broadcasted_iota(jnp.int32, (m, POOL_GC), 0), nb.bit_length() - 1)

    def cnt_fn(win):
        return jnp.minimum(win, t_idx + 1).astype(_f32)

    def carry_rows(ref, first, n):
        return _lane_cat([ref[first + q] for q in range(n)])

    def keep_rows(ref, first, ext):
        rows = ref.shape[1]
        for q in range(ext.shape[1] // LANES):
            ref[first + q] = ext[ext.shape[0] - rows:, q * LANES:(q + 1) * LANES]

    y = _trunk(
        x, mods, w, h2_ref, f_ref, act_ref, nb=nb, cnt_fn=cnt_fn,
        hist_pool=lambda: carry_rows(pool_carry, 0, D_POOL // LANES),
        hist_conv=lambda: carry_rows(conv_carry, 0, D_CONV // LANES),
        hist_ffn=lambda cols: carry_rows(ffn_carry, cols.start // LANES, FF_CHUNK // LANES),
        sink_pool=lambda ext: keep_rows(pool_carry, 0, ext),
        sink_conv=lambda ext: keep_rows(conv_carry, 0, ext),
        sink_ffn=lambda cols, ext: keep_rows(ffn_carry, cols.start // LANES, ext))

    for i in range(tt):
        for c in range(n_slabs):
            xs[c, pl.ds(i, nb, stride=pitch), :] = y[i * nb:(i + 1) * nb, c * LANES:(c + 1) * LANES]
    for b in range(nb):
        for c in range(n_slabs):
            y_ref[b, :, c * LANES:(c + 1) * LANES] = xs[c, b * pitch:b * pitch + tt, :]

    @pl.when(j == n_steps - 1)
    def _():
        def emit(out_ref, carry, n):
            first = carry.shape[1] // nb - n
            for b in range(nb):
                for q in range(carry.shape[0]):
                    out_ref[0, b, :, q * LANES:(q + 1) * LANES] = (
                        carry[q, pl.ds(first * nb + b, n, stride=nb), :])
        pool_rows = carry_rows(pool_carry, 0, D_POOL // LANES)
        for k in range(POOL_BUF):
            first = (POOL_HALO - POOL_BUF + k) * nb
            pool_out_ref[k] = pool_rows[first:first + nb]
        emit(conv_out_ref, conv_carry, CONV_HIST)
        emit(ffn_out_ref, ffn_carry, CONV_HIST)
        for k in range(len(bf_vmem)):
            bf_writeback(k).wait()


def _prompt(x, mod, g, w_in, pool_w, pool_scale, conv_w, w_out, w_up, ffn_conv_w, w_down):
    nb, seq, _ = x.shape
    assert nb == SUBLANES, "one vreg row per time step"
    tt = PROMPT_TT
    n_steps = seq // tt
    m = tt * nb
    n_slabs = D_MODEL // LANES
    big = (w_in, w_out, w_up, w_down)
    stage_widths = (D_IN_PROJ, D_MODEL, 2 * D_FF)
    in_hbm = pl.BlockSpec(memory_space=pl.ANY)

    def const_spec(a):
        nd = a.ndim
        return pl.BlockSpec(a.shape, lambda j: (0,) * nd, pipeline_mode=pl.Buffered(1))

    return pl.pallas_call(
        functools.partial(_prompt_kernel, nb=nb, tt=tt, n_steps=n_steps),
        grid=(n_steps,),
        in_specs=[pl.BlockSpec((nb, tt, D_MODEL), lambda j: (0, j, 0)), const_spec(mod),
                  const_spec(g), in_hbm, const_spec(pool_w), const_spec(pool_scale),
                  const_spec(conv_w), in_hbm, in_hbm, const_spec(ffn_conv_w), in_hbm],
        out_specs=[
            pl.BlockSpec((nb, tt, D_MODEL), lambda j: (0, j, 0)),
            pl.BlockSpec((POOL_BUF, nb, D_POOL), lambda j: (0, 0, 0)),
            pl.BlockSpec((1, nb, CONV_HIST, D_CONV), lambda j: (0, 0, 0, 0)),
            pl.BlockSpec((1, nb, CONV_HIST, 2 * D_FF), lambda j: (0, 0, 0, 0)),
        ] + [pl.BlockSpec(memory_space=pl.ANY) for _ in big],
        out_shape=[
            jax.ShapeDtypeStruct((nb, seq, D_MODEL), _f32),
            jax.ShapeDtypeStruct((POOL_BUF, nb, D_POOL), _f32),
            jax.ShapeDtypeStruct((1, nb, CONV_HIST, D_CONV), _f32),
            jax.ShapeDtypeStruct((1, nb, CONV_HIST, 2 * D_FF), _f32),
        ] + [jax.ShapeDtypeStruct(a.shape[1:], _bf16) for a in big],
        scratch_shapes=[
            pltpu.VMEM((n_slabs, nb * PROMPT_PITCH, LANES), _f32),
            pltpu.VMEM((D_POOL // LANES, POOL_HALO * nb, LANES), _f32),
            pltpu.VMEM((D_CONV // LANES, CONV_HIST * nb, LANES), _f32),
            pltpu.VMEM((2 * D_FF // LANES, CONV_HIST * nb, LANES), _f32),
            pltpu.VMEM((m, D_MODEL), _bf16),
            pltpu.VMEM((m, D_MODEL), _f32),
            pltpu.VMEM((m, D_FF), _bf16),
        ] + [pltpu.VMEM(a.shape[1:], _bf16) for a in big]
        + [pltpu.VMEM((WEIGHT_STAGE_SLOTS, WEIGHT_STAGE_ROWS[c], c), _f32) for c in stage_widths]
        + [pltpu.SemaphoreType.DMA((len(stage_widths), WEIGHT_STAGE_SLOTS)),
           pltpu.SemaphoreType.DMA((len(big),))],
        compiler_params=pltpu.CompilerParams(
            dimension_semantics=("arbitrary",),
            vmem_limit_bytes=V7X_VMEM_LIMIT_BYTES),
        name="prompt_trunk",
    )(x, mod, g, w_in, pool_w, pool_scale, conv_w, w_out, w_up, ffn_conv_w, w_down)


def _step_rows(n_blocks, n_steps, k, c, nb):
    return pl.ds(n_steps * c + k, nb, stride=n_steps * n_blocks)


def _sample_kernel(*refs, nb, nt):
    (x_hbm, sp_hbm, sc_hbm, sf_hbm, mod_hbm, g_ref, w_in_hbm, pool_w_ref, pool_scale_ref,
     conv_w_ref, w_out_hbm, w_up_hbm, ffn_conv_w_ref, w_down_hbm) = refs[:14]
    y_hbm, pool_hbm, conv_hbm, ffn_hbm = refs[14:18]
    (x_ref, sp_ref, sc_ref, sf_ref, mod_ref, w_in_ref, w_out_ref, w_up_ref, w_down_ref) = refs[18:27]
    y_ref, pool_out_ref, conv_out_ref, ffn_out_ref = refs[27:31]
    h2_ref, f_ref, act_ref, in_sems, out_sems = refs[31:]
    w = (g_ref, w_in_ref, pool_w_ref, pool_scale_ref, conv_w_ref, w_out_ref, w_up_ref,
         ffn_conv_w_ref, w_down_ref)
    d_blocks, c_blocks = D_MODEL // LANES, D_CONV // LANES

    n_groups = -(-N_FF_CHUNKS // FF_DOWN_GROUP)

    def group_cols(gi, base):
        first = gi * FF_DOWN_GROUP
        n = min(FF_DOWN_GROUP, N_FF_CHUNKS - first)
        return pl.ds(base + first * FF_CHUNK, n * FF_CHUNK)

    in_order = [("x", x_hbm, x_ref), ("mod", mod_hbm, mod_ref), ("w_in", w_in_hbm, w_in_ref),
                ("pool", sp_hbm, sp_ref), ("conv", sc_hbm, sc_ref), ("w_out", w_out_hbm, w_out_ref)]
    for gi in range(n_groups):
        for half, base in enumerate((0, D_FF)):
            cols = group_cols(gi, base)
            in_order.append((f"w_up{gi}{half}", w_up_hbm.at[:, cols], w_up_ref.at[:, cols]))
        if gi == 0:
            in_order.append(("ffn", sf_hbm, sf_ref))
        rows = group_cols(gi, 0)
        in_order.append((f"w_down{gi}", w_down_hbm.at[rows, :], w_down_ref.at[rows, :]))
    fetch = {name: pltpu.make_async_copy(src, dst, in_sems.at[k])
             for k, (name, src, dst) in enumerate(in_order)}

    def wait_ffn_group(gi):
        for name in (f"w_up{gi}0", f"w_up{gi}1", f"w_down{gi}"):
            fetch[name].wait()

    send = {name: pltpu.make_async_copy(src, dst, out_sems.at[k])
            for k, (name, src, dst) in enumerate((("y", y_ref, y_hbm), ("pool", pool_out_ref, pool_hbm),
                                                  ("conv", conv_out_ref, conv_hbm)))}
    ffn_sends = []

    def send_ffn(n, cols):
        slot = n % ffn_out_ref.shape[0]
        return pltpu.make_async_copy(ffn_out_ref.at[slot], ffn_hbm.at[0, :, :, cols],
                                     out_sems.at[3 + slot])

    for name, _, _ in in_order:
        fetch[name].start()

    def read_steps(ref, n_blocks, n_steps):
        return _row_cat([_lane_cat([ref[_step_rows(n_blocks, n_steps, k, c, nb), :]
                                    for c in range(n_blocks)]) for k in range(n_steps)])

    def write_steps(ref, n_blocks, n_steps, rows):
        for k in range(n_steps):
            for c in range(n_blocks):
                ref[_step_rows(n_blocks, n_steps, k, c, nb), :] = (
                    rows[k * nb:(k + 1) * nb, c * LANES:(c + 1) * LANES])

    def hist_ffn(cols):
        return _row_cat([sf_ref[:, k * 2 * D_FF + cols.start:k * 2 * D_FF + cols.stop]
                         for k in range(CONV_HIST)])

    def sink_pool(ext):
        first = ext.shape[0] - POOL_BUF * nb
        for k in range(POOL_BUF):
            pool_out_ref[k] = ext[first + k * nb:first + (k + 1) * nb]

    def sink_conv(ext):
        write_steps(conv_out_ref, c_blocks, CONV_HIST, ext[ext.shape[0] - CONV_HIST * nb:])

    def sink_ffn(cols, ext):
        n, slots = len(ffn_sends), ffn_out_ref.shape[0]
        if n >= slots:
            ffn_sends[n - slots].wait()
        first = ext.shape[0] - CONV_HIST * nb
        for k in range(CONV_HIST):
            ffn_out_ref[n % slots, :, k, :] = ext[first + k * nb:first + (k + 1) * nb]
        ffn_sends.append(send_ffn(n, pl.ds(cols.start, FF_CHUNK)))
        ffn_sends[n].start()

    fetch["x"].wait()
    fetch["mod"].wait()
    x = read_steps(x_ref, d_blocks, nt)
    mods = _fold_mods(mod_ref[...], g_ref, nt)
    h = _front_head(x, mods)
    mixed = {}
    front = _front_stages(
        x, h, mods, w, mixed, nb=nb,
        cnt_fn=lambda win: float(min(win, PAST_LEN + 1)),
        hist_pool=lambda: _row_cat([sp_ref[k] for k in range(POOL_BUF)]),
        hist_conv=lambda: read_steps(sc_ref, c_blocks, CONV_HIST),
        sink_pool=sink_pool, sink_conv=sink_conv)
    fetch["w_in"].wait()
    next(front)
    fetch["pool"].wait()
    fetch["conv"].wait()
    next(front)
    send["pool"].start()
    send["conv"].start()
    fetch["w_out"].wait()
    next(front)
    next(front)
    h2_ref[...] = mixed["h2"]

    fetch["ffn"].wait()
    wait_ffn_group(0)
    ups = _UpValues()
    for c in range(FFN_UPS_AHEAD):
        _ffn_issue(w, h2_ref, ups, c)
    ffn = _ffn_stages(w, h2_ref, f_ref, act_ref, ups, nb=nb, hist_ffn=hist_ffn, sink_ffn=sink_ffn)
    for c in range(N_FF_CHUNKS):
        ahead = c + FFN_UPS_AHEAD
        if ahead < N_FF_CHUNKS and ahead % FF_DOWN_GROUP == 0:
            wait_ffn_group(ahead // FF_DOWN_GROUP)
        next(ffn)
    write_steps(y_ref, d_blocks, nt, _finish(mixed["x1"], f_ref, mods[5]))
    send["y"].start()
    for copy in list(send.values()) + ffn_sends[-ffn_out_ref.shape[0]:]:
        copy.wait()


def _sample(x, sp, sc, sf, mod, g, w_in, pool_w, pool_scale, conv_w, w_out, w_up, ffn_conv_w, w_down,
            *, nb, nt):
    hbm = pl.BlockSpec(memory_space=pl.ANY)
    vmem = pl.BlockSpec(memory_space=pltpu.VMEM)
    copied_in = (x, sp, sc, sf, mod, w_in, w_out, w_up, w_down)
    out_shape = [
        jax.ShapeDtypeStruct(x.shape, _f32),
        jax.ShapeDtypeStruct(sp.shape, _f32),
        jax.ShapeDtypeStruct(sc.shape, _f32),
        jax.ShapeDtypeStruct((1, nb, CONV_HIST, 2 * D_FF), _f32),
    ]
    staged_out = [pltpu.VMEM(o.shape, o.dtype) for o in out_shape[:3]] + [
        pltpu.VMEM((SAMPLE_FFN_OUT_SLOTS, nb, CONV_HIST, FF_CHUNK), _f32)]
    return pl.pallas_call(
        functools.partial(_sample_kernel, nb=nb, nt=nt),
        in_specs=[hbm, hbm, hbm, hbm, hbm, vmem, hbm, vmem, vmem, vmem, hbm, hbm, vmem, hbm],
        out_specs=[hbm] * len(out_shape),
        out_shape=out_shape,
        scratch_shapes=[pltpu.VMEM(a.shape, a.dtype) for a in copied_in]
        + staged_out
        + [
            pltpu.VMEM((nt * nb, D_MODEL), _bf16),
            pltpu.VMEM((nt * nb, D_MODEL), _f32),
            pltpu.VMEM((nt * nb, D_FF), _bf16),
            pltpu.SemaphoreType.DMA((SAMPLE_FETCHES,)),
            pltpu.SemaphoreType.DMA((SAMPLE_SENDS,)),
        ],
        compiler_params=pltpu.CompilerParams(vmem_limit_bytes=V7X_VMEM_LIMIT_BYTES),
        name="sample_trunk",
    )(x, sp, sc, sf, mod, g, w_in, pool_w, pool_scale, conv_w, w_out, w_up, ffn_conv_w, w_down)


def _to_block_major(a, n_steps):
    nb, _, width = a.shape
    return (a.reshape(nb, n_steps, width // LANES, LANES).transpose(0, 2, 1, 3)
            .reshape(nb * (width // LANES) * n_steps, LANES))


def _from_block_major(a, nb, n_steps):
    blocks = a.shape[0] // (nb * n_steps)
    return (a.reshape(nb, blocks, n_steps, LANES).transpose(0, 2, 1, 3)
            .reshape(nb, n_steps, blocks * LANES))


def kernel(x_prompt, x_sample, state_pool, state_conv, state_ffn, c_prompt, c_sample, w_ada, b_ada,
           g_pre_mix, g_post_mix, g_pre_ffn, g_post_ffn, w_in, pool_w, pool_scale, conv_w, w_out,
           ffn_w_up, ffn_conv_w, ffn_w_down):
    assert w_ada.shape[0] == 1, "single trunk layer"
    nbs, nts, _ = x_sample.shape
    mod_p, mod_s = _ada(c_prompt, c_sample, w_ada[0], b_ada)
    g = jnp.concatenate([g_pre_mix, g_post_mix, g_pre_ffn, g_post_ffn], axis=0)
    y_p, pool_p, conv_p, ffn_p, w_in_bf, w_out_bf, w_up_bf, w_down_bf = _prompt(
        x_prompt, mod_p, g, w_in, pool_w[0], pool_scale, conv_w[0], w_out, ffn_w_up, ffn_conv_w[0],
        ffn_w_down)
    weights = (g, w_in_bf, pool_w[0], pool_scale, conv_w[0], w_out_bf, w_up_bf, ffn_conv_w[0],
               w_down_bf)

    y_s, pool_s, conv_s, ffn_s = _sample(
        _to_block_major(x_sample, nts),
        state_pool[0].transpose(1, 0, 2),
        _to_block_major(state_conv[0], CONV_HIST),
        state_ffn[0].reshape(nbs, CONV_HIST * 2 * D_FF),
        mod_s, *weights, nb=nbs, nt=nts)

    return (y_p, _from_block_major(y_s, nbs, nts), pool_p.transpose(1, 0, 2)[None], conv_p, ffn_p,
            pool_s.transpose(1, 0, 2)[None],
            _from_block_major(conv_s, nbs, CONV_HIST)[None],
            ffn_s)
```

```python
import functools

import jax
import jax.numpy as jnp
from jax import lax
from jax.experimental import pallas as pl
from jax.experimental.pallas import tpu as pltpu

D_MODEL = 1024
D_POOL = 512
D_CONV = 512
POOL_WINDOWS = (2, 4, 8, 16)
POOL_GC = 128
POOL_BUF = 15
CONV_W = 3
CONV_HIST = CONV_W - 1
D_FF = 2816
D_IN_PROJ = D_POOL + 3 * D_CONV
RMS_EPS = 1e-6
PAST_LEN = 16384

V7X_VMEM_LIMIT_BYTES = 58 * 1024 * 1024
SUBLANES = 8
LANES = 128
FF_CHUNK = 256
N_FF_CHUNKS = D_FF // FF_CHUNK
FFN_UPS_AHEAD = 1
FF_DOWN_GROUP = 3
SAMPLE_FETCHES = 7 + 3 * -(-N_FF_CHUNKS // FF_DOWN_GROUP)
SAMPLE_FFN_OUT_SLOTS = 4
SAMPLE_SENDS = 3 + SAMPLE_FFN_OUT_SLOTS
PROMPT_TT = 64
PROMPT_PITCH = PROMPT_TT + SUBLANES
WEIGHT_STAGE_ROWS = {2048: 128, 1024: 256, 5632: 64}
WEIGHT_STAGE_SLOTS = 4
ADA_ROWS = 256
POOL_HALO = 16

_bf16 = jnp.bfloat16
_f32 = jnp.float32


def _dot(a, b):
    return jnp.dot(a, b, preferred_element_type=_f32)


def _rms_scaled(x, scale):
    ms = jnp.mean(x * x, axis=-1, keepdims=True)
    return x * lax.rsqrt(ms + RMS_EPS) * scale


def _silu(a):
    return a * jax.nn.sigmoid(a)


def _lane_cat(parts):
    return jnp.concatenate(parts, axis=1)


def _row_cat(parts):
    return jnp.concatenate(parts, axis=0)


def _ada_kernel(cp_ref, cs_ref, w_hbm, b_ref, mp_ref, ms_ref, w_buf, sems):
    n_blocks, rows = w_buf.shape[0], w_buf.shape[1]

    def fetch(k):
        return pltpu.make_async_copy(w_hbm.at[pl.ds(k * rows, rows), :], w_buf.at[k], sems.at[k])

    for k in range(n_blocks):
        fetch(k).start()
    c = _silu(_row_cat([cp_ref[...], cs_ref[...]])).astype(_bf16)
    mod = b_ref[...]
    for k in range(n_blocks):
        fetch(k).wait()
        mod = mod + _dot(c[:, k * rows:(k + 1) * rows], w_buf[k].astype(_bf16))
    bp = cp_ref.shape[0]
    mp_ref[...] = mod[:bp]
    ms_ref[...] = mod[bp:]


def _ada(c_prompt, c_sample, w_ada, b_ada):
    n = w_ada.shape[1]
    bp, bs = c_prompt.shape[0], c_sample.shape[0]
    vmem = pl.BlockSpec(memory_space=pltpu.VMEM)
    return pl.pallas_call(
        _ada_kernel,
        in_specs=[vmem, vmem, pl.BlockSpec(memory_space=pl.ANY), vmem],
        out_specs=[vmem, vmem],
        out_shape=[
            jax.ShapeDtypeStruct((bp, n), _f32),
            jax.ShapeDtypeStruct((bs, n), _f32),
        ],
        scratch_shapes=[
            pltpu.VMEM((D_MODEL // ADA_ROWS, ADA_ROWS, n), _f32),
            pltpu.SemaphoreType.DMA((D_MODEL // ADA_ROWS,)),
        ],
        compiler_params=pltpu.CompilerParams(vmem_limit_bytes=V7X_VMEM_LIMIT_BYTES),
        name="ada_mod",
    )(c_prompt, c_sample, w_ada, b_ada)


def _front_head(x, mods):
    return (_rms_scaled(x, mods[1]) + mods[0]).astype(_bf16)


def _front_stages(x, h, mods, w, out, *, nb, cnt_fn, hist_pool, hist_conv, sink_pool, sink_conv):
    (_, w_in_ref, pool_w_ref, pool_scale_ref, conv_w_ref, w_out_ref) = w[:6]
    _, _, gate1, sh2, scale2, _ = mods
    m = x.shape[0]

    proj = _dot(h[...], w_in_ref[...])
    yield
    v_pool = proj[:, :D_POOL]
    x_conv = proj[:, D_POOL:D_POOL + D_CONV]
    gate_b = proj[:, D_POOL + D_CONV:D_POOL + 2 * D_CONV]
    gate_c = proj[:, D_POOL + 2 * D_CONV:]

    pool_ext = _row_cat([hist_pool(), v_pool])
    halo = (pool_ext.shape[0] - m) // nb
    y_pool = []
    for gi, win in enumerate(POOL_WINDOWS):
        sl = slice(gi * POOL_GC, (gi + 1) * POOL_GC)
        s = pool_ext[:, sl]
        step = 1
        while step < win:
            n = s.shape[0]
            s = s[step * nb:] + s[:n - step * nb]
            step *= 2
        first = (halo - (win - 1)) * nb
        d = s[first:first + m] / cnt_fn(win) - v_pool[:, sl]
        y_pool.append(_dot(d.astype(_bf16), pool_w_ref[gi].astype(_bf16)))
    y_pool = _lane_cat(y_pool) * pool_scale_ref[...]
    sink_pool(pool_ext)

    cx = gate_c * x_conv
    conv_ext = _row_cat([hist_conv(), cx])
    cw = conv_w_ref[...]
    conv = conv_ext[0:m] * cw[0:1] + conv_ext[nb:nb + m] * cw[1:2] + cx * cw[2:3]
    y_conv = gate_b * conv
    sink_conv(conv_ext)
    mixed = _lane_cat([y_pool, y_conv]).astype(_bf16)
    yield
    mix = _dot(mixed, w_out_ref[...])
    yield
    x1 = x + _rms_scaled(mix, gate1)
    out["x1"] = x1
    out["h2"] = (_rms_scaled(x1, scale2) + sh2).astype(_bf16)
    yield


class _UpValues:
    def __init__(self):
        self._v = {}

    def put(self, c, half, val):
        self._v[c, half] = val

    def get(self, c, half):
        return self._v.pop((c, half))


def _ffn_cols(c):
    return [slice(base + c * FF_CHUNK, base + (c + 1) * FF_CHUNK) for base in (0, D_FF)]


def _ffn_issue(w, h2_ref, ups, c):
    for half, cols in enumerate(_ffn_cols(c)):
        ups.put(c, half, _dot(h2_ref[...], w[6][:, cols]))


def _ffn_stages(w, h2_ref, f_ref, act_ref, ups, *, nb, hist_ffn, sink_ffn):
    _, ffn_conv_w_ref, w_down_ref = w[6:]
    m = h2_ref.shape[0]
    group_start = 0
    for c in range(N_FF_CHUNKS):
        if c + FFN_UPS_AHEAD < N_FF_CHUNKS:
            _ffn_issue(w, h2_ref, ups, c + FFN_UPS_AHEAD)
        halves = []
        for half, cols in enumerate(_ffn_cols(c)):
            up = ups.get(c, half)
            up_ext = _row_cat([hist_ffn(cols), up])
            fw = ffn_conv_w_ref[:, cols]
            halves.append(up_ext[0:m] * fw[0:1] + up_ext[nb:nb + m] * fw[1:2] + up * fw[2:3])
            sink_ffn(cols, up_ext)
        act_ref[:, c * FF_CHUNK:(c + 1) * FF_CHUNK] = (_silu(halves[0]) * halves[1]).astype(_bf16)
        if (c + 1) % FF_DOWN_GROUP == 0 or c + 1 == N_FF_CHUNKS:
            rows = slice(group_start * FF_CHUNK, (c + 1) * FF_CHUNK)
            contrib = _dot(act_ref[:, rows], w_down_ref[rows, :])
            if group_start == 0:
                f_ref[...] = contrib
            else:
                f_ref[...] += contrib
            group_start = c + 1
        yield


def _finish(x1, f_ref, gate2):
    return x1 + _rms_scaled(f_ref[...], gate2)


def _fold_mods(mod, g_ref, n_steps):
    g = g_ref[...]
    sh1, sc1, gt1, sh2, sc2, gt2 = [mod[:, i * D_MODEL:(i + 1) * D_MODEL] for i in range(6)]
    folded = [sh1, g[0:1] * (1.0 + sc1), gt1 * g[1:2], sh2, g[2:3] * (1.0 + sc2), gt2 * g[3:4]]
    return [_row_cat([a] * n_steps) for a in folded]


def _trunk(x, mods, w, h2_ref, f_ref, act_ref, *, nb, cnt_fn, hist_pool, hist_conv, hist_ffn,
           sink_pool, sink_conv, sink_ffn):
    out = {}
    for _ in _front_stages(x, _front_head(x, mods), mods, w, out, nb=nb, cnt_fn=cnt_fn,
                           hist_pool=hist_pool, hist_conv=hist_conv, sink_pool=sink_pool,
                           sink_conv=sink_conv):
        pass
    h2_ref[...] = out["h2"]
    ups = _UpValues()
    for c in range(FFN_UPS_AHEAD):
        _ffn_issue(w, h2_ref, ups, c)
    for _ in _ffn_stages(w, h2_ref, f_ref, act_ref, ups, nb=nb, hist_ffn=hist_ffn,
                         sink_ffn=sink_ffn):
        pass
    return _finish(out["x1"], f_ref, mods[5])


def _stream_cast(src, dst, stage, sems, sem_row):
    slots, rows = stage.shape[0], stage.shape[1]
    n = src.shape[0] // rows
    ahead = slots - 1

    def chunk(i, slot):
        return pltpu.make_async_copy(src.at[pl.ds(i * rows, rows), :], stage.at[slot],
                                     sems.at[sem_row, slot])

    for i in range(min(ahead, n)):
        chunk(i, i).start()

    def body(i, carry):
        slot = lax.rem(i, slots)

        @pl.when(i + ahead < n)
        def _():
            chunk(i + ahead, lax.rem(i + ahead, slots)).start()

        chunk(i, slot).wait()
        dst[pl.ds(pl.multiple_of(i * rows, rows), rows), :] = stage[slot].astype(_bf16)
        return carry

    lax.fori_loop(0, n, body, 0)


def _prompt_kernel(x_ref, mod_ref, *rest, nb, tt, n_steps):
    (g_ref, w_in_hbm, pool_w_ref, pool_scale_ref, conv_w_ref, w_out_hbm, w_up_hbm, ffn_conv_w_ref,
     w_down_hbm) = rest[:9]
    y_ref, pool_out_ref, conv_out_ref, ffn_out_ref = rest[9:13]
    bf_out = rest[13:17]
    (xs, pool_carry, conv_carry, ffn_carry, h2_ref, f_ref, act_ref,
     w_in_bf, w_out_bf, w_up_bf, w_down_bf, stage_in, stage_sq, stage_up,
     stage_sems, out_sems) = rest[17:]
    bf_vmem = (w_in_bf, w_out_bf, w_up_bf, w_down_bf)
    w = (g_ref, w_in_bf, pool_w_ref, pool_scale_ref, conv_w_ref, w_out_bf, w_up_bf, ffn_conv_w_ref,
         w_down_bf)
    j = pl.program_id(0)

    def bf_writeback(k):
        return pltpu.make_async_copy(bf_vmem[k], bf_out[k], out_sems.at[k])

    pitch = xs.shape[1] // nb
    n_slabs = D_MODEL // LANES
    m = tt * nb

    @pl.when(j == 0)
    def _():
        pool_carry[...] = jnp.zeros_like(pool_carry)
        conv_carry[...] = jnp.zeros_like(conv_carry)
        ffn_carry[...] = jnp.zeros_like(ffn_carry)
        _stream_cast(w_in_hbm.at[0], w_in_bf, stage_in, stage_sems, 0)
        _stream_cast(w_out_hbm.at[0], w_out_bf, stage_sq, stage_sems, 1)
        _stream_cast(w_up_hbm.at[0], w_up_bf, stage_up, stage_sems, 2)
        _stream_cast(w_down_hbm.at[0], w_down_bf, stage_sq, stage_sems, 1)
        for k in range(len(bf_vmem)):
            bf_writeback(k).start()

    for b in range(nb):
        for c in range(n_slabs):
            xs[c, b * pitch:b * pitch + tt, :] = x_ref[b, :, c * LANES:(c + 1) * LANES]
    x = _row_cat([_lane_cat([xs[c, pl.ds(i, nb, stride=pitch), :] for c in range(n_slabs)])
                  for i in range(tt)])

    mods = _fold_mods(mod_ref[...], w[0], tt)

    t_idx = j * tt + lax.shift_right_logical(
        lax.broadcasted_iota(jnp.int32, (m, POOL_GC), 0), nb.bit_length() - 1)

    def cnt_fn(win):
        return jnp.minimum(win, t_idx + 1).astype(_f32)

    def carry_rows(ref, first, n):
        return _lane_cat([ref[first + q] for q in range(n)])

    def keep_rows(ref, first, ext):
        rows = ref.shape[1]
        for q in range(ext.shape[1] // LANES):
            ref[first + q] = ext[ext.shape[0] - rows:, q * LANES:(q + 1) * LANES]

    y = _trunk(
        x, mods, w, h2_ref, f_ref, act_ref, nb=nb, cnt_fn=cnt_fn,
        hist_pool=lambda: carry_rows(pool_carry, 0, D_POOL // LANES),
        hist_conv=lambda: carry_rows(conv_carry, 0, D_CONV // LANES),
        hist_ffn=lambda cols: carry_rows(ffn_carry, cols.start // LANES, FF_CHUNK // LANES),
        sink_pool=lambda ext: keep_rows(pool_carry, 0, ext),
        sink_conv=lambda ext: keep_rows(conv_carry, 0, ext),
        sink_ffn=lambda cols, ext: keep_rows(ffn_carry, cols.start // LANES, ext))

    for i in range(tt):
        for c in range(n_slabs):
            xs[c, pl.ds(i, nb, stride=pitch), :] = y[i * nb:(i + 1) * nb, c * LANES:(c + 1) * LANES]
    for b in range(nb):
        for c in range(n_slabs):
            y_ref[b, :, c * LANES:(c + 1) * LANES] = xs[c, b * pitch:b * pitch + tt, :]

    @pl.when(j == n_steps - 1)
    def _():
        def emit(out_ref, carry, n):
            first = carry.shape[1] // nb - n
            for b in range(nb):
                for q in range(carry.shape[0]):
                    out_ref[0, b, :, q * LANES:(q + 1) * LANES] = (
                        carry[q, pl.ds(first * nb + b, n, stride=nb), :])
        pool_rows = carry_rows(pool_carry, 0, D_POOL // LANES)
        for k in range(POOL_BUF):
            first = (POOL_HALO - POOL_BUF + k) * nb
            pool_out_ref[k] = pool_rows[first:first + nb]
        emit(conv_out_ref, conv_carry, CONV_HIST)
        emit(ffn_out_ref, ffn_carry, CONV_HIST)
        for k in range(len(bf_vmem)):
            bf_writeback(k).wait()


def _prompt(x, mod, g, w_in, pool_w, pool_scale, conv_w, w_out, w_up, ffn_conv_w, w_down):
    nb, seq, _ = x.shape
    assert nb == SUBLANES, "one vreg row per time step"
    tt = PROMPT_TT
    n_steps = seq // tt
    m = tt * nb
    n_slabs = D_MODEL // LANES
    big = (w_in, w_out, w_up, w_down)
    stage_widths = (D_IN_PROJ, D_MODEL, 2 * D_FF)
    in_hbm = pl.BlockSpec(memory_space=pl.ANY)

    def const_spec(a):
        nd = a.ndim
        return pl.BlockSpec(a.shape, lambda j: (0,) * nd, pipeline_mode=pl.Buffered(1))

    return pl.pallas_call(
        functools.partial(_prompt_kernel, nb=nb, tt=tt, n_steps=n_steps),
        grid=(n_steps,),
        in_specs=[pl.BlockSpec((nb, tt, D_MODEL), lambda j: (0, j, 0)), const_spec(mod),
                  const_spec(g), in_hbm, const_spec(pool_w), const_spec(pool_scale),
                  const_spec(conv_w), in_hbm, in_hbm, const_spec(ffn_conv_w), in_hbm],
        out_specs=[
            pl.BlockSpec((nb, tt, D_MODEL), lambda j: (0, j, 0)),
            pl.BlockSpec((POOL_BUF, nb, D_POOL), lambda j: (0, 0, 0)),
            pl.BlockSpec((1, nb, CONV_HIST, D_CONV), lambda j: (0, 0, 0, 0)),
            pl.BlockSpec((1, nb, CONV_HIST, 2 * D_FF), lambda j: (0, 0, 0, 0)),
        ] + [pl.BlockSpec(memory_space=pl.ANY) for _ in big],
        out_shape=[
            jax.ShapeDtypeStruct((nb, seq, D_MODEL), _f32),
            jax.ShapeDtypeStruct((POOL_BUF, nb, D_POOL), _f32),
            jax.ShapeDtypeStruct((1, nb, CONV_HIST, D_CONV), _f32),
            jax.ShapeDtypeStruct((1, nb, CONV_HIST, 2 * D_FF), _f32),
        ] + [jax.ShapeDtypeStruct(a.shape[1:], _bf16) for a in big],
        scratch_shapes=[
            pltpu.VMEM((n_slabs, nb * PROMPT_PITCH, LANES), _f32),
            pltpu.VMEM((D_POOL // LANES, POOL_HALO * nb, LANES), _f32),
            pltpu.VMEM((D_CONV // LANES, CONV_HIST * nb, LANES), _f32),
            pltpu.VMEM((2 * D_FF // LANES, CONV_HIST * nb, LANES), _f32),
            pltpu.VMEM((m, D_MODEL), _bf16),
            pltpu.VMEM((m, D_MODEL), _f32),
            pltpu.VMEM((m, D_FF), _bf16),
        ] + [pltpu.VMEM(a.shape[1:], _bf16) for a in big]
        + [pltpu.VMEM((WEIGHT_STAGE_SLOTS, WEIGHT_STAGE_ROWS[c], c), _f32) for c in stage_widths]
        + [pltpu.SemaphoreType.DMA((len(stage_widths), WEIGHT_STAGE_SLOTS)),
           pltpu.SemaphoreType.DMA((len(big),))],
        compiler_params=pltpu.CompilerParams(
            dimension_semantics=("arbitrary",),
            vmem_limit_bytes=V7X_VMEM_LIMIT_BYTES),
        name="prompt_trunk",
    )(x, mod, g, w_in, pool_w, pool_scale, conv_w, w_out, w_up, ffn_conv_w, w_down)


def _step_rows(n_blocks, n_steps, k, c, nb):
    return pl.ds(n_steps * c + k, nb, stride=n_steps * n_blocks)


def _sample_kernel(*refs, nb, nt):
    (x_hbm, sp_hbm, sc_hbm, sf_hbm, mod_hbm, g_ref, w_in_hbm, pool_w_ref, pool_scale_ref,
     conv_w_ref, w_out_hbm, w_up_hbm, ffn_conv_w_ref, w_down_hbm) = refs[:14]
    y_hbm, pool_hbm, conv_hbm, ffn_hbm = refs[14:18]
    (x_ref, sp_ref, sc_ref, sf_ref, mod_ref, w_in_ref, w_out_ref, w_up_ref, w_down_ref) = refs[18:27]
    y_ref, pool_out_ref, conv_out_ref, ffn_out_ref = refs[27:31]
    h2_ref, f_ref, act_ref, in_sems, out_sems = refs[31:]
    w = (g_ref, w_in_ref, pool_w_ref, pool_scale_ref, conv_w_ref, w_out_ref, w_up_ref,
         ffn_conv_w_ref, w_down_ref)
    d_blocks, c_blocks = D_MODEL // LANES, D_CONV // LANES

    n_groups = -(-N_FF_CHUNKS // FF_DOWN_GROUP)

    def group_cols(gi, base):
        first = gi * FF_DOWN_GROUP
        n = min(FF_DOWN_GROUP, N_FF_CHUNKS - first)
        return pl.ds(base + first * FF_CHUNK, n * FF_CHUNK)

    in_order = [("x", x_hbm, x_ref), ("mod", mod_hbm, mod_ref), ("w_in", w_in_hbm, w_in_ref),
                ("pool", sp_hbm, sp_ref), ("conv", sc_hbm, sc_ref), ("w_out", w_out_hbm, w_out_ref)]
    for gi in range(n_groups):
        for half, base in enumerate((0, D_FF)):
            cols = group_cols(gi, base)
            in_order.append((f"w_up{gi}{half}", w_up_hbm.at[:, cols], w_up_ref.at[:, cols]))
        if gi == 0:
            in_order.append(("ffn", sf_hbm, sf_ref))
        rows = group_cols(gi, 0)
        in_order.append((f"w_down{gi}", w_down_hbm.at[rows, :], w_down_ref.at[rows, :]))
    fetch = {name: pltpu.make_async_copy(src, dst, in_sems.at[k])
             for k, (name, src, dst) in enumerate(in_order)}

    def wait_ffn_group(gi):
        for name in (f"w_up{gi}0", f"w_up{gi}1", f"w_down{gi}"):
            fetch[name].wait()

    send = {name: pltpu.make_async_copy(src, dst, out_sems.at[k])
            for k, (name, src, dst) in enumerate((("y", y_ref, y_hbm), ("pool", pool_out_ref, pool_hbm),
                                                  ("conv", conv_out_ref, conv_hbm)))}
    ffn_sends = []

    def send_ffn(n, cols):
        slot = n % ffn_out_ref.shape[0]
        return pltpu.make_async_copy(ffn_out_ref.at[slot], ffn_hbm.at[0, :, :, cols],
                                     out_sems.at[3 + slot])

    for name, _, _ in in_order:
        fetch[name].start()

    def read_steps(ref, n_blocks, n_steps):
        return _row_cat([_lane_cat([ref[_step_rows(n_blocks, n_steps, k, c, nb), :]
                                    for c in range(n_blocks)]) for k in range(n_steps)])

    def write_steps(ref, n_blocks, n_steps, rows):
        for k in range(n_steps):
            for c in range(n_blocks):
                ref[_step_rows(n_blocks, n_steps, k, c, nb), :] = (
                    rows[k * nb:(k + 1) * nb, c * LANES:(c + 1) * LANES])

    def hist_ffn(cols):
        return _row_cat([sf_ref[:, k * 2 * D_FF + cols.start:k * 2 * D_FF + cols.stop]
                         for k in range(CONV_HIST)])

    def sink_pool(ext):
        first = ext.shape[0] - POOL_BUF * nb
        for k in range(POOL_BUF):
            pool_out_ref[k] = ext[first + k * nb:first + (k + 1) * nb]

    def sink_conv(ext):
        write_steps(conv_out_ref, c_blocks, CONV_HIST, ext[ext.shape[0] - CONV_HIST * nb:])

    def sink_ffn(cols, ext):
        n, slots = len(ffn_sends), ffn_out_ref.shape[0]
        if n >= slots:
            ffn_sends[n - slots].wait()
        first = ext.shape[0] - CONV_HIST * nb
        for k in range(CONV_HIST):
            ffn_out_ref[n % slots, :, k, :] = ext[first + k * nb:first + (k + 1) * nb]
        ffn_sends.append(send_ffn(n, pl.ds(cols.start, FF_CHUNK)))
        ffn_sends[n].start()

    fetch["x"].wait()
    fetch["mod"].wait()
    x = read_steps(x_ref, d_blocks, nt)
    mods = _fold_mods(mod_ref[...], g_ref, nt)
    h = _front_head(x, mods)
    mixed = {}
    front = _front_stages(
        x, h, mods, w, mixed, nb=nb,
        cnt_fn=lambda win: float(min(win, PAST_LEN + 1)),
        hist_pool=lambda: _row_cat([sp_ref[k] for k in range(POOL_BUF)]),
        hist_conv=lambda: read_steps(sc_ref, c_blocks, CONV_HIST),
        sink_pool=sink_pool, sink_conv=sink_conv)
    fetch["w_in"].wait()
    next(front)
    fetch["pool"].wait()
    fetch["conv"].wait()
    next(front)
    send["pool"].start()
    send["conv"].start()
    fetch["w_out"].wait()
    next(front)
    next(front)
    h2_ref[...] = mixed["h2"]

    fetch["ffn"].wait()
    wait_ffn_group(0)
    ups = _UpValues()
    for c in range(FFN_UPS_AHEAD):
        _ffn_issue(w, h2_ref, ups, c)
    ffn = _ffn_stages(w, h2_ref, f_ref, act_ref, ups, nb=nb, hist_ffn=hist_ffn, sink_ffn=sink_ffn)
    for c in range(N_FF_CHUNKS):
        ahead = c + FFN_UPS_AHEAD
        if ahead < N_FF_CHUNKS and ahead % FF_DOWN_GROUP == 0:
            wait_ffn_group(ahead // FF_DOWN_GROUP)
        next(ffn)
    write_steps(y_ref, d_blocks, nt, _finish(mixed["x1"], f_ref, mods[5]))
    send["y"].start()
    for copy in list(send.values()) + ffn_sends[-ffn_out_ref.shape[0]:]:
        copy.wait()


def _sample(x, sp, sc, sf, mod, g, w_in, pool_w, pool_scale, conv_w, w_out, w_up, ffn_conv_w, w_down,
            *, nb, nt):
    hbm = pl.BlockSpec(memory_space=pl.ANY)
    vmem = pl.BlockSpec(memory_space=pltpu.VMEM)
    copied_in = (x, sp, sc, sf, mod, w_in, w_out, w_up, w_down)
    out_shape = [
        jax.ShapeDtypeStruct(x.shape, _f32),
        jax.ShapeDtypeStruct(sp.shape, _f32),
        jax.ShapeDtypeStruct(sc.shape, _f32),
        jax.ShapeDtypeStruct((1, nb, CONV_HIST, 2 * D_FF), _f32),
    ]
    staged_out = [pltpu.VMEM(o.shape, o.dtype) for o in out_shape[:3]] + [
        pltpu.VMEM((SAMPLE_FFN_OUT_SLOTS, nb, CONV_HIST, FF_CHUNK), _f32)]
    return pl.pallas_call(
        functools.partial(_sample_kernel, nb=nb, nt=nt),
        in_specs=[hbm, hbm, hbm, hbm, hbm, vmem, hbm, vmem, vmem, vmem, hbm, hbm, vmem, hbm],
        out_specs=[hbm] * len(out_shape),
        out_shape=out_shape,
        scratch_shapes=[pltpu.VMEM(a.shape, a.dtype) for a in copied_in]
        + staged_out
        + [
            pltpu.VMEM((nt * nb, D_MODEL), _bf16),
            pltpu.VMEM((nt * nb, D_MODEL), _f32),
            pltpu.VMEM((nt * nb, D_FF), _bf16),
            pltpu.SemaphoreType.DMA((SAMPLE_FETCHES,)),
            pltpu.SemaphoreType.DMA((SAMPLE_SENDS,)),
        ],
        compiler_params=pltpu.CompilerParams(vmem_limit_bytes=V7X_VMEM_LIMIT_BYTES),
        name="sample_trunk",
    )(x, sp, sc, sf, mod, g, w_in, pool_w, pool_scale, conv_w, w_out, w_up, ffn_conv_w, w_down)


def _to_block_major(a, n_steps):
    nb, _, width = a.shape
    return (a.reshape(nb, n_steps, width // LANES, LANES).transpose(0, 2, 1, 3)
            .reshape(nb * (width // LANES) * n_steps, LANES))


def _from_block_major(a, nb, n_steps):
    blocks = a.shape[0] // (nb * n_steps)
    return (a.reshape(nb, blocks, n_steps, LANES).transpose(0, 2, 1, 3)
            .reshape(nb, n_steps, blocks * LANES))


def kernel(x_prompt, x_sample, state_pool, state_conv, state_ffn, c_prompt, c_sample, w_ada, b_ada,
           g_pre_mix, g_post_mix, g_pre_ffn, g_post_ffn, w_in, pool_w, pool_scale, conv_w, w_out,
           ffn_w_up, ffn_conv_w, ffn_w_down):
    assert w_ada.shape[0] == 1, "single trunk layer"
    nbs, nts, _ = x_sample.shape
    mod_p, mod_s = _ada(c_prompt, c_sample, w_ada[0], b_ada)
    g = jnp.concatenate([g_pre_mix, g_post_mix, g_pre_ffn, g_post_ffn], axis=0)
    y_p, pool_p, conv_p, ffn_p, w_in_bf, w_out_bf, w_up_bf, w_down_bf = _prompt(
        x_prompt, mod_p, g, w_in, pool_w[0], pool_scale, conv_w[0], w_out, ffn_w_up, ffn_conv_w[0],
        ffn_w_down)
    weights = (g, w_in_bf, pool_w[0], pool_scale, conv_w[0], w_out_bf, w_up_bf, ffn_conv_w[0],
               w_down_bf)

    y_s, pool_s, conv_s, ffn_s = _sample(
        _to_block_major(x_sample, nts),
        state_pool[0].transpose(1, 0, 2),
        _to_block_major(state_conv[0], CONV_HIST),
        state_ffn[0].reshape(nbs, CONV_HIST * 2 * D_FF),
        mod_s, *weights, nb=nbs, nt=nts)

    return (y_p, _from_block_major(y_s, nbs, nts), pool_p.transpose(1, 0, 2)[None], conv_p, ffn_p,
            pool_s.transpose(1, 0, 2)[None],
            _from_block_major(conv_s, nbs, CONV_HIST)[None],
            ffn_s)
```

```python
import functools

import jax
import jax.numpy as jnp
from jax import lax
from jax.experimental import pallas as pl
from jax.experimental.pallas import tpu as pltpu

D_MODEL = 1024
D_POOL = 512
D_CONV = 512
POOL_WINDOWS = (2, 4, 8, 16)
POOL_GC = 128
POOL_BUF = 15
CONV_W = 3
CONV_HIST = CONV_W - 1
D_FF = 2816
D_IN_PROJ = D_POOL + 3 * D_CONV
RMS_EPS = 1e-6
PAST_LEN = 16384

V7X_VMEM_LIMIT_BYTES = 58 * 1024 * 1024
SUBLANES = 8
LANES = 128
FF_CHUNK = 256
N_FF_CHUNKS = D_FF // FF_CHUNK
FFN_UPS_AHEAD = 1
FF_DOWN_GROUP = 3
SAMPLE_FETCHES = 6 + 3 * -(-N_FF_CHUNKS // FF_DOWN_GROUP)
SAMPLE_FFN_IN_SLOTS = 4
SAMPLE_FFN_OUT_SLOTS = 4
SAMPLE_SENDS = 3 + SAMPLE_FFN_OUT_SLOTS
PROMPT_TT = 64
PROMPT_PITCH = PROMPT_TT + SUBLANES
WEIGHT_STAGE_ROWS = {2048: 128, 1024: 256, 5632: 64}
WEIGHT_STAGE_SLOTS = 4
ADA_ROWS = 256
POOL_HALO = 16

_bf16 = jnp.bfloat16
_f32 = jnp.float32


def _dot(a, b):
    return jnp.dot(a, b, preferred_element_type=_f32)


def _rms_scaled(x, scale):
    ms = jnp.mean(x * x, axis=-1, keepdims=True)
    return x * lax.rsqrt(ms + RMS_EPS) * scale


def _silu(a):
    return a * jax.nn.sigmoid(a)


def _lane_cat(parts):
    return jnp.concatenate(parts, axis=1)


def _row_cat(parts):
    return jnp.concatenate(parts, axis=0)


def _ada_kernel(cp_ref, cs_ref, w_hbm, b_ref, mp_ref, ms_ref, w_buf, sems):
    n_blocks, rows = w_buf.shape[0], w_buf.shape[1]

    def fetch(k):
        return pltpu.make_async_copy(w_hbm.at[pl.ds(k * rows, rows), :], w_buf.at[k], sems.at[k])

    for k in range(n_blocks):
        fetch(k).start()
    c = _silu(_row_cat([cp_ref[...], cs_ref[...]])).astype(_bf16)
    mod = b_ref[...]
    for k in range(n_blocks):
        fetch(k).wait()
        mod = mod + _dot(c[:, k * rows:(k + 1) * rows], w_buf[k].astype(_bf16))
    bp = cp_ref.shape[0]
    mp_ref[...] = mod[:bp]
    ms_ref[...] = mod[bp:]


def _ada(c_prompt, c_sample, w_ada, b_ada):
    n = w_ada.shape[1]
    bp, bs = c_prompt.shape[0], c_sample.shape[0]
    vmem = pl.BlockSpec(memory_space=pltpu.VMEM)
    return pl.pallas_call(
        _ada_kernel,
        in_specs=[vmem, vmem, pl.BlockSpec(memory_space=pl.ANY), vmem],
        out_specs=[vmem, vmem],
        out_shape=[
            jax.ShapeDtypeStruct((bp, n), _f32),
            jax.ShapeDtypeStruct((bs, n), _f32),
        ],
        scratch_shapes=[
            pltpu.VMEM((D_MODEL // ADA_ROWS, ADA_ROWS, n), _f32),
            pltpu.SemaphoreType.DMA((D_MODEL // ADA_ROWS,)),
        ],
        compiler_params=pltpu.CompilerParams(vmem_limit_bytes=V7X_VMEM_LIMIT_BYTES),
        name="ada_mod",
    )(c_prompt, c_sample, w_ada, b_ada)


def _taps(w_ref, width, cols):
    return [w_ref[:, k * width + cols.start:k * width + cols.stop] for k in range(CONV_W)]


def _front_head(x, mods):
    return (_rms_scaled(x, mods[1]) + mods[0]).astype(_bf16)


def _front_stages(x, h, mods, w, out, *, nb, cnt_fn, hist_pool, hist_conv, sink_pool, sink_conv):
    (_, w_in_ref, pool_w_ref, pool_scale_ref, conv_w_ref, w_out_ref) = w[:6]
    _, _, gate1, sh2, scale2, _ = mods
    m = x.shape[0]

    proj = _dot(h[...], w_in_ref[...])
    yield
    v_pool = proj[:, :D_POOL]
    x_conv = proj[:, D_POOL:D_POOL + D_CONV]
    gate_b = proj[:, D_POOL + D_CONV:D_POOL + 2 * D_CONV]
    gate_c = proj[:, D_POOL + 2 * D_CONV:]

    pool_ext = _row_cat([hist_pool(), v_pool])
    halo = (pool_ext.shape[0] - m) // nb
    y_pool = []
    for gi, win in enumerate(POOL_WINDOWS):
        sl = slice(gi * POOL_GC, (gi + 1) * POOL_GC)
        s = pool_ext[:, sl]
        step = 1
        while step < win:
            n = s.shape[0]
            s = s[step * nb:] + s[:n - step * nb]
            step *= 2
        first = (halo - (win - 1)) * nb
        d = s[first:first + m] / cnt_fn(win) - v_pool[:, sl]
        y_pool.append(_dot(d.astype(_bf16), pool_w_ref[gi].astype(_bf16)))
    y_pool = _lane_cat(y_pool) * pool_scale_ref[...]
    sink_pool(pool_ext)

    cx = gate_c * x_conv
    conv_ext = _row_cat([hist_conv(), cx])
    cw = _taps(conv_w_ref, D_CONV, slice(0, D_CONV))
    conv = conv_ext[0:m] * cw[0] + conv_ext[nb:nb + m] * cw[1] + cx * cw[2]
    y_conv = gate_b * conv
    sink_conv(conv_ext)
    mixed = _lane_cat([y_pool, y_conv]).astype(_bf16)
    yield
    mix = _dot(mixed, w_out_ref[...])
    yield
    x1 = x + _rms_scaled(mix, gate1)
    out["x1"] = x1
    out["h2"] = (_rms_scaled(x1, scale2) + sh2).astype(_bf16)
    yield


class _UpValues:
    def __init__(self):
        self._v = {}

    def put(self, c, half, val):
        self._v[c, half] = val

    def get(self, c, half):
        return self._v.pop((c, half))


def _ffn_cols(c):
    return [slice(base + c * FF_CHUNK, base + (c + 1) * FF_CHUNK) for base in (0, D_FF)]


def _ffn_issue(w, h2_ref, ups, c):
    for half, cols in enumerate(_ffn_cols(c)):
        ups.put(c, half, _dot(h2_ref[...], w[6][:, cols]))


def _ffn_stages(w, h2_ref, f_ref, act_ref, ups, *, nb, hist_ffn, sink_ffn):
    _, ffn_conv_w_ref, w_down_ref = w[6:]
    m = h2_ref.shape[0]
    group_start = 0
    for c in range(N_FF_CHUNKS):
        if c + FFN_UPS_AHEAD < N_FF_CHUNKS:
            _ffn_issue(w, h2_ref, ups, c + FFN_UPS_AHEAD)
        halves = []
        for half, cols in enumerate(_ffn_cols(c)):
            up = ups.get(c, half)
            up_ext = _row_cat([hist_ffn(cols), up])
            fw = _taps(ffn_conv_w_ref, 2 * D_FF, cols)
            halves.append(up_ext[0:m] * fw[0] + up_ext[nb:nb + m] * fw[1] + up * fw[2])
            sink_ffn(cols, up_ext)
        act_ref[:, c * FF_CHUNK:(c + 1) * FF_CHUNK] = (_silu(halves[0]) * halves[1]).astype(_bf16)
        if (c + 1) % FF_DOWN_GROUP == 0 or c + 1 == N_FF_CHUNKS:
            rows = slice(group_start * FF_CHUNK, (c + 1) * FF_CHUNK)
            contrib = _dot(act_ref[:, rows], w_down_ref[rows, :])
            if group_start == 0:
                f_ref[...] = contrib
            else:
                f_ref[...] += contrib
            group_start = c + 1
        yield


def _finish(x1, f_ref, gate2):
    return x1 + _rms_scaled(f_ref[...], gate2)


def _fold_mods(mod, g_refs, n_steps):
    g = [g_ref[...] for g_ref in g_refs]
    sh1, sc1, gt1, sh2, sc2, gt2 = [mod[:, i * D_MODEL:(i + 1) * D_MODEL] for i in range(6)]
    folded = [sh1, g[0] * (1.0 + sc1), gt1 * g[1], sh2, g[2] * (1.0 + sc2), gt2 * g[3]]
    return [_row_cat([a] * n_steps) for a in folded]


def _trunk(x, mods, w, h2_ref, f_ref, act_ref, *, nb, cnt_fn, hist_pool, hist_conv, hist_ffn,
           sink_pool, sink_conv, sink_ffn):
    out = {}
    for _ in _front_stages(x, _front_head(x, mods), mods, w, out, nb=nb, cnt_fn=cnt_fn,
                           hist_pool=hist_pool, hist_conv=hist_conv, sink_pool=sink_pool,
                           sink_conv=sink_conv):
        pass
    h2_ref[...] = out["h2"]
    ups = _UpValues()
    for c in range(FFN_UPS_AHEAD):
        _ffn_issue(w, h2_ref, ups, c)
    for _ in _ffn_stages(w, h2_ref, f_ref, act_ref, ups, nb=nb, hist_ffn=hist_ffn,
                         sink_ffn=sink_ffn):
        pass
    return _finish(out["x1"], f_ref, mods[5])


def _stream_cast(src, dst, stage, sems, sem_row):
    slots, rows = stage.shape[0], stage.shape[1]
    n = src.shape[0] // rows
    ahead = slots - 1

    def chunk(i, slot):
        return pltpu.make_async_copy(src.at[pl.ds(i * rows, rows), :], stage.at[slot],
                                     sems.at[sem_row, slot])

    for i in range(min(ahead, n)):
        chunk(i, i).start()

    def body(i, carry):
        slot = lax.rem(i, slots)

        @pl.when(i + ahead < n)
        def _():
            chunk(i + ahead, lax.rem(i + ahead, slots)).start()

        chunk(i, slot).wait()
        dst[pl.ds(pl.multiple_of(i * rows, rows), rows), :] = stage[slot].astype(_bf16)
        return carry

    lax.fori_loop(0, n, body, 0)


def _prompt_kernel(x_ref, mod_ref, *rest, nb, tt, n_steps):
    g_refs, rest = rest[:4], rest[4:]
    (w_in_hbm, pool_w_ref, pool_scale_ref, conv_w_ref, w_out_hbm, w_up_hbm, ffn_conv_w_ref,
     w_down_hbm) = rest[:8]
    y_ref, pool_out_ref, conv_out_ref, ffn_out_ref = rest[8:12]
    bf_out = rest[12:16]
    (xs, pool_carry, conv_carry, ffn_carry, h2_ref, f_ref, act_ref,
     w_in_bf, w_out_bf, w_up_bf, w_down_bf, stage_in, stage_sq, stage_up,
     stage_sems, out_sems) = rest[16:]
    bf_vmem = (w_in_bf, w_out_bf, w_up_bf, w_down_bf)
    w = (g_refs, w_in_bf, pool_w_ref, pool_scale_ref, conv_w_ref, w_out_bf, w_up_bf, ffn_conv_w_ref,
         w_down_bf)
    j = pl.program_id(0)

    def bf_writeback(k):
        return pltpu.make_async_copy(bf_vmem[k], bf_out[k], out_sems.at[k])

    pitch = xs.shape[1] // nb
    n_slabs = D_MODEL // LANES
    m = tt * nb

    @pl.when(j == 0)
    def _():
        pool_carry[...] = jnp.zeros_like(pool_carry)
        conv_carry[...] = jnp.zeros_like(conv_carry)
        ffn_carry[...] = jnp.zeros_like(ffn_carry)
        _stream_cast(w_in_hbm.at[0], w_in_bf, stage_in, stage_sems, 0)
        _stream_cast(w_out_hbm.at[0], w_out_bf, stage_sq, stage_sems, 1)
        _stream_cast(w_up_hbm.at[0], w_up_bf, stage_up, stage_sems, 2)
        _stream_cast(w_down_hbm.at[0], w_down_bf, stage_sq, stage_sems, 1)
        for k in range(len(bf_vmem)):
            bf_writeback(k).start()

    for b in range(nb):
        for c in range(n_slabs):
            xs[c, b * pitch:b * pitch + tt, :] = x_ref[b, :, c * LANES:(c + 1) * LANES]
    x = _row_cat([_lane_cat([xs[c, pl.ds(i, nb, stride=pitch), :] for c in range(n_slabs)])
                  for i in range(tt)])

    mods = _fold_mods(mod_ref[...], g_refs, tt)

    t_idx = j * tt + lax.shift_right_logical(
        lax.broadcasted_iota(jnp.int32, (m, POOL_GC), 0), nb.bit_length() - 1)

    def cnt_fn(win):
        return jnp.minimum(win, t_idx + 1).astype(_f32)

    def carry_rows(ref, first, n):
        return _lane_cat([ref[first + q] for q in range(n)])

    def keep_rows(ref, first, ext):
        rows = ref.shape[1]
        for q in range(ext.shape[1] // LANES):
            ref[first + q] = ext[ext.shape[0] - rows:, q * LANES:(q + 1) * LANES]

    y = _trunk(
        x, mods, w, h2_ref, f_ref, act_ref, nb=nb, cnt_fn=cnt_fn,
        hist_pool=lambda: carry_rows(pool_carry, 0, D_POOL // LANES),
        hist_conv=lambda: carry_rows(conv_carry, 0, D_CONV // LANES),
        hist_ffn=lambda cols: carry_rows(ffn_carry, cols.start // LANES, FF_CHUNK // LANES),
        sink_pool=lambda ext: keep_rows(pool_carry, 0, ext),
        sink_conv=lambda ext: keep_rows(conv_carry, 0, ext),
        sink_ffn=lambda cols, ext: keep_rows(ffn_carry, cols.start // LANES, ext))

    for i in range(tt):
        for c in range(n_slabs):
            xs[c, pl.ds(i, nb, stride=pitch), :] = y[i * nb:(i + 1) * nb, c * LANES:(c + 1) * LANES]
    for b in range(nb):
        for c in range(n_slabs):
            y_ref[b, :, c * LANES:(c + 1) * LANES] = xs[c, b * pitch:b * pitch + tt, :]

    @pl.when(j == n_steps - 1)
    def _():
        def emit(out_ref, carry, n):
            first = carry.shape[1] // nb - n
            for b in range(nb):
                for q in range(carry.shape[0]):
                    out_ref[0, b, :, q * LANES:(q + 1) * LANES] = (
                        carry[q, pl.ds(first * nb + b, n, stride=nb), :])
        pool_rows = carry_rows(pool_carry, 0, D_POOL // LANES)
        for k in range(POOL_BUF):
            first = (POOL_HALO - POOL_BUF + k) * nb
            pool_out_ref[k] = pool_rows[first:first + nb]
        emit(conv_out_ref, conv_carry, CONV_HIST)
        emit(ffn_out_ref, ffn_carry, CONV_HIST)
        for k in range(len(bf_vmem)):
            bf_writeback(k).wait()


def _prompt(x, mod, g, w_in, pool_w, pool_scale, conv_w, w_out, w_up, ffn_conv_w, w_down):
    nb, seq, _ = x.shape
    assert nb == SUBLANES, "one vreg row per time step"
    tt = PROMPT_TT
    n_steps = seq // tt
    m = tt * nb
    n_slabs = D_MODEL // LANES
    big = (w_in, w_out, w_up, w_down)
    stage_widths = (D_IN_PROJ, D_MODEL, 2 * D_FF)
    in_hbm = pl.BlockSpec(memory_space=pl.ANY)

    def const_spec(a):
        nd = a.ndim
        return pl.BlockSpec(a.shape, lambda j: (0,) * nd, pipeline_mode=pl.Buffered(1))

    return pl.pallas_call(
        functools.partial(_prompt_kernel, nb=nb, tt=tt, n_steps=n_steps),
        grid=(n_steps,),
        in_specs=[pl.BlockSpec((nb, tt, D_MODEL), lambda j: (0, j, 0)), const_spec(mod),
                  *[const_spec(a) for a in g], in_hbm, const_spec(pool_w), const_spec(pool_scale),
                  const_spec(conv_w), in_hbm, in_hbm, const_spec(ffn_conv_w), in_hbm],
        out_specs=[
            pl.BlockSpec((nb, tt, D_MODEL), lambda j: (0, j, 0)),
            pl.BlockSpec((POOL_BUF, nb, D_POOL), lambda j: (0, 0, 0)),
            pl.BlockSpec((1, nb, CONV_HIST, D_CONV), lambda j: (0, 0, 0, 0)),
            pl.BlockSpec((1, nb, CONV_HIST, 2 * D_FF), lambda j: (0, 0, 0, 0)),
        ] + [pl.BlockSpec(memory_space=pl.ANY) for _ in big],
        out_shape=[
            jax.ShapeDtypeStruct((nb, seq, D_MODEL), _f32),
            jax.ShapeDtypeStruct((POOL_BUF, nb, D_POOL), _f32),
            jax.ShapeDtypeStruct((1, nb, CONV_HIST, D_CONV), _f32),
            jax.ShapeDtypeStruct((1, nb, CONV_HIST, 2 * D_FF), _f32),
        ] + [jax.ShapeDtypeStruct(a.shape[1:], _bf16) for a in big],
        scratch_shapes=[
            pltpu.VMEM((n_slabs, nb * PROMPT_PITCH, LANES), _f32),
            pltpu.VMEM((D_POOL // LANES, POOL_HALO * nb, LANES), _f32),
            pltpu.VMEM((D_CONV // LANES, CONV_HIST * nb, LANES), _f32),
            pltpu.VMEM((2 * D_FF // LANES, CONV_HIST * nb, LANES), _f32),
            pltpu.VMEM((m, D_MODEL), _bf16),
            pltpu.VMEM((m, D_MODEL), _f32),
            pltpu.VMEM((m, D_FF), _bf16),
        ] + [pltpu.VMEM(a.shape[1:], _bf16) for a in big]
        + [pltpu.VMEM((WEIGHT_STAGE_SLOTS, WEIGHT_STAGE_ROWS[c], c), _f32) for c in stage_widths]
        + [pltpu.SemaphoreType.DMA((len(stage_widths), WEIGHT_STAGE_SLOTS)),
           pltpu.SemaphoreType.DMA((len(big),))],
        compiler_params=pltpu.CompilerParams(
            dimension_semantics=("arbitrary",),
            vmem_limit_bytes=V7X_VMEM_LIMIT_BYTES),
        name="prompt_trunk",
    )(x, mod, *g, w_in, pool_w, pool_scale, conv_w, w_out, w_up, ffn_conv_w, w_down)


def _step_rows(n_blocks, n_steps, k, c, nb):
    return pl.ds(n_steps * c + k, nb, stride=n_steps * n_blocks)


def _sample_kernel(*refs, nb, nt):
    refs = iter(refs)

    def take(n):
        return [next(refs) for _ in range(n)]

    x_hbm, sp_hbm, sc_hbm, sf_hbm, mod_hbm = take(5)
    g_refs = take(4)
    (w_in_hbm, pool_w_ref, pool_scale_ref, conv_w_ref, w_out_hbm, w_up_hbm, ffn_conv_w_ref,
     w_down_hbm) = take(8)
    y_hbm, pool_hbm, conv_hbm, ffn_hbm = take(4)
    (x_ref, sp_ref, sc_ref, mod_ref, w_in_ref, w_out_ref, w_up_ref, w_down_ref, sf_ref) = take(9)
    y_ref, pool_out_ref, conv_out_ref, ffn_out_ref = take(4)
    h2_ref, f_ref, act_ref, in_sems, sf_sems, out_sems = take(6)
    w = (g_refs, w_in_ref, pool_w_ref, pool_scale_ref, conv_w_ref, w_out_ref, w_up_ref,
         ffn_conv_w_ref, w_down_ref)
    d_blocks, c_blocks = D_MODEL // LANES, D_CONV // LANES

    n_groups = -(-N_FF_CHUNKS // FF_DOWN_GROUP)

    def group_cols(gi, base):
        first = gi * FF_DOWN_GROUP
        n = min(FF_DOWN_GROUP, N_FF_CHUNKS - first)
        return pl.ds(base + first * FF_CHUNK, n * FF_CHUNK)

    in_order = [("x", x_hbm, x_ref), ("mod", mod_hbm, mod_ref), ("w_in", w_in_hbm, w_in_ref),
                ("pool", sp_hbm, sp_ref), ("conv", sc_hbm, sc_ref), ("w_out", w_out_hbm, w_out_ref)]
    for gi in range(n_groups):
        for half, base in enumerate((0, D_FF)):
            cols = group_cols(gi, base)
            in_order.append((f"w_up{gi}{half}", w_up_hbm.at[:, cols], w_up_ref.at[:, cols]))
        rows = group_cols(gi, 0)
        in_order.append((f"w_down{gi}", w_down_hbm.at[rows, :], w_down_ref.at[rows, :]))
    fetch = {name: pltpu.make_async_copy(src, dst, in_sems.at[k])
             for k, (name, src, dst) in enumerate(in_order)}

    def wait_ffn_group(gi):
        for name in (f"w_up{gi}0", f"w_up{gi}1", f"w_down{gi}"):
            fetch[name].wait()

    send = {name: pltpu.make_async_copy(src, dst, out_sems.at[k])
            for k, (name, src, dst) in enumerate((("y", y_ref, y_hbm), ("pool", pool_out_ref, pool_hbm),
                                                  ("conv", conv_out_ref, conv_hbm)))}
    ffn_sends = []

    def send_ffn(n, cols):
        slot = n % ffn_out_ref.shape[0]
        return pltpu.make_async_copy(ffn_out_ref.at[slot], ffn_hbm.at[0, :, :, cols],
                                     out_sems.at[3 + slot])

    for name, _, _ in in_order:
        fetch[name].start()

    def read_steps(ref, n_blocks, n_steps):
        return _row_cat([_lane_cat([ref[_step_rows(n_blocks, n_steps, k, c, nb), :]
                                    for c in range(n_blocks)]) for k in range(n_steps)])

    def write_steps(ref, n_blocks, n_steps, rows):
        for k in range(n_steps):
            for c in range(n_blocks):
                ref[_step_rows(n_blocks, n_steps, k, c, nb), :] = (
                    rows[k * nb:(k + 1) * nb, c * LANES:(c + 1) * LANES])

    sf_cols = [cols for c in range(N_FF_CHUNKS) for cols in _ffn_cols(c)]
    sf_slots = sf_ref.shape[0]
    sf_used = []

    def fetch_sf(n):
        return pltpu.make_async_copy(sf_hbm.at[0, :, :, pl.ds(sf_cols[n].start, FF_CHUNK)],
                                     sf_ref.at[n % sf_slots], sf_sems.at[n % sf_slots])

    for n in range(sf_slots - 1):
        fetch_sf(n).start()

    def hist_ffn(cols):
        n = len(sf_used)
        assert sf_cols[n] == cols, "FFN history pieces are fetched in the order of use"
        sf_used.append(cols)
        fetch_sf(n).wait()
        rows = _row_cat([sf_ref[n % sf_slots, :, k, :] for k in range(CONV_HIST)])
        if n + sf_slots - 1 < len(sf_cols):
            fetch_sf(n + sf_slots - 1).start()
        return rows

    def sink_pool(ext):
        first = ext.shape[0] - POOL_BUF * nb
        for k in range(POOL_BUF):
            pool_out_ref[k] = ext[first + k * nb:first + (k + 1) * nb]

    def sink_conv(ext):
        write_steps(conv_out_ref, c_blocks, CONV_HIST, ext[ext.shape[0] - CONV_HIST * nb:])

    def sink_ffn(cols, ext):
        n, slots = len(ffn_sends), ffn_out_ref.shape[0]
        if n >= slots:
            ffn_sends[n - slots].wait()
        first = ext.shape[0] - CONV_HIST * nb
        for k in range(CONV_HIST):
            ffn_out_ref[n % slots, :, k, :] = ext[first + k * nb:first + (k + 1) * nb]
        ffn_sends.append(send_ffn(n, pl.ds(cols.start, FF_CHUNK)))
        ffn_sends[n].start()

    fetch["x"].wait()
    fetch["mod"].wait()
    x = read_steps(x_ref, d_blocks, nt)
    mods = _fold_mods(mod_ref[...], g_refs, nt)
    h = _front_head(x, mods)
    mixed = {}
    front = _front_stages(
        x, h, mods, w, mixed, nb=nb,
        cnt_fn=lambda win: float(min(win, PAST_LEN + 1)),
        hist_pool=lambda: _row_cat([sp_ref[k] for k in range(POOL_BUF)]),
        hist_conv=lambda: read_steps(sc_ref, c_blocks, CONV_HIST),
        sink_pool=sink_pool, sink_conv=sink_conv)
    fetch["w_in"].wait()
    next(front)
    fetch["pool"].wait()
    fetch["conv"].wait()
    next(front)
    send["pool"].start()
    send["conv"].start()
    fetch["w_out"].wait()
    next(front)
    next(front)
    h2_ref[...] = mixed["h2"]

    wait_ffn_group(0)
    ups = _UpValues()
    for c in range(FFN_UPS_AHEAD):
        _ffn_issue(w, h2_ref, ups, c)
    ffn = _ffn_stages(w, h2_ref, f_ref, act_ref, ups, nb=nb, hist_ffn=hist_ffn, sink_ffn=sink_ffn)
    for c in range(N_FF_CHUNKS):
        ahead = c + FFN_UPS_AHEAD
        if ahead < N_FF_CHUNKS and ahead % FF_DOWN_GROUP == 0:
            wait_ffn_group(ahead // FF_DOWN_GROUP)
        next(ffn)
    write_steps(y_ref, d_blocks, nt, _finish(mixed["x1"], f_ref, mods[5]))
    send["y"].start()
    for copy in list(send.values()) + ffn_sends[-ffn_out_ref.shape[0]:]:
        copy.wait()


def _sample(x, sp, sc, sf, mod, g, w_in, pool_w, pool_scale, conv_w, w_out, w_up, ffn_conv_w, w_down,
            *, nb, nt):
    hbm = pl.BlockSpec(memory_space=pl.ANY)
    vmem = pl.BlockSpec(memory_space=pltpu.VMEM)
    copied_in = (x, sp, sc, mod, w_in, w_out, w_up, w_down)
    out_shape = [
        jax.ShapeDtypeStruct(x.shape, _f32),
        jax.ShapeDtypeStruct(sp.shape, _f32),
        jax.ShapeDtypeStruct(sc.shape, _f32),
        jax.ShapeDtypeStruct((1, nb, CONV_HIST, 2 * D_FF), _f32),
    ]
    staged_out = [pltpu.VMEM(o.shape, o.dtype) for o in out_shape[:3]] + [
        pltpu.VMEM((SAMPLE_FFN_OUT_SLOTS, nb, CONV_HIST, FF_CHUNK), _f32)]
    return pl.pallas_call(
        functools.partial(_sample_kernel, nb=nb, nt=nt),
        in_specs=[hbm, hbm, hbm, hbm, hbm, vmem, vmem, vmem, vmem, hbm, vmem, vmem, vmem, hbm, hbm,
                  vmem, hbm],
        out_specs=[hbm] * len(out_shape),
        out_shape=out_shape,
        scratch_shapes=[pltpu.VMEM(a.shape, a.dtype) for a in copied_in]
        + [pltpu.VMEM((SAMPLE_FFN_IN_SLOTS, nb, CONV_HIST, FF_CHUNK), _f32)]
        + staged_out
        + [
            pltpu.VMEM((nt * nb, D_MODEL), _bf16),
            pltpu.VMEM((nt * nb, D_MODEL), _f32),
            pltpu.VMEM((nt * nb, D_FF), _bf16),
            pltpu.SemaphoreType.DMA((SAMPLE_FETCHES,)),
            pltpu.SemaphoreType.DMA((SAMPLE_FFN_IN_SLOTS,)),
            pltpu.SemaphoreType.DMA((SAMPLE_SENDS,)),
        ],
        compiler_params=pltpu.CompilerParams(vmem_limit_bytes=V7X_VMEM_LIMIT_BYTES),
        name="sample_trunk",
    )(x, sp, sc, sf, mod, *g, w_in, pool_w, pool_scale, conv_w, w_out, w_up, ffn_conv_w, w_down)


def _to_block_major(a, n_steps):
    nb, _, width = a.shape
    return (a.reshape(nb, n_steps, width // LANES, LANES).transpose(0, 2, 1, 3)
            .reshape(nb * (width // LANES) * n_steps, LANES))


def _from_block_major(a, nb, n_steps):
    blocks = a.shape[0] // (nb * n_steps)
    return (a.reshape(nb, blocks, n_steps, LANES).transpose(0, 2, 1, 3)
            .reshape(nb, n_steps, blocks * LANES))


def kernel(x_prompt, x_sample, state_pool, state_conv, state_ffn, c_prompt, c_sample, w_ada, b_ada,
           g_pre_mix, g_post_mix, g_pre_ffn, g_post_ffn, w_in, pool_w, pool_scale, conv_w, w_out,
           ffn_w_up, ffn_conv_w, ffn_w_down):
    assert w_ada.shape[0] == 1, "single trunk layer"
    nbs, nts, _ = x_sample.shape
    mod_p, mod_s = _ada(c_prompt, c_sample, w_ada[0], b_ada)
    g = (g_pre_mix, g_post_mix, g_pre_ffn, g_post_ffn)
    conv_taps = conv_w.reshape(1, CONV_W * D_CONV)
    ffn_taps = ffn_conv_w.reshape(1, CONV_W * 2 * D_FF)
    y_p, pool_p, conv_p, ffn_p, w_in_bf, w_out_bf, w_up_bf, w_down_bf = _prompt(
        x_prompt, mod_p, g, w_in, pool_w[0], pool_scale, conv_taps, w_out, ffn_w_up, ffn_taps,
        ffn_w_down)
    weights = (g, w_in_bf, pool_w[0], pool_scale, conv_taps, w_out_bf, w_up_bf, ffn_taps,
               w_down_bf)

    y_s, pool_s, conv_s, ffn_s = _sample(
        _to_block_major(x_sample, nts),
        state_pool[0].transpose(1, 0, 2),
        _to_block_major(state_conv[0], CONV_HIST),
        state_ffn,
        mod_s, *weights, nb=nbs, nt=nts)

    return (y_p, _from_block_major(y_s, nbs, nts), pool_p.transpose(1, 0, 2)[None], conv_p, ffn_p,
            pool_s.transpose(1, 0, 2)[None],
            _from_block_major(conv_s, nbs, CONV_HIST)[None],
            ffn_s)
```

```python
import functools

import jax
import jax.numpy as jnp
from jax import lax
from jax.experimental import pallas as pl
from jax.experimental.pallas import tpu as pltpu

D_MODEL = 1024
D_POOL = 512
D_CONV = 512
POOL_WINDOWS = (2, 4, 8, 16)
POOL_GC = 128
POOL_BUF = 15
CONV_W = 3
CONV_HIST = CONV_W - 1
D_FF = 2816
D_IN_PROJ = D_POOL + 3 * D_CONV
RMS_EPS = 1e-6
PAST_LEN = 16384

V7X_VMEM_LIMIT_BYTES = 58 * 1024 * 1024
SUBLANES = 8
LANES = 128
FF_CHUNK = 256
N_FF_CHUNKS = D_FF // FF_CHUNK
FFN_UPS_AHEAD = 1
FF_DOWN_GROUP = 3
SAMPLE_FFN_SLOTS = 4
PROMPT_TT = 64
PROMPT_PITCH = PROMPT_TT + SUBLANES
WEIGHT_STAGE_ROWS = {2048: 128, 1024: 256, 5632: 64}
WEIGHT_STAGE_SLOTS = 4
ADA_ROWS = 256
POOL_HALO = 16

_bf16 = jnp.bfloat16
_f32 = jnp.float32


def _dot(a, b):
    return jnp.dot(a, b, preferred_element_type=_f32)


def _rms_scaled(x, scale):
    ms = jnp.mean(x * x, axis=-1, keepdims=True)
    return x * lax.rsqrt(ms + RMS_EPS) * scale


def _silu(a):
    return a * jax.nn.sigmoid(a)


def _lane_cat(parts):
    return jnp.concatenate(parts, axis=1)


def _row_cat(parts):
    return jnp.concatenate(parts, axis=0)


def _ada_kernel(cp_ref, cs_ref, w_hbm, b_ref, mp_ref, ms_ref, w_buf, sems):
    n_blocks, rows = w_buf.shape[0], w_buf.shape[1]

    def fetch(k):
        return pltpu.make_async_copy(w_hbm.at[pl.ds(k * rows, rows), :], w_buf.at[k], sems.at[k])

    for k in range(n_blocks):
        fetch(k).start()
    c = _silu(_row_cat([cp_ref[...], cs_ref[...]])).astype(_bf16)
    mod = b_ref[...]
    for k in range(n_blocks):
        fetch(k).wait()
        mod = mod + _dot(c[:, k * rows:(k + 1) * rows], w_buf[k].astype(_bf16))
    bp = cp_ref.shape[0]
    mp_ref[...] = mod[:bp]
    ms_ref[...] = mod[bp:]


def _ada(c_prompt, c_sample, w_ada, b_ada):
    n = w_ada.shape[1]
    bp, bs = c_prompt.shape[0], c_sample.shape[0]
    vmem = pl.BlockSpec(memory_space=pltpu.VMEM)
    return pl.pallas_call(
        _ada_kernel,
        in_specs=[vmem, vmem, pl.BlockSpec(memory_space=pl.ANY), vmem],
        out_specs=[vmem, vmem],
        out_shape=[
            jax.ShapeDtypeStruct((bp, n), _f32),
            jax.ShapeDtypeStruct((bs, n), _f32),
        ],
        scratch_shapes=[
            pltpu.VMEM((D_MODEL // ADA_ROWS, ADA_ROWS, n), _f32),
            pltpu.SemaphoreType.DMA((D_MODEL // ADA_ROWS,)),
        ],
        compiler_params=pltpu.CompilerParams(vmem_limit_bytes=V7X_VMEM_LIMIT_BYTES),
        name="ada_mod",
    )(c_prompt, c_sample, w_ada, b_ada)


def _taps(w_ref, width, cols):
    return [w_ref[:, k * width + cols.start:k * width + cols.stop] for k in range(CONV_W)]


def _front_head(x, mods):
    return (_rms_scaled(x, mods[1]) + mods[0]).astype(_bf16)


def _front_stages(x, h, mods, w, out, *, nb, cnt_fn, hist_pool, hist_conv, sink_pool, sink_conv):
    (_, w_in_ref, pool_w_ref, pool_scale_ref, conv_w_ref, w_out_ref) = w[:6]
    _, _, gate1, sh2, scale2, _ = mods
    m = x.shape[0]

    proj = _dot(h[...], w_in_ref[...])
    yield
    v_pool = proj[:, :D_POOL]
    x_conv = proj[:, D_POOL:D_POOL + D_CONV]
    gate_b = proj[:, D_POOL + D_CONV:D_POOL + 2 * D_CONV]
    gate_c = proj[:, D_POOL + 2 * D_CONV:]

    pool_ext = _row_cat([hist_pool(), v_pool])
    halo = (pool_ext.shape[0] - m) // nb
    y_pool = []
    for gi, win in enumerate(POOL_WINDOWS):
        sl = slice(gi * POOL_GC, (gi + 1) * POOL_GC)
        s = pool_ext[:, sl]
        step = 1
        while step < win:
            n = s.shape[0]
            s = s[step * nb:] + s[:n - step * nb]
            step *= 2
        first = (halo - (win - 1)) * nb
        d = s[first:first + m] / cnt_fn(win) - v_pool[:, sl]
        y_pool.append(_dot(d.astype(_bf16), pool_w_ref[gi].astype(_bf16)))
    y_pool = _lane_cat(y_pool) * pool_scale_ref[...]
    sink_pool(pool_ext)

    cx = gate_c * x_conv
    conv_ext = _row_cat([hist_conv(), cx])
    cw = _taps(conv_w_ref, D_CONV, slice(0, D_CONV))
    conv = conv_ext[0:m] * cw[0] + conv_ext[nb:nb + m] * cw[1] + cx * cw[2]
    y_conv = gate_b * conv
    sink_conv(conv_ext)
    mixed = _lane_cat([y_pool, y_conv]).astype(_bf16)
    yield
    mix = _dot(mixed, w_out_ref[...])
    yield
    x1 = x + _rms_scaled(mix, gate1)
    out["x1"] = x1
    out["h2"] = (_rms_scaled(x1, scale2) + sh2).astype(_bf16)
    yield


class _UpValues:
    def __init__(self):
        self._v = {}

    def put(self, c, half, val):
        self._v[c, half] = val

    def get(self, c, half):
        return self._v.pop((c, half))


def _ffn_cols(c):
    return [slice(base + c * FF_CHUNK, base + (c + 1) * FF_CHUNK) for base in (0, D_FF)]


def _ffn_issue(w, h2_ref, ups, c):
    for half, cols in enumerate(_ffn_cols(c)):
        ups.put(c, half, _dot(h2_ref[...], w[6][:, cols]))


def _ffn_stages(w, h2_ref, f_ref, act_ref, ups, *, nb, hist_ffn, sink_ffn):
    _, ffn_conv_w_ref, w_down_ref = w[6:]
    m = h2_ref.shape[0]
    group_start = 0
    for c in range(N_FF_CHUNKS):
        if c + FFN_UPS_AHEAD < N_FF_CHUNKS:
            _ffn_issue(w, h2_ref, ups, c + FFN_UPS_AHEAD)
        halves = []
        for half, cols in enumerate(_ffn_cols(c)):
            up = ups.get(c, half)
            up_ext = _row_cat([hist_ffn(cols), up])
            fw = _taps(ffn_conv_w_ref, 2 * D_FF, cols)
            halves.append(up_ext[0:m] * fw[0] + up_ext[nb:nb + m] * fw[1] + up * fw[2])
            sink_ffn(cols, up_ext)
        act_ref[:, c * FF_CHUNK:(c + 1) * FF_CHUNK] = (_silu(halves[0]) * halves[1]).astype(_bf16)
        if (c + 1) % FF_DOWN_GROUP == 0 or c + 1 == N_FF_CHUNKS:
            rows = slice(group_start * FF_CHUNK, (c + 1) * FF_CHUNK)
            contrib = _dot(act_ref[:, rows], w_down_ref[rows, :])
            if group_start == 0:
                f_ref[...] = contrib
            else:
                f_ref[...] += contrib
            group_start = c + 1
        yield


def _finish(x1, f_ref, gate2):
    return x1 + _rms_scaled(f_ref[...], gate2)


def _fold_mods(mod, g_refs, n_steps):
    g = [g_ref[...] for g_ref in g_refs]
    sh1, sc1, gt1, sh2, sc2, gt2 = [mod[:, i * D_MODEL:(i + 1) * D_MODEL] for i in range(6)]
    folded = [sh1, g[0] * (1.0 + sc1), gt1 * g[1], sh2, g[2] * (1.0 + sc2), gt2 * g[3]]
    return [_row_cat([a] * n_steps) for a in folded]


def _trunk(x, mods, w, h2_ref, f_ref, act_ref, *, nb, cnt_fn, hist_pool, hist_conv, hist_ffn,
           sink_pool, sink_conv, sink_ffn):
    out = {}
    for _ in _front_stages(x, _front_head(x, mods), mods, w, out, nb=nb, cnt_fn=cnt_fn,
                           hist_pool=hist_pool, hist_conv=hist_conv, sink_pool=sink_pool,
                           sink_conv=sink_conv):
        pass
    h2_ref[...] = out["h2"]
    ups = _UpValues()
    for c in range(FFN_UPS_AHEAD):
        _ffn_issue(w, h2_ref, ups, c)
    for _ in _ffn_stages(w, h2_ref, f_ref, act_ref, ups, nb=nb, hist_ffn=hist_ffn,
                         sink_ffn=sink_ffn):
        pass
    return _finish(out["x1"], f_ref, mods[5])


def _stream_cast(src, dst, stage, sems, sem_row):
    slots, rows = stage.shape[0], stage.shape[1]
    n = src.shape[0] // rows
    ahead = slots - 1

    def chunk(i, slot):
        return pltpu.make_async_copy(src.at[pl.ds(i * rows, rows), :], stage.at[slot],
                                     sems.at[sem_row, slot])

    for i in range(min(ahead, n)):
        chunk(i, i).start()

    def body(i, carry):
        slot = lax.rem(i, slots)

        @pl.when(i + ahead < n)
        def _():
            chunk(i + ahead, lax.rem(i + ahead, slots)).start()

        chunk(i, slot).wait()
        dst[pl.ds(pl.multiple_of(i * rows, rows), rows), :] = stage[slot].astype(_bf16)
        return carry

    lax.fori_loop(0, n, body, 0)


def _prompt_kernel(x_ref, mod_ref, *rest, nb, tt, n_steps):
    g_refs, rest = rest[:4], rest[4:]
    (w_in_hbm, pool_w_ref, pool_scale_ref, conv_w_ref, w_out_hbm, w_up_hbm, ffn_conv_w_ref,
     w_down_hbm) = rest[:8]
    y_ref, pool_out_ref, conv_out_ref, ffn_out_ref = rest[8:12]
    bf_out = rest[12:16]
    (xs, pool_carry, conv_carry, ffn_carry, h2_ref, f_ref, act_ref,
     w_in_bf, w_out_bf, w_up_bf, w_down_bf, stage_in, stage_sq, stage_up,
     stage_sems, out_sems) = rest[16:]
    bf_vmem = (w_in_bf, w_out_bf, w_up_bf, w_down_bf)
    w = (g_refs, w_in_bf, pool_w_ref, pool_scale_ref, conv_w_ref, w_out_bf, w_up_bf, ffn_conv_w_ref,
         w_down_bf)
    j = pl.program_id(0)

    def bf_writeback(k):
        return pltpu.make_async_copy(bf_vmem[k], bf_out[k], out_sems.at[k])

    pitch = xs.shape[1] // nb
    n_slabs = D_MODEL // LANES
    m = tt * nb

    @pl.when(j == 0)
    def _():
        pool_carry[...] = jnp.zeros_like(pool_carry)
        conv_carry[...] = jnp.zeros_like(conv_carry)
        ffn_carry[...] = jnp.zeros_like(ffn_carry)
        _stream_cast(w_in_hbm.at[0], w_in_bf, stage_in, stage_sems, 0)
        _stream_cast(w_out_hbm.at[0], w_out_bf, stage_sq, stage_sems, 1)
        _stream_cast(w_up_hbm.at[0], w_up_bf, stage_up, stage_sems, 2)
        _stream_cast(w_down_hbm.at[0], w_down_bf, stage_sq, stage_sems, 1)
        for k in range(len(bf_vmem)):
            bf_writeback(k).start()

    for b in range(nb):
        for c in range(n_slabs):
            xs[c, b * pitch:b * pitch + tt, :] = x_ref[b, :, c * LANES:(c + 1) * LANES]
    x = _row_cat([_lane_cat([xs[c, pl.ds(i, nb, stride=pitch), :] for c in range(n_slabs)])
                  for i in range(tt)])

    mods = _fold_mods(mod_ref[...], g_refs, tt)

    t_idx = j * tt + lax.shift_right_logical(
        lax.broadcasted_iota(jnp.int32, (m, POOL_GC), 0), nb.bit_length() - 1)

    def cnt_fn(win):
        return jnp.minimum(win, t_idx + 1).astype(_f32)

    def carry_rows(ref, first, n):
        return _lane_cat([ref[first + q] for q in range(n)])

    def keep_rows(ref, first, ext):
        rows = ref.shape[1]
        for q in range(ext.shape[1] // LANES):
            ref[first + q] = ext[ext.shape[0] - rows:, q * LANES:(q + 1) * LANES]

    y = _trunk(
        x, mods, w, h2_ref, f_ref, act_ref, nb=nb, cnt_fn=cnt_fn,
        hist_pool=lambda: carry_rows(pool_carry, 0, D_POOL // LANES),
        hist_conv=lambda: carry_rows(conv_carry, 0, D_CONV // LANES),
        hist_ffn=lambda cols: carry_rows(ffn_carry, cols.start // LANES, FF_CHUNK // LANES),
        sink_pool=lambda ext: keep_rows(pool_carry, 0, ext),
        sink_conv=lambda ext: keep_rows(conv_carry, 0, ext),
        sink_ffn=lambda cols, ext: keep_rows(ffn_carry, cols.start // LANES, ext))

    for i in range(tt):
        for c in range(n_slabs):
            xs[c, pl.ds(i, nb, stride=pitch), :] = y[i * nb:(i + 1) * nb, c * LANES:(c + 1) * LANES]
    for b in range(nb):
        for c in range(n_slabs):
            y_ref[b, :, c * LANES:(c + 1) * LANES] = xs[c, b * pitch:b * pitch + tt, :]

    @pl.when(j == n_steps - 1)
    def _():
        def emit(out_ref, carry, n):
            first = carry.shape[1] // nb - n
            for b in range(nb):
                for q in range(carry.shape[0]):
                    out_ref[0, b, :, q * LANES:(q + 1) * LANES] = (
                        carry[q, pl.ds(first * nb + b, n, stride=nb), :])
        pool_rows = carry_rows(pool_carry, 0, D_POOL // LANES)
        for k in range(POOL_BUF):
            first = (POOL_HALO - POOL_BUF + k) * nb
            pool_out_ref[k] = pool_rows[first:first + nb]
        emit(conv_out_ref, conv_carry, CONV_HIST)
        emit(ffn_out_ref, ffn_carry, CONV_HIST)
        for k in range(len(bf_vmem)):
            bf_writeback(k).wait()


def _prompt(x, mod, g, w_in, pool_w, pool_scale, conv_w, w_out, w_up, ffn_conv_w, w_down):
    nb, seq, _ = x.shape
    assert nb == SUBLANES, "one vreg row per time step"
    tt = PROMPT_TT
    n_steps = seq // tt
    m = tt * nb
    n_slabs = D_MODEL // LANES
    big = (w_in, w_out, w_up, w_down)
    stage_widths = (D_IN_PROJ, D_MODEL, 2 * D_FF)
    in_hbm = pl.BlockSpec(memory_space=pl.ANY)

    def const_spec(a):
        nd = a.ndim
        return pl.BlockSpec(a.shape, lambda j: (0,) * nd, pipeline_mode=pl.Buffered(1))

    return pl.pallas_call(
        functools.partial(_prompt_kernel, nb=nb, tt=tt, n_steps=n_steps),
        grid=(n_steps,),
        in_specs=[pl.BlockSpec((nb, tt, D_MODEL), lambda j: (0, j, 0)), const_spec(mod),
                  *[const_spec(a) for a in g], in_hbm, const_spec(pool_w), const_spec(pool_scale),
                  const_spec(conv_w), in_hbm, in_hbm, const_spec(ffn_conv_w), in_hbm],
        out_specs=[
            pl.BlockSpec((nb, tt, D_MODEL), lambda j: (0, j, 0)),
            pl.BlockSpec((POOL_BUF, nb, D_POOL), lambda j: (0, 0, 0)),
            pl.BlockSpec((1, nb, CONV_HIST, D_CONV), lambda j: (0, 0, 0, 0)),
            pl.BlockSpec((1, nb, CONV_HIST, 2 * D_FF), lambda j: (0, 0, 0, 0)),
        ] + [pl.BlockSpec(memory_space=pl.ANY) for _ in big],
        out_shape=[
            jax.ShapeDtypeStruct((nb, seq, D_MODEL), _f32),
            jax.ShapeDtypeStruct((POOL_BUF, nb, D_POOL), _f32),
            jax.ShapeDtypeStruct((1, nb, CONV_HIST, D_CONV), _f32),
            jax.ShapeDtypeStruct((1, nb, CONV_HIST, 2 * D_FF), _f32),
        ] + [jax.ShapeDtypeStruct(a.shape[1:], _bf16) for a in big],
        scratch_shapes=[
            pltpu.VMEM((n_slabs, nb * PROMPT_PITCH, LANES), _f32),
            pltpu.VMEM((D_POOL // LANES, POOL_HALO * nb, LANES), _f32),
            pltpu.VMEM((D_CONV // LANES, CONV_HIST * nb, LANES), _f32),
            pltpu.VMEM((2 * D_FF // LANES, CONV_HIST * nb, LANES), _f32),
            pltpu.VMEM((m, D_MODEL), _bf16),
            pltpu.VMEM((m, D_MODEL), _f32),
            pltpu.VMEM((m, D_FF), _bf16),
        ] + [pltpu.VMEM(a.shape[1:], _bf16) for a in big]
        + [pltpu.VMEM((WEIGHT_STAGE_SLOTS, WEIGHT_STAGE_ROWS[c], c), _f32) for c in stage_widths]
        + [pltpu.SemaphoreType.DMA((len(stage_widths), WEIGHT_STAGE_SLOTS)),
           pltpu.SemaphoreType.DMA((len(big),))],
        compiler_params=pltpu.CompilerParams(
            dimension_semantics=("arbitrary",),
            vmem_limit_bytes=V7X_VMEM_LIMIT_BYTES),
        name="prompt_trunk",
    )(x, mod, *g, w_in, pool_w, pool_scale, conv_w, w_out, w_up, ffn_conv_w, w_down)


def _sample_kernel(*refs, nb, nt):
    refs = iter(refs)

    def take(n):
        return [next(refs) for _ in range(n)]

    x_hbm, sp_hbm, sc_hbm, sf_hbm, mod_hbm = take(5)
    g_refs = take(4)
    (w_in_hbm, pool_w_ref, pool_scale_ref, conv_w_ref, w_out_hbm, w_up_hbm, ffn_conv_w_ref,
     w_down_hbm) = take(8)
    y_hbm, pool_hbm, conv_hbm, ffn_hbm = take(4)
    (x_ref, sp_ref, sc_ref, mod_ref, w_in_ref, w_out_ref, w_up_ref, w_down_ref, sf_ref) = take(9)
    y_ref, pool_out_ref, conv_out_ref, ffn_out_ref = take(4)
    h2_ref, f_ref, act_ref, in_sems, sf_sems, out_sems, ffn_out_sems = take(7)
    w = (g_refs, w_in_ref, pool_w_ref, pool_scale_ref, conv_w_ref, w_out_ref, w_up_ref,
         ffn_conv_w_ref, w_down_ref)

    n_groups = -(-N_FF_CHUNKS // FF_DOWN_GROUP)

    def group_cols(gi, base):
        first = gi * FF_DOWN_GROUP
        n = min(FF_DOWN_GROUP, N_FF_CHUNKS - first)
        return pl.ds(base + first * FF_CHUNK, n * FF_CHUNK)

    in_order = [("x", [(x_hbm.at[:, t, :], x_ref.at[t]) for t in range(nt)]),
                ("mod", [(mod_hbm, mod_ref)]), ("w_in", [(w_in_hbm, w_in_ref)]),
                ("pool", [(sp_hbm, sp_ref)]),
                ("conv", [(sc_hbm.at[0, :, k, :], sc_ref.at[k]) for k in range(CONV_HIST)]),
                ("w_out", [(w_out_hbm, w_out_ref)])]
    for gi in range(n_groups):
        cols = [group_cols(gi, base) for base in (0, D_FF)]
        rows = group_cols(gi, 0)
        in_order.append((f"ffn_w{gi}", [(w_up_hbm.at[:, c], w_up_ref.at[:, c]) for c in cols]
                         + [(w_down_hbm.at[rows, :], w_down_ref.at[rows, :])]))
    fetch, k = {}, 0
    for name, pairs in in_order:
        fetch[name] = [pltpu.make_async_copy(src, dst, in_sems.at[k + i])
                       for i, (src, dst) in enumerate(pairs)]
        k += len(pairs)

    out_order = [("y", [(y_ref.at[t], y_hbm.at[:, t, :]) for t in range(nt)]),
                 ("pool", [(pool_out_ref, pool_hbm)]),
                 ("conv", [(conv_out_ref.at[k], conv_hbm.at[0, :, k, :]) for k in range(CONV_HIST)])]
    send, k = {}, 0
    for name, pairs in out_order:
        send[name] = [pltpu.make_async_copy(src, dst, out_sems.at[k + i])
                      for i, (src, dst) in enumerate(pairs)]
        k += len(pairs)

    def start(copies):
        for copy in copies:
            copy.start()

    def wait(copies):
        for copy in copies:
            copy.wait()

    for name, _ in in_order:
        start(fetch[name])

    piece_cols = [cols for c in range(N_FF_CHUNKS) for cols in _ffn_cols(c)]

    def ffn_piece(hbm, ring, sems, n, to_hbm):
        slot = n % ring.shape[0]
        copies = []
        for k in range(CONV_HIST):
            ends = (ring.at[slot, k], hbm.at[0, :, k, pl.ds(piece_cols[n].start, FF_CHUNK)])
            src, dst = ends if to_hbm else ends[::-1]
            copies.append(pltpu.make_async_copy(src, dst, sems.at[slot, k]))
        return copies

    sf_slots = sf_ref.shape[0]
    sf_used = []
    for n in range(sf_slots - 1):
        start(ffn_piece(sf_hbm, sf_ref, sf_sems, n, False))

    def hist_ffn(cols):
        n = len(sf_used)
        assert piece_cols[n] == cols, "FFN history pieces are fetched in the order of use"
        sf_used.append(cols)
        wait(ffn_piece(sf_hbm, sf_ref, sf_sems, n, False))
        rows = _row_cat([sf_ref[n % sf_slots, k] for k in range(CONV_HIST)])
        if n + sf_slots - 1 < len(piece_cols):
            start(ffn_piece(sf_hbm, sf_ref, sf_sems, n + sf_slots - 1, False))
        return rows

    ffn_sent = []

    def sink_ffn(cols, ext):
        n, slots = len(ffn_sent), ffn_out_ref.shape[0]
        assert piece_cols[n] == cols
        if n >= slots:
            wait(ffn_sent[n - slots])
        first = ext.shape[0] - CONV_HIST * nb
        for k in range(CONV_HIST):
            ffn_out_ref[n % slots, k] = ext[first + k * nb:first + (k + 1) * nb]
        ffn_sent.append(ffn_piece(ffn_hbm, ffn_out_ref, ffn_out_sems, n, True))
        start(ffn_sent[n])

    def sink_pool(ext):
        first = ext.shape[0] - POOL_BUF * nb
        for k in range(POOL_BUF):
            pool_out_ref[k] = ext[first + k * nb:first + (k + 1) * nb]

    def sink_conv(ext):
        first = ext.shape[0] - CONV_HIST * nb
        for k in range(CONV_HIST):
            conv_out_ref[k] = ext[first + k * nb:first + (k + 1) * nb]

    wait(fetch["x"])
    wait(fetch["mod"])
    x = _row_cat([x_ref[t] for t in range(nt)])
    mods = _fold_mods(mod_ref[...], g_refs, nt)
    h = _front_head(x, mods)
    mixed = {}
    front = _front_stages(
        x, h, mods, w, mixed, nb=nb,
        cnt_fn=lambda win: float(min(win, PAST_LEN + 1)),
        hist_pool=lambda: _row_cat([sp_ref[k] for k in range(POOL_BUF)]),
        hist_conv=lambda: _row_cat([sc_ref[k] for k in range(CONV_HIST)]),
        sink_pool=sink_pool, sink_conv=sink_conv)
    wait(fetch["w_in"])
    next(front)
    wait(fetch["pool"])
    wait(fetch["conv"])
    next(front)
    start(send["pool"])
    start(send["conv"])
    wait(fetch["w_out"])
    next(front)
    next(front)
    h2_ref[...] = mixed["h2"]

    wait(fetch["ffn_w0"])
    ups = _UpValues()
    for c in range(FFN_UPS_AHEAD):
        _ffn_issue(w, h2_ref, ups, c)
    ffn = _ffn_stages(w, h2_ref, f_ref, act_ref, ups, nb=nb, hist_ffn=hist_ffn, sink_ffn=sink_ffn)
    for c in range(N_FF_CHUNKS):
        ahead = c + FFN_UPS_AHEAD
        if ahead < N_FF_CHUNKS and ahead % FF_DOWN_GROUP == 0:
            wait(fetch[f"ffn_w{ahead // FF_DOWN_GROUP}"])
        next(ffn)
    y = _finish(mixed["x1"], f_ref, mods[5])
    for t in range(nt):
        y_ref[t] = y[t * nb:(t + 1) * nb]
    start(send["y"])
    for copies in list(send.values()) + ffn_sent[-ffn_out_ref.shape[0]:]:
        wait(copies)


def _sample(x, sp, sc, sf, mod, g, w_in, pool_w, pool_scale, conv_w, w_out, w_up, ffn_conv_w, w_down):
    nb, nt, _ = x.shape
    hbm = pl.BlockSpec(memory_space=pl.ANY)
    vmem = pl.BlockSpec(memory_space=pltpu.VMEM)
    out_shape = [
        jax.ShapeDtypeStruct(x.shape, _f32),
        jax.ShapeDtypeStruct(sp.shape, _f32),
        jax.ShapeDtypeStruct(sc.shape, _f32),
        jax.ShapeDtypeStruct(sf.shape, _f32),
    ]
    step_major = [pltpu.VMEM((nt, nb, D_MODEL), _f32),
                  pltpu.VMEM(sp.shape, _f32),
                  pltpu.VMEM((CONV_HIST, nb, D_CONV), _f32)]
    n_fetches = nt + 1 + 1 + 1 + CONV_HIST + 1 + 3 * -(-N_FF_CHUNKS // FF_DOWN_GROUP)
    n_sends = nt + 1 + CONV_HIST
    return pl.pallas_call(
        functools.partial(_sample_kernel, nb=nb, nt=nt),
        in_specs=[hbm, hbm, hbm, hbm, hbm, vmem, vmem, vmem, vmem, hbm, vmem, vmem, vmem, hbm, hbm,
                  vmem, hbm],
        out_specs=[hbm] * len(out_shape),
        out_shape=out_shape,
        scratch_shapes=step_major
        + [pltpu.VMEM(a.shape, a.dtype) for a in (mod, w_in, w_out, w_up, w_down)]
        + [pltpu.VMEM((SAMPLE_FFN_SLOTS, CONV_HIST, nb, FF_CHUNK), _f32)]
        + step_major
        + [pltpu.VMEM((SAMPLE_FFN_SLOTS, CONV_HIST, nb, FF_CHUNK), _f32)]
        + [
            pltpu.VMEM((nt * nb, D_MODEL), _bf16),
            pltpu.VMEM((nt * nb, D_MODEL), _f32),
            pltpu.VMEM((nt * nb, D_FF), _bf16),
            pltpu.SemaphoreType.DMA((n_fetches,)),
            pltpu.SemaphoreType.DMA((SAMPLE_FFN_SLOTS, CONV_HIST)),
            pltpu.SemaphoreType.DMA((n_sends,)),
            pltpu.SemaphoreType.DMA((SAMPLE_FFN_SLOTS, CONV_HIST)),
        ],
        compiler_params=pltpu.CompilerParams(vmem_limit_bytes=V7X_VMEM_LIMIT_BYTES),
        name="sample_trunk",
    )(x, sp, sc, sf, mod, *g, w_in, pool_w, pool_scale, conv_w, w_out, w_up, ffn_conv_w, w_down)


def kernel(x_prompt, x_sample, state_pool, state_conv, state_ffn, c_prompt, c_sample, w_ada, b_ada,
           g_pre_mix, g_post_mix, g_pre_ffn, g_post_ffn, w_in, pool_w, pool_scale, conv_w, w_out,
           ffn_w_up, ffn_conv_w, ffn_w_down):
    assert w_ada.shape[0] == 1, "single trunk layer"
    mod_p, mod_s = _ada(c_prompt, c_sample, w_ada[0], b_ada)
    g = (g_pre_mix, g_post_mix, g_pre_ffn, g_post_ffn)
    conv_taps = conv_w.reshape(1, CONV_W * D_CONV)
    ffn_taps = ffn_conv_w.reshape(1, CONV_W * 2 * D_FF)
    y_p, pool_p, conv_p, ffn_p, w_in_bf, w_out_bf, w_up_bf, w_down_bf = _prompt(
        x_prompt, mod_p, g, w_in, pool_w[0], pool_scale, conv_taps, w_out, ffn_w_up, ffn_taps,
        ffn_w_down)
    y_s, pool_s, conv_s, ffn_s = _sample(
        x_sample, state_pool[0].transpose(1, 0, 2), state_conv, state_ffn, mod_s, g, w_in_bf,
        pool_w[0], pool_scale, conv_taps, w_out_bf, w_up_bf, ffn_taps, w_down_bf)
    return (y_p, y_s, pool_p.transpose(1, 0, 2)[None], conv_p, ffn_p,
            pool_s.transpose(1, 0, 2)[None], conv_s, ffn_s)
```

```python
import functools

import jax
import jax.numpy as jnp
from jax import lax
from jax.experimental import pallas as pl
from jax.experimental.pallas import tpu as pltpu

D_MODEL = 1024
D_POOL = 512
D_CONV = 512
POOL_WINDOWS = (2, 4, 8, 16)
POOL_GC = 128
POOL_BUF = 15
CONV_W = 3
CONV_HIST = CONV_W - 1
D_FF = 2816
D_IN_PROJ = D_POOL + 3 * D_CONV
RMS_EPS = 1e-6
PAST_LEN = 16384

V7X_VMEM_LIMIT_BYTES = 58 * 1024 * 1024
SUBLANES = 8
LANES = 128
FF_CHUNK = 256
N_FF_CHUNKS = D_FF // FF_CHUNK
FFN_UPS_AHEAD = 1
FF_DOWN_GROUP = 3
PROMPT_TT = 64
WEIGHT_STAGE_ROWS = {2048: 128, 1024: 256, 5632: 64}
WEIGHT_STAGE_SLOTS = 4
ADA_ROWS = 256
POOL_HALO = 16

_bf16 = jnp.bfloat16
_f32 = jnp.float32


def _dot(a, b):
    return jnp.dot(a, b, preferred_element_type=_f32)


def _rms_scaled(x, scale):
    ms = jnp.mean(x * x, axis=-1, keepdims=True)
    return x * lax.rsqrt(ms + RMS_EPS) * scale


def _silu(a):
    return a * jax.nn.sigmoid(a)


def _lane_cat(parts):
    return jnp.concatenate(parts, axis=1)


def _row_cat(parts):
    return jnp.concatenate(parts, axis=0)


def _ada_kernel(cp_ref, cs_ref, w_hbm, b_ref, mp_ref, ms_ref, w_buf, sems):
    n_blocks, rows = w_buf.shape[0], w_buf.shape[1]

    def fetch(k):
        return pltpu.make_async_copy(w_hbm.at[pl.ds(k * rows, rows), :], w_buf.at[k], sems.at[k])

    for k in range(n_blocks):
        fetch(k).start()
    c = _silu(_row_cat([cp_ref[...], cs_ref[...]])).astype(_bf16)
    mod = b_ref[...]
    for k in range(n_blocks):
        fetch(k).wait()
        mod = mod + _dot(c[:, k * rows:(k + 1) * rows], w_buf[k].astype(_bf16))
    bp = cp_ref.shape[0]
    mp_ref[...] = mod[:bp]
    ms_ref[...] = mod[bp:]


def _ada(c_prompt, c_sample, w_ada, b_ada):
    n = w_ada.shape[1]
    bp, bs = c_prompt.shape[0], c_sample.shape[0]
    vmem = pl.BlockSpec(memory_space=pltpu.VMEM)
    return pl.pallas_call(
        _ada_kernel,
        in_specs=[vmem, vmem, pl.BlockSpec(memory_space=pl.ANY), vmem],
        out_specs=[vmem, vmem],
        out_shape=[
            jax.ShapeDtypeStruct((bp, n), _f32),
            jax.ShapeDtypeStruct((bs, n), _f32),
        ],
        scratch_shapes=[
            pltpu.VMEM((D_MODEL // ADA_ROWS, ADA_ROWS, n), _f32),
            pltpu.SemaphoreType.DMA((D_MODEL // ADA_ROWS,)),
        ],
        compiler_params=pltpu.CompilerParams(vmem_limit_bytes=V7X_VMEM_LIMIT_BYTES),
        name="ada_mod",
    )(c_prompt, c_sample, w_ada, b_ada)


def _taps(w_ref, width, cols):
    return [w_ref[:, k * width + cols.start:k * width + cols.stop] for k in range(CONV_W)]


def _front_head(x, mods):
    return (_rms_scaled(x, mods[1]) + mods[0]).astype(_bf16)


def _front_stages(x, h, mods, w, out, *, nb, cnt_fn, hist_pool, hist_conv, sink_pool, sink_conv):
    (_, w_in_ref, pool_w_ref, pool_scale_ref, conv_w_ref, w_out_ref) = w[:6]
    _, _, gate1, sh2, scale2, _ = mods
    m = x.shape[0]

    proj = _dot(h[...], w_in_ref[...])
    yield
    v_pool = proj[:, :D_POOL]
    x_conv = proj[:, D_POOL:D_POOL + D_CONV]
    gate_b = proj[:, D_POOL + D_CONV:D_POOL + 2 * D_CONV]
    gate_c = proj[:, D_POOL + 2 * D_CONV:]

    pool_ext = _row_cat([hist_pool(), v_pool])
    halo = (pool_ext.shape[0] - m) // nb
    y_pool = []
    for gi, win in enumerate(POOL_WINDOWS):
        sl = slice(gi * POOL_GC, (gi + 1) * POOL_GC)
        s = pool_ext[:, sl]
        step = 1
        while step < win:
            n = s.shape[0]
            s = s[step * nb:] + s[:n - step * nb]
            step *= 2
        first = (halo - (win - 1)) * nb
        d = s[first:first + m] / cnt_fn(win) - v_pool[:, sl]
        y_pool.append(_dot(d.astype(_bf16), pool_w_ref[gi].astype(_bf16)))
    y_pool = _lane_cat(y_pool) * pool_scale_ref[...]
    sink_pool(pool_ext)

    cx = gate_c * x_conv
    conv_ext = _row_cat([hist_conv(), cx])
    cw = _taps(conv_w_ref, D_CONV, slice(0, D_CONV))
    conv = conv_ext[0:m] * cw[0] + conv_ext[nb:nb + m] * cw[1] + cx * cw[2]
    y_conv = gate_b * conv
    sink_conv(conv_ext)
    mixed = _lane_cat([y_pool, y_conv]).astype(_bf16)
    yield
    mix = _dot(mixed, w_out_ref[...])
    yield
    x1 = x + _rms_scaled(mix, gate1)
    out["x1"] = x1
    out["h2"] = (_rms_scaled(x1, scale2) + sh2).astype(_bf16)
    yield


class _UpValues:
    def __init__(self):
        self._v = {}

    def put(self, c, half, val):
        self._v[c, half] = val

    def get(self, c, half):
        return self._v.pop((c, half))


def _ffn_cols(c):
    return [slice(base + c * FF_CHUNK, base + (c + 1) * FF_CHUNK) for base in (0, D_FF)]


def _ffn_issue(w, h2_ref, ups, c):
    for half, cols in enumerate(_ffn_cols(c)):
        ups.put(c, half, _dot(h2_ref[...], w[6][:, cols]))


def _ffn_stages(w, h2_ref, f_ref, act_ref, ups, out, *, nb, hist_ffn, sink_ffn):
    _, ffn_conv_w_ref, w_down_ref = w[6:]
    m = h2_ref.shape[0]
    group_start = 0
    for c in range(N_FF_CHUNKS):
        if c + FFN_UPS_AHEAD < N_FF_CHUNKS:
            _ffn_issue(w, h2_ref, ups, c + FFN_UPS_AHEAD)
        halves = []
        for half, cols in enumerate(_ffn_cols(c)):
            up = ups.get(c, half)
            up_ext = _row_cat([hist_ffn(cols), up])
            fw = _taps(ffn_conv_w_ref, 2 * D_FF, cols)
            halves.append(up_ext[0:m] * fw[0] + up_ext[nb:nb + m] * fw[1] + up * fw[2])
            sink_ffn(cols, up_ext)
        act_ref[:, c * FF_CHUNK:(c + 1) * FF_CHUNK] = (_silu(halves[0]) * halves[1]).astype(_bf16)
        if (c + 1) % FF_DOWN_GROUP == 0 or c + 1 == N_FF_CHUNKS:
            rows = slice(group_start * FF_CHUNK, (c + 1) * FF_CHUNK)
            contrib = _dot(act_ref[:, rows], w_down_ref[rows, :])
            total = contrib if group_start == 0 else f_ref[...] + contrib
            if c + 1 == N_FF_CHUNKS:
                out["f"] = total
            else:
                f_ref[...] = total
            group_start = c + 1
        yield


def _finish(x1, f, gate2):
    return x1 + _rms_scaled(f, gate2)


def _fold_mods(mod, g_refs, n_steps):
    g = [g_ref[...] for g_ref in g_refs]
    sh1, sc1, gt1, sh2, sc2, gt2 = [mod[:, i * D_MODEL:(i + 1) * D_MODEL] for i in range(6)]
    folded = [sh1, g[0] * (1.0 + sc1), gt1 * g[1], sh2, g[2] * (1.0 + sc2), gt2 * g[3]]
    return [_row_cat([a] * n_steps) for a in folded]


def _trunk(x, mods, w, h2_ref, f_ref, act_ref, *, nb, cnt_fn, hist_pool, hist_conv, hist_ffn,
           sink_pool, sink_conv, sink_ffn):
    out = {}
    for _ in _front_stages(x, _front_head(x, mods), mods, w, out, nb=nb, cnt_fn=cnt_fn,
                           hist_pool=hist_pool, hist_conv=hist_conv, sink_pool=sink_pool,
                           sink_conv=sink_conv):
        pass
    h2_ref[...] = out["h2"]
    ups = _UpValues()
    for c in range(FFN_UPS_AHEAD):
        _ffn_issue(w, h2_ref, ups, c)
    for _ in _ffn_stages(w, h2_ref, f_ref, act_ref, ups, out, nb=nb, hist_ffn=hist_ffn,
                         sink_ffn=sink_ffn):
        pass
    return _finish(out["x1"], out["f"], mods[5])


def _stream_cast(src, dst, stage, sems, sem_row):
    slots, rows = stage.shape[0], stage.shape[1]
    n = src.shape[0] // rows
    ahead = slots - 1

    def chunk(i, slot):
        return pltpu.make_async_copy(src.at[pl.ds(i * rows, rows), :], stage.at[slot],
                                     sems.at[sem_row, slot])

    for i in range(min(ahead, n)):
        chunk(i, i).start()

    def body(i, carry):
        slot = lax.rem(i, slots)

        @pl.when(i + ahead < n)
        def _():
            chunk(i + ahead, lax.rem(i + ahead, slots)).start()

        chunk(i, slot).wait()
        dst[pl.ds(pl.multiple_of(i * rows, rows), rows), :] = stage[slot].astype(_bf16)
        return carry

    lax.fori_loop(0, n, body, 0)


def _prompt_kernel(x_hbm, mod_ref, *rest, nb, tt, n_steps):
    g_refs, rest = rest[:4], rest[4:]
    (w_in_hbm, pool_w_ref, pool_scale_ref, conv_w_ref, w_out_hbm, w_up_hbm, ffn_conv_w_ref,
     w_down_hbm) = rest[:8]
    y_hbm, pool_out_ref, conv_out_ref, ffn_out_ref = rest[8:12]
    bf_out = rest[12:16]
    (x_buf, y_buf, pool_carry, conv_carry, ffn_carry, h2_ref, f_ref, act_ref,
     w_in_bf, w_out_bf, w_up_bf, w_down_bf, stage_in, stage_sq, stage_up,
     x_sems, y_sems, stage_sems, out_sems) = rest[16:]
    bf_vmem = (w_in_bf, w_out_bf, w_up_bf, w_down_bf)
    w = (g_refs, w_in_bf, pool_w_ref, pool_scale_ref, conv_w_ref, w_out_bf, w_up_bf, ffn_conv_w_ref,
         w_down_bf)
    j = pl.program_id(0)

    def bf_writeback(k):
        return pltpu.make_async_copy(bf_vmem[k], bf_out[k], out_sems.at[k])

    m = tt * nb
    slot = lax.rem(j, 2)

    def tile_copies(tile, buf_slot, to_hbm):
        copies = []
        for b in range(nb):
            hbm = (y_hbm if to_hbm else x_hbm).at[b, pl.ds(tile * tt, tt), :]
            if to_hbm:
                copies.append(pltpu.make_async_copy(y_buf.at[buf_slot, :, b, :], hbm,
                                                    y_sems.at[buf_slot, b]))
            else:
                copies.append(pltpu.make_async_copy(hbm, x_buf.at[buf_slot, :, b, :],
                                                    x_sems.at[buf_slot, b]))
        return copies

    def start(copies):
        for copy in copies:
            copy.start()

    def wait(copies):
        for copy in copies:
            copy.wait()

    @pl.when(j == 0)
    def _():
        start(tile_copies(0, 0, False))
        pool_carry[...] = jnp.zeros_like(pool_carry)
        conv_carry[...] = jnp.zeros_like(conv_carry)
        ffn_carry[...] = jnp.zeros_like(ffn_carry)
        _stream_cast(w_in_hbm.at[0], w_in_bf, stage_in, stage_sems, 0)
        _stream_cast(w_out_hbm.at[0], w_out_bf, stage_sq, stage_sems, 1)
        _stream_cast(w_up_hbm.at[0], w_up_bf, stage_up, stage_sems, 2)
        _stream_cast(w_down_hbm.at[0], w_down_bf, stage_sq, stage_sems, 1)
        for k in range(len(bf_vmem)):
            bf_writeback(k).start()

    @pl.when(j + 1 < n_steps)
    def _():
        start(tile_copies(j + 1, 1 - slot, False))

    wait(tile_copies(j, slot, False))
    x = x_buf[slot].reshape(m, D_MODEL)

    mods = _fold_mods(mod_ref[...], g_refs, tt)

    t_idx = j * tt + lax.shift_right_logical(
        lax.broadcasted_iota(jnp.int32, (m, POOL_GC), 0), nb.bit_length() - 1)

    def cnt_fn(win):
        return jnp.minimum(win, t_idx + 1).astype(_f32)

    def carry_rows(ref, first, n):
        return _lane_cat([ref[first + q] for q in range(n)])

    def keep_rows(ref, first, ext):
        rows = ref.shape[1]
        for q in range(ext.shape[1] // LANES):
            ref[first + q] = ext[ext.shape[0] - rows:, q * LANES:(q + 1) * LANES]

    y = _trunk(
        x, mods, w, h2_ref, f_ref, act_ref, nb=nb, cnt_fn=cnt_fn,
        hist_pool=lambda: carry_rows(pool_carry, 0, D_POOL // LANES),
        hist_conv=lambda: carry_rows(conv_carry, 0, D_CONV // LANES),
        hist_ffn=lambda cols: carry_rows(ffn_carry, cols.start // LANES, FF_CHUNK // LANES),
        sink_pool=lambda ext: keep_rows(pool_carry, 0, ext),
        sink_conv=lambda ext: keep_rows(conv_carry, 0, ext),
        sink_ffn=lambda cols, ext: keep_rows(ffn_carry, cols.start // LANES, ext))

    @pl.when(j >= 2)
    def _():
        wait(tile_copies(j - 2, slot, True))

    y_buf[slot] = y.reshape(tt, nb, D_MODEL)
    start(tile_copies(j, slot, True))

    @pl.when(j == n_steps - 1)
    def _():
        def emit(out_ref, carry, n):
            first = carry.shape[1] // nb - n
            for b in range(nb):
                for q in range(carry.shape[0]):
                    out_ref[0, b, :, q * LANES:(q + 1) * LANES] = (
                        carry[q, pl.ds(first * nb + b, n, stride=nb), :])
        pool_rows = carry_rows(pool_carry, 0, D_POOL // LANES)
        for k in range(POOL_BUF):
            first = (POOL_HALO - POOL_BUF + k) * nb
            pool_out_ref[k] = pool_rows[first:first + nb]
        emit(conv_out_ref, conv_carry, CONV_HIST)
        emit(ffn_out_ref, ffn_carry, CONV_HIST)
        for k in range(len(bf_vmem)):
            bf_writeback(k).wait()
        if n_steps >= 2:
            wait(tile_copies(j - 1, 1 - slot, True))
        wait(tile_copies(j, slot, True))


def _prompt(x, mod, g, w_in, pool_w, pool_scale, conv_w, w_out, w_up, ffn_conv_w, w_down):
    nb, seq, _ = x.shape
    assert nb == SUBLANES, "one vreg row per time step"
    tt = PROMPT_TT
    n_steps = seq // tt
    m = tt * nb
    big = (w_in, w_out, w_up, w_down)
    stage_widths = (D_IN_PROJ, D_MODEL, 2 * D_FF)
    in_hbm = pl.BlockSpec(memory_space=pl.ANY)

    def const_spec(a):
        nd = a.ndim
        return pl.BlockSpec(a.shape, lambda j: (0,) * nd, pipeline_mode=pl.Buffered(1))

    return pl.pallas_call(
        functools.partial(_prompt_kernel, nb=nb, tt=tt, n_steps=n_steps),
        grid=(n_steps,),
        in_specs=[in_hbm, const_spec(mod),
                  *[const_spec(a) for a in g], in_hbm, const_spec(pool_w), const_spec(pool_scale),
                  const_spec(conv_w), in_hbm, in_hbm, const_spec(ffn_conv_w), in_hbm],
        out_specs=[
            in_hbm,
            pl.BlockSpec((POOL_BUF, nb, D_POOL), lambda j: (0, 0, 0)),
            pl.BlockSpec((1, nb, CONV_HIST, D_CONV), lambda j: (0, 0, 0, 0)),
            pl.BlockSpec((1, nb, CONV_HIST, 2 * D_FF), lambda j: (0, 0, 0, 0)),
        ] + [pl.BlockSpec(memory_space=pl.ANY) for _ in big],
        out_shape=[
            jax.ShapeDtypeStruct((nb, seq, D_MODEL), _f32),
            jax.ShapeDtypeStruct((POOL_BUF, nb, D_POOL), _f32),
            jax.ShapeDtypeStruct((1, nb, CONV_HIST, D_CONV), _f32),
            jax.ShapeDtypeStruct((1, nb, CONV_HIST, 2 * D_FF), _f32),
        ] + [jax.ShapeDtypeStruct(a.shape[1:], _bf16) for a in big],
        scratch_shapes=[
            pltpu.VMEM((2, tt, nb, D_MODEL), _f32),
            pltpu.VMEM((2, tt, nb, D_MODEL), _f32),
            pltpu.VMEM((D_POOL // LANES, POOL_HALO * nb, LANES), _f32),
            pltpu.VMEM((D_CONV // LANES, CONV_HIST * nb, LANES), _f32),
            pltpu.VMEM((2 * D_FF // LANES, CONV_HIST * nb, LANES), _f32),
            pltpu.VMEM((m, D_MODEL), _bf16),
            pltpu.VMEM((m, D_MODEL), _f32),
            pltpu.VMEM((m, D_FF), _bf16),
        ] + [pltpu.VMEM(a.shape[1:], _bf16) for a in big]
        + [pltpu.VMEM((WEIGHT_STAGE_SLOTS, WEIGHT_STAGE_ROWS[c], c), _f32) for c in stage_widths]
        + [pltpu.SemaphoreType.DMA((2, nb)), pltpu.SemaphoreType.DMA((2, nb)),
           pltpu.SemaphoreType.DMA((len(stage_widths), WEIGHT_STAGE_SLOTS)),
           pltpu.SemaphoreType.DMA((len(big),))],
        compiler_params=pltpu.CompilerParams(
            dimension_semantics=("arbitrary",),
            vmem_limit_bytes=V7X_VMEM_LIMIT_BYTES),
        name="prompt_trunk",
    )(x, mod, *g, w_in, pool_w, pool_scale, conv_w, w_out, w_up, ffn_conv_w, w_down)


def _sample_kernel(*refs, nb, nt):
    refs = iter(refs)

    def take(n):
        return [next(refs) for _ in range(n)]

    x_hbm, sp_hbm, sc_hbm, sf_hbm, mod_hbm = take(5)
    g_refs = take(4)
    (w_in_hbm, pool_w_ref, pool_scale_ref, conv_w_ref, w_out_hbm, w_up_hbm, ffn_conv_w_ref,
     w_down_hbm) = take(8)
    y_hbm, pool_hbm, conv_hbm, ffn_hbm = take(4)
    (x_ref, sp_ref, sc_ref, mod_ref, w_in_ref, w_out_ref, w_up_ref, w_down_ref, sf_ref) = take(9)
    y_ref, pool_out_ref, conv_out_ref, ffn_out_ref = take(4)
    h2_ref, f_ref, act_ref, in_sems, out_sems = take(5)
    w = (g_refs, w_in_ref, pool_w_ref, pool_scale_ref, conv_w_ref, w_out_ref, w_up_ref,
         ffn_conv_w_ref, w_down_ref)

    n_groups = -(-N_FF_CHUNKS // FF_DOWN_GROUP)

    def group_cols(gi, base):
        first = gi * FF_DOWN_GROUP
        n = min(FF_DOWN_GROUP, N_FF_CHUNKS - first)
        return pl.ds(base + first * FF_CHUNK, n * FF_CHUNK)

    in_order = [("x", [(x_hbm.at[:, t, :], x_ref.at[t]) for t in range(nt)]),
                ("mod", [(mod_hbm, mod_ref)]), ("w_in", [(w_in_hbm, w_in_ref)]),
                ("pool", [(sp_hbm, sp_ref)]),
                ("conv", [(sc_hbm.at[0, :, k, :], sc_ref.at[k]) for k in range(CONV_HIST)]),
                ("w_out", [(w_out_hbm, w_out_ref)]),
                ("ffn", [(sf_hbm.at[0, :, k, :], sf_ref.at[k]) for k in range(CONV_HIST)])]
    for gi in range(n_groups):
        cols = [group_cols(gi, base) for base in (0, D_FF)]
        rows = group_cols(gi, 0)
        in_order.append((f"ffn_w{gi}", [(w_up_hbm.at[:, c], w_up_ref.at[:, c]) for c in cols]
                         + [(w_down_hbm.at[rows, :], w_down_ref.at[rows, :])]))
    fetch, k = {}, 0
    for name, pairs in in_order:
        fetch[name] = [pltpu.make_async_copy(src, dst, in_sems.at[k + i])
                       for i, (src, dst) in enumerate(pairs)]
        k += len(pairs)

    out_order = [("y", [(y_ref.at[t], y_hbm.at[:, t, :]) for t in range(nt)]),
                 ("pool", [(pool_out_ref, pool_hbm)]),
                 ("conv", [(conv_out_ref.at[k], conv_hbm.at[0, :, k, :]) for k in range(CONV_HIST)]),
                 ("ffn", [(ffn_out_ref.at[k], ffn_hbm.at[0, :, k, :]) for k in range(CONV_HIST)])]
    send, k = {}, 0
    for name, pairs in out_order:
        send[name] = [pltpu.make_async_copy(src, dst, out_sems.at[k + i])
                      for i, (src, dst) in enumerate(pairs)]
        k += len(pairs)

    def start(copies):
        for copy in copies:
            copy.start()

    def wait(copies):
        for copy in copies:
            copy.wait()

    for name, _ in in_order:
        start(fetch[name])

    def hist_ffn(cols):
        return _row_cat([sf_ref[k, :, cols] for k in range(CONV_HIST)])

    def sink_ffn(cols, ext):
        first = ext.shape[0] - CONV_HIST * nb
        for k in range(CONV_HIST):
            ffn_out_ref[k, :, cols] = ext[first + k * nb:first + (k + 1) * nb]

    def sink_pool(ext):
        first = ext.shape[0] - POOL_BUF * nb
        for k in range(POOL_BUF):
            pool_out_ref[k] = ext[first + k * nb:first + (k + 1) * nb]

    def sink_conv(ext):
        first = ext.shape[0] - CONV_HIST * nb
        for k in range(CONV_HIST):
            conv_out_ref[k] = ext[first + k * nb:first + (k + 1) * nb]

    wait(fetch["x"])
    wait(fetch["mod"])
    x = _row_cat([x_ref[t] for t in range(nt)])
    mods = _fold_mods(mod_ref[...], g_refs, nt)
    h = _front_head(x, mods)
    mixed = {}
    front = _front_stages(
        x, h, mods, w, mixed, nb=nb,
        cnt_fn=lambda win: float(min(win, PAST_LEN + 1)),
        hist_pool=lambda: _row_cat([sp_ref[k] for k in range(POOL_BUF)]),
        hist_conv=lambda: _row_cat([sc_ref[k] for k in range(CONV_HIST)]),
        sink_pool=sink_pool, sink_conv=sink_conv)
    wait(fetch["w_in"])
    next(front)
    wait(fetch["pool"])
    wait(fetch["conv"])
    next(front)
    start(send["pool"])
    start(send["conv"])
    wait(fetch["w_out"])
    next(front)
    next(front)
    h2_ref[...] = mixed["h2"]

    wait(fetch["ffn"])
    wait(fetch["ffn_w0"])
    ups = _UpValues()
    for c in range(FFN_UPS_AHEAD):
        _ffn_issue(w, h2_ref, ups, c)
    ffn = _ffn_stages(w, h2_ref, f_ref, act_ref, ups, mixed, nb=nb, hist_ffn=hist_ffn,
                      sink_ffn=sink_ffn)
    for c in range(N_FF_CHUNKS):
        if c + FFN_UPS_AHEAD == FF_DOWN_GROUP:
            for gi in range(1, n_groups):
                wait(fetch[f"ffn_w{gi}"])
        next(ffn)
    y = _finish(mixed["x1"], mixed["f"], mods[5])
    for t in range(nt):
        y_ref[t] = y[t * nb:(t + 1) * nb]
    start(send["ffn"])
    start(send["y"])
    for copies in send.values():
        wait(copies)


def _sample(x, sp, sc, sf, mod, g, w_in, pool_w, pool_scale, conv_w, w_out, w_up, ffn_conv_w, w_down):
    nb, nt, _ = x.shape
    hbm = pl.BlockSpec(memory_space=pl.ANY)
    vmem = pl.BlockSpec(memory_space=pltpu.VMEM)
    out_shape = [
        jax.ShapeDtypeStruct(x.shape, _f32),
        jax.ShapeDtypeStruct(sp.shape, _f32),
        jax.ShapeDtypeStruct(sc.shape, _f32),
        jax.ShapeDtypeStruct(sf.shape, _f32),
    ]
    step_major = [pltpu.VMEM((nt, nb, D_MODEL), _f32),
                  pltpu.VMEM(sp.shape, _f32),
                  pltpu.VMEM((CONV_HIST, nb, D_CONV), _f32)]
    n_fetches = nt + 1 + 1 + 1 + CONV_HIST + 1 + CONV_HIST + 3 * -(-N_FF_CHUNKS // FF_DOWN_GROUP)
    n_sends = nt + 1 + CONV_HIST + CONV_HIST
    return pl.pallas_call(
        functools.partial(_sample_kernel, nb=nb, nt=nt),
        in_specs=[hbm, hbm, hbm, hbm, hbm, vmem, vmem, vmem, vmem, hbm, vmem, vmem, vmem, hbm, hbm,
                  vmem, hbm],
        out_specs=[hbm] * len(out_shape),
        out_shape=out_shape,
        scratch_shapes=step_major
        + [pltpu.VMEM(a.shape, a.dtype) for a in (mod, w_in, w_out, w_up, w_down)]
        + [pltpu.VMEM((CONV_HIST, nb, 2 * D_FF), _f32)]
        + step_major
        + [pltpu.VMEM((CONV_HIST, nb, 2 * D_FF), _f32)]
        + [
            pltpu.VMEM((nt * nb, D_MODEL), _bf16),
            pltpu.VMEM((nt * nb, D_MODEL), _f32),
            pltpu.VMEM((nt * nb, D_FF), _bf16),
            pltpu.SemaphoreType.DMA((n_fetches,)),
            pltpu.SemaphoreType.DMA((n_sends,)),
        ],
        compiler_params=pltpu.CompilerParams(vmem_limit_bytes=V7X_VMEM_LIMIT_BYTES),
        name="sample_trunk",
    )(x, sp, sc, sf, mod, *g, w_in, pool_w, pool_scale, conv_w, w_out, w_up, ffn_conv_w, w_down)


def kernel(x_prompt, x_sample, state_pool, state_conv, state_ffn, c_prompt, c_sample, w_ada, b_ada,
           g_pre_mix, g_post_mix, g_pre_ffn, g_post_ffn, w_in, pool_w, pool_scale, conv_w, w_out,
           ffn_w_up, ffn_conv_w, ffn_w_down):
    assert w_ada.shape[0] == 1, "single trunk layer"
    mod_p, mod_s = _ada(c_prompt, c_sample, w_ada[0], b_ada)
    g = (g_pre_mix, g_post_mix, g_pre_ffn, g_post_ffn)
    conv_taps = conv_w.reshape(1, CONV_W * D_CONV)
    ffn_taps = ffn_conv_w.reshape(1, CONV_W * 2 * D_FF)
    y_p, pool_p, conv_p, ffn_p, w_in_bf, w_out_bf, w_up_bf, w_down_bf = _prompt(
        x_prompt, mod_p, g, w_in, pool_w[0], pool_scale, conv_taps, w_out, ffn_w_up, ffn_taps,
        ffn_w_down)
    y_s, pool_s, conv_s, ffn_s = _sample(
        x_sample, state_pool[0].transpose(1, 0, 2), state_conv, state_ffn, mod_s, g, w_in_bf,
        pool_w[0], pool_scale, conv_taps, w_out_bf, w_up_bf, ffn_taps, w_down_bf)
    return (y_p, y_s, pool_p.transpose(1, 0, 2)[None], conv_p, ffn_p,
            pool_s.transpose(1, 0, 2)[None], conv_s, ffn_s)
```

```python
import functools

import jax
import jax.numpy as jnp
from jax import lax
from jax.experimental import pallas as pl
from jax.experimental.pallas import tpu as pltpu

D_MODEL = 1024
D_POOL = 512
D_CONV = 512
POOL_WINDOWS = (2, 4, 8, 16)
POOL_GC = 128
POOL_BUF = 15
CONV_W = 3
CONV_HIST = CONV_W - 1
D_FF = 2816
D_IN_PROJ = D_POOL + 3 * D_CONV
RMS_EPS = 1e-6
PAST_LEN = 16384

V7X_VMEM_LIMIT_BYTES = 58 * 1024 * 1024
SUBLANES = 8
LANES = 128
FF_CHUNK = 256
N_FF_CHUNKS = D_FF // FF_CHUNK
FFN_UPS_AHEAD = 1
FF_DOWN_GROUP = 3
PROMPT_TT = 64
WEIGHT_CHUNK = 256
WEIGHT_COL_SLOTS = 6
WEIGHT_ROW_SLOTS = 5
ADA_ROWS = 256
POOL_HALO = 16

_bf16 = jnp.bfloat16
_f32 = jnp.float32


def _dot(a, b):
    return jnp.dot(a, b, preferred_element_type=_f32)


def _rms_scaled(x, scale):
    ms = jnp.mean(x * x, axis=-1, keepdims=True)
    return x * lax.rsqrt(ms + RMS_EPS) * scale


def _silu(a):
    return a * jax.nn.sigmoid(a)


def _lane_cat(parts):
    return jnp.concatenate(parts, axis=1)


def _row_cat(parts):
    return jnp.concatenate(parts, axis=0)


def _ada_kernel(cp_ref, cs_ref, w_hbm, b_ref, mp_ref, ms_ref, w_buf, sems):
    n_blocks, rows = w_buf.shape[0], w_buf.shape[1]

    def fetch(k):
        return pltpu.make_async_copy(w_hbm.at[pl.ds(k * rows, rows), :], w_buf.at[k], sems.at[k])

    for k in range(n_blocks):
        fetch(k).start()
    c = _silu(_row_cat([cp_ref[...], cs_ref[...]])).astype(_bf16)
    mod = b_ref[...]
    for k in range(n_blocks):
        fetch(k).wait()
        mod = mod + _dot(c[:, k * rows:(k + 1) * rows], w_buf[k].astype(_bf16))
    bp = cp_ref.shape[0]
    mp_ref[...] = mod[:bp]
    ms_ref[...] = mod[bp:]


def _ada(c_prompt, c_sample, w_ada, b_ada):
    n = w_ada.shape[1]
    bp, bs = c_prompt.shape[0], c_sample.shape[0]
    vmem = pl.BlockSpec(memory_space=pltpu.VMEM)
    return pl.pallas_call(
        _ada_kernel,
        in_specs=[vmem, vmem, pl.BlockSpec(memory_space=pl.ANY), vmem],
        out_specs=[vmem, vmem],
        out_shape=[
            jax.ShapeDtypeStruct((bp, n), _f32),
            jax.ShapeDtypeStruct((bs, n), _f32),
        ],
        scratch_shapes=[
            pltpu.VMEM((D_MODEL // ADA_ROWS, ADA_ROWS, n), _f32),
            pltpu.SemaphoreType.DMA((D_MODEL // ADA_ROWS,)),
        ],
        compiler_params=pltpu.CompilerParams(vmem_limit_bytes=V7X_VMEM_LIMIT_BYTES),
        name="ada_mod",
    )(c_prompt, c_sample, w_ada, b_ada)


def _taps(w_ref, width, cols):
    return [w_ref[:, k * width + cols.start:k * width + cols.stop] for k in range(CONV_W)]


def _front_head(x, mods):
    return (_rms_scaled(x, mods[1]) + mods[0]).astype(_bf16)


def _front_stages(x, h, mods, w, out, *, nb, cnt_fn, hist_pool, hist_conv, sink_pool, sink_conv):
    (_, w_in_ref, pool_w_ref, pool_scale_ref, conv_w_ref, w_out_ref) = w[:6]
    _, _, gate1, sh2, scale2, _ = mods
    m = x.shape[0]

    proj = _dot(h[...], w_in_ref[...])
    yield
    v_pool = proj[:, :D_POOL]
    x_conv = proj[:, D_POOL:D_POOL + D_CONV]
    gate_b = proj[:, D_POOL + D_CONV:D_POOL + 2 * D_CONV]
    gate_c = proj[:, D_POOL + 2 * D_CONV:]

    pool_ext = _row_cat([hist_pool(), v_pool])
    halo = (pool_ext.shape[0] - m) // nb
    y_pool = []
    for gi, win in enumerate(POOL_WINDOWS):
        sl = slice(gi * POOL_GC, (gi + 1) * POOL_GC)
        s = pool_ext[:, sl]
        step = 1
        while step < win:
            n = s.shape[0]
            s = s[step * nb:] + s[:n - step * nb]
            step *= 2
        first = (halo - (win - 1)) * nb
        d = s[first:first + m] / cnt_fn(win) - v_pool[:, sl]
        y_pool.append(_dot(d.astype(_bf16), pool_w_ref[gi].astype(_bf16)))
    y_pool = _lane_cat(y_pool) * pool_scale_ref[...]
    sink_pool(pool_ext)

    cx = gate_c * x_conv
    conv_ext = _row_cat([hist_conv(), cx])
    cw = _taps(conv_w_ref, D_CONV, slice(0, D_CONV))
    conv = conv_ext[0:m] * cw[0] + conv_ext[nb:nb + m] * cw[1] + cx * cw[2]
    y_conv = gate_b * conv
    sink_conv(conv_ext)
    mixed = _lane_cat([y_pool, y_conv]).astype(_bf16)
    yield
    mix = _dot(mixed, w_out_ref[...])
    yield
    x1 = x + _rms_scaled(mix, gate1)
    out["x1"] = x1
    out["h2"] = (_rms_scaled(x1, scale2) + sh2).astype(_bf16)
    yield


class _UpValues:
    def __init__(self):
        self._v = {}

    def put(self, c, half, val):
        self._v[c, half] = val

    def get(self, c, half):
        return self._v.pop((c, half))


def _ffn_cols(c):
    return [slice(base + c * FF_CHUNK, base + (c + 1) * FF_CHUNK) for base in (0, D_FF)]


def _ffn_issue(w, h2_ref, ups, c):
    for half, cols in enumerate(_ffn_cols(c)):
        ups.put(c, half, _dot(h2_ref[...], w[6][:, cols]))


def _ffn_stages(w, h2_ref, f_ref, act_ref, ups, out, *, nb, hist_ffn, sink_ffn):
    _, ffn_conv_w_ref, w_down_ref = w[6:]
    m = h2_ref.shape[0]
    group_start = 0
    for c in range(N_FF_CHUNKS):
        if c + FFN_UPS_AHEAD < N_FF_CHUNKS:
            _ffn_issue(w, h2_ref, ups, c + FFN_UPS_AHEAD)
        halves = []
        for half, cols in enumerate(_ffn_cols(c)):
            up = ups.get(c, half)
            up_ext = _row_cat([hist_ffn(cols), up])
            fw = _taps(ffn_conv_w_ref, 2 * D_FF, cols)
            halves.append(up_ext[0:m] * fw[0] + up_ext[nb:nb + m] * fw[1] + up * fw[2])
            sink_ffn(cols, up_ext)
        act_ref[:, c * FF_CHUNK:(c + 1) * FF_CHUNK] = (_silu(halves[0]) * halves[1]).astype(_bf16)
        if (c + 1) % FF_DOWN_GROUP == 0 or c + 1 == N_FF_CHUNKS:
            rows = slice(group_start * FF_CHUNK, (c + 1) * FF_CHUNK)
            contrib = _dot(act_ref[:, rows], w_down_ref[rows, :])
            total = contrib if group_start == 0 else f_ref[...] + contrib
            if c + 1 == N_FF_CHUNKS:
                out["f"] = total
            else:
                f_ref[...] = total
            group_start = c + 1
        yield


def _finish(x1, f, gate2):
    return x1 + _rms_scaled(f, gate2)


def _fold_mods(mod, g_refs, n_steps):
    g = [g_ref[...] for g_ref in g_refs]
    sh1, sc1, gt1, sh2, sc2, gt2 = [mod[:, i * D_MODEL:(i + 1) * D_MODEL] for i in range(6)]
    folded = [sh1, g[0] * (1.0 + sc1), gt1 * g[1], sh2, g[2] * (1.0 + sc2), gt2 * g[3]]
    return [_row_cat([a] * n_steps) for a in folded]


FRONT_STAGES = ("in_proj", "mixers", "out_proj", "post_mix")


def _trunk(x, mods, w, h2_ref, f_ref, act_ref, *, nb, cnt_fn, hist_pool, hist_conv, hist_ffn,
           sink_pool, sink_conv, sink_ffn, before_stage=lambda name: None):
    out = {}
    front = _front_stages(x, _front_head(x, mods), mods, w, out, nb=nb, cnt_fn=cnt_fn,
                          hist_pool=hist_pool, hist_conv=hist_conv, sink_pool=sink_pool,
                          sink_conv=sink_conv)
    for name in FRONT_STAGES:
        before_stage(name)
        next(front)
    h2_ref[...] = out["h2"]
    ups = _UpValues()
    for c in range(FFN_UPS_AHEAD):
        _ffn_issue(w, h2_ref, ups, c)
    ffn = _ffn_stages(w, h2_ref, f_ref, act_ref, ups, out, nb=nb, hist_ffn=hist_ffn,
                      sink_ffn=sink_ffn)
    for c in range(N_FF_CHUNKS):
        before_stage(f"ffn{c}")
        next(ffn)
    return _finish(out["x1"], out["f"], mods[5])


class _WeightStream:
    def __init__(self, rings, sems):
        self._rings, self._sems = rings, sems
        self._seq = {name: [] for name in rings}
        self._started = {name: 0 for name in rings}
        self._cast = {name: 0 for name in rings}

    def add(self, ring, src, store):
        self._seq[ring].append((src, store))
        return ring, len(self._seq[ring]) - 1

    def _copy(self, chunk):
        ring, pos = chunk
        slot = pos % self._rings[ring].shape[0]
        return pltpu.make_async_copy(self._seq[ring][pos][0], self._rings[ring].at[slot],
                                     self._sems[ring].at[slot])

    def _pump(self):
        for ring, seq in self._seq.items():
            slots = self._rings[ring].shape[0]
            while self._started[ring] < min(len(seq), self._cast[ring] + slots):
                self._copy((ring, self._started[ring])).start()
                self._started[ring] += 1

    def sync(self, chunks):
        for chunk in chunks:
            assert chunk[1] < self._started[chunk[0]], "chunks are needed in the order listed"
            self._copy(chunk).wait()
        self._pump()
        for ring, pos in chunks:
            assert pos == self._cast[ring], "each ring is cast in order"
            stage = self._rings[ring]
            self._seq[ring][pos][1](stage[pos % stage.shape[0]].astype(_bf16))
            self._cast[ring] += 1

    def start(self):
        self._pump()

    def finished(self):
        return all(self._cast[ring] == len(seq) for ring, seq in self._seq.items())


def _prompt_kernel(x_hbm, mod_ref, *rest, nb, tt, n_steps):
    g_refs, rest = rest[:4], rest[4:]
    (w_in_hbm, pool_w_ref, pool_scale_ref, conv_w_ref, w_out_hbm, w_up_hbm, ffn_conv_w_ref,
     w_down_hbm) = rest[:8]
    y_hbm, pool_out_ref, conv_out_ref, ffn_out_ref = rest[8:12]
    bf_out = rest[12:16]
    (x_buf, y_buf, pool_carry, conv_carry, ffn_carry, h2_ref, f_ref, act_ref,
     w_in_bf, w_out_bf, w_up_bf, w_down_bf, stage_cols, stage_rows,
     x_sems, y_sems, col_sems, row_sems, out_sems) = rest[16:]
    bf_vmem = (w_in_bf, w_out_bf, w_up_bf, w_down_bf)
    w = (g_refs, w_in_bf, pool_w_ref, pool_scale_ref, conv_w_ref, w_out_bf, w_up_bf, ffn_conv_w_ref,
         w_down_bf)
    j = pl.program_id(0)

    def bf_writeback(k):
        return pltpu.make_async_copy(bf_vmem[k], bf_out[k], out_sems.at[k])

    m = tt * nb
    slot = lax.rem(j, 2)

    def tile_copies(tile, buf_slot, to_hbm):
        copies = []
        for b in range(nb):
            hbm = (y_hbm if to_hbm else x_hbm).at[b, pl.ds(tile * tt, tt), :]
            if to_hbm:
                copies.append(pltpu.make_async_copy(y_buf.at[buf_slot, :, b, :], hbm,
                                                    y_sems.at[buf_slot, b]))
            else:
                copies.append(pltpu.make_async_copy(hbm, x_buf.at[buf_slot, :, b, :],
                                                    x_sems.at[buf_slot, b]))
        return copies

    def start(copies):
        for copy in copies:
            copy.start()

    def wait(copies):
        for copy in copies:
            copy.wait()

    def weight_stream():
        stream = _WeightStream({"cols": stage_cols, "rows": stage_rows},
                               {"cols": col_sems, "rows": row_sems})
        n = WEIGHT_CHUNK

        def col_chunk(hbm, bf, first):
            def store(v):
                bf[:, first:first + n] = v
            return stream.add("cols", hbm.at[0, :, pl.ds(first, n)], store)

        def row_chunk(hbm, bf, first):
            def store(v):
                bf[first:first + n, :] = v
            return stream.add("rows", hbm.at[0, pl.ds(first, n), :], store)

        plan = {"in_proj": [[col_chunk(w_in_hbm, w_in_bf, first)] for first in range(0, D_IN_PROJ, n)],
                "mixers": [[row_chunk(w_out_hbm, w_out_bf, first) for first in range(0, D_MODEL, n)]]}
        ffn = [[col_chunk(w_up_hbm, w_up_bf, cols.start) for cols in _ffn_cols(c)]
               + [row_chunk(w_down_hbm, w_down_bf, c * FF_CHUNK)] for c in range(N_FF_CHUNKS)]
        stages = list(FRONT_STAGES[2:]) + [f"ffn{c}" for c in range(N_FF_CHUNKS)]
        for name, chunks in zip(stages, ffn):
            plan[name] = [chunks]
        return stream, plan

    def tile(before_stage):
        x = x_buf[slot].reshape(m, D_MODEL)
        mods = _fold_mods(mod_ref[...], g_refs, tt)
        t_idx = j * tt + lax.shift_right_logical(
            lax.broadcasted_iota(jnp.int32, (m, POOL_GC), 0), nb.bit_length() - 1)

        def cnt_fn(win):
            return jnp.minimum(win, t_idx + 1).astype(_f32)

        def carry_rows(ref, first, n):
            return _lane_cat([ref[first + q] for q in range(n)])

        def keep_rows(ref, first, ext):
            rows = ref.shape[1]
            for q in range(ext.shape[1] // LANES):
                ref[first + q] = ext[ext.shape[0] - rows:, q * LANES:(q + 1) * LANES]

        return _trunk(
            x, mods, w, h2_ref, f_ref, act_ref, nb=nb, cnt_fn=cnt_fn,
            hist_pool=lambda: carry_rows(pool_carry, 0, D_POOL // LANES),
            hist_conv=lambda: carry_rows(conv_carry, 0, D_CONV // LANES),
            hist_ffn=lambda cols: carry_rows(ffn_carry, cols.start // LANES, FF_CHUNK // LANES),
            sink_pool=lambda ext: keep_rows(pool_carry, 0, ext),
            sink_conv=lambda ext: keep_rows(conv_carry, 0, ext),
            sink_ffn=lambda cols, ext: keep_rows(ffn_carry, cols.start // LANES, ext),
            before_stage=before_stage)

    @pl.when(j == 0)
    def _():
        start(tile_copies(0, 0, False))

    @pl.when(j + 1 < n_steps)
    def _():
        start(tile_copies(j + 1, 1 - slot, False))

    @pl.when(j == 0)
    def _():
        stream, plan = weight_stream()
        stream.start()
        pool_carry[...] = jnp.zeros_like(pool_carry)
        conv_carry[...] = jnp.zeros_like(conv_carry)
        ffn_carry[...] = jnp.zeros_like(ffn_carry)
        wait(tile_copies(0, 0, False))

        def before_stage(name):
            for chunks in plan.get(name, []):
                stream.sync(chunks)

        y_buf[0] = tile(before_stage).reshape(tt, nb, D_MODEL)
        assert stream.finished()
        for k in range(len(bf_vmem)):
            bf_writeback(k).start()

    @pl.when(j > 0)
    def _():
        wait(tile_copies(j, slot, False))
        y = tile(lambda name: None)

        @pl.when(j >= 2)
        def _():
            wait(tile_copies(j - 2, slot, True))

        y_buf[slot] = y.reshape(tt, nb, D_MODEL)

    start(tile_copies(j, slot, True))

    @pl.when(j == n_steps - 1)
    def _():
        def emit(out_ref, carry, n):
            first = carry.shape[1] // nb - n
            for b in range(nb):
                for q in range(carry.shape[0]):
                    out_ref[0, b, :, q * LANES:(q + 1) * LANES] = (
                        carry[q, pl.ds(first * nb + b, n, stride=nb), :])
        pool_rows = _lane_cat([pool_carry[q] for q in range(D_POOL // LANES)])
        for k in range(POOL_BUF):
            first = (POOL_HALO - POOL_BUF + k) * nb
            pool_out_ref[k] = pool_rows[first:first + nb]
        emit(conv_out_ref, conv_carry, CONV_HIST)
        emit(ffn_out_ref, ffn_carry, CONV_HIST)
        for k in range(len(bf_vmem)):
            bf_writeback(k).wait()
        if n_steps >= 2:
            wait(tile_copies(j - 1, 1 - slot, True))
        wait(tile_copies(j, slot, True))


def _prompt(x, mod, g, w_in, pool_w, pool_scale, conv_w, w_out, w_up, ffn_conv_w, w_down):
    nb, seq, _ = x.shape
    assert nb == SUBLANES, "one vreg row per time step"
    tt = PROMPT_TT
    n_steps = seq // tt
    m = tt * nb
    big = (w_in, w_out, w_up, w_down)
    in_hbm = pl.BlockSpec(memory_space=pl.ANY)

    def const_spec(a):
        nd = a.ndim
        return pl.BlockSpec(a.shape, lambda j: (0,) * nd, pipeline_mode=pl.Buffered(1))

    return pl.pallas_call(
        functools.partial(_prompt_kernel, nb=nb, tt=tt, n_steps=n_steps),
        grid=(n_steps,),
        in_specs=[in_hbm, const_spec(mod),
                  *[const_spec(a) for a in g], in_hbm, const_spec(pool_w), const_spec(pool_scale),
                  const_spec(conv_w), in_hbm, in_hbm, const_spec(ffn_conv_w), in_hbm],
        out_specs=[
            in_hbm,
            pl.BlockSpec((POOL_BUF, nb, D_POOL), lambda j: (0, 0, 0)),
            pl.BlockSpec((1, nb, CONV_HIST, D_CONV), lambda j: (0, 0, 0, 0)),
            pl.BlockSpec((1, nb, CONV_HIST, 2 * D_FF), lambda j: (0, 0, 0, 0)),
        ] + [pl.BlockSpec(memory_space=pl.ANY) for _ in big],
        out_shape=[
            jax.ShapeDtypeStruct((nb, seq, D_MODEL), _f32),
            jax.ShapeDtypeStruct((POOL_BUF, nb, D_POOL), _f32),
            jax.ShapeDtypeStruct((1, nb, CONV_HIST, D_CONV), _f32),
            jax.ShapeDtypeStruct((1, nb, CONV_HIST, 2 * D_FF), _f32),
        ] + [jax.ShapeDtypeStruct(a.shape[1:], _bf16) for a in big],
        scratch_shapes=[
            pltpu.VMEM((2, tt, nb, D_MODEL), _f32),
            pltpu.VMEM((2, tt, nb, D_MODEL), _f32),
            pltpu.VMEM((D_POOL // LANES, POOL_HALO * nb, LANES), _f32),
            pltpu.VMEM((D_CONV // LANES, CONV_HIST * nb, LANES), _f32),
            pltpu.VMEM((2 * D_FF // LANES, CONV_HIST * nb, LANES), _f32),
            pltpu.VMEM((m, D_MODEL), _bf16),
            pltpu.VMEM((m, D_MODEL), _f32),
            pltpu.VMEM((m, D_FF), _bf16),
        ] + [pltpu.VMEM(a.shape[1:], _bf16) for a in big]
        + [pltpu.VMEM((WEIGHT_COL_SLOTS, D_MODEL, WEIGHT_CHUNK), _f32),
           pltpu.VMEM((WEIGHT_ROW_SLOTS, WEIGHT_CHUNK, D_MODEL), _f32)]
        + [pltpu.SemaphoreType.DMA((2, nb)), pltpu.SemaphoreType.DMA((2, nb)),
           pltpu.SemaphoreType.DMA((WEIGHT_COL_SLOTS,)), pltpu.SemaphoreType.DMA((WEIGHT_ROW_SLOTS,)),
           pltpu.SemaphoreType.DMA((len(big),))],
        compiler_params=pltpu.CompilerParams(
            dimension_semantics=("arbitrary",),
            vmem_limit_bytes=V7X_VMEM_LIMIT_BYTES),
        name="prompt_trunk",
    )(x, mod, *g, w_in, pool_w, pool_scale, conv_w, w_out, w_up, ffn_conv_w, w_down)


def _sample_kernel(*refs, nb, nt):
    refs = iter(refs)

    def take(n):
        return [next(refs) for _ in range(n)]

    x_hbm, sp_hbm, sc_hbm, sf_hbm, mod_hbm = take(5)
    g_refs = take(4)
    (w_in_hbm, pool_w_ref, pool_scale_ref, conv_w_ref, w_out_hbm, w_up_hbm, ffn_conv_w_ref,
     w_down_hbm) = take(8)
    y_hbm, pool_hbm, conv_hbm, ffn_hbm = take(4)
    (x_ref, sp_ref, sc_ref, mod_ref, w_in_ref, w_out_ref, w_up_ref, w_down_ref, sf_ref) = take(9)
    y_ref, pool_out_ref, conv_out_ref, ffn_out_ref = take(4)
    h2_ref, f_ref, act_ref, in_sems, out_sems = take(5)
    w = (g_refs, w_in_ref, pool_w_ref, pool_scale_ref, conv_w_ref, w_out_ref, w_up_ref,
         ffn_conv_w_ref, w_down_ref)

    n_groups = -(-N_FF_CHUNKS // FF_DOWN_GROUP)

    def group_cols(gi, base):
        first = gi * FF_DOWN_GROUP
        n = min(FF_DOWN_GROUP, N_FF_CHUNKS - first)
        return pl.ds(base + first * FF_CHUNK, n * FF_CHUNK)

    in_order = [("x", [(x_hbm.at[:, t, :], x_ref.at[t]) for t in range(nt)]),
                ("mod", [(mod_hbm, mod_ref)]), ("w_in", [(w_in_hbm, w_in_ref)]),
                ("pool", [(sp_hbm, sp_ref)]),
                ("conv", [(sc_hbm.at[0, :, k, :], sc_ref.at[k]) for k in range(CONV_HIST)]),
                ("w_out", [(w_out_hbm, w_out_ref)]),
                ("ffn", [(sf_hbm.at[0, :, k, :], sf_ref.at[k]) for k in range(CONV_HIST)])]
    for gi in range(n_groups):
        cols = [group_cols(gi, base) for base in (0, D_FF)]
        rows = group_cols(gi, 0)
        in_order.append((f"ffn_w{gi}", [(w_up_hbm.at[:, c], w_up_ref.at[:, c]) for c in cols]
                         + [(w_down_hbm.at[rows, :], w_down_ref.at[rows, :])]))
    fetch, k = {}, 0
    for name, pairs in in_order:
        fetch[name] = [pltpu.make_async_copy(src, dst, in_sems.at[k + i])
                       for i, (src, dst) in enumerate(pairs)]
        k += len(pairs)

    out_order = [("y", [(y_ref.at[t], y_hbm.at[:, t, :]) for t in range(nt)]),
                 ("pool", [(pool_out_ref, pool_hbm)]),
                 ("conv", [(conv_out_ref.at[k], conv_hbm.at[0, :, k, :]) for k in range(CONV_HIST)]),
                 ("ffn", [(ffn_out_ref.at[k], ffn_hbm.at[0, :, k, :]) for k in range(CONV_HIST)])]
    send, k = {}, 0
    for name, pairs in out_order:
        send[name] = [pltpu.make_async_copy(src, dst, out_sems.at[k + i])
                      for i, (src, dst) in enumerate(pairs)]
        k += len(pairs)

    def start(copies):
        for copy in copies:
            copy.start()

    def wait(copies):
        for copy in copies:
            copy.wait()

    for name, _ in in_order:
        start(fetch[name])

    def hist_ffn(cols):
        return _row_cat([sf_ref[k, :, cols] for k in range(CONV_HIST)])

    def sink_ffn(cols, ext):
        first = ext.shape[0] - CONV_HIST * nb
        for k in range(CONV_HIST):
            ffn_out_ref[k, :, cols] = ext[first + k * nb:first + (k + 1) * nb]

    def sink_pool(ext):
        first = ext.shape[0] - POOL_BUF * nb
        for k in range(POOL_BUF):
            pool_out_ref[k] = ext[first + k * nb:first + (k + 1) * nb]

    def sink_conv(ext):
        first = ext.shape[0] - CONV_HIST * nb
        for k in range(CONV_HIST):
            conv_out_ref[k] = ext[first + k * nb:first + (k + 1) * nb]

    wait(fetch["x"])
    wait(fetch["mod"])
    x = _row_cat([x_ref[t] for t in range(nt)])
    mods = _fold_mods(mod_ref[...], g_refs, nt)
    h = _front_head(x, mods)
    mixed = {}
    front = _front_stages(
        x, h, mods, w, mixed, nb=nb,
        cnt_fn=lambda win: float(min(win, PAST_LEN + 1)),
        hist_pool=lambda: _row_cat([sp_ref[k] for k in range(POOL_BUF)]),
        hist_conv=lambda: _row_cat([sc_ref[k] for k in range(CONV_HIST)]),
        sink_pool=sink_pool, sink_conv=sink_conv)
    wait(fetch["w_in"])
    next(front)
    wait(fetch["pool"])
    wait(fetch["conv"])
    next(front)
    start(send["pool"])
    start(send["conv"])
    wait(fetch["w_out"])
    next(front)
    next(front)
    h2_ref[...] = mixed["h2"]

    wait(fetch["ffn"])
    wait(fetch["ffn_w0"])
    ups = _UpValues()
    for c in range(FFN_UPS_AHEAD):
        _ffn_issue(w, h2_ref, ups, c)
    ffn = _ffn_stages(w, h2_ref, f_ref, act_ref, ups, mixed, nb=nb, hist_ffn=hist_ffn,
                      sink_ffn=sink_ffn)
    for c in range(N_FF_CHUNKS):
        if c + FFN_UPS_AHEAD == FF_DOWN_GROUP:
            for gi in range(1, n_groups):
                wait(fetch[f"ffn_w{gi}"])
        next(ffn)
    y = _finish(mixed["x1"], mixed["f"], mods[5])
    for t in range(nt):
        y_ref[t] = y[t * nb:(t + 1) * nb]
    start(send["ffn"])
    start(send["y"])
    for copies in send.values():
        wait(copies)


def _sample(x, sp, sc, sf, mod, g, w_in, pool_w, pool_scale, conv_w, w_out, w_up, ffn_conv_w, w_down):
    nb, nt, _ = x.shape
    hbm = pl.BlockSpec(memory_space=pl.ANY)
    vmem = pl.BlockSpec(memory_space=pltpu.VMEM)
    out_shape = [
        jax.ShapeDtypeStruct(x.shape, _f32),
        jax.ShapeDtypeStruct(sp.shape, _f32),
        jax.ShapeDtypeStruct(sc.shape, _f32),
        jax.ShapeDtypeStruct(sf.shape, _f32),
    ]
    step_major = [pltpu.VMEM((nt, nb, D_MODEL), _f32),
                  pltpu.VMEM(sp.shape, _f32),
                  pltpu.VMEM((CONV_HIST, nb, D_CONV), _f32)]
    n_fetches = nt + 1 + 1 + 1 + CONV_HIST + 1 + CONV_HIST + 3 * -(-N_FF_CHUNKS // FF_DOWN_GROUP)
    n_sends = nt + 1 + CONV_HIST + CONV_HIST
    return pl.pallas_call(
        functools.partial(_sample_kernel, nb=nb, nt=nt),
        in_specs=[hbm, hbm, hbm, hbm, hbm, vmem, vmem, vmem, vmem, hbm, vmem, vmem, vmem, hbm, hbm,
                  vmem, hbm],
        out_specs=[hbm] * len(out_shape),
        out_shape=out_shape,
        scratch_shapes=step_major
        + [pltpu.VMEM(a.shape, a.dtype) for a in (mod, w_in, w_out, w_up, w_down)]
        + [pltpu.VMEM((CONV_HIST, nb, 2 * D_FF), _f32)]
        + step_major
        + [pltpu.VMEM((CONV_HIST, nb, 2 * D_FF), _f32)]
        + [
            pltpu.VMEM((nt * nb, D_MODEL), _bf16),
            pltpu.VMEM((nt * nb, D_MODEL), _f32),
            pltpu.VMEM((nt * nb, D_FF), _bf16),
            pltpu.SemaphoreType.DMA((n_fetches,)),
            pltpu.SemaphoreType.DMA((n_sends,)),
        ],
        compiler_params=pltpu.CompilerParams(vmem_limit_bytes=V7X_VMEM_LIMIT_BYTES),
        name="sample_trunk",
    )(x, sp, sc, sf, mod, *g, w_in, pool_w, pool_scale, conv_w, w_out, w_up, ffn_conv_w, w_down)


def kernel(x_prompt, x_sample, state_pool, state_conv, state_ffn, c_prompt, c_sample, w_ada, b_ada,
           g_pre_mix, g_post_mix, g_pre_ffn, g_post_ffn, w_in, pool_w, pool_scale, conv_w, w_out,
           ffn_w_up, ffn_conv_w, ffn_w_down):
    assert w_ada.shape[0] == 1, "single trunk layer"
    mod_p, mod_s = _ada(c_prompt, c_sample, w_ada[0], b_ada)
    g = (g_pre_mix, g_post_mix, g_pre_ffn, g_post_ffn)
    conv_taps = conv_w.reshape(1, CONV_W * D_CONV)
    ffn_taps = ffn_conv_w.reshape(1, CONV_W * 2 * D_FF)
    y_p, pool_p, conv_p, ffn_p, w_in_bf, w_out_bf, w_up_bf, w_down_bf = _prompt(
        x_prompt, mod_p, g, w_in, pool_w[0], pool_scale, conv_taps, w_out, ffn_w_up, ffn_taps,
        ffn_w_down)
    y_s, pool_s, conv_s, ffn_s = _sample(
        x_sample, state_pool[0].transpose(1, 0, 2), state_conv, state_ffn, mod_s, g, w_in_bf,
        pool_w[0], pool_scale, conv_taps, w_out_bf, w_up_bf, ffn_taps, w_down_bf)
    return (y_p, y_s, pool_p.transpose(1, 0, 2)[None], conv_p, ffn_p,
            pool_s.transpose(1, 0, 2)[None], conv_s, ffn_s)
```

```python
import functools

import jax
import jax.numpy as jnp
from jax import lax
from jax.experimental import pallas as pl
from jax.experimental.pallas import tpu as pltpu

D_MODEL = 1024
D_POOL = 512
D_CONV = 512
POOL_WINDOWS = (2, 4, 8, 16)
POOL_GC = 128
POOL_BUF = 15
CONV_W = 3
CONV_HIST = CONV_W - 1
D_FF = 2816
D_IN_PROJ = D_POOL + 3 * D_CONV
RMS_EPS = 1e-6
PAST_LEN = 16384

V7X_VMEM_LIMIT_BYTES = 58 * 1024 * 1024
SUBLANES = 8
LANES = 128
FF_CHUNK = 256
N_FF_CHUNKS = D_FF // FF_CHUNK
FFN_UPS_AHEAD = 1
FF_DOWN_GROUP = 3
PROMPT_TT = 64
WEIGHT_STAGE_ROWS = {2048: 128, 1024: 256, 5632: 64}
WEIGHT_STAGE_SLOTS = 4
ADA_ROWS = 256
POOL_HALO = 16

_bf16 = jnp.bfloat16
_f32 = jnp.float32


def _dot(a, b):
    return jnp.dot(a, b, preferred_element_type=_f32)


def _rms_scaled(x, scale):
    ms = jnp.mean(x * x, axis=-1, keepdims=True)
    return x * lax.rsqrt(ms + RMS_EPS) * scale


def _silu(a):
    return a * jax.nn.sigmoid(a)


def _lane_cat(parts):
    return jnp.concatenate(parts, axis=1)


def _row_cat(parts):
    return jnp.concatenate(parts, axis=0)


def _ada_kernel(cp_ref, cs_ref, w_hbm, b_ref, mp_ref, ms_ref, w_buf, sems):
    n_blocks, rows = w_buf.shape[0], w_buf.shape[1]

    def fetch(k):
        return pltpu.make_async_copy(w_hbm.at[pl.ds(k * rows, rows), :], w_buf.at[k], sems.at[k])

    for k in range(n_blocks):
        fetch(k).start()
    c = _silu(_row_cat([cp_ref[...], cs_ref[...]])).astype(_bf16)
    mod = b_ref[...]
    for k in range(n_blocks):
        fetch(k).wait()
        mod = mod + _dot(c[:, k * rows:(k + 1) * rows], w_buf[k].astype(_bf16))
    bp = cp_ref.shape[0]
    mp_ref[...] = mod[:bp]
    ms_ref[...] = mod[bp:]


def _ada(c_prompt, c_sample, w_ada, b_ada):
    n = w_ada.shape[1]
    bp, bs = c_prompt.shape[0], c_sample.shape[0]
    vmem = pl.BlockSpec(memory_space=pltpu.VMEM)
    return pl.pallas_call(
        _ada_kernel,
        in_specs=[vmem, vmem, pl.BlockSpec(memory_space=pl.ANY), vmem],
        out_specs=[vmem, vmem],
        out_shape=[
            jax.ShapeDtypeStruct((bp, n), _f32),
            jax.ShapeDtypeStruct((bs, n), _f32),
        ],
        scratch_shapes=[
            pltpu.VMEM((D_MODEL // ADA_ROWS, ADA_ROWS, n), _f32),
            pltpu.SemaphoreType.DMA((D_MODEL // ADA_ROWS,)),
        ],
        compiler_params=pltpu.CompilerParams(vmem_limit_bytes=V7X_VMEM_LIMIT_BYTES),
        name="ada_mod",
    )(c_prompt, c_sample, w_ada, b_ada)


def _taps(w_ref, width, cols):
    return [w_ref[:, k * width + cols.start:k * width + cols.stop] for k in range(CONV_W)]


def _front_head(x, mods):
    return (_rms_scaled(x, mods[1]) + mods[0]).astype(_bf16)


def _front_stages(x, h, mods, w, out, *, nb, cnt_fn, hist_pool, hist_conv, sink_pool, sink_conv):
    (_, w_in_ref, pool_w_ref, pool_scale_ref, conv_w_ref, w_out_ref) = w[:6]
    _, _, gate1, sh2, scale2, _ = mods
    m = x.shape[0]

    proj = _dot(h[...], w_in_ref[...])
    yield
    v_pool = proj[:, :D_POOL]
    x_conv = proj[:, D_POOL:D_POOL + D_CONV]
    gate_b = proj[:, D_POOL + D_CONV:D_POOL + 2 * D_CONV]
    gate_c = proj[:, D_POOL + 2 * D_CONV:]

    pool_ext = _row_cat([hist_pool(), v_pool])
    halo = (pool_ext.shape[0] - m) // nb
    y_pool = []
    for gi, win in enumerate(POOL_WINDOWS):
        sl = slice(gi * POOL_GC, (gi + 1) * POOL_GC)
        s = pool_ext[:, sl]
        step = 1
        while step < win:
            n = s.shape[0]
            s = s[step * nb:] + s[:n - step * nb]
            step *= 2
        first = (halo - (win - 1)) * nb
        d = s[first:first + m] / cnt_fn(win) - v_pool[:, sl]
        y_pool.append(_dot(d.astype(_bf16), pool_w_ref[gi].astype(_bf16)))
    y_pool = _lane_cat(y_pool) * pool_scale_ref[...]
    sink_pool(pool_ext)

    cx = gate_c * x_conv
    conv_ext = _row_cat([hist_conv(), cx])
    cw = _taps(conv_w_ref, D_CONV, slice(0, D_CONV))
    conv = conv_ext[0:m] * cw[0] + conv_ext[nb:nb + m] * cw[1] + cx * cw[2]
    y_conv = gate_b * conv
    sink_conv(conv_ext)
    mixed = _lane_cat([y_pool, y_conv]).astype(_bf16)
    yield
    mix = _dot(mixed, w_out_ref[...])
    yield
    x1 = x + _rms_scaled(mix, gate1)
    out["x1"] = x1
    out["h2"] = (_rms_scaled(x1, scale2) + sh2).astype(_bf16)
    yield


class _UpValues:
    def __init__(self):
        self._v = {}

    def put(self, c, half, val):
        self._v[c, half] = val

    def get(self, c, half):
        return self._v.pop((c, half))


def _ffn_cols(c):
    return [slice(base + c * FF_CHUNK, base + (c + 1) * FF_CHUNK) for base in (0, D_FF)]


def _ffn_issue(w, h2_ref, ups, c):
    for half, cols in enumerate(_ffn_cols(c)):
        ups.put(c, half, _dot(h2_ref[...], w[6][:, cols]))


def _ffn_stages(w, h2_ref, f_ref, act_ref, ups, out, *, nb, hist_ffn, sink_ffn):
    _, ffn_conv_w_ref, w_down_ref = w[6:]
    m = h2_ref.shape[0]
    group_start = 0
    for c in range(N_FF_CHUNKS):
        if c + FFN_UPS_AHEAD < N_FF_CHUNKS:
            _ffn_issue(w, h2_ref, ups, c + FFN_UPS_AHEAD)
        halves = []
        for half, cols in enumerate(_ffn_cols(c)):
            up = ups.get(c, half)
            up_ext = _row_cat([hist_ffn(cols), up])
            fw = _taps(ffn_conv_w_ref, 2 * D_FF, cols)
            halves.append(up_ext[0:m] * fw[0] + up_ext[nb:nb + m] * fw[1] + up * fw[2])
            sink_ffn(cols, up_ext)
        act_ref[:, c * FF_CHUNK:(c + 1) * FF_CHUNK] = (_silu(halves[0]) * halves[1]).astype(_bf16)
        if (c + 1) % FF_DOWN_GROUP == 0 or c + 1 == N_FF_CHUNKS:
            rows = slice(group_start * FF_CHUNK, (c + 1) * FF_CHUNK)
            contrib = _dot(act_ref[:, rows], w_down_ref[rows, :])
            total = contrib if group_start == 0 else f_ref[...] + contrib
            if c + 1 == N_FF_CHUNKS:
                out["f"] = total
            else:
                f_ref[...] = total
            group_start = c + 1
        yield


def _finish(x1, f, gate2):
    return x1 + _rms_scaled(f, gate2)


def _fold_mods(mod, g_refs, n_steps):
    g = [g_ref[...] for g_ref in g_refs]
    sh1, sc1, gt1, sh2, sc2, gt2 = [mod[:, i * D_MODEL:(i + 1) * D_MODEL] for i in range(6)]
    folded = [sh1, g[0] * (1.0 + sc1), gt1 * g[1], sh2, g[2] * (1.0 + sc2), gt2 * g[3]]
    return [_row_cat([a] * n_steps) for a in folded]


def _trunk(x, mods, w, h2_ref, f_ref, act_ref, *, nb, cnt_fn, hist_pool, hist_conv, hist_ffn,
           sink_pool, sink_conv, sink_ffn):
    out = {}
    for _ in _front_stages(x, _front_head(x, mods), mods, w, out, nb=nb, cnt_fn=cnt_fn,
                           hist_pool=hist_pool, hist_conv=hist_conv, sink_pool=sink_pool,
                           sink_conv=sink_conv):
        pass
    h2_ref[...] = out["h2"]
    ups = _UpValues()
    for c in range(FFN_UPS_AHEAD):
        _ffn_issue(w, h2_ref, ups, c)
    for _ in _ffn_stages(w, h2_ref, f_ref, act_ref, ups, out, nb=nb, hist_ffn=hist_ffn,
                         sink_ffn=sink_ffn):
        pass
    return _finish(out["x1"], out["f"], mods[5])


def _stream_cast(src, dst, stage, sems, sem_row):
    slots, rows = stage.shape[0], stage.shape[1]
    n = src.shape[0] // rows
    ahead = slots - 1

    def chunk(i, slot):
        return pltpu.make_async_copy(src.at[pl.ds(i * rows, rows), :], stage.at[slot],
                                     sems.at[sem_row, slot])

    for i in range(min(ahead, n)):
        chunk(i, i).start()

    def body(i, carry):
        slot = lax.rem(i, slots)

        @pl.when(i + ahead < n)
        def _():
            chunk(i + ahead, lax.rem(i + ahead, slots)).start()

        chunk(i, slot).wait()
        dst[pl.ds(pl.multiple_of(i * rows, rows), rows), :] = stage[slot].astype(_bf16)
        return carry

    lax.fori_loop(0, n, body, 0)


def _prompt_kernel(x_hbm, mod_ref, *rest, nb, tt, n_steps):
    g_refs, rest = rest[:4], rest[4:]
    (w_in_hbm, pool_w_ref, pool_scale_ref, conv_w_ref, w_out_hbm, w_up_hbm, ffn_conv_w_ref,
     w_down_hbm) = rest[:8]
    y_hbm, pool_out_ref, conv_out_ref, ffn_out_ref = rest[8:12]
    bf_out = rest[12:16]
    (x_buf, y_buf, pool_carry, conv_carry, ffn_carry, h2_ref, f_ref, act_ref,
     w_in_bf, w_out_bf, w_up_bf, w_down_bf, stage_in, stage_sq, stage_up,
     x_sems, y_sems, stage_sems, out_sems) = rest[16:]
    bf_vmem = (w_in_bf, w_out_bf, w_up_bf, w_down_bf)
    w = (g_refs, w_in_bf, pool_w_ref, pool_scale_ref, conv_w_ref, w_out_bf, w_up_bf, ffn_conv_w_ref,
         w_down_bf)
    j = pl.program_id(0)

    def bf_writeback(k):
        return pltpu.make_async_copy(bf_vmem[k], bf_out[k], out_sems.at[k])

    m = tt * nb
    slot = lax.rem(j, 2)

    def tile_copies(tile, buf_slot, to_hbm):
        copies = []
        for b in range(nb):
            hbm = (y_hbm if to_hbm else x_hbm).at[b, pl.ds(tile * tt, tt), :]
            if to_hbm:
                copies.append(pltpu.make_async_copy(y_buf.at[buf_slot, :, b, :], hbm,
                                                    y_sems.at[buf_slot, b]))
            else:
                copies.append(pltpu.make_async_copy(hbm, x_buf.at[buf_slot, :, b, :],
                                                    x_sems.at[buf_slot, b]))
        return copies

    def start(copies):
        for copy in copies:
            copy.start()

    def wait(copies):
        for copy in copies:
            copy.wait()

    @pl.when(j == 0)
    def _():
        start(tile_copies(0, 0, False))
        pool_carry[...] = jnp.zeros_like(pool_carry)
        conv_carry[...] = jnp.zeros_like(conv_carry)
        ffn_carry[...] = jnp.zeros_like(ffn_carry)
        _stream_cast(w_in_hbm.at[0], w_in_bf, stage_in, stage_sems, 0)
        _stream_cast(w_out_hbm.at[0], w_out_bf, stage_sq, stage_sems, 1)
        _stream_cast(w_up_hbm.at[0], w_up_bf, stage_up, stage_sems, 2)
        _stream_cast(w_down_hbm.at[0], w_down_bf, stage_sq, stage_sems, 1)
        for k in range(len(bf_vmem)):
            bf_writeback(k).start()

    @pl.when(j + 1 < n_steps)
    def _():
        start(tile_copies(j + 1, 1 - slot, False))

    wait(tile_copies(j, slot, False))
    x = x_buf[slot].reshape(m, D_MODEL)

    mods = _fold_mods(mod_ref[...], g_refs, tt)

    t_idx = j * tt + lax.shift_right_logical(
        lax.broadcasted_iota(jnp.int32, (m, POOL_GC), 0), nb.bit_length() - 1)

    def cnt_fn(win):
        return jnp.minimum(win, t_idx + 1).astype(_f32)

    def carry_rows(ref, first, n):
        return _lane_cat([ref[first + q] for q in range(n)])

    def keep_rows(ref, first, ext):
        rows = ref.shape[1]
        for q in range(ext.shape[1] // LANES):
            ref[first + q] = ext[ext.shape[0] - rows:, q * LANES:(q + 1) * LANES]

    y = _trunk(
        x, mods, w, h2_ref, f_ref, act_ref, nb=nb, cnt_fn=cnt_fn,
        hist_pool=lambda: carry_rows(pool_carry, 0, D_POOL // LANES),
        hist_conv=lambda: carry_rows(conv_carry, 0, D_CONV // LANES),
        hist_ffn=lambda cols: carry_rows(ffn_carry, cols.start // LANES, FF_CHUNK // LANES),
        sink_pool=lambda ext: keep_rows(pool_carry, 0, ext),
        sink_conv=lambda ext: keep_rows(conv_carry, 0, ext),
        sink_ffn=lambda cols, ext: keep_rows(ffn_carry, cols.start // LANES, ext))

    @pl.when(j >= 2)
    def _():
        wait(tile_copies(j - 2, slot, True))

    y_buf[slot] = y.reshape(tt, nb, D_MODEL)
    start(tile_copies(j, slot, True))

    @pl.when(j == n_steps - 1)
    def _():
        def emit(out_ref, carry, n):
            first = carry.shape[1] // nb - n
            for b in range(nb):
                for q in range(carry.shape[0]):
                    out_ref[0, b, :, q * LANES:(q + 1) * LANES] = (
                        carry[q, pl.ds(first * nb + b, n, stride=nb), :])
        pool_rows = carry_rows(pool_carry, 0, D_POOL // LANES)
        for k in range(POOL_BUF):
            first = (POOL_HALO - POOL_BUF + k) * nb
            pool_out_ref[k] = pool_rows[first:first + nb]
        emit(conv_out_ref, conv_carry, CONV_HIST)
        emit(ffn_out_ref, ffn_carry, CONV_HIST)
        for k in range(len(bf_vmem)):
            bf_writeback(k).wait()
        if n_steps >= 2:
            wait(tile_copies(j - 1, 1 - slot, True))
        wait(tile_copies(j, slot, True))


def _prompt(x, mod, g, w_in, pool_w, pool_scale, conv_w, w_out, w_up, ffn_conv_w, w_down):
    nb, seq, _ = x.shape
    assert nb == SUBLANES, "one vreg row per time step"
    tt = PROMPT_TT
    n_steps = seq // tt
    m = tt * nb
    big = (w_in, w_out, w_up, w_down)
    stage_widths = (D_IN_PROJ, D_MODEL, 2 * D_FF)
    in_hbm = pl.BlockSpec(memory_space=pl.ANY)

    def const_spec(a):
        nd = a.ndim
        return pl.BlockSpec(a.shape, lambda j: (0,) * nd, pipeline_mode=pl.Buffered(1))

    return pl.pallas_call(
        functools.partial(_prompt_kernel, nb=nb, tt=tt, n_steps=n_steps),
        grid=(n_steps,),
        in_specs=[in_hbm, const_spec(mod),
                  *[const_spec(a) for a in g], in_hbm, const_spec(pool_w), const_spec(pool_scale),
                  const_spec(conv_w), in_hbm, in_hbm, const_spec(ffn_conv_w), in_hbm],
        out_specs=[
            in_hbm,
            pl.BlockSpec((POOL_BUF, nb, D_POOL), lambda j: (0, 0, 0)),
            pl.BlockSpec((1, nb, CONV_HIST, D_CONV), lambda j: (0, 0, 0, 0)),
            pl.BlockSpec((1, nb, CONV_HIST, 2 * D_FF), lambda j: (0, 0, 0, 0)),
        ] + [pl.BlockSpec(memory_space=pl.ANY) for _ in big],
        out_shape=[
            jax.ShapeDtypeStruct((nb, seq, D_MODEL), _f32),
            jax.ShapeDtypeStruct((POOL_BUF, nb, D_POOL), _f32),
            jax.ShapeDtypeStruct((1, nb, CONV_HIST, D_CONV), _f32),
            jax.ShapeDtypeStruct((1, nb, CONV_HIST, 2 * D_FF), _f32),
        ] + [jax.ShapeDtypeStruct(a.shape[1:], _bf16) for a in big],
        scratch_shapes=[
            pltpu.VMEM((2, tt, nb, D_MODEL), _f32),
            pltpu.VMEM((2, tt, nb, D_MODEL), _f32),
            pltpu.VMEM((D_POOL // LANES, POOL_HALO * nb, LANES), _f32),
            pltpu.VMEM((D_CONV // LANES, CONV_HIST * nb, LANES), _f32),
            pltpu.VMEM((2 * D_FF // LANES, CONV_HIST * nb, LANES), _f32),
            pltpu.VMEM((m, D_MODEL), _bf16),
            pltpu.VMEM((m, D_MODEL), _f32),
            pltpu.VMEM((m, D_FF), _bf16),
        ] + [pltpu.VMEM(a.shape[1:], _bf16) for a in big]
        + [pltpu.VMEM((WEIGHT_STAGE_SLOTS, WEIGHT_STAGE_ROWS[c], c), _f32) for c in stage_widths]
        + [pltpu.SemaphoreType.DMA((2, nb)), pltpu.SemaphoreType.DMA((2, nb)),
           pltpu.SemaphoreType.DMA((len(stage_widths), WEIGHT_STAGE_SLOTS)),
           pltpu.SemaphoreType.DMA((len(big),))],
        compiler_params=pltpu.CompilerParams(
            dimension_semantics=("arbitrary",),
            vmem_limit_bytes=V7X_VMEM_LIMIT_BYTES),
        name="prompt_trunk",
    )(x, mod, *g, w_in, pool_w, pool_scale, conv_w, w_out, w_up, ffn_conv_w, w_down)


def _sample_kernel(*refs, nb, nt):
    refs = iter(refs)

    def take(n):
        return [next(refs) for _ in range(n)]

    x_hbm, sp_hbm, sc_hbm, sf_hbm, mod_hbm = take(5)
    g_refs = take(4)
    (w_in_hbm, pool_w_ref, pool_scale_ref, conv_w_ref, w_out_hbm, w_up_hbm, ffn_conv_w_ref,
     w_down_hbm) = take(8)
    y_hbm, pool_hbm, conv_hbm, ffn_hbm = take(4)
    (x_ref, sp_ref, sc_ref, mod_ref, w_in_ref, w_out_ref, w_up_ref, w_down_ref, sf_ref) = take(9)
    y_ref, pool_out_ref, conv_out_ref, ffn_out_ref = take(4)
    h2_ref, f_ref, act_ref, in_sems, out_sems = take(5)
    w = (g_refs, w_in_ref, pool_w_ref, pool_scale_ref, conv_w_ref, w_out_ref, w_up_ref,
         ffn_conv_w_ref, w_down_ref)

    n_groups = -(-N_FF_CHUNKS // FF_DOWN_GROUP)

    def group_cols(gi, base):
        first = gi * FF_DOWN_GROUP
        n = min(FF_DOWN_GROUP, N_FF_CHUNKS - first)
        return pl.ds(base + first * FF_CHUNK, n * FF_CHUNK)

    in_order = [("x", [(x_hbm.at[:, t, :], x_ref.at[t]) for t in range(nt)]),
                ("mod", [(mod_hbm, mod_ref)]), ("w_in", [(w_in_hbm, w_in_ref)]),
                ("pool", [(sp_hbm, sp_ref)]),
                ("conv", [(sc_hbm.at[0, :, k, :], sc_ref.at[k]) for k in range(CONV_HIST)]),
                ("w_out", [(w_out_hbm, w_out_ref)]),
                ("ffn", [(sf_hbm.at[0, :, k, :], sf_ref.at[k]) for k in range(CONV_HIST)])]
    for gi in range(n_groups):
        cols = [group_cols(gi, base) for base in (0, D_FF)]
        rows = group_cols(gi, 0)
        in_order.append((f"ffn_w{gi}", [(w_up_hbm.at[:, c], w_up_ref.at[:, c]) for c in cols]
                         + [(w_down_hbm.at[rows, :], w_down_ref.at[rows, :])]))
    fetch, k = {}, 0
    for name, pairs in in_order:
        fetch[name] = [pltpu.make_async_copy(src, dst, in_sems.at[k + i])
                       for i, (src, dst) in enumerate(pairs)]
        k += len(pairs)

    out_order = [("y", [(y_ref.at[t], y_hbm.at[:, t, :]) for t in range(nt)]),
                 ("pool", [(pool_out_ref, pool_hbm)]),
                 ("conv", [(conv_out_ref.at[k], conv_hbm.at[0, :, k, :]) for k in range(CONV_HIST)])]
    for gi in range(n_groups):
        out_order.append((f"ffn{gi}", [
            (ffn_out_ref.at[k, :, group_cols(gi, base)], ffn_hbm.at[0, :, k, group_cols(gi, base)])
            for base in (0, D_FF) for k in range(CONV_HIST)]))
    send, k = {}, 0
    for name, pairs in out_order:
        send[name] = [pltpu.make_async_copy(src, dst, out_sems.at[k + i])
                      for i, (src, dst) in enumerate(pairs)]
        k += len(pairs)

    def start(copies):
        for copy in copies:
            copy.start()

    def wait(copies):
        for copy in copies:
            copy.wait()

    for name, _ in in_order:
        start(fetch[name])

    def hist_ffn(cols):
        return _row_cat([sf_ref[k, :, cols] for k in range(CONV_HIST)])

    def sink_ffn(cols, ext):
        first = ext.shape[0] - CONV_HIST * nb
        for k in range(CONV_HIST):
            ffn_out_ref[k, :, cols] = ext[first + k * nb:first + (k + 1) * nb]

    def sink_pool(ext):
        first = ext.shape[0] - POOL_BUF * nb
        for k in range(POOL_BUF):
            pool_out_ref[k] = ext[first + k * nb:first + (k + 1) * nb]

    def sink_conv(ext):
        first = ext.shape[0] - CONV_HIST * nb
        for k in range(CONV_HIST):
            conv_out_ref[k] = ext[first + k * nb:first + (k + 1) * nb]

    wait(fetch["x"])
    wait(fetch["mod"])
    x = _row_cat([x_ref[t] for t in range(nt)])
    mods = _fold_mods(mod_ref[...], g_refs, nt)
    h = _front_head(x, mods)
    mixed = {}
    front = _front_stages(
        x, h, mods, w, mixed, nb=nb,
        cnt_fn=lambda win: float(min(win, PAST_LEN + 1)),
        hist_pool=lambda: _row_cat([sp_ref[k] for k in range(POOL_BUF)]),
        hist_conv=lambda: _row_cat([sc_ref[k] for k in range(CONV_HIST)]),
        sink_pool=sink_pool, sink_conv=sink_conv)
    wait(fetch["w_in"])
    next(front)
    wait(fetch["pool"])
    wait(fetch["conv"])
    next(front)
    start(send["pool"])
    start(send["conv"])
    wait(fetch["w_out"])
    next(front)
    next(front)
    h2_ref[...] = mixed["h2"]

    wait(fetch["ffn"])
    wait(fetch["ffn_w0"])
    ups = _UpValues()
    for c in range(FFN_UPS_AHEAD):
        _ffn_issue(w, h2_ref, ups, c)
    ffn = _ffn_stages(w, h2_ref, f_ref, act_ref, ups, mixed, nb=nb, hist_ffn=hist_ffn,
                      sink_ffn=sink_ffn)
    groups_sent = 0
    for c in range(N_FF_CHUNKS):
        ahead = c + FFN_UPS_AHEAD
        if ahead < N_FF_CHUNKS and ahead % FF_DOWN_GROUP == 0:
            wait(fetch[f"ffn_w{ahead // FF_DOWN_GROUP}"])
            while (groups_sent + 1) * FF_DOWN_GROUP <= c:
                start(send[f"ffn{groups_sent}"])
                groups_sent += 1
        next(ffn)
    y = _finish(mixed["x1"], mixed["f"], mods[5])
    for t in range(nt):
        y_ref[t] = y[t * nb:(t + 1) * nb]
    for gi in range(groups_sent, n_groups):
        start(send[f"ffn{gi}"])
    start(send["y"])
    for copies in send.values():
        wait(copies)


def _sample(x, sp, sc, sf, mod, g, w_in, pool_w, pool_scale, conv_w, w_out, w_up, ffn_conv_w, w_down):
    nb, nt, _ = x.shape
    hbm = pl.BlockSpec(memory_space=pl.ANY)
    vmem = pl.BlockSpec(memory_space=pltpu.VMEM)
    out_shape = [
        jax.ShapeDtypeStruct(x.shape, _f32),
        jax.ShapeDtypeStruct(sp.shape, _f32),
        jax.ShapeDtypeStruct(sc.shape, _f32),
        jax.ShapeDtypeStruct(sf.shape, _f32),
    ]
    step_major = [pltpu.VMEM((nt, nb, D_MODEL), _f32),
                  pltpu.VMEM(sp.shape, _f32),
                  pltpu.VMEM((CONV_HIST, nb, D_CONV), _f32)]
    n_fetches = nt + 1 + 1 + 1 + CONV_HIST + 1 + CONV_HIST + 3 * -(-N_FF_CHUNKS // FF_DOWN_GROUP)
    n_sends = nt + 1 + CONV_HIST + 2 * CONV_HIST * -(-N_FF_CHUNKS // FF_DOWN_GROUP)
    return pl.pallas_call(
        functools.partial(_sample_kernel, nb=nb, nt=nt),
        in_specs=[hbm, hbm, hbm, hbm, hbm, vmem, vmem, vmem, vmem, hbm, vmem, vmem, vmem, hbm, hbm,
                  vmem, hbm],
        out_specs=[hbm] * len(out_shape),
        out_shape=out_shape,
        scratch_shapes=step_major
        + [pltpu.VMEM(a.shape, a.dtype) for a in (mod, w_in, w_out, w_up, w_down)]
        + [pltpu.VMEM((CONV_HIST, nb, 2 * D_FF), _f32)]
        + step_major
        + [pltpu.VMEM((CONV_HIST, nb, 2 * D_FF), _f32)]
        + [
            pltpu.VMEM((nt * nb, D_MODEL), _bf16),
            pltpu.VMEM((nt * nb, D_MODEL), _f32),
            pltpu.VMEM((nt * nb, D_FF), _bf16),
            pltpu.SemaphoreType.DMA((n_fetches,)),
            pltpu.SemaphoreType.DMA((n_sends,)),
        ],
        compiler_params=pltpu.CompilerParams(vmem_limit_bytes=V7X_VMEM_LIMIT_BYTES),
        name="sample_trunk",
    )(x, sp, sc, sf, mod, *g, w_in, pool_w, pool_scale, conv_w, w_out, w_up, ffn_conv_w, w_down)


def kernel(x_prompt, x_sample, state_pool, state_conv, state_ffn, c_prompt, c_sample, w_ada, b_ada,
           g_pre_mix, g_post_mix, g_pre_ffn, g_post_ffn, w_in, pool_w, pool_scale, conv_w, w_out,
           ffn_w_up, ffn_conv_w, ffn_w_down):
    assert w_ada.shape[0] == 1, "single trunk layer"
    mod_p, mod_s = _ada(c_prompt, c_sample, w_ada[0], b_ada)
    g = (g_pre_mix, g_post_mix, g_pre_ffn, g_post_ffn)
    conv_taps = conv_w.reshape(1, CONV_W * D_CONV)
    ffn_taps = ffn_conv_w.reshape(1, CONV_W * 2 * D_FF)
    y_p, pool_p, conv_p, ffn_p, w_in_bf, w_out_bf, w_up_bf, w_down_bf = _prompt(
        x_prompt, mod_p, g, w_in, pool_w[0], pool_scale, conv_taps, w_out, ffn_w_up, ffn_taps,
        ffn_w_down)
    y_s, pool_s, conv_s, ffn_s = _sample(
        x_sample, state_pool[0].transpose(1, 0, 2), state_conv, state_ffn, mod_s, g, w_in_bf,
        pool_w[0], pool_scale, conv_taps, w_out_bf, w_up_bf, ffn_taps, w_down_bf)
    return (y_p, y_s, pool_p.transpose(1, 0, 2)[None], conv_p, ffn_p,
            pool_s.transpose(1, 0, 2)[None], conv_s, ffn_s)
```

```python
import functools

import jax
import jax.numpy as jnp
from jax import lax
from jax.experimental import pallas as pl
from jax.experimental.pallas import tpu as pltpu

D_MODEL = 1024
D_POOL = 512
D_CONV = 512
POOL_WINDOWS = (2, 4, 8, 16)
POOL_GC = 128
POOL_BUF = 15
CONV_W = 3
CONV_HIST = CONV_W - 1
D_FF = 2816
D_IN_PROJ = D_POOL + 3 * D_CONV
RMS_EPS = 1e-6
PAST_LEN = 16384

V7X_VMEM_LIMIT_BYTES = 58 * 1024 * 1024
SUBLANES = 8
LANES = 128
FF_CHUNK = 256
N_FF_CHUNKS = D_FF // FF_CHUNK
FFN_UPS_AHEAD = 1
FF_DOWN_GROUP = 3
ROW_BLOCKS = 2
PROMPT_TT = 64
WEIGHT_STAGE_ROWS = {2048: 128, 1024: 256, 5632: 64}
WEIGHT_STAGE_SLOTS = 4
ADA_ROWS = 256
POOL_HALO = 16

_bf16 = jnp.bfloat16
_f32 = jnp.float32


def _dot(a, b):
    return jnp.dot(a, b, preferred_element_type=_f32)


def _rms_scaled(x, scale):
    ms = jnp.mean(x * x, axis=-1, keepdims=True)
    return x * lax.rsqrt(ms + RMS_EPS) * scale


def _silu(a):
    return a * jax.nn.sigmoid(a)


def _lane_cat(parts):
    return jnp.concatenate(parts, axis=1)


def _row_cat(parts):
    return jnp.concatenate(parts, axis=0)


def _ada_kernel(cp_ref, cs_ref, w_hbm, b_ref, mp_ref, ms_ref, w_buf, sems):
    n_blocks, rows = w_buf.shape[0], w_buf.shape[1]

    def fetch(k):
        return pltpu.make_async_copy(w_hbm.at[pl.ds(k * rows, rows), :], w_buf.at[k], sems.at[k])

    for k in range(n_blocks):
        fetch(k).start()
    c = _silu(_row_cat([cp_ref[...], cs_ref[...]])).astype(_bf16)
    mod = b_ref[...]
    for k in range(n_blocks):
        fetch(k).wait()
        mod = mod + _dot(c[:, k * rows:(k + 1) * rows], w_buf[k].astype(_bf16))
    bp = cp_ref.shape[0]
    mp_ref[...] = mod[:bp]
    ms_ref[...] = mod[bp:]


def _ada(c_prompt, c_sample, w_ada, b_ada):
    n = w_ada.shape[1]
    bp, bs = c_prompt.shape[0], c_sample.shape[0]
    vmem = pl.BlockSpec(memory_space=pltpu.VMEM)
    return pl.pallas_call(
        _ada_kernel,
        in_specs=[vmem, vmem, pl.BlockSpec(memory_space=pl.ANY), vmem],
        out_specs=[vmem, vmem],
        out_shape=[
            jax.ShapeDtypeStruct((bp, n), _f32),
            jax.ShapeDtypeStruct((bs, n), _f32),
        ],
        scratch_shapes=[
            pltpu.VMEM((D_MODEL // ADA_ROWS, ADA_ROWS, n), _f32),
            pltpu.SemaphoreType.DMA((D_MODEL // ADA_ROWS,)),
        ],
        compiler_params=pltpu.CompilerParams(vmem_limit_bytes=V7X_VMEM_LIMIT_BYTES),
        name="ada_mod",
    )(c_prompt, c_sample, w_ada, b_ada)


def _taps(w_ref, width, cols):
    return [w_ref[:, k * width + cols.start:k * width + cols.stop] for k in range(CONV_W)]


def _row_blocks(m):
    n = m // ROW_BLOCKS
    return [slice(k * n, (k + 1) * n) for k in range(ROW_BLOCKS)]


def _front_head(x, mods):
    return (_rms_scaled(x, mods[1]) + mods[0]).astype(_bf16)


def _front_stages(x, h, mods, w, out, *, nb, cnt_fn, hist_pool, hist_conv, sink_pool, sink_conv):
    (_, w_in_ref, pool_w_ref, pool_scale_ref, conv_w_ref, w_out_ref) = w[:6]
    _, _, gate1, sh2, scale2, _ = mods
    m = x.shape[0]

    proj = _dot(h[...], w_in_ref[...])
    yield
    v_pool = proj[:, :D_POOL]
    x_conv = proj[:, D_POOL:D_POOL + D_CONV]
    gate_b = proj[:, D_POOL + D_CONV:D_POOL + 2 * D_CONV]
    gate_c = proj[:, D_POOL + 2 * D_CONV:]

    pool_ext = _row_cat([hist_pool(), v_pool])
    halo = (pool_ext.shape[0] - m) // nb
    y_pool = []
    for gi, win in enumerate(POOL_WINDOWS):
        sl = slice(gi * POOL_GC, (gi + 1) * POOL_GC)
        s = pool_ext[:, sl]
        step = 1
        while step < win:
            n = s.shape[0]
            s = s[step * nb:] + s[:n - step * nb]
            step *= 2
        first = (halo - (win - 1)) * nb
        d = s[first:first + m] / cnt_fn(win) - v_pool[:, sl]
        y_pool.append(_dot(d.astype(_bf16), pool_w_ref[gi].astype(_bf16)))
    y_pool = _lane_cat(y_pool) * pool_scale_ref[...]
    sink_pool(pool_ext)

    cx = gate_c * x_conv
    conv_ext = _row_cat([hist_conv(), cx])
    cw = _taps(conv_w_ref, D_CONV, slice(0, D_CONV))
    conv = conv_ext[0:m] * cw[0] + conv_ext[nb:nb + m] * cw[1] + cx * cw[2]
    y_conv = gate_b * conv
    sink_conv(conv_ext)
    mixed = _lane_cat([y_pool, y_conv]).astype(_bf16)
    yield
    blocks = _row_blocks(m)
    mix = [_dot(mixed[rows], w_out_ref[...]) for rows in blocks]
    yield
    out["x1"], out["h2"] = [], []
    for rows, mix_rows in zip(blocks, mix):
        x1 = x[rows] + _rms_scaled(mix_rows, gate1[rows])
        out["x1"].append(x1)
        out["h2"].append((_rms_scaled(x1, scale2[rows]) + sh2[rows]).astype(_bf16))
        yield


class _UpValues:
    def __init__(self):
        self._v = {}

    def put(self, c, half, val):
        self._v.setdefault((c, half), []).append(val)

    def get(self, c, half):
        return _row_cat(self._v.pop((c, half)))


def _ffn_cols(c):
    return [slice(base + c * FF_CHUNK, base + (c + 1) * FF_CHUNK) for base in (0, D_FF)]


def _ffn_issue(w, h2_ref, ups, c, rows=slice(None)):
    for half, cols in enumerate(_ffn_cols(c)):
        ups.put(c, half, _dot(h2_ref[rows, :], w[6][:, cols]))


def _ffn_stages(w, h2_ref, f_ref, act_ref, ups, out, *, nb, hist_ffn, sink_ffn):
    _, ffn_conv_w_ref, w_down_ref = w[6:]
    m = h2_ref.shape[0]
    group_start = 0
    for c in range(N_FF_CHUNKS):
        if c + FFN_UPS_AHEAD < N_FF_CHUNKS:
            _ffn_issue(w, h2_ref, ups, c + FFN_UPS_AHEAD)
        halves = []
        for half, cols in enumerate(_ffn_cols(c)):
            up = ups.get(c, half)
            up_ext = _row_cat([hist_ffn(cols), up])
            fw = _taps(ffn_conv_w_ref, 2 * D_FF, cols)
            halves.append(up_ext[0:m] * fw[0] + up_ext[nb:nb + m] * fw[1] + up * fw[2])
            sink_ffn(cols, up_ext)
        act_ref[:, c * FF_CHUNK:(c + 1) * FF_CHUNK] = (_silu(halves[0]) * halves[1]).astype(_bf16)
        if (c + 1) % FF_DOWN_GROUP == 0 or c + 1 == N_FF_CHUNKS:
            rows = slice(group_start * FF_CHUNK, (c + 1) * FF_CHUNK)
            if c + 1 == N_FF_CHUNKS:
                assert group_start > 0, "earlier groups have left their partial sums in f_ref"
                contrib = [_dot(act_ref[blk, rows], w_down_ref[rows, :]) for blk in _row_blocks(m)]
                out["f"] = [f_ref[blk, :] + part for blk, part in zip(_row_blocks(m), contrib)]
            else:
                contrib = _dot(act_ref[:, rows], w_down_ref[rows, :])
                f_ref[...] = contrib if group_start == 0 else f_ref[...] + contrib
            group_start = c + 1
        yield


def _ffn_head(front, out, w, h2_ref, ups):
    for blk, rows in enumerate(_row_blocks(h2_ref.shape[0])):
        next(front)
        h2_ref[rows, :] = out["h2"][blk]
        for c in range(FFN_UPS_AHEAD):
            _ffn_issue(w, h2_ref, ups, c, rows)


def _finish(out, gate2):
    blocks = _row_blocks(gate2.shape[0])
    return _row_cat([x1 + _rms_scaled(f, gate2[rows])
                     for rows, x1, f in zip(blocks, out["x1"], out["f"])])


def _fold_mods(mod, g_refs, n_steps):
    g = [g_ref[...] for g_ref in g_refs]
    sh1, sc1, gt1, sh2, sc2, gt2 = [mod[:, i * D_MODEL:(i + 1) * D_MODEL] for i in range(6)]
    folded = [sh1, g[0] * (1.0 + sc1), gt1 * g[1], sh2, g[2] * (1.0 + sc2), gt2 * g[3]]
    return [_row_cat([a] * n_steps) for a in folded]


def _trunk(x, mods, w, h2_ref, f_ref, act_ref, *, nb, cnt_fn, hist_pool, hist_conv, hist_ffn,
           sink_pool, sink_conv, sink_ffn):
    out = {}
    front = _front_stages(x, _front_head(x, mods), mods, w, out, nb=nb, cnt_fn=cnt_fn,
                          hist_pool=hist_pool, hist_conv=hist_conv, sink_pool=sink_pool,
                          sink_conv=sink_conv)
    for _ in range(3):
        next(front)
    ups = _UpValues()
    _ffn_head(front, out, w, h2_ref, ups)
    for _ in _ffn_stages(w, h2_ref, f_ref, act_ref, ups, out, nb=nb, hist_ffn=hist_ffn,
                         sink_ffn=sink_ffn):
        pass
    return _finish(out, mods[5])


def _stream_cast(src, dst, stage, sems, sem_row):
    slots, rows = stage.shape[0], stage.shape[1]
    n = src.shape[0] // rows
    ahead = slots - 1

    def chunk(i, slot):
        return pltpu.make_async_copy(src.at[pl.ds(i * rows, rows), :], stage.at[slot],
                                     sems.at[sem_row, slot])

    for i in range(min(ahead, n)):
        chunk(i, i).start()

    def body(i, carry):
        slot = lax.rem(i, slots)

        @pl.when(i + ahead < n)
        def _():
            chunk(i + ahead, lax.rem(i + ahead, slots)).start()

        chunk(i, slot).wait()
        dst[pl.ds(pl.multiple_of(i * rows, rows), rows), :] = stage[slot].astype(_bf16)
        return carry

    lax.fori_loop(0, n, body, 0)


def _prompt_kernel(x_hbm, mod_ref, *rest, nb, tt, n_steps):
    g_refs, rest = rest[:4], rest[4:]
    (w_in_hbm, pool_w_ref, pool_scale_ref, conv_w_ref, w_out_hbm, w_up_hbm, ffn_conv_w_ref,
     w_down_hbm) = rest[:8]
    y_hbm, pool_out_ref, conv_out_ref, ffn_out_ref = rest[8:12]
    bf_out = rest[12:16]
    (x_buf, y_buf, pool_carry, conv_carry, ffn_carry, h2_ref, f_ref, act_ref,
     w_in_bf, w_out_bf, w_up_bf, w_down_bf, stage_in, stage_sq, stage_up,
     x_sems, y_sems, stage_sems, out_sems) = rest[16:]
    bf_vmem = (w_in_bf, w_out_bf, w_up_bf, w_down_bf)
    w = (g_refs, w_in_bf, pool_w_ref, pool_scale_ref, conv_w_ref, w_out_bf, w_up_bf, ffn_conv_w_ref,
         w_down_bf)
    j = pl.program_id(0)

    def bf_writeback(k):
        return pltpu.make_async_copy(bf_vmem[k], bf_out[k], out_sems.at[k])

    m = tt * nb
    slot = lax.rem(j, 2)

    def tile_copies(tile, buf_slot, to_hbm):
        copies = []
        for b in range(nb):
            hbm = (y_hbm if to_hbm else x_hbm).at[b, pl.ds(tile * tt, tt), :]
            if to_hbm:
                copies.append(pltpu.make_async_copy(y_buf.at[buf_slot, :, b, :], hbm,
                                                    y_sems.at[buf_slot, b]))
            else:
                copies.append(pltpu.make_async_copy(hbm, x_buf.at[buf_slot, :, b, :],
                                                    x_sems.at[buf_slot, b]))
        return copies

    def start(copies):
        for copy in copies:
            copy.start()

    def wait(copies):
        for copy in copies:
            copy.wait()

    @pl.when(j == 0)
    def _():
        start(tile_copies(0, 0, False))
        pool_carry[...] = jnp.zeros_like(pool_carry)
        conv_carry[...] = jnp.zeros_like(conv_carry)
        ffn_carry[...] = jnp.zeros_like(ffn_carry)
        _stream_cast(w_in_hbm.at[0], w_in_bf, stage_in, stage_sems, 0)
        _stream_cast(w_out_hbm.at[0], w_out_bf, stage_sq, stage_sems, 1)
        _stream_cast(w_up_hbm.at[0], w_up_bf, stage_up, stage_sems, 2)
        _stream_cast(w_down_hbm.at[0], w_down_bf, stage_sq, stage_sems, 1)
        for k in range(len(bf_vmem)):
            bf_writeback(k).start()

    @pl.when(j + 1 < n_steps)
    def _():
        start(tile_copies(j + 1, 1 - slot, False))

    wait(tile_copies(j, slot, False))
    x = x_buf[slot].reshape(m, D_MODEL)

    mods = _fold_mods(mod_ref[...], g_refs, tt)

    t_idx = j * tt + lax.shift_right_logical(
        lax.broadcasted_iota(jnp.int32, (m, POOL_GC), 0), nb.bit_length() - 1)

    def cnt_fn(win):
        return jnp.minimum(win, t_idx + 1).astype(_f32)

    def carry_rows(ref, first, n):
        return _lane_cat([ref[first + q] for q in range(n)])

    def keep_rows(ref, first, ext):
        rows = ref.shape[1]
        for q in range(ext.shape[1] // LANES):
            ref[first + q] = ext[ext.shape[0] - rows:, q * LANES:(q + 1) * LANES]

    y = _trunk(
        x, mods, w, h2_ref, f_ref, act_ref, nb=nb, cnt_fn=cnt_fn,
        hist_pool=lambda: carry_rows(pool_carry, 0, D_POOL // LANES),
        hist_conv=lambda: carry_rows(conv_carry, 0, D_CONV // LANES),
        hist_ffn=lambda cols: carry_rows(ffn_carry, cols.start // LANES, FF_CHUNK // LANES),
        sink_pool=lambda ext: keep_rows(pool_carry, 0, ext),
        sink_conv=lambda ext: keep_rows(conv_carry, 0, ext),
        sink_ffn=lambda cols, ext: keep_rows(ffn_carry, cols.start // LANES, ext))

    @pl.when(j >= 2)
    def _():
        wait(tile_copies(j - 2, slot, True))

    y_buf[slot] = y.reshape(tt, nb, D_MODEL)
    start(tile_copies(j, slot, True))

    @pl.when(j == n_steps - 1)
    def _():
        def emit(out_ref, carry, n):
            first = carry.shape[1] // nb - n
            for b in range(nb):
                for q in range(carry.shape[0]):
                    out_ref[0, b, :, q * LANES:(q + 1) * LANES] = (
                        carry[q, pl.ds(first * nb + b, n, stride=nb), :])
        pool_rows = carry_rows(pool_carry, 0, D_POOL // LANES)
        for k in range(POOL_BUF):
            first = (POOL_HALO - POOL_BUF + k) * nb
            pool_out_ref[k] = pool_rows[first:first + nb]
        emit(conv_out_ref, conv_carry, CONV_HIST)
        emit(ffn_out_ref, ffn_carry, CONV_HIST)
        for k in range(len(bf_vmem)):
            bf_writeback(k).wait()
        if n_steps >= 2:
            wait(tile_copies(j - 1, 1 - slot, True))
        wait(tile_copies(j, slot, True))


def _prompt(x, mod, g, w_in, pool_w, pool_scale, conv_w, w_out, w_up, ffn_conv_w, w_down):
    nb, seq, _ = x.shape
    assert nb == SUBLANES, "one vreg row per time step"
    tt = PROMPT_TT
    n_steps = seq // tt
    m = tt * nb
    big = (w_in, w_out, w_up, w_down)
    stage_widths = (D_IN_PROJ, D_MODEL, 2 * D_FF)
    in_hbm = pl.BlockSpec(memory_space=pl.ANY)

    def const_spec(a):
        nd = a.ndim
        return pl.BlockSpec(a.shape, lambda j: (0,) * nd, pipeline_mode=pl.Buffered(1))

    return pl.pallas_call(
        functools.partial(_prompt_kernel, nb=nb, tt=tt, n_steps=n_steps),
        grid=(n_steps,),
        in_specs=[in_hbm, const_spec(mod),
                  *[const_spec(a) for a in g], in_hbm, const_spec(pool_w), const_spec(pool_scale),
                  const_spec(conv_w), in_hbm, in_hbm, const_spec(ffn_conv_w), in_hbm],
        out_specs=[
            in_hbm,
            pl.BlockSpec((POOL_BUF, nb, D_POOL), lambda j: (0, 0, 0)),
            pl.BlockSpec((1, nb, CONV_HIST, D_CONV), lambda j: (0, 0, 0, 0)),
            pl.BlockSpec((1, nb, CONV_HIST, 2 * D_FF), lambda j: (0, 0, 0, 0)),
        ] + [pl.BlockSpec(memory_space=pl.ANY) for _ in big],
        out_shape=[
            jax.ShapeDtypeStruct((nb, seq, D_MODEL), _f32),
            jax.ShapeDtypeStruct((POOL_BUF, nb, D_POOL), _f32),
            jax.ShapeDtypeStruct((1, nb, CONV_HIST, D_CONV), _f32),
            jax.ShapeDtypeStruct((1, nb, CONV_HIST, 2 * D_FF), _f32),
        ] + [jax.ShapeDtypeStruct(a.shape[1:], _bf16) for a in big],
        scratch_shapes=[
            pltpu.VMEM((2, tt, nb, D_MODEL), _f32),
            pltpu.VMEM((2, tt, nb, D_MODEL), _f32),
            pltpu.VMEM((D_POOL // LANES, POOL_HALO * nb, LANES), _f32),
            pltpu.VMEM((D_CONV // LANES, CONV_HIST * nb, LANES), _f32),
            pltpu.VMEM((2 * D_FF // LANES, CONV_HIST * nb, LANES), _f32),
            pltpu.VMEM((m, D_MODEL), _bf16),
            pltpu.VMEM((m, D_MODEL), _f32),
            pltpu.VMEM((m, D_FF), _bf16),
        ] + [pltpu.VMEM(a.shape[1:], _bf16) for a in big]
        + [pltpu.VMEM((WEIGHT_STAGE_SLOTS, WEIGHT_STAGE_ROWS[c], c), _f32) for c in stage_widths]
        + [pltpu.SemaphoreType.DMA((2, nb)), pltpu.SemaphoreType.DMA((2, nb)),
           pltpu.SemaphoreType.DMA((len(stage_widths), WEIGHT_STAGE_SLOTS)),
           pltpu.SemaphoreType.DMA((len(big),))],
        compiler_params=pltpu.CompilerParams(
            dimension_semantics=("arbitrary",),
            vmem_limit_bytes=V7X_VMEM_LIMIT_BYTES),
        name="prompt_trunk",
    )(x, mod, *g, w_in, pool_w, pool_scale, conv_w, w_out, w_up, ffn_conv_w, w_down)


def _sample_kernel(*refs, nb, nt):
    refs = iter(refs)

    def take(n):
        return [next(refs) for _ in range(n)]

    x_hbm, sp_hbm, sc_hbm, sf_hbm, mod_hbm = take(5)
    g_refs = take(4)
    (w_in_hbm, pool_w_ref, pool_scale_ref, conv_w_ref, w_out_hbm, w_up_hbm, ffn_conv_w_ref,
     w_down_hbm) = take(8)
    y_hbm, pool_hbm, conv_hbm, ffn_hbm = take(4)
    (x_ref, sp_ref, sc_ref, mod_ref, w_in_ref, w_out_ref, w_up_ref, w_down_ref, sf_ref) = take(9)
    y_ref, pool_out_ref, conv_out_ref, ffn_out_ref = take(4)
    h2_ref, f_ref, act_ref, in_sems, out_sems = take(5)
    w = (g_refs, w_in_ref, pool_w_ref, pool_scale_ref, conv_w_ref, w_out_ref, w_up_ref,
         ffn_conv_w_ref, w_down_ref)

    n_groups = -(-N_FF_CHUNKS // FF_DOWN_GROUP)

    def group_cols(gi, base):
        first = gi * FF_DOWN_GROUP
        n = min(FF_DOWN_GROUP, N_FF_CHUNKS - first)
        return pl.ds(base + first * FF_CHUNK, n * FF_CHUNK)

    in_order = [("x", [(x_hbm.at[:, t, :], x_ref.at[t]) for t in range(nt)]),
                ("mod", [(mod_hbm, mod_ref)]), ("w_in", [(w_in_hbm, w_in_ref)]),
                ("pool", [(sp_hbm, sp_ref)]),
                ("conv", [(sc_hbm.at[0, :, k, :], sc_ref.at[k]) for k in range(CONV_HIST)]),
                ("w_out", [(w_out_hbm, w_out_ref)]),
                ("ffn", [(sf_hbm.at[0, :, k, :], sf_ref.at[k]) for k in range(CONV_HIST)])]
    for gi in range(n_groups):
        cols = [group_cols(gi, base) for base in (0, D_FF)]
        rows = group_cols(gi, 0)
        in_order.append((f"ffn_w{gi}", [(w_up_hbm.at[:, c], w_up_ref.at[:, c]) for c in cols]
                         + [(w_down_hbm.at[rows, :], w_down_ref.at[rows, :])]))
    fetch, k = {}, 0
    for name, pairs in in_order:
        fetch[name] = [pltpu.make_async_copy(src, dst, in_sems.at[k + i])
                       for i, (src, dst) in enumerate(pairs)]
        k += len(pairs)

    out_order = [("y", [(y_ref.at[t], y_hbm.at[:, t, :]) for t in range(nt)]),
                 ("pool", [(pool_out_ref, pool_hbm)]),
                 ("conv", [(conv_out_ref.at[k], conv_hbm.at[0, :, k, :]) for k in range(CONV_HIST)])]
    for gi in range(n_groups):
        out_order.append((f"ffn{gi}", [
            (ffn_out_ref.at[k, :, group_cols(gi, base)], ffn_hbm.at[0, :, k, group_cols(gi, base)])
            for base in (0, D_FF) for k in range(CONV_HIST)]))
    send, k = {}, 0
    for name, pairs in out_order:
        send[name] = [pltpu.make_async_copy(src, dst, out_sems.at[k + i])
                      for i, (src, dst) in enumerate(pairs)]
        k += len(pairs)

    def start(copies):
        for copy in copies:
            copy.start()

    def wait(copies):
        for copy in copies:
            copy.wait()

    for name, _ in in_order:
        start(fetch[name])

    def hist_ffn(cols):
        return _row_cat([sf_ref[k, :, cols] for k in range(CONV_HIST)])

    def sink_ffn(cols, ext):
        first = ext.shape[0] - CONV_HIST * nb
        for k in range(CONV_HIST):
            ffn_out_ref[k, :, cols] = ext[first + k * nb:first + (k + 1) * nb]

    def sink_pool(ext):
        first = ext.shape[0] - POOL_BUF * nb
        for k in range(POOL_BUF):
            pool_out_ref[k] = ext[first + k * nb:first + (k + 1) * nb]

    def sink_conv(ext):
        first = ext.shape[0] - CONV_HIST * nb
        for k in range(CONV_HIST):
            conv_out_ref[k] = ext[first + k * nb:first + (k + 1) * nb]

    wait(fetch["x"])
    wait(fetch["mod"])
    x = _row_cat([x_ref[t] for t in range(nt)])
    mods = _fold_mods(mod_ref[...], g_refs, nt)
    h = _front_head(x, mods)
    mixed = {}
    front = _front_stages(
        x, h, mods, w, mixed, nb=nb,
        cnt_fn=lambda win: float(min(win, PAST_LEN + 1)),
        hist_pool=lambda: _row_cat([sp_ref[k] for k in range(POOL_BUF)]),
        hist_conv=lambda: _row_cat([sc_ref[k] for k in range(CONV_HIST)]),
        sink_pool=sink_pool, sink_conv=sink_conv)
    wait(fetch["w_in"])
    next(front)
    wait(fetch["pool"])
    wait(fetch["conv"])
    next(front)
    start(send["pool"])
    start(send["conv"])
    wait(fetch["w_out"])
    next(front)

    wait(fetch["ffn"])
    wait(fetch["ffn_w0"])
    ups = _UpValues()
    _ffn_head(front, mixed, w, h2_ref, ups)
    ffn = _ffn_stages(w, h2_ref, f_ref, act_ref, ups, mixed, nb=nb, hist_ffn=hist_ffn,
                      sink_ffn=sink_ffn)
    groups_sent = 0
    for c in range(N_FF_CHUNKS):
        ahead = c + FFN_UPS_AHEAD
        if ahead < N_FF_CHUNKS and ahead % FF_DOWN_GROUP == 0:
            wait(fetch[f"ffn_w{ahead // FF_DOWN_GROUP}"])
            while (groups_sent + 1) * FF_DOWN_GROUP <= c:
                start(send[f"ffn{groups_sent}"])
                groups_sent += 1
        next(ffn)
    y = _finish(mixed, mods[5])
    for t in range(nt):
        y_ref[t] = y[t * nb:(t + 1) * nb]
    for gi in range(groups_sent, n_groups):
        start(send[f"ffn{gi}"])
    start(send["y"])
    for copies in send.values():
        wait(copies)


def _sample(x, sp, sc, sf, mod, g, w_in, pool_w, pool_scale, conv_w, w_out, w_up, ffn_conv_w, w_down):
    nb, nt, _ = x.shape
    hbm = pl.BlockSpec(memory_space=pl.ANY)
    vmem = pl.BlockSpec(memory_space=pltpu.VMEM)
    out_shape = [
        jax.ShapeDtypeStruct(x.shape, _f32),
        jax.ShapeDtypeStruct(sp.shape, _f32),
        jax.ShapeDtypeStruct(sc.shape, _f32),
        jax.ShapeDtypeStruct(sf.shape, _f32),
    ]
    step_major = [pltpu.VMEM((nt, nb, D_MODEL), _f32),
                  pltpu.VMEM(sp.shape, _f32),
                  pltpu.VMEM((CONV_HIST, nb, D_CONV), _f32)]
    n_fetches = nt + 1 + 1 + 1 + CONV_HIST + 1 + CONV_HIST + 3 * -(-N_FF_CHUNKS // FF_DOWN_GROUP)
    n_sends = nt + 1 + CONV_HIST + 2 * CONV_HIST * -(-N_FF_CHUNKS // FF_DOWN_GROUP)
    return pl.pallas_call(
        functools.partial(_sample_kernel, nb=nb, nt=nt),
        in_specs=[hbm, hbm, hbm, hbm, hbm, vmem, vmem, vmem, vmem, hbm, vmem, vmem, vmem, hbm, hbm,
                  vmem, hbm],
        out_specs=[hbm] * len(out_shape),
        out_shape=out_shape,
        scratch_shapes=step_major
        + [pltpu.VMEM(a.shape, a.dtype) for a in (mod, w_in, w_out, w_up, w_down)]
        + [pltpu.VMEM((CONV_HIST, nb, 2 * D_FF), _f32)]
        + step_major
        + [pltpu.VMEM((CONV_HIST, nb, 2 * D_FF), _f32)]
        + [
            pltpu.VMEM((nt * nb, D_MODEL), _bf16),
            pltpu.VMEM((nt * nb, D_MODEL), _f32),
            pltpu.VMEM((nt * nb, D_FF), _bf16),
            pltpu.SemaphoreType.DMA((n_fetches,)),
            pltpu.SemaphoreType.DMA((n_sends,)),
        ],
        compiler_params=pltpu.CompilerParams(vmem_limit_bytes=V7X_VMEM_LIMIT_BYTES),
        name="sample_trunk",
    )(x, sp, sc, sf, mod, *g, w_in, pool_w, pool_scale, conv_w, w_out, w_up, ffn_conv_w, w_down)


def kernel(x_prompt, x_sample, state_pool, state_conv, state_ffn, c_prompt, c_sample, w_ada, b_ada,
           g_pre_mix, g_post_mix, g_pre_ffn, g_post_ffn, w_in, pool_w, pool_scale, conv_w, w_out,
           ffn_w_up, ffn_conv_w, ffn_w_down):
    assert w_ada.shape[0] == 1, "single trunk layer"
    mod_p, mod_s = _ada(c_prompt, c_sample, w_ada[0], b_ada)
    g = (g_pre_mix, g_post_mix, g_pre_ffn, g_post_ffn)
    conv_taps = conv_w.reshape(1, CONV_W * D_CONV)
    ffn_taps = ffn_conv_w.reshape(1, CONV_W * 2 * D_FF)
    y_p, pool_p, conv_p, ffn_p, w_in_bf, w_out_bf, w_up_bf, w_down_bf = _prompt(
        x_prompt, mod_p, g, w_in, pool_w[0], pool_scale, conv_taps, w_out, ffn_w_up, ffn_taps,
        ffn_w_down)
    y_s, pool_s, conv_s, ffn_s = _sample(
        x_sample, state_pool[0].transpose(1, 0, 2), state_conv, state_ffn, mod_s, g, w_in_bf,
        pool_w[0], pool_scale, conv_taps, w_out_bf, w_up_bf, ffn_taps, w_down_bf)
    return (y_p, y_s, pool_p.transpose(1, 0, 2)[None], conv_p, ffn_p,
            pool_s.transpose(1, 0, 2)[None], conv_s, ffn_s)
```

```python
import functools

import jax
import jax.numpy as jnp
from jax import lax
from jax.experimental import pallas as pl
from jax.experimental.pallas import tpu as pltpu

D_MODEL = 1024
D_POOL = 512
D_CONV = 512
POOL_WINDOWS = (2, 4, 8, 16)
POOL_GC = 128
POOL_BUF = 15
CONV_W = 3
CONV_HIST = CONV_W - 1
D_FF = 2816
D_IN_PROJ = D_POOL + 3 * D_CONV
RMS_EPS = 1e-6
PAST_LEN = 16384

V7X_VMEM_LIMIT_BYTES = 58 * 1024 * 1024
SUBLANES = 8
LANES = 128
FF_CHUNK = 256
N_FF_CHUNKS = D_FF // FF_CHUNK
FFN_UPS_AHEAD = 1
FF_DOWN_GROUP = 3
ROW_BLOCKS = 2
PROMPT_TT = 64
WEIGHT_STAGE_ROWS = {2048: 128, 1024: 256, 5632: 64}
WEIGHT_STAGE_SLOTS = 4
ADA_ROWS = 256
POOL_HALO = 16

_bf16 = jnp.bfloat16
_f32 = jnp.float32


def _dot(a, b):
    return jnp.dot(a, b, preferred_element_type=_f32)


def _rms_scaled(x, scale):
    ms = jnp.mean(x * x, axis=-1, keepdims=True)
    return x * lax.rsqrt(ms + RMS_EPS) * scale


def _silu(a):
    return a * jax.nn.sigmoid(a)


def _lane_cat(parts):
    return jnp.concatenate(parts, axis=1)


def _row_cat(parts):
    return jnp.concatenate(parts, axis=0)


def _ada_kernel(cp_ref, cs_ref, w_hbm, b_ref, mp_ref, ms_ref, w_buf, sems):
    n_blocks, rows = w_buf.shape[0], w_buf.shape[1]

    def fetch(k):
        return pltpu.make_async_copy(w_hbm.at[pl.ds(k * rows, rows), :], w_buf.at[k], sems.at[k])

    for k in range(n_blocks):
        fetch(k).start()
    c = _silu(_row_cat([cp_ref[...], cs_ref[...]])).astype(_bf16)
    mod = b_ref[...]
    for k in range(n_blocks):
        fetch(k).wait()
        mod = mod + _dot(c[:, k * rows:(k + 1) * rows], w_buf[k].astype(_bf16))
    bp = cp_ref.shape[0]
    mp_ref[...] = mod[:bp]
    ms_ref[...] = mod[bp:]


def _ada(c_prompt, c_sample, w_ada, b_ada):
    n = w_ada.shape[1]
    bp, bs = c_prompt.shape[0], c_sample.shape[0]
    vmem = pl.BlockSpec(memory_space=pltpu.VMEM)
    return pl.pallas_call(
        _ada_kernel,
        in_specs=[vmem, vmem, pl.BlockSpec(memory_space=pl.ANY), vmem],
        out_specs=[vmem, vmem],
        out_shape=[
            jax.ShapeDtypeStruct((bp, n), _f32),
            jax.ShapeDtypeStruct((bs, n), _f32),
        ],
        scratch_shapes=[
            pltpu.VMEM((D_MODEL // ADA_ROWS, ADA_ROWS, n), _f32),
            pltpu.SemaphoreType.DMA((D_MODEL // ADA_ROWS,)),
        ],
        compiler_params=pltpu.CompilerParams(vmem_limit_bytes=V7X_VMEM_LIMIT_BYTES),
        name="ada_mod",
    )(c_prompt, c_sample, w_ada, b_ada)


def _taps(w_ref, width, cols):
    return [w_ref[:, k * width + cols.start:k * width + cols.stop] for k in range(CONV_W)]


def _row_blocks(m):
    n = m // ROW_BLOCKS
    return [slice(k * n, (k + 1) * n) for k in range(ROW_BLOCKS)]


def _front_head(x, mods):
    return (_rms_scaled(x, mods[1]) + mods[0]).astype(_bf16)


def _front_stages(x, h, mods, w, out, *, nb, cnt_fn, hist_pool, hist_conv, sink_pool, sink_conv):
    (_, w_in_ref, pool_w_ref, pool_scale_ref, conv_w_ref, w_out_ref) = w[:6]
    _, _, gate1, sh2, scale2, _ = mods
    m = x.shape[0]

    proj = _dot(h[...], w_in_ref[...])
    yield
    v_pool = proj[:, :D_POOL]
    x_conv = proj[:, D_POOL:D_POOL + D_CONV]
    gate_b = proj[:, D_POOL + D_CONV:D_POOL + 2 * D_CONV]
    gate_c = proj[:, D_POOL + 2 * D_CONV:]

    pool_ext = _row_cat([hist_pool(), v_pool])
    halo = (pool_ext.shape[0] - m) // nb
    y_pool = []
    for gi, win in enumerate(POOL_WINDOWS):
        sl = slice(gi * POOL_GC, (gi + 1) * POOL_GC)
        s = pool_ext[:, sl]
        step = 1
        while step < win:
            n = s.shape[0]
            s = s[step * nb:] + s[:n - step * nb]
            step *= 2
        first = (halo - (win - 1)) * nb
        d = s[first:first + m] / cnt_fn(win) - v_pool[:, sl]
        y_pool.append(_dot(d.astype(_bf16), pool_w_ref[gi].astype(_bf16)))
    y_pool = _lane_cat(y_pool) * pool_scale_ref[...]
    sink_pool(pool_ext)

    cx = gate_c * x_conv
    conv_ext = _row_cat([hist_conv(), cx])
    cw = _taps(conv_w_ref, D_CONV, slice(0, D_CONV))
    conv = conv_ext[0:m] * cw[0] + conv_ext[nb:nb + m] * cw[1] + cx * cw[2]
    y_conv = gate_b * conv
    sink_conv(conv_ext)
    mixed = _lane_cat([y_pool, y_conv]).astype(_bf16)
    yield
    blocks = _row_blocks(m)
    mix = [_dot(mixed[rows], w_out_ref[...]) for rows in blocks]
    yield
    out["x1"], out["h2"] = [], []
    for rows, mix_rows in zip(blocks, mix):
        x1 = x[rows] + _rms_scaled(mix_rows, gate1[rows])
        out["x1"].append(x1)
        out["h2"].append((_rms_scaled(x1, scale2[rows]) + sh2[rows]).astype(_bf16))
        yield


class _UpValues:
    def __init__(self):
        self._v = {}

    def put(self, c, half, val):
        self._v.setdefault((c, half), []).append(val)

    def get(self, c, half):
        return _row_cat(self._v.pop((c, half)))


def _ffn_cols(c):
    return [slice(base + c * FF_CHUNK, base + (c + 1) * FF_CHUNK) for base in (0, D_FF)]


def _ffn_issue(w, h2_ref, ups, c, rows=slice(None)):
    for half, cols in enumerate(_ffn_cols(c)):
        ups.put(c, half, _dot(h2_ref[rows, :], w[6][:, cols]))


def _ffn_stages(w, h2_ref, f_ref, act_ref, ups, out, *, nb, hist_ffn, sink_ffn):
    _, ffn_conv_w_ref, w_down_ref = w[6:]
    m = h2_ref.shape[0]
    group_start = 0
    for c in range(N_FF_CHUNKS):
        if c + FFN_UPS_AHEAD < N_FF_CHUNKS:
            _ffn_issue(w, h2_ref, ups, c + FFN_UPS_AHEAD)
        halves = []
        for half, cols in enumerate(_ffn_cols(c)):
            up = ups.get(c, half)
            up_ext = _row_cat([hist_ffn(cols), up])
            fw = _taps(ffn_conv_w_ref, 2 * D_FF, cols)
            halves.append(up_ext[0:m] * fw[0] + up_ext[nb:nb + m] * fw[1] + up * fw[2])
            sink_ffn(cols, up_ext)
        act_ref[:, c * FF_CHUNK:(c + 1) * FF_CHUNK] = (_silu(halves[0]) * halves[1]).astype(_bf16)
        if (c + 1) % FF_DOWN_GROUP == 0 or c + 1 == N_FF_CHUNKS:
            rows = slice(group_start * FF_CHUNK, (c + 1) * FF_CHUNK)
            if c + 1 == N_FF_CHUNKS:
                assert group_start > 0, "earlier groups have left their partial sums in f_ref"
                contrib = [_dot(act_ref[blk, rows], w_down_ref[rows, :]) for blk in _row_blocks(m)]
                out["f"] = [f_ref[blk, :] + part for blk, part in zip(_row_blocks(m), contrib)]
            else:
                contrib = _dot(act_ref[:, rows], w_down_ref[rows, :])
                f_ref[...] = contrib if group_start == 0 else f_ref[...] + contrib
            group_start = c + 1
        yield


def _ffn_head(front, out, w, h2_ref, ups):
    for blk, rows in enumerate(_row_blocks(h2_ref.shape[0])):
        next(front)
        h2_ref[rows, :] = out["h2"][blk]
        for c in range(FFN_UPS_AHEAD):
            _ffn_issue(w, h2_ref, ups, c, rows)


def _finish(out, gate2):
    blocks = _row_blocks(gate2.shape[0])
    return _row_cat([x1 + _rms_scaled(f, gate2[rows])
                     for rows, x1, f in zip(blocks, out["x1"], out["f"])])


def _fold_mods(mod, g_refs, n_steps):
    g = [g_ref[...] for g_ref in g_refs]
    sh1, sc1, gt1, sh2, sc2, gt2 = [mod[:, i * D_MODEL:(i + 1) * D_MODEL] for i in range(6)]
    folded = [sh1, g[0] * (1.0 + sc1), gt1 * g[1], sh2, g[2] * (1.0 + sc2), gt2 * g[3]]
    return [_row_cat([a] * n_steps) for a in folded]


def _trunk(x, mods, w, h2_ref, f_ref, act_ref, *, nb, cnt_fn, hist_pool, hist_conv, hist_ffn,
           sink_pool, sink_conv, sink_ffn):
    out = {}
    front = _front_stages(x, _front_head(x, mods), mods, w, out, nb=nb, cnt_fn=cnt_fn,
                          hist_pool=hist_pool, hist_conv=hist_conv, sink_pool=sink_pool,
                          sink_conv=sink_conv)
    for _ in range(3):
        next(front)
    ups = _UpValues()
    _ffn_head(front, out, w, h2_ref, ups)
    for _ in _ffn_stages(w, h2_ref, f_ref, act_ref, ups, out, nb=nb, hist_ffn=hist_ffn,
                         sink_ffn=sink_ffn):
        pass
    return _finish(out, mods[5])


def _stream_cast(src, dst, stage, sems, sem_row):
    slots, rows = stage.shape[0], stage.shape[1]
    n = src.shape[0] // rows
    ahead = slots - 1

    def chunk(i, slot):
        return pltpu.make_async_copy(src.at[pl.ds(i * rows, rows), :], stage.at[slot],
                                     sems.at[sem_row, slot])

    for i in range(min(ahead, n)):
        chunk(i, i).start()

    def body(i, carry):
        slot = lax.rem(i, slots)

        @pl.when(i + ahead < n)
        def _():
            chunk(i + ahead, lax.rem(i + ahead, slots)).start()

        chunk(i, slot).wait()
        dst[pl.ds(pl.multiple_of(i * rows, rows), rows), :] = stage[slot].astype(_bf16)
        return carry

    lax.fori_loop(0, n, body, 0)


def _prompt_kernel(x_hbm, mod_ref, *rest, nb, tt, n_steps):
    g_refs, rest = rest[:4], rest[4:]
    (w_in_hbm, pool_w_ref, pool_scale_ref, conv_w_ref, w_out_hbm, w_up_hbm, ffn_conv_w_ref,
     w_down_hbm) = rest[:8]
    y_hbm, pool_out_ref, conv_out_ref, ffn_out_ref = rest[8:12]
    bf_out = rest[12:16]
    (x_buf, y_buf, pool_carry, conv_carry, ffn_carry, h2_ref, f_ref, act_ref,
     w_in_bf, w_out_bf, w_up_bf, w_down_bf, stage_in, stage_sq, stage_up,
     x_sems, y_sems, stage_sems, out_sems) = rest[16:]
    bf_vmem = (w_in_bf, w_out_bf, w_up_bf, w_down_bf)
    w = (g_refs, w_in_bf, pool_w_ref, pool_scale_ref, conv_w_ref, w_out_bf, w_up_bf, ffn_conv_w_ref,
         w_down_bf)
    j = pl.program_id(0)

    def bf_writeback(k):
        return pltpu.make_async_copy(bf_vmem[k], bf_out[k], out_sems.at[k])

    m = tt * nb
    slot = lax.rem(j, 2)

    def tile_copies(tile, buf_slot, to_hbm):
        copies = []
        for b in range(nb):
            hbm = (y_hbm if to_hbm else x_hbm).at[b, pl.ds(tile * tt, tt), :]
            if to_hbm:
                copies.append(pltpu.make_async_copy(y_buf.at[buf_slot, :, b, :], hbm,
                                                    y_sems.at[buf_slot, b]))
            else:
                copies.append(pltpu.make_async_copy(hbm, x_buf.at[buf_slot, :, b, :],
                                                    x_sems.at[buf_slot, b]))
        return copies

    def start(copies):
        for copy in copies:
            copy.start()

    def wait(copies):
        for copy in copies:
            copy.wait()

    @pl.when(j == 0)
    def _():
        start(tile_copies(0, 0, False))
        pool_carry[...] = jnp.zeros_like(pool_carry)
        conv_carry[...] = jnp.zeros_like(conv_carry)
        ffn_carry[...] = jnp.zeros_like(ffn_carry)
        _stream_cast(w_in_hbm.at[0], w_in_bf, stage_in, stage_sems, 0)
        _stream_cast(w_out_hbm.at[0], w_out_bf, stage_sq, stage_sems, 1)
        _stream_cast(w_up_hbm.at[0], w_up_bf, stage_up, stage_sems, 2)
        _stream_cast(w_down_hbm.at[0], w_down_bf, stage_sq, stage_sems, 1)
        for k in range(len(bf_vmem)):
            bf_writeback(k).start()

    @pl.when(j + 1 < n_steps)
    def _():
        start(tile_copies(j + 1, 1 - slot, False))

    @pl.when(j >= 2)
    def _():
        wait(tile_copies(j - 2, slot, True))

    wait(tile_copies(j, slot, False))
    x = x_buf[slot].reshape(m, D_MODEL)

    mods = _fold_mods(mod_ref[...], g_refs, tt)

    t_idx = j * tt + lax.shift_right_logical(
        lax.broadcasted_iota(jnp.int32, (m, POOL_GC), 0), nb.bit_length() - 1)

    def cnt_fn(win):
        return jnp.minimum(win, t_idx + 1).astype(_f32)

    def carry_rows(ref, first, n):
        return _lane_cat([ref[first + q] for q in range(n)])

    def keep_rows(ref, first, ext):
        rows = ref.shape[1]
        for q in range(ext.shape[1] // LANES):
            ref[first + q] = ext[ext.shape[0] - rows:, q * LANES:(q + 1) * LANES]

    y = _trunk(
        x, mods, w, h2_ref, f_ref, act_ref, nb=nb, cnt_fn=cnt_fn,
        hist_pool=lambda: carry_rows(pool_carry, 0, D_POOL // LANES),
        hist_conv=lambda: carry_rows(conv_carry, 0, D_CONV // LANES),
        hist_ffn=lambda cols: carry_rows(ffn_carry, cols.start // LANES, FF_CHUNK // LANES),
        sink_pool=lambda ext: keep_rows(pool_carry, 0, ext),
        sink_conv=lambda ext: keep_rows(conv_carry, 0, ext),
        sink_ffn=lambda cols, ext: keep_rows(ffn_carry, cols.start // LANES, ext))

    y_buf[slot] = y.reshape(tt, nb, D_MODEL)
    start(tile_copies(j, slot, True))

    @pl.when(j == n_steps - 1)
    def _():
        def emit(out_ref, carry, n):
            first = carry.shape[1] // nb - n
            for b in range(nb):
                for q in range(carry.shape[0]):
                    out_ref[0, b, :, q * LANES:(q + 1) * LANES] = (
                        carry[q, pl.ds(first * nb + b, n, stride=nb), :])
        pool_rows = carry_rows(pool_carry, 0, D_POOL // LANES)
        for k in range(POOL_BUF):
            first = (POOL_HALO - POOL_BUF + k) * nb
            pool_out_ref[k] = pool_rows[first:first + nb]
        emit(conv_out_ref, conv_carry, CONV_HIST)
        emit(ffn_out_ref, ffn_carry, CONV_HIST)
        for k in range(len(bf_vmem)):
            bf_writeback(k).wait()
        if n_steps >= 2:
            wait(tile_copies(j - 1, 1 - slot, True))
        wait(tile_copies(j, slot, True))


def _prompt(x, mod, g, w_in, pool_w, pool_scale, conv_w, w_out, w_up, ffn_conv_w, w_down):
    nb, seq, _ = x.shape
    assert nb == SUBLANES, "one vreg row per time step"
    tt = PROMPT_TT
    n_steps = seq // tt
    m = tt * nb
    big = (w_in, w_out, w_up, w_down)
    stage_widths = (D_IN_PROJ, D_MODEL, 2 * D_FF)
    in_hbm = pl.BlockSpec(memory_space=pl.ANY)

    def const_spec(a):
        nd = a.ndim
        return pl.BlockSpec(a.shape, lambda j: (0,) * nd, pipeline_mode=pl.Buffered(1))

    return pl.pallas_call(
        functools.partial(_prompt_kernel, nb=nb, tt=tt, n_steps=n_steps),
        grid=(n_steps,),
        in_specs=[in_hbm, const_spec(mod),
                  *[const_spec(a) for a in g], in_hbm, const_spec(pool_w), const_spec(pool_scale),
                  const_spec(conv_w), in_hbm, in_hbm, const_spec(ffn_conv_w), in_hbm],
        out_specs=[
            in_hbm,
            pl.BlockSpec((POOL_BUF, nb, D_POOL), lambda j: (0, 0, 0)),
            pl.BlockSpec((1, nb, CONV_HIST, D_CONV), lambda j: (0, 0, 0, 0)),
            pl.BlockSpec((1, nb, CONV_HIST, 2 * D_FF), lambda j: (0, 0, 0, 0)),
        ] + [pl.BlockSpec(memory_space=pl.ANY) for _ in big],
        out_shape=[
            jax.ShapeDtypeStruct((nb, seq, D_MODEL), _f32),
            jax.ShapeDtypeStruct((POOL_BUF, nb, D_POOL), _f32),
            jax.ShapeDtypeStruct((1, nb, CONV_HIST, D_CONV), _f32),
            jax.ShapeDtypeStruct((1, nb, CONV_HIST, 2 * D_FF), _f32),
        ] + [jax.ShapeDtypeStruct(a.shape[1:], _bf16) for a in big],
        scratch_shapes=[
            pltpu.VMEM((2, tt, nb, D_MODEL), _f32),
            pltpu.VMEM((2, tt, nb, D_MODEL), _f32),
            pltpu.VMEM((D_POOL // LANES, POOL_HALO * nb, LANES), _f32),
            pltpu.VMEM((D_CONV // LANES, CONV_HIST * nb, LANES), _f32),
            pltpu.VMEM((2 * D_FF // LANES, CONV_HIST * nb, LANES), _f32),
            pltpu.VMEM((m, D_MODEL), _bf16),
            pltpu.VMEM((m, D_MODEL), _f32),
            pltpu.VMEM((m, D_FF), _bf16),
        ] + [pltpu.VMEM(a.shape[1:], _bf16) for a in big]
        + [pltpu.VMEM((WEIGHT_STAGE_SLOTS, WEIGHT_STAGE_ROWS[c], c), _f32) for c in stage_widths]
        + [pltpu.SemaphoreType.DMA((2, nb)), pltpu.SemaphoreType.DMA((2, nb)),
           pltpu.SemaphoreType.DMA((len(stage_widths), WEIGHT_STAGE_SLOTS)),
           pltpu.SemaphoreType.DMA((len(big),))],
        compiler_params=pltpu.CompilerParams(
            dimension_semantics=("arbitrary",),
            vmem_limit_bytes=V7X_VMEM_LIMIT_BYTES),
        name="prompt_trunk",
    )(x, mod, *g, w_in, pool_w, pool_scale, conv_w, w_out, w_up, ffn_conv_w, w_down)


def _sample_kernel(*refs, nb, nt):
    refs = iter(refs)

    def take(n):
        return [next(refs) for _ in range(n)]

    x_hbm, sp_hbm, sc_hbm, sf_hbm, mod_hbm = take(5)
    g_refs = take(4)
    (w_in_hbm, pool_w_ref, pool_scale_ref, conv_w_ref, w_out_hbm, w_up_hbm, ffn_conv_w_ref,
     w_down_hbm) = take(8)
    y_hbm, pool_hbm, conv_hbm, ffn_hbm = take(4)
    (x_ref, sp_ref, sc_ref, mod_ref, w_in_ref, w_out_ref, w_up_ref, w_down_ref, sf_ref) = take(9)
    y_ref, pool_out_ref, conv_out_ref, ffn_out_ref = take(4)
    h2_ref, f_ref, act_ref, in_sems, out_sems = take(5)
    w = (g_refs, w_in_ref, pool_w_ref, pool_scale_ref, conv_w_ref, w_out_ref, w_up_ref,
         ffn_conv_w_ref, w_down_ref)

    n_groups = -(-N_FF_CHUNKS // FF_DOWN_GROUP)

    def group_cols(gi, base):
        first = gi * FF_DOWN_GROUP
        n = min(FF_DOWN_GROUP, N_FF_CHUNKS - first)
        return pl.ds(base + first * FF_CHUNK, n * FF_CHUNK)

    in_order = [("x", [(x_hbm.at[:, t, :], x_ref.at[t]) for t in range(nt)]),
                ("mod", [(mod_hbm, mod_ref)]), ("w_in", [(w_in_hbm, w_in_ref)]),
                ("pool", [(sp_hbm, sp_ref)]),
                ("conv", [(sc_hbm.at[0, :, k, :], sc_ref.at[k]) for k in range(CONV_HIST)]),
                ("w_out", [(w_out_hbm, w_out_ref)]),
                ("ffn", [(sf_hbm.at[0, :, k, :], sf_ref.at[k]) for k in range(CONV_HIST)])]
    for gi in range(n_groups):
        cols = [group_cols(gi, base) for base in (0, D_FF)]
        rows = group_cols(gi, 0)
        in_order.append((f"ffn_w{gi}", [(w_up_hbm.at[:, c], w_up_ref.at[:, c]) for c in cols]
                         + [(w_down_hbm.at[rows, :], w_down_ref.at[rows, :])]))
    fetch, k = {}, 0
    for name, pairs in in_order:
        fetch[name] = [pltpu.make_async_copy(src, dst, in_sems.at[k + i])
                       for i, (src, dst) in enumerate(pairs)]
        k += len(pairs)

    out_order = [("y", [(y_ref.at[t], y_hbm.at[:, t, :]) for t in range(nt)]),
                 ("pool", [(pool_out_ref, pool_hbm)]),
                 ("conv", [(conv_out_ref.at[k], conv_hbm.at[0, :, k, :]) for k in range(CONV_HIST)])]
    for gi in range(n_groups):
        out_order.append((f"ffn{gi}", [
            (ffn_out_ref.at[k, :, group_cols(gi, base)], ffn_hbm.at[0, :, k, group_cols(gi, base)])
            for base in (0, D_FF) for k in range(CONV_HIST)]))
    send, k = {}, 0
    for name, pairs in out_order:
        send[name] = [pltpu.make_async_copy(src, dst, out_sems.at[k + i])
                      for i, (src, dst) in enumerate(pairs)]
        k += len(pairs)

    def start(copies):
        for copy in copies:
            copy.start()

    def wait(copies):
        for copy in copies:
            copy.wait()

    for name, _ in in_order:
        start(fetch[name])

    def hist_ffn(cols):
        return _row_cat([sf_ref[k, :, cols] for k in range(CONV_HIST)])

    def sink_ffn(cols, ext):
        first = ext.shape[0] - CONV_HIST * nb
        for k in range(CONV_HIST):
            ffn_out_ref[k, :, cols] = ext[first + k * nb:first + (k + 1) * nb]

    def sink_pool(ext):
        first = ext.shape[0] - POOL_BUF * nb
        for k in range(POOL_BUF):
            pool_out_ref[k] = ext[first + k * nb:first + (k + 1) * nb]

    def sink_conv(ext):
        first = ext.shape[0] - CONV_HIST * nb
        for k in range(CONV_HIST):
            conv_out_ref[k] = ext[first + k * nb:first + (k + 1) * nb]

    wait(fetch["x"])
    wait(fetch["mod"])
    x = _row_cat([x_ref[t] for t in range(nt)])
    mods = _fold_mods(mod_ref[...], g_refs, nt)
    h = _front_head(x, mods)
    mixed = {}
    front = _front_stages(
        x, h, mods, w, mixed, nb=nb,
        cnt_fn=lambda win: float(min(win, PAST_LEN + 1)),
        hist_pool=lambda: _row_cat([sp_ref[k] for k in range(POOL_BUF)]),
        hist_conv=lambda: _row_cat([sc_ref[k] for k in range(CONV_HIST)]),
        sink_pool=sink_pool, sink_conv=sink_conv)
    wait(fetch["w_in"])
    next(front)
    wait(fetch["pool"])
    wait(fetch["conv"])
    next(front)
    start(send["pool"])
    start(send["conv"])
    wait(fetch["w_out"])
    next(front)

    wait(fetch["ffn"])
    wait(fetch["ffn_w0"])
    ups = _UpValues()
    _ffn_head(front, mixed, w, h2_ref, ups)
    ffn = _ffn_stages(w, h2_ref, f_ref, act_ref, ups, mixed, nb=nb, hist_ffn=hist_ffn,
                      sink_ffn=sink_ffn)
    groups_sent = 0
    for c in range(N_FF_CHUNKS):
        ahead = c + FFN_UPS_AHEAD
        if ahead < N_FF_CHUNKS and ahead % FF_DOWN_GROUP == 0:
            wait(fetch[f"ffn_w{ahead // FF_DOWN_GROUP}"])
            while (groups_sent + 1) * FF_DOWN_GROUP <= c:
                start(send[f"ffn{groups_sent}"])
                groups_sent += 1
        next(ffn)
    y = _finish(mixed, mods[5])
    for t in range(nt):
        y_ref[t] = y[t * nb:(t + 1) * nb]
    for gi in range(groups_sent, n_groups):
        start(send[f"ffn{gi}"])
    start(send["y"])
    for copies in send.values():
        wait(copies)


def _sample(x, sp, sc, sf, mod, g, w_in, pool_w, pool_scale, conv_w, w_out, w_up, ffn_conv_w, w_down):
    nb, nt, _ = x.shape
    hbm = pl.BlockSpec(memory_space=pl.ANY)
    vmem = pl.BlockSpec(memory_space=pltpu.VMEM)
    out_shape = [
        jax.ShapeDtypeStruct(x.shape, _f32),
        jax.ShapeDtypeStruct(sp.shape, _f32),
        jax.ShapeDtypeStruct(sc.shape, _f32),
        jax.ShapeDtypeStruct(sf.shape, _f32),
    ]
    step_major = [pltpu.VMEM((nt, nb, D_MODEL), _f32),
                  pltpu.VMEM(sp.shape, _f32),
                  pltpu.VMEM((CONV_HIST, nb, D_CONV), _f32)]
    n_fetches = nt + 1 + 1 + 1 + CONV_HIST + 1 + CONV_HIST + 3 * -(-N_FF_CHUNKS // FF_DOWN_GROUP)
    n_sends = nt + 1 + CONV_HIST + 2 * CONV_HIST * -(-N_FF_CHUNKS // FF_DOWN_GROUP)
    return pl.pallas_call(
        functools.partial(_sample_kernel, nb=nb, nt=nt),
        in_specs=[hbm, hbm, hbm, hbm, hbm, vmem, vmem, vmem, vmem, hbm, vmem, vmem, vmem, hbm, hbm,
                  vmem, hbm],
        out_specs=[hbm] * len(out_shape),
        out_shape=out_shape,
        scratch_shapes=step_major
        + [pltpu.VMEM(a.shape, a.dtype) for a in (mod, w_in, w_out, w_up, w_down)]
        + [pltpu.VMEM((CONV_HIST, nb, 2 * D_FF), _f32)]
        + step_major
        + [pltpu.VMEM((CONV_HIST, nb, 2 * D_FF), _f32)]
        + [
            pltpu.VMEM((nt * nb, D_MODEL), _bf16),
            pltpu.VMEM((nt * nb, D_MODEL), _f32),
            pltpu.VMEM((nt * nb, D_FF), _bf16),
            pltpu.SemaphoreType.DMA((n_fetches,)),
            pltpu.SemaphoreType.DMA((n_sends,)),
        ],
        compiler_params=pltpu.CompilerParams(vmem_limit_bytes=V7X_VMEM_LIMIT_BYTES),
        name="sample_trunk",
    )(x, sp, sc, sf, mod, *g, w_in, pool_w, pool_scale, conv_w, w_out, w_up, ffn_conv_w, w_down)


def kernel(x_prompt, x_sample, state_pool, state_conv, state_ffn, c_prompt, c_sample, w_ada, b_ada,
           g_pre_mix, g_post_mix, g_pre_ffn, g_post_ffn, w_in, pool_w, pool_scale, conv_w, w_out,
           ffn_w_up, ffn_conv_w, ffn_w_down):
    assert w_ada.shape[0] == 1, "single trunk layer"
    mod_p, mod_s = _ada(c_prompt, c_sample, w_ada[0], b_ada)
    g = (g_pre_mix, g_post_mix, g_pre_ffn, g_post_ffn)
    conv_taps = conv_w.reshape(1, CONV_W * D_CONV)
    ffn_taps = ffn_conv_w.reshape(1, CONV_W * 2 * D_FF)
    y_p, pool_p, conv_p, ffn_p, w_in_bf, w_out_bf, w_up_bf, w_down_bf = _prompt(
        x_prompt, mod_p, g, w_in, pool_w[0], pool_scale, conv_taps, w_out, ffn_w_up, ffn_taps,
        ffn_w_down)
    y_s, pool_s, conv_s, ffn_s = _sample(
        x_sample, state_pool[0].transpose(1, 0, 2), state_conv, state_ffn, mod_s, g, w_in_bf,
        pool_w[0], pool_scale, conv_taps, w_out_bf, w_up_bf, ffn_taps, w_down_bf)
    return (y_p, y_s, pool_p.transpose(1, 0, 2)[None], conv_p, ffn_p,
            pool_s.transpose(1, 0, 2)[None], conv_s, ffn_s)
```

```python
import functools

import jax
import jax.numpy as jnp
from jax import lax
from jax.experimental import pallas as pl
from jax.experimental.pallas import tpu as pltpu

D_MODEL = 1024
D_POOL = 512
D_CONV = 512
POOL_WINDOWS = (2, 4, 8, 16)
POOL_GC = 128
POOL_BUF = 15
CONV_W = 3
CONV_HIST = CONV_W - 1
D_FF = 2816
D_IN_PROJ = D_POOL + 3 * D_CONV
RMS_EPS = 1e-6
PAST_LEN = 16384

V7X_VMEM_LIMIT_BYTES = 58 * 1024 * 1024
SUBLANES = 8
LANES = 128
FF_CHUNK = 256
N_FF_CHUNKS = D_FF // FF_CHUNK
FFN_UPS_AHEAD = 2
FF_DOWN_GROUP = 3
ROW_BLOCKS = 2
PROMPT_TT = 64
WEIGHT_STAGE_ROWS = {2048: 128, 1024: 256, 5632: 64}
WEIGHT_STAGE_SLOTS = 4
ADA_ROWS = 256
DMA_THREADS = 2
POOL_HALO = 16

_bf16 = jnp.bfloat16
_f32 = jnp.float32


def _dot(a, b):
    return jnp.dot(a, b, preferred_element_type=_f32)


def _rms_scaled(x, scale):
    ms = jnp.mean(x * x, axis=-1, keepdims=True)
    return x * lax.rsqrt(ms + RMS_EPS) * scale


def _silu(a):
    return a * jax.nn.sigmoid(a)


def _lane_cat(parts):
    return jnp.concatenate(parts, axis=1)


def _row_cat(parts):
    return jnp.concatenate(parts, axis=0)


def _ada_kernel(cp_ref, cs_ref, w_hbm, b_ref, mp_ref, ms_ref, w_buf, sems):
    n_blocks, rows = w_buf.shape[0], w_buf.shape[1]

    def fetch(k):
        return pltpu.make_async_copy(w_hbm.at[pl.ds(k * rows, rows), :], w_buf.at[k], sems.at[k])

    for k in range(n_blocks):
        fetch(k).start(priority=k % DMA_THREADS)
    c = _silu(_row_cat([cp_ref[...], cs_ref[...]])).astype(_bf16)
    mod = b_ref[...]
    for k in range(n_blocks):
        fetch(k).wait()
        mod = mod + _dot(c[:, k * rows:(k + 1) * rows], w_buf[k].astype(_bf16))
    bp = cp_ref.shape[0]
    mp_ref[...] = mod[:bp]
    ms_ref[...] = mod[bp:]


def _ada(c_prompt, c_sample, w_ada, b_ada):
    n = w_ada.shape[1]
    bp, bs = c_prompt.shape[0], c_sample.shape[0]
    vmem = pl.BlockSpec(memory_space=pltpu.VMEM)
    return pl.pallas_call(
        _ada_kernel,
        in_specs=[vmem, vmem, pl.BlockSpec(memory_space=pl.ANY), vmem],
        out_specs=[vmem, vmem],
        out_shape=[
            jax.ShapeDtypeStruct((bp, n), _f32),
            jax.ShapeDtypeStruct((bs, n), _f32),
        ],
        scratch_shapes=[
            pltpu.VMEM((D_MODEL // ADA_ROWS, ADA_ROWS, n), _f32),
            pltpu.SemaphoreType.DMA((D_MODEL // ADA_ROWS,)),
        ],
        compiler_params=pltpu.CompilerParams(vmem_limit_bytes=V7X_VMEM_LIMIT_BYTES),
        name="ada_mod",
    )(c_prompt, c_sample, w_ada, b_ada)


def _taps(w_ref, width, cols):
    return [w_ref[:, k * width + cols.start:k * width + cols.stop] for k in range(CONV_W)]


def _row_blocks(m):
    n = m // ROW_BLOCKS
    return [slice(k * n, (k + 1) * n) for k in range(ROW_BLOCKS)]


def _front_head(x, mods):
    return (_rms_scaled(x, mods[1]) + mods[0]).astype(_bf16)


def _front_stages(x, h, mods, w, out, *, nb, cnt_fn, hist_pool, hist_conv, sink_pool, sink_conv):
    (_, w_in_ref, pool_w_ref, pool_scale_ref, conv_w_ref, w_out_ref) = w[:6]
    _, _, gate1, sh2, scale2, _ = mods
    m = x.shape[0]

    proj = _dot(h[...], w_in_ref[...])
    yield
    v_pool = proj[:, :D_POOL]
    x_conv = proj[:, D_POOL:D_POOL + D_CONV]
    gate_b = proj[:, D_POOL + D_CONV:D_POOL + 2 * D_CONV]
    gate_c = proj[:, D_POOL + 2 * D_CONV:]

    pool_ext = _row_cat([hist_pool(), v_pool])
    halo = (pool_ext.shape[0] - m) // nb
    y_pool = []
    for gi, win in enumerate(POOL_WINDOWS):
        sl = slice(gi * POOL_GC, (gi + 1) * POOL_GC)
        s = pool_ext[:, sl]
        step = 1
        while step < win:
            n = s.shape[0]
            s = s[step * nb:] + s[:n - step * nb]
            step *= 2
        first = (halo - (win - 1)) * nb
        d = s[first:first + m] / cnt_fn(win) - v_pool[:, sl]
        y_pool.append(_dot(d.astype(_bf16), pool_w_ref[gi].astype(_bf16)))
    y_pool = _lane_cat(y_pool) * pool_scale_ref[...]
    sink_pool(pool_ext)

    cx = gate_c * x_conv
    conv_ext = _row_cat([hist_conv(), cx])
    cw = _taps(conv_w_ref, D_CONV, slice(0, D_CONV))
    conv = conv_ext[0:m] * cw[0] + conv_ext[nb:nb + m] * cw[1] + cx * cw[2]
    y_conv = gate_b * conv
    sink_conv(conv_ext)
    mixed = _lane_cat([y_pool, y_conv]).astype(_bf16)
    yield
    blocks = _row_blocks(m)
    mix = [_dot(mixed[rows], w_out_ref[...]) for rows in blocks]
    yield
    out["x1"], out["h2"] = [], []
    for rows, mix_rows in zip(blocks, mix):
        x1 = x[rows] + _rms_scaled(mix_rows, gate1[rows])
        out["x1"].append(x1)
        out["h2"].append((_rms_scaled(x1, scale2[rows]) + sh2[rows]).astype(_bf16))
        yield


class _UpValues:
    def __init__(self):
        self._v = {}

    def put(self, c, half, val):
        self._v.setdefault((c, half), []).append(val)

    def get(self, c, half):
        return _row_cat(self._v.pop((c, half)))


def _ffn_cols(c):
    return [slice(base + c * FF_CHUNK, base + (c + 1) * FF_CHUNK) for base in (0, D_FF)]


def _ffn_issue(w, h2_ref, ups, c, rows=slice(None)):
    for half, cols in enumerate(_ffn_cols(c)):
        ups.put(c, half, _dot(h2_ref[rows, :], w[6][:, cols]))


def _ffn_stages(w, h2_ref, f_ref, act_ref, ups, out, *, nb, hist_ffn, sink_ffn):
    _, ffn_conv_w_ref, w_down_ref = w[6:]
    m = h2_ref.shape[0]
    last = N_FF_CHUNKS - 1
    groups = [range(first, min(first + FF_DOWN_GROUP, N_FF_CHUNKS))
              for first in range(0, N_FF_CHUNKS, FF_DOWN_GROUP)]
    assert len(groups) >= 2, "the final norm adds to earlier groups' partial sums in f_ref"

    def group_rows(group):
        return slice(group.start * FF_CHUNK, group.stop * FF_CHUNK)

    def down(group):
        rows = group_rows(group)
        contrib = _dot(act_ref[:, rows], w_down_ref[rows, :])
        f_ref[...] = contrib if group.start == 0 else f_ref[...] + contrib

    kept = groups[-2] if groups[-2].stop - 1 < last else None
    for c in range(N_FF_CHUNKS):
        if c + FFN_UPS_AHEAD < N_FF_CHUNKS:
            _ffn_issue(w, h2_ref, ups, c + FFN_UPS_AHEAD)
        if c == last and kept is not None:
            down(kept)
        halves = []
        for half, cols in enumerate(_ffn_cols(c)):
            up = ups.get(c, half)
            up_ext = _row_cat([hist_ffn(cols), up])
            fw = _taps(ffn_conv_w_ref, 2 * D_FF, cols)
            halves.append(up_ext[0:m] * fw[0] + up_ext[nb:nb + m] * fw[1] + up * fw[2])
            sink_ffn(cols, up_ext)
        act_ref[:, c * FF_CHUNK:(c + 1) * FF_CHUNK] = (_silu(halves[0]) * halves[1]).astype(_bf16)
        for group in groups:
            if c == group.stop - 1 and c < last and group is not kept:
                down(group)
        if c == last:
            rows = group_rows(groups[-1])
            contrib = [_dot(act_ref[blk, rows], w_down_ref[rows, :]) for blk in _row_blocks(m)]
            out["f"] = [f_ref[blk, :] + part for blk, part in zip(_row_blocks(m), contrib)]
        yield


def _ffn_head(front, out, w, h2_ref, ups):
    for blk, rows in enumerate(_row_blocks(h2_ref.shape[0])):
        next(front)
        h2_ref[rows, :] = out["h2"][blk]
        for c in range(FFN_UPS_AHEAD):
            _ffn_issue(w, h2_ref, ups, c, rows)


def _finish(out, gate2):
    blocks = _row_blocks(gate2.shape[0])
    return _row_cat([x1 + _rms_scaled(f, gate2[rows])
                     for rows, x1, f in zip(blocks, out["x1"], out["f"])])


def _fold_mods(mod, g_refs, n_steps):
    g = [g_ref[...] for g_ref in g_refs]
    sh1, sc1, gt1, sh2, sc2, gt2 = [mod[:, i * D_MODEL:(i + 1) * D_MODEL] for i in range(6)]
    folded = [sh1, g[0] * (1.0 + sc1), gt1 * g[1], sh2, g[2] * (1.0 + sc2), gt2 * g[3]]
    return [_row_cat([a] * n_steps) for a in folded]


def _trunk(x, mods, w, h2_ref, f_ref, act_ref, *, nb, cnt_fn, hist_pool, hist_conv, hist_ffn,
           sink_pool, sink_conv, sink_ffn):
    out = {}
    front = _front_stages(x, _front_head(x, mods), mods, w, out, nb=nb, cnt_fn=cnt_fn,
                          hist_pool=hist_pool, hist_conv=hist_conv, sink_pool=sink_pool,
                          sink_conv=sink_conv)
    for _ in range(3):
        next(front)
    ups = _UpValues()
    _ffn_head(front, out, w, h2_ref, ups)
    for _ in _ffn_stages(w, h2_ref, f_ref, act_ref, ups, out, nb=nb, hist_ffn=hist_ffn,
                         sink_ffn=sink_ffn):
        pass
    return _finish(out, mods[5])


def _stream_cast(src, dst, stage, sems, sem_row):
    slots, rows = stage.shape[0], stage.shape[1]
    n = src.shape[0] // rows
    ahead = slots - 1

    def chunk(i, slot):
        return pltpu.make_async_copy(src.at[pl.ds(i * rows, rows), :], stage.at[slot],
                                     sems.at[sem_row, slot])

    for i in range(min(ahead, n)):
        chunk(i, i).start(priority=i % DMA_THREADS)

    def body(i, carry):
        slot = lax.rem(i, slots)
        nxt = i + ahead
        @pl.when(nxt < n)
        def _():
            for thread in range(DMA_THREADS):
                @pl.when(lax.rem(nxt, DMA_THREADS) == thread)
                def _():
                    chunk(nxt, lax.rem(nxt, slots)).start(priority=thread)

        chunk(i, slot).wait()
        dst[pl.ds(pl.multiple_of(i * rows, rows), rows), :] = stage[slot].astype(_bf16)
        return carry

    lax.fori_loop(0, n, body, 0)


def _prompt_kernel(x_hbm, mod_ref, *rest, nb, tt, n_steps):
    g_refs, rest = rest[:4], rest[4:]
    (w_in_hbm, pool_w_ref, pool_scale_ref, conv_w_ref, w_out_hbm, w_up_hbm, ffn_conv_w_ref,
     w_down_hbm) = rest[:8]
    y_hbm, pool_out_ref, conv_out_ref, ffn_out_ref = rest[8:12]
    bf_out = rest[12:16]
    (x_buf, y_buf, pool_carry, conv_carry, ffn_carry, h2_ref, f_ref, act_ref,
     w_in_bf, w_out_bf, w_up_bf, w_down_bf, stage_in, stage_sq, stage_up,
     x_sems, y_sems, stage_sems, out_sems) = rest[16:]
    bf_vmem = (w_in_bf, w_out_bf, w_up_bf, w_down_bf)
    w = (g_refs, w_in_bf, pool_w_ref, pool_scale_ref, conv_w_ref, w_out_bf, w_up_bf, ffn_conv_w_ref,
         w_down_bf)
    j = pl.program_id(0)

    def bf_writeback(k):
        return pltpu.make_async_copy(bf_vmem[k], bf_out[k], out_sems.at[k])

    m = tt * nb
    slot = lax.rem(j, 2)

    def tile_copies(tile, buf_slot, to_hbm):
        copies = []
        for b in range(nb):
            hbm = (y_hbm if to_hbm else x_hbm).at[b, pl.ds(tile * tt, tt), :]
            if to_hbm:
                copies.append(pltpu.make_async_copy(y_buf.at[buf_slot, :, b, :], hbm,
                                                    y_sems.at[buf_slot, b]))
            else:
                copies.append(pltpu.make_async_copy(hbm, x_buf.at[buf_slot, :, b, :],
                                                    x_sems.at[buf_slot, b]))
        return copies

    def start(copies):
        for copy in copies:
            copy.start()

    def wait(copies):
        for copy in copies:
            copy.wait()

    @pl.when(j == 0)
    def _():
        start(tile_copies(0, 0, False))
        pool_carry[...] = jnp.zeros_like(pool_carry)
        conv_carry[...] = jnp.zeros_like(conv_carry)
        ffn_carry[...] = jnp.zeros_like(ffn_carry)
        _stream_cast(w_in_hbm.at[0], w_in_bf, stage_in, stage_sems, 0)
        _stream_cast(w_out_hbm.at[0], w_out_bf, stage_sq, stage_sems, 1)
        _stream_cast(w_up_hbm.at[0], w_up_bf, stage_up, stage_sems, 2)
        _stream_cast(w_down_hbm.at[0], w_down_bf, stage_sq, stage_sems, 1)
        for k in range(len(bf_vmem)):
            bf_writeback(k).start()

    @pl.when(j + 1 < n_steps)
    def _():
        start(tile_copies(j + 1, 1 - slot, False))

    @pl.when(j >= 2)
    def _():
        wait(tile_copies(j - 2, slot, True))

    wait(tile_copies(j, slot, False))
    x = x_buf[slot].reshape(m, D_MODEL)

    mods = _fold_mods(mod_ref[...], g_refs, tt)

    t_idx = j * tt + lax.shift_right_logical(
        lax.broadcasted_iota(jnp.int32, (m, POOL_GC), 0), nb.bit_length() - 1)

    def cnt_fn(win):
        return jnp.minimum(win, t_idx + 1).astype(_f32)

    def carry_rows(ref, first, n):
        return _lane_cat([ref[first + q] for q in range(n)])

    def keep_rows(ref, first, ext):
        rows = ref.shape[1]
        for q in range(ext.shape[1] // LANES):
            ref[first + q] = ext[ext.shape[0] - rows:, q * LANES:(q + 1) * LANES]

    y = _trunk(
        x, mods, w, h2_ref, f_ref, act_ref, nb=nb, cnt_fn=cnt_fn,
        hist_pool=lambda: carry_rows(pool_carry, 0, D_POOL // LANES),
        hist_conv=lambda: carry_rows(conv_carry, 0, D_CONV // LANES),
        hist_ffn=lambda cols: carry_rows(ffn_carry, cols.start // LANES, FF_CHUNK // LANES),
        sink_pool=lambda ext: keep_rows(pool_carry, 0, ext),
        sink_conv=lambda ext: keep_rows(conv_carry, 0, ext),
        sink_ffn=lambda cols, ext: keep_rows(ffn_carry, cols.start // LANES, ext))

    y_buf[slot] = y.reshape(tt, nb, D_MODEL)
    start(tile_copies(j, slot, True))

    @pl.when(j == n_steps - 1)
    def _():
        def emit(out_ref, carry, n):
            first = carry.shape[1] // nb - n
            for b in range(nb):
                for q in range(carry.shape[0]):
                    out_ref[0, b, :, q * LANES:(q + 1) * LANES] = (
                        carry[q, pl.ds(first * nb + b, n, stride=nb), :])
        pool_rows = carry_rows(pool_carry, 0, D_POOL // LANES)
        for k in range(POOL_BUF):
            first = (POOL_HALO - POOL_BUF + k) * nb
            pool_out_ref[k] = pool_rows[first:first + nb]
        emit(conv_out_ref, conv_carry, CONV_HIST)
        emit(ffn_out_ref, ffn_carry, CONV_HIST)
        for k in range(len(bf_vmem)):
            bf_writeback(k).wait()
        if n_steps >= 2:
            wait(tile_copies(j - 1, 1 - slot, True))
        wait(tile_copies(j, slot, True))


def _prompt(x, mod, g, w_in, pool_w, pool_scale, conv_w, w_out, w_up, ffn_conv_w, w_down):
    nb, seq, _ = x.shape
    assert nb == SUBLANES, "one vreg row per time step"
    tt = PROMPT_TT
    n_steps = seq // tt
    m = tt * nb
    big = (w_in, w_out, w_up, w_down)
    stage_widths = (D_IN_PROJ, D_MODEL, 2 * D_FF)
    in_hbm = pl.BlockSpec(memory_space=pl.ANY)

    def const_spec(a):
        nd = a.ndim
        return pl.BlockSpec(a.shape, lambda j: (0,) * nd, pipeline_mode=pl.Buffered(1))

    return pl.pallas_call(
        functools.partial(_prompt_kernel, nb=nb, tt=tt, n_steps=n_steps),
        grid=(n_steps,),
        in_specs=[in_hbm, const_spec(mod),
                  *[const_spec(a) for a in g], in_hbm, const_spec(pool_w), const_spec(pool_scale),
                  const_spec(conv_w), in_hbm, in_hbm, const_spec(ffn_conv_w), in_hbm],
        out_specs=[
            in_hbm,
            pl.BlockSpec((POOL_BUF, nb, D_POOL), lambda j: (0, 0, 0)),
            pl.BlockSpec((1, nb, CONV_HIST, D_CONV), lambda j: (0, 0, 0, 0)),
            pl.BlockSpec((1, nb, CONV_HIST, 2 * D_FF), lambda j: (0, 0, 0, 0)),
        ] + [pl.BlockSpec(memory_space=pl.ANY) for _ in big],
        out_shape=[
            jax.ShapeDtypeStruct((nb, seq, D_MODEL), _f32),
            jax.ShapeDtypeStruct((POOL_BUF, nb, D_POOL), _f32),
            jax.ShapeDtypeStruct((1, nb, CONV_HIST, D_CONV), _f32),
            jax.ShapeDtypeStruct((1, nb, CONV_HIST, 2 * D_FF), _f32),
        ] + [jax.ShapeDtypeStruct(a.shape[1:], _bf16) for a in big],
        scratch_shapes=[
            pltpu.VMEM((2, tt, nb, D_MODEL), _f32),
            pltpu.VMEM((2, tt, nb, D_MODEL), _f32),
            pltpu.VMEM((D_POOL // LANES, POOL_HALO * nb, LANES), _f32),
            pltpu.VMEM((D_CONV // LANES, CONV_HIST * nb, LANES), _f32),
            pltpu.VMEM((2 * D_FF // LANES, CONV_HIST * nb, LANES), _f32),
            pltpu.VMEM((m, D_MODEL), _bf16),
            pltpu.VMEM((m, D_MODEL), _f32),
            pltpu.VMEM((m, D_FF), _bf16),
        ] + [pltpu.VMEM(a.shape[1:], _bf16) for a in big]
        + [pltpu.VMEM((WEIGHT_STAGE_SLOTS, WEIGHT_STAGE_ROWS[c], c), _f32) for c in stage_widths]
        + [pltpu.SemaphoreType.DMA((2, nb)), pltpu.SemaphoreType.DMA((2, nb)),
           pltpu.SemaphoreType.DMA((len(stage_widths), WEIGHT_STAGE_SLOTS)),
           pltpu.SemaphoreType.DMA((len(big),))],
        compiler_params=pltpu.CompilerParams(
            dimension_semantics=("arbitrary",),
            vmem_limit_bytes=V7X_VMEM_LIMIT_BYTES),
        name="prompt_trunk",
    )(x, mod, *g, w_in, pool_w, pool_scale, conv_w, w_out, w_up, ffn_conv_w, w_down)


def _sample_kernel(*refs, nb, nt):
    refs = iter(refs)

    def take(n):
        return [next(refs) for _ in range(n)]

    x_hbm, sp_hbm, sc_hbm, sf_hbm, mod_hbm = take(5)
    g_refs = take(4)
    (w_in_hbm, pool_w_ref, pool_scale_ref, conv_w_ref, w_out_hbm, w_up_hbm, ffn_conv_w_ref,
     w_down_hbm) = take(8)
    y_hbm, pool_hbm, conv_hbm, ffn_hbm = take(4)
    (x_ref, sp_ref, sc_ref, mod_ref, w_in_ref, w_out_ref, w_up_ref, w_down_ref, sf_ref) = take(9)
    y_ref, pool_out_ref, conv_out_ref, ffn_out_ref = take(4)
    h2_ref, f_ref, act_ref, in_sems, out_sems = take(5)
    w = (g_refs, w_in_ref, pool_w_ref, pool_scale_ref, conv_w_ref, w_out_ref, w_up_ref,
         ffn_conv_w_ref, w_down_ref)

    n_groups = -(-N_FF_CHUNKS // FF_DOWN_GROUP)

    def group_cols(gi, base):
        first = gi * FF_DOWN_GROUP
        n = min(FF_DOWN_GROUP, N_FF_CHUNKS - first)
        return pl.ds(base + first * FF_CHUNK, n * FF_CHUNK)

    in_order = [("x", [(x_hbm.at[:, t, :], x_ref.at[t]) for t in range(nt)]),
                ("mod", [(mod_hbm, mod_ref)]), ("w_in", [(w_in_hbm, w_in_ref)]),
                ("pool", [(sp_hbm, sp_ref)]),
                ("conv", [(sc_hbm.at[0, :, k, :], sc_ref.at[k]) for k in range(CONV_HIST)]),
                ("w_out", [(w_out_hbm, w_out_ref)]),
                ("ffn", [(sf_hbm.at[0, :, k, :], sf_ref.at[k]) for k in range(CONV_HIST)])]
    for gi in range(n_groups):
        cols = [group_cols(gi, base) for base in (0, D_FF)]
        rows = group_cols(gi, 0)
        in_order.append((f"ffn_w{gi}", [(w_up_hbm.at[:, c], w_up_ref.at[:, c]) for c in cols]
                         + [(w_down_hbm.at[rows, :], w_down_ref.at[rows, :])]))
    fetch, k = {}, 0
    for name, pairs in in_order:
        fetch[name] = [pltpu.make_async_copy(src, dst, in_sems.at[k + i])
                       for i, (src, dst) in enumerate(pairs)]
        k += len(pairs)

    out_order = [("y", [(y_ref.at[t], y_hbm.at[:, t, :]) for t in range(nt)]),
                 ("pool", [(pool_out_ref, pool_hbm)]),
                 ("conv", [(conv_out_ref.at[k], conv_hbm.at[0, :, k, :]) for k in range(CONV_HIST)])]
    for gi in range(n_groups):
        out_order.append((f"ffn{gi}", [
            (ffn_out_ref.at[k, :, group_cols(gi, base)], ffn_hbm.at[0, :, k, group_cols(gi, base)])
            for base in (0, D_FF) for k in range(CONV_HIST)]))
    send, k = {}, 0
    for name, pairs in out_order:
        send[name] = [pltpu.make_async_copy(src, dst, out_sems.at[k + i])
                      for i, (src, dst) in enumerate(pairs)]
        k += len(pairs)

    def start(copies):
        for copy in copies:
            copy.start()

    def wait(copies):
        for copy in copies:
            copy.wait()

    for name, _ in in_order:
        for copy in fetch[name]:
            copy.start(priority=1 if name.startswith(("w_", "ffn_w")) else 0)

    def hist_ffn(cols):
        return _row_cat([sf_ref[k, :, cols] for k in range(CONV_HIST)])

    def sink_ffn(cols, ext):
        first = ext.shape[0] - CONV_HIST * nb
        for k in range(CONV_HIST):
            ffn_out_ref[k, :, cols] = ext[first + k * nb:first + (k + 1) * nb]

    def sink_pool(ext):
        first = ext.shape[0] - POOL_BUF * nb
        for k in range(POOL_BUF):
            pool_out_ref[k] = ext[first + k * nb:first + (k + 1) * nb]

    def sink_conv(ext):
        first = ext.shape[0] - CONV_HIST * nb
        for k in range(CONV_HIST):
            conv_out_ref[k] = ext[first + k * nb:first + (k + 1) * nb]

    wait(fetch["x"])
    wait(fetch["mod"])
    x = _row_cat([x_ref[t] for t in range(nt)])
    mods = _fold_mods(mod_ref[...], g_refs, nt)
    h = _front_head(x, mods)
    mixed = {}
    front = _front_stages(
        x, h, mods, w, mixed, nb=nb,
        cnt_fn=lambda win: float(min(win, PAST_LEN + 1)),
        hist_pool=lambda: _row_cat([sp_ref[k] for k in range(POOL_BUF)]),
        hist_conv=lambda: _row_cat([sc_ref[k] for k in range(CONV_HIST)]),
        sink_pool=sink_pool, sink_conv=sink_conv)
    wait(fetch["w_in"])
    next(front)
    wait(fetch["pool"])
    wait(fetch["conv"])
    next(front)
    start(send["pool"])
    start(send["conv"])
    wait(fetch["w_out"])
    next(front)

    wait(fetch["ffn"])
    wait(fetch["ffn_w0"])
    ups = _UpValues()
    _ffn_head(front, mixed, w, h2_ref, ups)
    ffn = _ffn_stages(w, h2_ref, f_ref, act_ref, ups, mixed, nb=nb, hist_ffn=hist_ffn,
                      sink_ffn=sink_ffn)
    groups_sent = 0
    for c in range(N_FF_CHUNKS):
        ahead = c + FFN_UPS_AHEAD
        if ahead < N_FF_CHUNKS and ahead % FF_DOWN_GROUP == 0:
            wait(fetch[f"ffn_w{ahead // FF_DOWN_GROUP}"])
            while (groups_sent + 1) * FF_DOWN_GROUP <= c:
                start(send[f"ffn{groups_sent}"])
                groups_sent += 1
        next(ffn)
    y = _finish(mixed, mods[5])
    for t in range(nt):
        y_ref[t] = y[t * nb:(t + 1) * nb]
    for gi in range(groups_sent, n_groups):
        start(send[f"ffn{gi}"])
    start(send["y"])
    for copies in send.values():
        wait(copies)


def _sample(x, sp, sc, sf, mod, g, w_in, pool_w, pool_scale, conv_w, w_out, w_up, ffn_conv_w, w_down):
    nb, nt, _ = x.shape
    hbm = pl.BlockSpec(memory_space=pl.ANY)
    vmem = pl.BlockSpec(memory_space=pltpu.VMEM)
    out_shape = [
        jax.ShapeDtypeStruct(x.shape, _f32),
        jax.ShapeDtypeStruct(sp.shape, _f32),
        jax.ShapeDtypeStruct(sc.shape, _f32),
        jax.ShapeDtypeStruct(sf.shape, _f32),
    ]
    step_major = [pltpu.VMEM((nt, nb, D_MODEL), _f32),
                  pltpu.VMEM(sp.shape, _f32),
                  pltpu.VMEM((CONV_HIST, nb, D_CONV), _f32)]
    n_fetches = nt + 1 + 1 + 1 + CONV_HIST + 1 + CONV_HIST + 3 * -(-N_FF_CHUNKS // FF_DOWN_GROUP)
    n_sends = nt + 1 + CONV_HIST + 2 * CONV_HIST * -(-N_FF_CHUNKS // FF_DOWN_GROUP)
    return pl.pallas_call(
        functools.partial(_sample_kernel, nb=nb, nt=nt),
        in_specs=[hbm, hbm, hbm, hbm, hbm, vmem, vmem, vmem, vmem, hbm, vmem, vmem, vmem, hbm, hbm,
                  vmem, hbm],
        out_specs=[hbm] * len(out_shape),
        out_shape=out_shape,
        scratch_shapes=step_major
        + [pltpu.VMEM(a.shape, a.dtype) for a in (mod, w_in, w_out, w_up, w_down)]
        + [pltpu.VMEM((CONV_HIST, nb, 2 * D_FF), _f32)]
        + step_major
        + [pltpu.VMEM((CONV_HIST, nb, 2 * D_FF), _f32)]
        + [
            pltpu.VMEM((nt * nb, D_MODEL), _bf16),
            pltpu.VMEM((nt * nb, D_MODEL), _f32),
            pltpu.VMEM((nt * nb, D_FF), _bf16),
            pltpu.SemaphoreType.DMA((n_fetches,)),
            pltpu.SemaphoreType.DMA((n_sends,)),
        ],
        compiler_params=pltpu.CompilerParams(vmem_limit_bytes=V7X_VMEM_LIMIT_BYTES),
        name="sample_trunk",
    )(x, sp, sc, sf, mod, *g, w_in, pool_w, pool_scale, conv_w, w_out, w_up, ffn_conv_w, w_down)


def kernel(x_prompt, x_sample, state_pool, state_conv, state_ffn, c_prompt, c_sample, w_ada, b_ada,
           g_pre_mix, g_post_mix, g_pre_ffn, g_post_ffn, w_in, pool_w, pool_scale, conv_w, w_out,
           ffn_w_up, ffn_conv_w, ffn_w_down):
    assert w_ada.shape[0] == 1, "single trunk layer"
    mod_p, mod_s = _ada(c_prompt, c_sample, w_ada[0], b_ada)
    g = (g_pre_mix, g_post_mix, g_pre_ffn, g_post_ffn)
    conv_taps = conv_w.reshape(1, CONV_W * D_CONV)
    ffn_taps = ffn_conv_w.reshape(1, CONV_W * 2 * D_FF)
    y_p, pool_p, conv_p, ffn_p, w_in_bf, w_out_bf, w_up_bf, w_down_bf = _prompt(
        x_prompt, mod_p, g, w_in, pool_w[0], pool_scale, conv_taps, w_out, ffn_w_up, ffn_taps,
        ffn_w_down)
    y_s, pool_s, conv_s, ffn_s = _sample(
        x_sample, state_pool[0].transpose(1, 0, 2), state_conv, state_ffn, mod_s, g, w_in_bf,
        pool_w[0], pool_scale, conv_taps, w_out_bf, w_up_bf, ffn_taps, w_down_bf)
    return (y_p, y_s, pool_p.transpose(1, 0, 2)[None], conv_p, ffn_p,
            pool_s.transpose(1, 0, 2)[None], conv_s, ffn_s)
```

```python
import functools

import jax
import jax.numpy as jnp
from jax import lax
from jax.experimental import pallas as pl
from jax.experimental.pallas import tpu as pltpu

D_MODEL = 1024
D_POOL = 512
D_CONV = 512
POOL_WINDOWS = (2, 4, 8, 16)
POOL_GC = 128
POOL_BUF = 15
CONV_W = 3
CONV_HIST = CONV_W - 1
D_FF = 2816
D_IN_PROJ = D_POOL + 3 * D_CONV
RMS_EPS = 1e-6
PAST_LEN = 16384

V7X_VMEM_LIMIT_BYTES = 58 * 1024 * 1024
SUBLANES = 8
LANES = 128
FF_CHUNK = 256
N_FF_CHUNKS = D_FF // FF_CHUNK
FFN_UPS_AHEAD = 2
FF_DOWN_GROUP = 3
ROW_BLOCKS = 2
PROMPT_TT = 64
WEIGHT_STAGE_ROWS = {2048: 128, 1024: 256, 5632: 64}
WEIGHT_STAGE_SLOTS = 4
ADA_ROWS = 256
POOL_HALO = 16

_bf16 = jnp.bfloat16
_f32 = jnp.float32


def _dot(a, b):
    return jnp.dot(a, b, preferred_element_type=_f32)


def _rms_scaled(x, scale):
    ms = jnp.mean(x * x, axis=-1, keepdims=True)
    return x * lax.rsqrt(ms + RMS_EPS) * scale


def _silu(a):
    return a * jax.nn.sigmoid(a)


def _lane_cat(parts):
    return jnp.concatenate(parts, axis=1)


def _row_cat(parts):
    return jnp.concatenate(parts, axis=0)


def _ada_kernel(cp_ref, cs_ref, w_hbm, b_ref, mp_ref, ms_ref, w_buf, sems):
    n_blocks, rows = w_buf.shape[0], w_buf.shape[1]

    def fetch(k):
        return pltpu.make_async_copy(w_hbm.at[pl.ds(k * rows, rows), :], w_buf.at[k], sems.at[k])

    for k in range(n_blocks):
        fetch(k).start()
    c = _silu(_row_cat([cp_ref[...], cs_ref[...]])).astype(_bf16)
    mod = b_ref[...]
    for k in range(n_blocks):
        fetch(k).wait()
        mod = mod + _dot(c[:, k * rows:(k + 1) * rows], w_buf[k].astype(_bf16))
    bp = cp_ref.shape[0]
    mp_ref[...] = mod[:bp]
    ms_ref[...] = mod[bp:]


def _ada(c_prompt, c_sample, w_ada, b_ada):
    n = w_ada.shape[1]
    bp, bs = c_prompt.shape[0], c_sample.shape[0]
    vmem = pl.BlockSpec(memory_space=pltpu.VMEM)
    return pl.pallas_call(
        _ada_kernel,
        in_specs=[vmem, vmem, pl.BlockSpec(memory_space=pl.ANY), vmem],
        out_specs=[vmem, vmem],
        out_shape=[
            jax.ShapeDtypeStruct((bp, n), _f32),
            jax.ShapeDtypeStruct((bs, n), _f32),
        ],
        scratch_shapes=[
            pltpu.VMEM((D_MODEL // ADA_ROWS, ADA_ROWS, n), _f32),
            pltpu.SemaphoreType.DMA((D_MODEL // ADA_ROWS,)),
        ],
        compiler_params=pltpu.CompilerParams(
            vmem_limit_bytes=D_MODEL * n * 4 + ADA_ROWS * n * 8),
        name="ada_mod",
    )(c_prompt, c_sample, w_ada, b_ada)


def _taps(w_ref, width, cols):
    return [w_ref[:, k * width + cols.start:k * width + cols.stop] for k in range(CONV_W)]


def _row_blocks(m):
    n = m // ROW_BLOCKS
    return [slice(k * n, (k + 1) * n) for k in range(ROW_BLOCKS)]


def _front_head(x, mods):
    return (_rms_scaled(x, mods[1]) + mods[0]).astype(_bf16)


def _front_stages(x, h, mods, w, out, *, nb, cnt_fn, hist_pool, hist_conv, sink_pool, sink_conv):
    (_, w_in_ref, pool_w_ref, pool_scale_ref, conv_w_ref, w_out_ref) = w[:6]
    _, _, gate1, sh2, scale2, _ = mods
    m = x.shape[0]

    proj = _dot(h[...], w_in_ref[...])
    yield
    v_pool = proj[:, :D_POOL]
    x_conv = proj[:, D_POOL:D_POOL + D_CONV]
    gate_b = proj[:, D_POOL + D_CONV:D_POOL + 2 * D_CONV]
    gate_c = proj[:, D_POOL + 2 * D_CONV:]

    pool_ext = _row_cat([hist_pool(), v_pool])
    halo = (pool_ext.shape[0] - m) // nb
    y_pool = []
    for gi, win in enumerate(POOL_WINDOWS):
        sl = slice(gi * POOL_GC, (gi + 1) * POOL_GC)
        s = pool_ext[:, sl]
        step = 1
        while step < win:
            n = s.shape[0]
            s = s[step * nb:] + s[:n - step * nb]
            step *= 2
        first = (halo - (win - 1)) * nb
        d = s[first:first + m] / cnt_fn(win) - v_pool[:, sl]
        y_pool.append(_dot(d.astype(_bf16), pool_w_ref[gi].astype(_bf16)))
    y_pool = _lane_cat(y_pool) * pool_scale_ref[...]
    sink_pool(pool_ext)

    cx = gate_c * x_conv
    conv_ext = _row_cat([hist_conv(), cx])
    cw = _taps(conv_w_ref, D_CONV, slice(0, D_CONV))
    conv = conv_ext[0:m] * cw[0] + conv_ext[nb:nb + m] * cw[1] + cx * cw[2]
    y_conv = gate_b * conv
    sink_conv(conv_ext)
    mixed = _lane_cat([y_pool, y_conv]).astype(_bf16)
    yield
    blocks = _row_blocks(m)
    mix = [_dot(mixed[rows], w_out_ref[...]) for rows in blocks]
    yield
    out["x1"], out["h2"] = [], []
    for rows, mix_rows in zip(blocks, mix):
        x1 = x[rows] + _rms_scaled(mix_rows, gate1[rows])
        out["x1"].append(x1)
        out["h2"].append((_rms_scaled(x1, scale2[rows]) + sh2[rows]).astype(_bf16))
        yield


class _UpValues:
    def __init__(self):
        self._v = {}

    def put(self, c, half, val):
        self._v.setdefault((c, half), []).append(val)

    def get(self, c, half):
        return _row_cat(self._v.pop((c, half)))


def _ffn_cols(c):
    return [slice(base + c * FF_CHUNK, base + (c + 1) * FF_CHUNK) for base in (0, D_FF)]


def _ffn_issue(w, h2_ref, ups, c, rows=slice(None)):
    for half, cols in enumerate(_ffn_cols(c)):
        ups.put(c, half, _dot(h2_ref[rows, :], w[6][:, cols]))


def _ffn_stages(w, h2_ref, f_ref, act_ref, ups, out, *, nb, hist_ffn, sink_ffn):
    _, ffn_conv_w_ref, w_down_ref = w[6:]
    m = h2_ref.shape[0]
    last = N_FF_CHUNKS - 1
    groups = [range(first, min(first + FF_DOWN_GROUP, N_FF_CHUNKS))
              for first in range(0, N_FF_CHUNKS, FF_DOWN_GROUP)]
    assert len(groups) >= 2, "the final norm adds to earlier groups' partial sums in f_ref"

    def group_rows(group):
        return slice(group.start * FF_CHUNK, group.stop * FF_CHUNK)

    def down(group):
        rows = group_rows(group)
        contrib = _dot(act_ref[:, rows], w_down_ref[rows, :])
        f_ref[...] = contrib if group.start == 0 else f_ref[...] + contrib

    kept = groups[-2] if groups[-2].stop - 1 < last else None
    for c in range(N_FF_CHUNKS):
        if c + FFN_UPS_AHEAD < N_FF_CHUNKS:
            _ffn_issue(w, h2_ref, ups, c + FFN_UPS_AHEAD)
        if c == last and kept is not None:
            down(kept)
        halves = []
        for half, cols in enumerate(_ffn_cols(c)):
            up = ups.get(c, half)
            up_ext = _row_cat([hist_ffn(cols), up])
            fw = _taps(ffn_conv_w_ref, 2 * D_FF, cols)
            halves.append(up_ext[0:m] * fw[0] + up_ext[nb:nb + m] * fw[1] + up * fw[2])
            sink_ffn(cols, up_ext)
        act_ref[:, c * FF_CHUNK:(c + 1) * FF_CHUNK] = (_silu(halves[0]) * halves[1]).astype(_bf16)
        for group in groups:
            if c == group.stop - 1 and c < last and group is not kept:
                down(group)
        if c == last:
            rows = group_rows(groups[-1])
            contrib = [_dot(act_ref[blk, rows], w_down_ref[rows, :]) for blk in _row_blocks(m)]
            out["f"] = [f_ref[blk, :] + part for blk, part in zip(_row_blocks(m), contrib)]
        yield


def _ffn_head(front, out, w, h2_ref, ups):
    for blk, rows in enumerate(_row_blocks(h2_ref.shape[0])):
        next(front)
        h2_ref[rows, :] = out["h2"][blk]
        for c in range(FFN_UPS_AHEAD):
            _ffn_issue(w, h2_ref, ups, c, rows)


def _finish(out, gate2):
    blocks = _row_blocks(gate2.shape[0])
    return _row_cat([x1 + _rms_scaled(f, gate2[rows])
                     for rows, x1, f in zip(blocks, out["x1"], out["f"])])


def _fold_mods(mod, g_refs, n_steps):
    g = [g_ref[...] for g_ref in g_refs]
    sh1, sc1, gt1, sh2, sc2, gt2 = [mod[:, i * D_MODEL:(i + 1) * D_MODEL] for i in range(6)]
    folded = [sh1, g[0] * (1.0 + sc1), gt1 * g[1], sh2, g[2] * (1.0 + sc2), gt2 * g[3]]
    return [_row_cat([a] * n_steps) for a in folded]


def _trunk(x, mods, w, h2_ref, f_ref, act_ref, *, nb, cnt_fn, hist_pool, hist_conv, hist_ffn,
           sink_pool, sink_conv, sink_ffn):
    out = {}
    front = _front_stages(x, _front_head(x, mods), mods, w, out, nb=nb, cnt_fn=cnt_fn,
                          hist_pool=hist_pool, hist_conv=hist_conv, sink_pool=sink_pool,
                          sink_conv=sink_conv)
    for _ in range(3):
        next(front)
    ups = _UpValues()
    _ffn_head(front, out, w, h2_ref, ups)
    for _ in _ffn_stages(w, h2_ref, f_ref, act_ref, ups, out, nb=nb, hist_ffn=hist_ffn,
                         sink_ffn=sink_ffn):
        pass
    return _finish(out, mods[5])


def _stream_cast(src, dst, stage, sems, sem_row):
    slots, rows = stage.shape[0], stage.shape[1]
    n = src.shape[0] // rows
    ahead = slots - 1

    def chunk(i, slot):
        return pltpu.make_async_copy(src.at[pl.ds(i * rows, rows), :], stage.at[slot],
                                     sems.at[sem_row, slot])

    for i in range(min(ahead, n)):
        chunk(i, i).start()

    def body(i, carry):
        slot = lax.rem(i, slots)

        @pl.when(i + ahead < n)
        def _():
            chunk(i + ahead, lax.rem(i + ahead, slots)).start()

        chunk(i, slot).wait()
        dst[pl.ds(pl.multiple_of(i * rows, rows), rows), :] = stage[slot].astype(_bf16)
        return carry

    lax.fori_loop(0, n, body, 0)


def _prompt_kernel(x_hbm, mod_ref, *rest, nb, tt, n_steps):
    g_refs, rest = rest[:4], rest[4:]
    (w_in_hbm, pool_w_ref, pool_scale_ref, conv_w_ref, w_out_hbm, w_up_hbm, ffn_conv_w_ref,
     w_down_hbm) = rest[:8]
    y_hbm, pool_out_ref, conv_out_ref, ffn_out_ref = rest[8:12]
    bf_out = rest[12:16]
    (x_buf, y_buf, pool_carry, conv_carry, ffn_carry, h2_ref, f_ref, act_ref,
     w_in_bf, w_out_bf, w_up_bf, w_down_bf, stage_in, stage_sq, stage_up,
     x_sems, y_sems, stage_sems, out_sems) = rest[16:]
    bf_vmem = (w_in_bf, w_out_bf, w_up_bf, w_down_bf)
    w = (g_refs, w_in_bf, pool_w_ref, pool_scale_ref, conv_w_ref, w_out_bf, w_up_bf, ffn_conv_w_ref,
         w_down_bf)
    j = pl.program_id(0)

    def bf_writeback(k):
        return pltpu.make_async_copy(bf_vmem[k], bf_out[k], out_sems.at[k])

    m = tt * nb
    slot = lax.rem(j, 2)

    def tile_copies(tile, buf_slot, to_hbm):
        copies = []
        for b in range(nb):
            hbm = (y_hbm if to_hbm else x_hbm).at[b, pl.ds(tile * tt, tt), :]
            if to_hbm:
                copies.append(pltpu.make_async_copy(y_buf.at[buf_slot, :, b, :], hbm,
                                                    y_sems.at[buf_slot, b]))
            else:
                copies.append(pltpu.make_async_copy(hbm, x_buf.at[buf_slot, :, b, :],
                                                    x_sems.at[buf_slot, b]))
        return copies

    def start(copies):
        for copy in copies:
            copy.start()

    def wait(copies):
        for copy in copies:
            copy.wait()

    @pl.when(j == 0)
    def _():
        start(tile_copies(0, 0, False))
        pool_carry[...] = jnp.zeros_like(pool_carry)
        conv_carry[...] = jnp.zeros_like(conv_carry)
        ffn_carry[...] = jnp.zeros_like(ffn_carry)
        _stream_cast(w_in_hbm.at[0], w_in_bf, stage_in, stage_sems, 0)
        _stream_cast(w_out_hbm.at[0], w_out_bf, stage_sq, stage_sems, 1)
        _stream_cast(w_up_hbm.at[0], w_up_bf, stage_up, stage_sems, 2)
        _stream_cast(w_down_hbm.at[0], w_down_bf, stage_sq, stage_sems, 1)
        for k in range(len(bf_vmem)):
            bf_writeback(k).start()

    @pl.when(j + 1 < n_steps)
    def _():
        start(tile_copies(j + 1, 1 - slot, False))

    @pl.when(j >= 2)
    def _():
        wait(tile_copies(j - 2, slot, True))

    wait(tile_copies(j, slot, False))
    x = x_buf[slot].reshape(m, D_MODEL)

    mods = _fold_mods(mod_ref[...], g_refs, tt)

    t_idx = j * tt + lax.shift_right_logical(
        lax.broadcasted_iota(jnp.int32, (m, POOL_GC), 0), nb.bit_length() - 1)

    def cnt_fn(win):
        return jnp.minimum(win, t_idx + 1).astype(_f32)

    def carry_rows(ref, first, n):
        return _lane_cat([ref[first + q] for q in range(n)])

    def keep_rows(ref, first, ext):
        rows = ref.shape[1]
        for q in range(ext.shape[1] // LANES):
            ref[first + q] = ext[ext.shape[0] - rows:, q * LANES:(q + 1) * LANES]

    y = _trunk(
        x, mods, w, h2_ref, f_ref, act_ref, nb=nb, cnt_fn=cnt_fn,
        hist_pool=lambda: carry_rows(pool_carry, 0, D_POOL // LANES),
        hist_conv=lambda: carry_rows(conv_carry, 0, D_CONV // LANES),
        hist_ffn=lambda cols: carry_rows(ffn_carry, cols.start // LANES, FF_CHUNK // LANES),
        sink_pool=lambda ext: keep_rows(pool_carry, 0, ext),
        sink_conv=lambda ext: keep_rows(conv_carry, 0, ext),
        sink_ffn=lambda cols, ext: keep_rows(ffn_carry, cols.start // LANES, ext))

    y_buf[slot] = y.reshape(tt, nb, D_MODEL)
    start(tile_copies(j, slot, True))

    @pl.when(j == n_steps - 1)
    def _():
        def emit(out_ref, carry, n):
            first = carry.shape[1] // nb - n
            for b in range(nb):
                for q in range(carry.shape[0]):
                    out_ref[0, b, :, q * LANES:(q + 1) * LANES] = (
                        carry[q, pl.ds(first * nb + b, n, stride=nb), :])
        pool_rows = carry_rows(pool_carry, 0, D_POOL // LANES)
        for k in range(POOL_BUF):
            first = (POOL_HALO - POOL_BUF + k) * nb
            pool_out_ref[k] = pool_rows[first:first + nb]
        emit(conv_out_ref, conv_carry, CONV_HIST)
        emit(ffn_out_ref, ffn_carry, CONV_HIST)
        for k in range(len(bf_vmem)):
            bf_writeback(k).wait()
        if n_steps >= 2:
            wait(tile_copies(j - 1, 1 - slot, True))
        wait(tile_copies(j, slot, True))


def _prompt(x, mod, g, w_in, pool_w, pool_scale, conv_w, w_out, w_up, ffn_conv_w, w_down):
    nb, seq, _ = x.shape
    assert nb == SUBLANES, "one vreg row per time step"
    tt = PROMPT_TT
    n_steps = seq // tt
    m = tt * nb
    big = (w_in, w_out, w_up, w_down)
    stage_widths = (D_IN_PROJ, D_MODEL, 2 * D_FF)
    in_hbm = pl.BlockSpec(memory_space=pl.ANY)

    def const_spec(a):
        nd = a.ndim
        return pl.BlockSpec(a.shape, lambda j: (0,) * nd, pipeline_mode=pl.Buffered(1))

    return pl.pallas_call(
        functools.partial(_prompt_kernel, nb=nb, tt=tt, n_steps=n_steps),
        grid=(n_steps,),
        in_specs=[in_hbm, const_spec(mod),
                  *[const_spec(a) for a in g], in_hbm, const_spec(pool_w), const_spec(pool_scale),
                  const_spec(conv_w), in_hbm, in_hbm, const_spec(ffn_conv_w), in_hbm],
        out_specs=[
            in_hbm,
            pl.BlockSpec((POOL_BUF, nb, D_POOL), lambda j: (0, 0, 0)),
            pl.BlockSpec((1, nb, CONV_HIST, D_CONV), lambda j: (0, 0, 0, 0)),
            pl.BlockSpec((1, nb, CONV_HIST, 2 * D_FF), lambda j: (0, 0, 0, 0)),
        ] + [pl.BlockSpec(memory_space=pl.ANY) for _ in big],
        out_shape=[
            jax.ShapeDtypeStruct((nb, seq, D_MODEL), _f32),
            jax.ShapeDtypeStruct((POOL_BUF, nb, D_POOL), _f32),
            jax.ShapeDtypeStruct((1, nb, CONV_HIST, D_CONV), _f32),
            jax.ShapeDtypeStruct((1, nb, CONV_HIST, 2 * D_FF), _f32),
        ] + [jax.ShapeDtypeStruct(a.shape[1:], _bf16) for a in big],
        scratch_shapes=[
            pltpu.VMEM((2, tt, nb, D_MODEL), _f32),
            pltpu.VMEM((2, tt, nb, D_MODEL), _f32),
            pltpu.VMEM((D_POOL // LANES, POOL_HALO * nb, LANES), _f32),
            pltpu.VMEM((D_CONV // LANES, CONV_HIST * nb, LANES), _f32),
            pltpu.VMEM((2 * D_FF // LANES, CONV_HIST * nb, LANES), _f32),
            pltpu.VMEM((m, D_MODEL), _bf16),
            pltpu.VMEM((m, D_MODEL), _f32),
            pltpu.VMEM((m, D_FF), _bf16),
        ] + [pltpu.VMEM(a.shape[1:], _bf16) for a in big]
        + [pltpu.VMEM((WEIGHT_STAGE_SLOTS, WEIGHT_STAGE_ROWS[c], c), _f32) for c in stage_widths]
        + [pltpu.SemaphoreType.DMA((2, nb)), pltpu.SemaphoreType.DMA((2, nb)),
           pltpu.SemaphoreType.DMA((len(stage_widths), WEIGHT_STAGE_SLOTS)),
           pltpu.SemaphoreType.DMA((len(big),))],
        compiler_params=pltpu.CompilerParams(
            dimension_semantics=("arbitrary",),
            vmem_limit_bytes=V7X_VMEM_LIMIT_BYTES),
        name="prompt_trunk",
    )(x, mod, *g, w_in, pool_w, pool_scale, conv_w, w_out, w_up, ffn_conv_w, w_down)


def _sample_kernel(*refs, nb, nt):
    refs = iter(refs)

    def take(n):
        return [next(refs) for _ in range(n)]

    x_hbm, sp_hbm, sc_hbm, sf_hbm, mod_hbm = take(5)
    g_refs = take(4)
    (w_in_hbm, pool_w_ref, pool_scale_ref, conv_w_ref, w_out_hbm, w_up_hbm, ffn_conv_w_ref,
     w_down_hbm) = take(8)
    y_hbm, pool_hbm, conv_hbm, ffn_hbm = take(4)
    (x_ref, sp_ref, sc_ref, mod_ref, w_in_ref, w_out_ref, w_up_ref, w_down_ref, sf_ref) = take(9)
    y_ref, pool_out_ref, conv_out_ref, ffn_out_ref = take(4)
    h2_ref, f_ref, act_ref, in_sems, out_sems = take(5)
    w = (g_refs, w_in_ref, pool_w_ref, pool_scale_ref, conv_w_ref, w_out_ref, w_up_ref,
         ffn_conv_w_ref, w_down_ref)

    n_groups = -(-N_FF_CHUNKS // FF_DOWN_GROUP)

    def group_cols(gi, base):
        first = gi * FF_DOWN_GROUP
        n = min(FF_DOWN_GROUP, N_FF_CHUNKS - first)
        return pl.ds(base + first * FF_CHUNK, n * FF_CHUNK)

    in_order = [("x", [(x_hbm.at[:, t, :], x_ref.at[t]) for t in range(nt)]),
                ("mod", [(mod_hbm, mod_ref)]), ("w_in", [(w_in_hbm, w_in_ref)]),
                ("pool", [(sp_hbm, sp_ref)]),
                ("conv", [(sc_hbm.at[0, :, k, :], sc_ref.at[k]) for k in range(CONV_HIST)]),
                ("w_out", [(w_out_hbm, w_out_ref)]),
                ("ffn", [(sf_hbm.at[0, :, k, :], sf_ref.at[k]) for k in range(CONV_HIST)])]
    for gi in range(n_groups):
        cols = [group_cols(gi, base) for base in (0, D_FF)]
        rows = group_cols(gi, 0)
        in_order.append((f"ffn_w{gi}", [(w_up_hbm.at[:, c], w_up_ref.at[:, c]) for c in cols]
                         + [(w_down_hbm.at[rows, :], w_down_ref.at[rows, :])]))
    fetch, k = {}, 0
    for name, pairs in in_order:
        fetch[name] = [pltpu.make_async_copy(src, dst, in_sems.at[k + i])
                       for i, (src, dst) in enumerate(pairs)]
        k += len(pairs)

    out_order = [("y", [(y_ref.at[t], y_hbm.at[:, t, :]) for t in range(nt)]),
                 ("pool", [(pool_out_ref, pool_hbm)]),
                 ("conv", [(conv_out_ref.at[k], conv_hbm.at[0, :, k, :]) for k in range(CONV_HIST)])]
    for gi in range(n_groups):
        out_order.append((f"ffn{gi}", [
            (ffn_out_ref.at[k, :, group_cols(gi, base)], ffn_hbm.at[0, :, k, group_cols(gi, base)])
            for base in (0, D_FF) for k in range(CONV_HIST)]))
    send, k = {}, 0
    for name, pairs in out_order:
        send[name] = [pltpu.make_async_copy(src, dst, out_sems.at[k + i])
                      for i, (src, dst) in enumerate(pairs)]
        k += len(pairs)

    def start(copies):
        for copy in copies:
            copy.start()

    def wait(copies):
        for copy in copies:
            copy.wait()

    for name, _ in in_order:
        start(fetch[name])

    def hist_ffn(cols):
        return _row_cat([sf_ref[k, :, cols] for k in range(CONV_HIST)])

    def sink_ffn(cols, ext):
        first = ext.shape[0] - CONV_HIST * nb
        for k in range(CONV_HIST):
            ffn_out_ref[k, :, cols] = ext[first + k * nb:first + (k + 1) * nb]

    def sink_pool(ext):
        first = ext.shape[0] - POOL_BUF * nb
        for k in range(POOL_BUF):
            pool_out_ref[k] = ext[first + k * nb:first + (k + 1) * nb]

    def sink_conv(ext):
        first = ext.shape[0] - CONV_HIST * nb
        for k in range(CONV_HIST):
            conv_out_ref[k] = ext[first + k * nb:first + (k + 1) * nb]

    wait(fetch["x"])
    wait(fetch["mod"])
    x = _row_cat([x_ref[t] for t in range(nt)])
    mods = _fold_mods(mod_ref[...], g_refs, nt)
    h = _front_head(x, mods)
    mixed = {}
    front = _front_stages(
        x, h, mods, w, mixed, nb=nb,
        cnt_fn=lambda win: float(min(win, PAST_LEN + 1)),
        hist_pool=lambda: _row_cat([sp_ref[k] for k in range(POOL_BUF)]),
        hist_conv=lambda: _row_cat([sc_ref[k] for k in range(CONV_HIST)]),
        sink_pool=sink_pool, sink_conv=sink_conv)
    wait(fetch["w_in"])
    next(front)
    wait(fetch["pool"])
    wait(fetch["conv"])
    next(front)
    start(send["pool"])
    start(send["conv"])
    wait(fetch["w_out"])
    next(front)

    wait(fetch["ffn"])
    wait(fetch["ffn_w0"])
    ups = _UpValues()
    _ffn_head(front, mixed, w, h2_ref, ups)
    ffn = _ffn_stages(w, h2_ref, f_ref, act_ref, ups, mixed, nb=nb, hist_ffn=hist_ffn,
                      sink_ffn=sink_ffn)
    groups_sent = 0
    for c in range(N_FF_CHUNKS):
        ahead = c + FFN_UPS_AHEAD
        if ahead < N_FF_CHUNKS and ahead % FF_DOWN_GROUP == 0:
            wait(fetch[f"ffn_w{ahead // FF_DOWN_GROUP}"])
            while (groups_sent + 1) * FF_DOWN_GROUP <= c:
                start(send[f"ffn{groups_sent}"])
                groups_sent += 1
        next(ffn)
    y = _finish(mixed, mods[5])
    for t in range(nt):
        y_ref[t] = y[t * nb:(t + 1) * nb]
    for gi in range(groups_sent, n_groups):
        start(send[f"ffn{gi}"])
    start(send["y"])
    for copies in send.values():
        wait(copies)


def _sample(x, sp, sc, sf, mod, g, w_in, pool_w, pool_scale, conv_w, w_out, w_up, ffn_conv_w, w_down):
    nb, nt, _ = x.shape
    hbm = pl.BlockSpec(memory_space=pl.ANY)
    vmem = pl.BlockSpec(memory_space=pltpu.VMEM)
    out_shape = [
        jax.ShapeDtypeStruct(x.shape, _f32),
        jax.ShapeDtypeStruct(sp.shape, _f32),
        jax.ShapeDtypeStruct(sc.shape, _f32),
        jax.ShapeDtypeStruct(sf.shape, _f32),
    ]
    step_major = [pltpu.VMEM((nt, nb, D_MODEL), _f32),
                  pltpu.VMEM(sp.shape, _f32),
                  pltpu.VMEM((CONV_HIST, nb, D_CONV), _f32)]
    n_fetches = nt + 1 + 1 + 1 + CONV_HIST + 1 + CONV_HIST + 3 * -(-N_FF_CHUNKS // FF_DOWN_GROUP)
    n_sends = nt + 1 + CONV_HIST + 2 * CONV_HIST * -(-N_FF_CHUNKS // FF_DOWN_GROUP)
    return pl.pallas_call(
        functools.partial(_sample_kernel, nb=nb, nt=nt),
        in_specs=[hbm, hbm, hbm, hbm, hbm, vmem, vmem, vmem, vmem, hbm, vmem, vmem, vmem, hbm, hbm,
                  vmem, hbm],
        out_specs=[hbm] * len(out_shape),
        out_shape=out_shape,
        scratch_shapes=step_major
        + [pltpu.VMEM(a.shape, a.dtype) for a in (mod, w_in, w_out, w_up, w_down)]
        + [pltpu.VMEM((CONV_HIST, nb, 2 * D_FF), _f32)]
        + step_major
        + [pltpu.VMEM((CONV_HIST, nb, 2 * D_FF), _f32)]
        + [
            pltpu.VMEM((nt * nb, D_MODEL), _bf16),
            pltpu.VMEM((nt * nb, D_MODEL), _f32),
            pltpu.VMEM((nt * nb, D_FF), _bf16),
            pltpu.SemaphoreType.DMA((n_fetches,)),
            pltpu.SemaphoreType.DMA((n_sends,)),
        ],
        compiler_params=pltpu.CompilerParams(vmem_limit_bytes=V7X_VMEM_LIMIT_BYTES),
        name="sample_trunk",
    )(x, sp, sc, sf, mod, *g, w_in, pool_w, pool_scale, conv_w, w_out, w_up, ffn_conv_w, w_down)


def kernel(x_prompt, x_sample, state_pool, state_conv, state_ffn, c_prompt, c_sample, w_ada, b_ada,
           g_pre_mix, g_post_mix, g_pre_ffn, g_post_ffn, w_in, pool_w, pool_scale, conv_w, w_out,
           ffn_w_up, ffn_conv_w, ffn_w_down):
    assert w_ada.shape[0] == 1, "single trunk layer"
    mod_p, mod_s = _ada(c_prompt, c_sample, w_ada[0], b_ada)
    g = (g_pre_mix, g_post_mix, g_pre_ffn, g_post_ffn)
    conv_taps = conv_w.reshape(1, CONV_W * D_CONV)
    ffn_taps = ffn_conv_w.reshape(1, CONV_W * 2 * D_FF)
    y_p, pool_p, conv_p, ffn_p, w_in_bf, w_out_bf, w_up_bf, w_down_bf = _prompt(
        x_prompt, mod_p, g, w_in, pool_w[0], pool_scale, conv_taps, w_out, ffn_w_up, ffn_taps,
        ffn_w_down)
    y_s, pool_s, conv_s, ffn_s = _sample(
        x_sample, state_pool[0].transpose(1, 0, 2), state_conv, state_ffn, mod_s, g, w_in_bf,
        pool_w[0], pool_scale, conv_taps, w_out_bf, w_up_bf, ffn_taps, w_down_bf)
    return (y_p, y_s, pool_p.transpose(1, 0, 2)[None], conv_p, ffn_p,
            pool_s.transpose(1, 0, 2)[None], conv_s, ffn_s)
```

```python
import functools

import jax
import jax.numpy as jnp
from jax import lax
from jax.experimental import pallas as pl
from jax.experimental.pallas import tpu as pltpu

D_MODEL = 1024
D_POOL = 512
D_CONV = 512
POOL_WINDOWS = (2, 4, 8, 16)
POOL_GC = 128
POOL_BUF = 15
CONV_W = 3
CONV_HIST = CONV_W - 1
D_FF = 2816
D_IN_PROJ = D_POOL + 3 * D_CONV
RMS_EPS = 1e-6
PAST_LEN = 16384

V7X_VMEM_LIMIT_BYTES = 58 * 1024 * 1024
SUBLANES = 8
LANES = 128
FF_CHUNK = 256
N_FF_CHUNKS = D_FF // FF_CHUNK
FFN_UPS_AHEAD = 2
FF_DOWN_GROUP = 3
ROW_BLOCKS = 2
PROMPT_TT = 64
WEIGHT_STAGE_ROWS = {2048: 128, 1024: 256, 5632: 64}
WEIGHT_STAGE_SLOTS = 4
ADA_ROWS = 256
POOL_HALO = 16

_bf16 = jnp.bfloat16
_f32 = jnp.float32


def _dot(a, b):
    return jnp.dot(a, b, preferred_element_type=_f32)


def _rms_scaled(x, scale):
    ms = jnp.mean(x * x, axis=-1, keepdims=True)
    return x * lax.rsqrt(ms + RMS_EPS) * scale


def _silu(a):
    return a * jax.nn.sigmoid(a)


def _lane_cat(parts):
    return jnp.concatenate(parts, axis=1)


def _row_cat(parts):
    return jnp.concatenate(parts, axis=0)


def _ada_kernel(cp_hbm, cs_hbm, w_hbm, b_hbm, mp_hbm, ms_hbm, c_buf, b_buf, mod_buf, w_buf, sems,
                io_sems):
    n_blocks, rows = w_buf.shape[0], w_buf.shape[1]
    bp = cp_hbm.shape[0]
    prompt_rows, sample_rows = pl.ds(0, bp), pl.ds(bp, cs_hbm.shape[0])

    def fetch(k):
        return pltpu.make_async_copy(w_hbm.at[pl.ds(k * rows, rows), :], w_buf.at[k], sems.at[k])

    small = [pltpu.make_async_copy(cp_hbm, c_buf.at[prompt_rows], io_sems.at[0]),
             pltpu.make_async_copy(cs_hbm, c_buf.at[sample_rows], io_sems.at[1]),
             pltpu.make_async_copy(b_hbm, b_buf, io_sems.at[2])]
    for copy in small:
        copy.start()
    for k in range(n_blocks):
        fetch(k).start()
    for copy in small:
        copy.wait()
    c = _silu(c_buf[...]).astype(_bf16)
    mod = b_buf[...]
    for k in range(n_blocks):
        fetch(k).wait()
        mod = mod + _dot(c[:, k * rows:(k + 1) * rows], w_buf[k].astype(_bf16))
    mod_buf[...] = mod
    results = [pltpu.make_async_copy(mod_buf.at[prompt_rows], mp_hbm, io_sems.at[0]),
               pltpu.make_async_copy(mod_buf.at[sample_rows], ms_hbm, io_sems.at[1])]
    for copy in results:
        copy.start()
    for copy in results:
        copy.wait()


def _ada(c_prompt, c_sample, w_ada, b_ada):
    n = w_ada.shape[1]
    bp, bs = c_prompt.shape[0], c_sample.shape[0]
    hbm = pl.BlockSpec(memory_space=pl.ANY)
    return pl.pallas_call(
        _ada_kernel,
        in_specs=[hbm, hbm, hbm, hbm],
        out_specs=[hbm, hbm],
        out_shape=[
            jax.ShapeDtypeStruct((bp, n), _f32),
            jax.ShapeDtypeStruct((bs, n), _f32),
        ],
        scratch_shapes=[
            pltpu.VMEM((bp + bs, D_MODEL), _f32),
            pltpu.VMEM(b_ada.shape, _f32),
            pltpu.VMEM((bp + bs, n), _f32),
            pltpu.VMEM((D_MODEL // ADA_ROWS, ADA_ROWS, n), _f32),
            pltpu.SemaphoreType.DMA((D_MODEL // ADA_ROWS,)),
            pltpu.SemaphoreType.DMA((3,)),
        ],
        compiler_params=pltpu.CompilerParams(
            vmem_limit_bytes=D_MODEL * n * 4 + ADA_ROWS * n * 8),
        name="ada_mod",
    )(c_prompt, c_sample, w_ada, b_ada)


def _taps(w_ref, width, cols):
    return [w_ref[:, k * width + cols.start:k * width + cols.stop] for k in range(CONV_W)]


def _row_blocks(m):
    n = m // ROW_BLOCKS
    return [slice(k * n, (k + 1) * n) for k in range(ROW_BLOCKS)]


def _front_head(x, mods):
    return (_rms_scaled(x, mods[1]) + mods[0]).astype(_bf16)


def _front_stages(x, h, mods, w, out, *, nb, cnt_fn, hist_pool, hist_conv, sink_pool, sink_conv):
    (_, w_in_ref, pool_w_ref, pool_scale_ref, conv_w_ref, w_out_ref) = w[:6]
    _, _, gate1, sh2, scale2, _ = mods
    m = x.shape[0]

    proj = _dot(h[...], w_in_ref[...])
    yield
    v_pool = proj[:, :D_POOL]
    x_conv = proj[:, D_POOL:D_POOL + D_CONV]
    gate_b = proj[:, D_POOL + D_CONV:D_POOL + 2 * D_CONV]
    gate_c = proj[:, D_POOL + 2 * D_CONV:]

    pool_ext = _row_cat([hist_pool(), v_pool])
    halo = (pool_ext.shape[0] - m) // nb
    y_pool = []
    for gi, win in enumerate(POOL_WINDOWS):
        sl = slice(gi * POOL_GC, (gi + 1) * POOL_GC)
        s = pool_ext[:, sl]
        step = 1
        while step < win:
            n = s.shape[0]
            s = s[step * nb:] + s[:n - step * nb]
            step *= 2
        first = (halo - (win - 1)) * nb
        d = s[first:first + m] / cnt_fn(win) - v_pool[:, sl]
        y_pool.append(_dot(d.astype(_bf16), pool_w_ref[gi].astype(_bf16)))
    y_pool = _lane_cat(y_pool) * pool_scale_ref[...]
    sink_pool(pool_ext)

    cx = gate_c * x_conv
    conv_ext = _row_cat([hist_conv(), cx])
    cw = _taps(conv_w_ref, D_CONV, slice(0, D_CONV))
    conv = conv_ext[0:m] * cw[0] + conv_ext[nb:nb + m] * cw[1] + cx * cw[2]
    y_conv = gate_b * conv
    sink_conv(conv_ext)
    mixed = _lane_cat([y_pool, y_conv]).astype(_bf16)
    yield
    blocks = _row_blocks(m)
    mix = [_dot(mixed[rows], w_out_ref[...]) for rows in blocks]
    yield
    out["x1"], out["h2"] = [], []
    for rows, mix_rows in zip(blocks, mix):
        x1 = x[rows] + _rms_scaled(mix_rows, gate1[rows])
        out["x1"].append(x1)
        out["h2"].append((_rms_scaled(x1, scale2[rows]) + sh2[rows]).astype(_bf16))
        yield


class _UpValues:
    def __init__(self):
        self._v = {}

    def put(self, c, half, val):
        self._v.setdefault((c, half), []).append(val)

    def get(self, c, half):
        return _row_cat(self._v.pop((c, half)))


def _ffn_cols(c):
    return [slice(base + c * FF_CHUNK, base + (c + 1) * FF_CHUNK) for base in (0, D_FF)]


def _ffn_issue(w, h2_ref, ups, c, rows=slice(None)):
    for half, cols in enumerate(_ffn_cols(c)):
        ups.put(c, half, _dot(h2_ref[rows, :], w[6][:, cols]))


def _ffn_stages(w, h2_ref, f_ref, act_ref, ups, out, *, nb, hist_ffn, sink_ffn):
    _, ffn_conv_w_ref, w_down_ref = w[6:]
    m = h2_ref.shape[0]
    last = N_FF_CHUNKS - 1
    groups = [range(first, min(first + FF_DOWN_GROUP, N_FF_CHUNKS))
              for first in range(0, N_FF_CHUNKS, FF_DOWN_GROUP)]
    assert len(groups) >= 2, "the final norm adds to earlier groups' partial sums in f_ref"

    def group_rows(group):
        return slice(group.start * FF_CHUNK, group.stop * FF_CHUNK)

    def down(group):
        rows = group_rows(group)
        contrib = _dot(act_ref[:, rows], w_down_ref[rows, :])
        f_ref[...] = contrib if group.start == 0 else f_ref[...] + contrib

    kept = groups[-2] if groups[-2].stop - 1 < last else None
    for c in range(N_FF_CHUNKS):
        if c + FFN_UPS_AHEAD < N_FF_CHUNKS:
            _ffn_issue(w, h2_ref, ups, c + FFN_UPS_AHEAD)
        if c == last and kept is not None:
            down(kept)
        halves = []
        for half, cols in enumerate(_ffn_cols(c)):
            up = ups.get(c, half)
            up_ext = _row_cat([hist_ffn(cols), up])
            fw = _taps(ffn_conv_w_ref, 2 * D_FF, cols)
            halves.append(up_ext[0:m] * fw[0] + up_ext[nb:nb + m] * fw[1] + up * fw[2])
            sink_ffn(cols, up_ext)
        act_ref[:, c * FF_CHUNK:(c + 1) * FF_CHUNK] = (_silu(halves[0]) * halves[1]).astype(_bf16)
        for group in groups:
            if c == group.stop - 1 and c < last and group is not kept:
                down(group)
        if c == last:
            rows = group_rows(groups[-1])
            contrib = [_dot(act_ref[blk, rows], w_down_ref[rows, :]) for blk in _row_blocks(m)]
            out["f"] = [f_ref[blk, :] + part for blk, part in zip(_row_blocks(m), contrib)]
        yield


def _ffn_head(front, out, w, h2_ref, ups):
    for blk, rows in enumerate(_row_blocks(h2_ref.shape[0])):
        next(front)
        h2_ref[rows, :] = out["h2"][blk]
        for c in range(FFN_UPS_AHEAD):
            _ffn_issue(w, h2_ref, ups, c, rows)


def _finish(out, gate2):
    blocks = _row_blocks(gate2.shape[0])
    return _row_cat([x1 + _rms_scaled(f, gate2[rows])
                     for rows, x1, f in zip(blocks, out["x1"], out["f"])])


def _fold_mods(mod, g_refs, n_steps):
    g = [g_ref[...] for g_ref in g_refs]
    sh1, sc1, gt1, sh2, sc2, gt2 = [mod[:, i * D_MODEL:(i + 1) * D_MODEL] for i in range(6)]
    folded = [sh1, g[0] * (1.0 + sc1), gt1 * g[1], sh2, g[2] * (1.0 + sc2), gt2 * g[3]]
    return [_row_cat([a] * n_steps) for a in folded]


def _trunk(x, mods, w, h2_ref, f_ref, act_ref, *, nb, cnt_fn, hist_pool, hist_conv, hist_ffn,
           sink_pool, sink_conv, sink_ffn):
    out = {}
    front = _front_stages(x, _front_head(x, mods), mods, w, out, nb=nb, cnt_fn=cnt_fn,
                          hist_pool=hist_pool, hist_conv=hist_conv, sink_pool=sink_pool,
                          sink_conv=sink_conv)
    for _ in range(3):
        next(front)
    ups = _UpValues()
    _ffn_head(front, out, w, h2_ref, ups)
    for _ in _ffn_stages(w, h2_ref, f_ref, act_ref, ups, out, nb=nb, hist_ffn=hist_ffn,
                         sink_ffn=sink_ffn):
        pass
    return _finish(out, mods[5])


def _stream_cast(src, dst, stage, sems, sem_row):
    slots, rows = stage.shape[0], stage.shape[1]
    n = src.shape[0] // rows
    ahead = slots - 1

    def chunk(i, slot):
        return pltpu.make_async_copy(src.at[pl.ds(i * rows, rows), :], stage.at[slot],
                                     sems.at[sem_row, slot])

    for i in range(min(ahead, n)):
        chunk(i, i).start()

    def body(i, carry):
        slot = lax.rem(i, slots)

        @pl.when(i + ahead < n)
        def _():
            chunk(i + ahead, lax.rem(i + ahead, slots)).start()

        chunk(i, slot).wait()
        dst[pl.ds(pl.multiple_of(i * rows, rows), rows), :] = stage[slot].astype(_bf16)
        return carry

    lax.fori_loop(0, n, body, 0)


def _prompt_kernel(x_hbm, mod_ref, *rest, nb, tt, n_steps):
    g_refs, rest = rest[:4], rest[4:]
    (w_in_hbm, pool_w_ref, pool_scale_ref, conv_w_ref, w_out_hbm, w_up_hbm, ffn_conv_w_ref,
     w_down_hbm) = rest[:8]
    y_hbm, pool_out_ref, conv_out_ref, ffn_out_ref = rest[8:12]
    bf_out = rest[12:16]
    (x_buf, y_buf, pool_carry, conv_carry, ffn_carry, h2_ref, f_ref, act_ref,
     w_in_bf, w_out_bf, w_up_bf, w_down_bf, stage_in, stage_sq, stage_up,
     x_sems, y_sems, stage_sems, out_sems) = rest[16:]
    bf_vmem = (w_in_bf, w_out_bf, w_up_bf, w_down_bf)
    w = (g_refs, w_in_bf, pool_w_ref, pool_scale_ref, conv_w_ref, w_out_bf, w_up_bf, ffn_conv_w_ref,
         w_down_bf)
    j = pl.program_id(0)

    def bf_writeback(k):
        return pltpu.make_async_copy(bf_vmem[k], bf_out[k], out_sems.at[k])

    m = tt * nb
    slot = lax.rem(j, 2)

    def tile_copies(tile, buf_slot, to_hbm):
        copies = []
        for b in range(nb):
            hbm = (y_hbm if to_hbm else x_hbm).at[b, pl.ds(tile * tt, tt), :]
            if to_hbm:
                copies.append(pltpu.make_async_copy(y_buf.at[buf_slot, :, b, :], hbm,
                                                    y_sems.at[buf_slot, b]))
            else:
                copies.append(pltpu.make_async_copy(hbm, x_buf.at[buf_slot, :, b, :],
                                                    x_sems.at[buf_slot, b]))
        return copies

    def start(copies):
        for copy in copies:
            copy.start()

    def wait(copies):
        for copy in copies:
            copy.wait()

    @pl.when(j == 0)
    def _():
        start(tile_copies(0, 0, False))
        pool_carry[...] = jnp.zeros_like(pool_carry)
        conv_carry[...] = jnp.zeros_like(conv_carry)
        ffn_carry[...] = jnp.zeros_like(ffn_carry)
        _stream_cast(w_in_hbm.at[0], w_in_bf, stage_in, stage_sems, 0)
        _stream_cast(w_out_hbm.at[0], w_out_bf, stage_sq, stage_sems, 1)
        _stream_cast(w_up_hbm.at[0], w_up_bf, stage_up, stage_sems, 2)
        _stream_cast(w_down_hbm.at[0], w_down_bf, stage_sq, stage_sems, 1)
        for k in range(len(bf_vmem)):
            bf_writeback(k).start()

    @pl.when(j + 1 < n_steps)
    def _():
        start(tile_copies(j + 1, 1 - slot, False))

    @pl.when(j >= 2)
    def _():
        wait(tile_copies(j - 2, slot, True))

    wait(tile_copies(j, slot, False))
    x = x_buf[slot].reshape(m, D_MODEL)

    mods = _fold_mods(mod_ref[...], g_refs, tt)

    t_idx = j * tt + lax.shift_right_logical(
        lax.broadcasted_iota(jnp.int32, (m, POOL_GC), 0), nb.bit_length() - 1)

    def cnt_fn(win):
        return jnp.minimum(win, t_idx + 1).astype(_f32)

    def carry_rows(ref, first, n):
        return _lane_cat([ref[first + q] for q in range(n)])

    def keep_rows(ref, first, ext):
        rows = ref.shape[1]
        for q in range(ext.shape[1] // LANES):
            ref[first + q] = ext[ext.shape[0] - rows:, q * LANES:(q + 1) * LANES]

    y = _trunk(
        x, mods, w, h2_ref, f_ref, act_ref, nb=nb, cnt_fn=cnt_fn,
        hist_pool=lambda: carry_rows(pool_carry, 0, D_POOL // LANES),
        hist_conv=lambda: carry_rows(conv_carry, 0, D_CONV // LANES),
        hist_ffn=lambda cols: carry_rows(ffn_carry, cols.start // LANES, FF_CHUNK // LANES),
        sink_pool=lambda ext: keep_rows(pool_carry, 0, ext),
        sink_conv=lambda ext: keep_rows(conv_carry, 0, ext),
        sink_ffn=lambda cols, ext: keep_rows(ffn_carry, cols.start // LANES, ext))

    y_buf[slot] = y.reshape(tt, nb, D_MODEL)
    start(tile_copies(j, slot, True))

    @pl.when(j == n_steps - 1)
    def _():
        def emit(out_ref, carry, n):
            first = carry.shape[1] // nb - n
            for b in range(nb):
                for q in range(carry.shape[0]):
                    out_ref[0, b, :, q * LANES:(q + 1) * LANES] = (
                        carry[q, pl.ds(first * nb + b, n, stride=nb), :])
        pool_rows = carry_rows(pool_carry, 0, D_POOL // LANES)
        for k in range(POOL_BUF):
            first = (POOL_HALO - POOL_BUF + k) * nb
            pool_out_ref[k] = pool_rows[first:first + nb]
        emit(conv_out_ref, conv_carry, CONV_HIST)
        emit(ffn_out_ref, ffn_carry, CONV_HIST)
        for k in range(len(bf_vmem)):
            bf_writeback(k).wait()
        if n_steps >= 2:
            wait(tile_copies(j - 1, 1 - slot, True))
        wait(tile_copies(j, slot, True))


def _prompt(x, mod, g, w_in, pool_w, pool_scale, conv_w, w_out, w_up, ffn_conv_w, w_down):
    nb, seq, _ = x.shape
    assert nb == SUBLANES, "one vreg row per time step"
    tt = PROMPT_TT
    n_steps = seq // tt
    m = tt * nb
    big = (w_in, w_out, w_up, w_down)
    stage_widths = (D_IN_PROJ, D_MODEL, 2 * D_FF)
    in_hbm = pl.BlockSpec(memory_space=pl.ANY)

    def const_spec(a):
        nd = a.ndim
        return pl.BlockSpec(a.shape, lambda j: (0,) * nd, pipeline_mode=pl.Buffered(1))

    return pl.pallas_call(
        functools.partial(_prompt_kernel, nb=nb, tt=tt, n_steps=n_steps),
        grid=(n_steps,),
        in_specs=[in_hbm, const_spec(mod),
                  *[const_spec(a) for a in g], in_hbm, const_spec(pool_w), const_spec(pool_scale),
                  const_spec(conv_w), in_hbm, in_hbm, const_spec(ffn_conv_w), in_hbm],
        out_specs=[
            in_hbm,
            pl.BlockSpec((POOL_BUF, nb, D_POOL), lambda j: (0, 0, 0)),
            pl.BlockSpec((1, nb, CONV_HIST, D_CONV), lambda j: (0, 0, 0, 0)),
            pl.BlockSpec((1, nb, CONV_HIST, 2 * D_FF), lambda j: (0, 0, 0, 0)),
        ] + [pl.BlockSpec(memory_space=pl.ANY) for _ in big],
        out_shape=[
            jax.ShapeDtypeStruct((nb, seq, D_MODEL), _f32),
            jax.ShapeDtypeStruct((POOL_BUF, nb, D_POOL), _f32),
            jax.ShapeDtypeStruct((1, nb, CONV_HIST, D_CONV), _f32),
            jax.ShapeDtypeStruct((1, nb, CONV_HIST, 2 * D_FF), _f32),
        ] + [jax.ShapeDtypeStruct(a.shape[1:], _bf16) for a in big],
        scratch_shapes=[
            pltpu.VMEM((2, tt, nb, D_MODEL), _f32),
            pltpu.VMEM((2, tt, nb, D_MODEL), _f32),
            pltpu.VMEM((D_POOL // LANES, POOL_HALO * nb, LANES), _f32),
            pltpu.VMEM((D_CONV // LANES, CONV_HIST * nb, LANES), _f32),
            pltpu.VMEM((2 * D_FF // LANES, CONV_HIST * nb, LANES), _f32),
            pltpu.VMEM((m, D_MODEL), _bf16),
            pltpu.VMEM((m, D_MODEL), _f32),
            pltpu.VMEM((m, D_FF), _bf16),
        ] + [pltpu.VMEM(a.shape[1:], _bf16) for a in big]
        + [pltpu.VMEM((WEIGHT_STAGE_SLOTS, WEIGHT_STAGE_ROWS[c], c), _f32) for c in stage_widths]
        + [pltpu.SemaphoreType.DMA((2, nb)), pltpu.SemaphoreType.DMA((2, nb)),
           pltpu.SemaphoreType.DMA((len(stage_widths), WEIGHT_STAGE_SLOTS)),
           pltpu.SemaphoreType.DMA((len(big),))],
        compiler_params=pltpu.CompilerParams(
            dimension_semantics=("arbitrary",),
            vmem_limit_bytes=V7X_VMEM_LIMIT_BYTES),
        name="prompt_trunk",
    )(x, mod, *g, w_in, pool_w, pool_scale, conv_w, w_out, w_up, ffn_conv_w, w_down)


def _sample_kernel(*refs, nb, nt):
    refs = iter(refs)

    def take(n):
        return [next(refs) for _ in range(n)]

    x_hbm, sp_hbm, sc_hbm, sf_hbm, mod_hbm = take(5)
    g_refs = take(4)
    (w_in_hbm, pool_w_ref, pool_scale_ref, conv_w_ref, w_out_hbm, w_up_hbm, ffn_conv_w_ref,
     w_down_hbm) = take(8)
    y_hbm, pool_hbm, conv_hbm, ffn_hbm = take(4)
    (x_ref, sp_ref, sc_ref, mod_ref, w_in_ref, w_out_ref, w_up_ref, w_down_ref, sf_ref) = take(9)
    y_ref, pool_out_ref, conv_out_ref, ffn_out_ref = take(4)
    h2_ref, f_ref, act_ref, in_sems, out_sems = take(5)
    w = (g_refs, w_in_ref, pool_w_ref, pool_scale_ref, conv_w_ref, w_out_ref, w_up_ref,
         ffn_conv_w_ref, w_down_ref)

    n_groups = -(-N_FF_CHUNKS // FF_DOWN_GROUP)

    def group_cols(gi, base):
        first = gi * FF_DOWN_GROUP
        n = min(FF_DOWN_GROUP, N_FF_CHUNKS - first)
        return pl.ds(base + first * FF_CHUNK, n * FF_CHUNK)

    in_order = [("x", [(x_hbm.at[:, t, :], x_ref.at[t]) for t in range(nt)]),
                ("mod", [(mod_hbm, mod_ref)]), ("w_in", [(w_in_hbm, w_in_ref)]),
                ("pool", [(sp_hbm, sp_ref)]),
                ("conv", [(sc_hbm.at[0, :, k, :], sc_ref.at[k]) for k in range(CONV_HIST)]),
                ("w_out", [(w_out_hbm, w_out_ref)]),
                ("ffn", [(sf_hbm.at[0, :, k, :], sf_ref.at[k]) for k in range(CONV_HIST)])]
    for gi in range(n_groups):
        cols = [group_cols(gi, base) for base in (0, D_FF)]
        rows = group_cols(gi, 0)
        in_order.append((f"ffn_w{gi}", [(w_up_hbm.at[:, c], w_up_ref.at[:, c]) for c in cols]
                         + [(w_down_hbm.at[rows, :], w_down_ref.at[rows, :])]))
    fetch, k = {}, 0
    for name, pairs in in_order:
        fetch[name] = [pltpu.make_async_copy(src, dst, in_sems.at[k + i])
                       for i, (src, dst) in enumerate(pairs)]
        k += len(pairs)

    out_order = [("y", [(y_ref.at[t], y_hbm.at[:, t, :]) for t in range(nt)]),
                 ("pool", [(pool_out_ref, pool_hbm)]),
                 ("conv", [(conv_out_ref.at[k], conv_hbm.at[0, :, k, :]) for k in range(CONV_HIST)])]
    for gi in range(n_groups):
        out_order.append((f"ffn{gi}", [
            (ffn_out_ref.at[k, :, group_cols(gi, base)], ffn_hbm.at[0, :, k, group_cols(gi, base)])
            for base in (0, D_FF) for k in range(CONV_HIST)]))
    send, k = {}, 0
    for name, pairs in out_order:
        send[name] = [pltpu.make_async_copy(src, dst, out_sems.at[k + i])
                      for i, (src, dst) in enumerate(pairs)]
        k += len(pairs)

    def start(copies):
        for copy in copies:
            copy.start()

    def wait(copies):
        for copy in copies:
            copy.wait()

    for name, _ in in_order:
        start(fetch[name])

    def hist_ffn(cols):
        return _row_cat([sf_ref[k, :, cols] for k in range(CONV_HIST)])

    def sink_ffn(cols, ext):
        first = ext.shape[0] - CONV_HIST * nb
        for k in range(CONV_HIST):
            ffn_out_ref[k, :, cols] = ext[first + k * nb:first + (k + 1) * nb]

    def sink_pool(ext):
        first = ext.shape[0] - POOL_BUF * nb
        for k in range(POOL_BUF):
            pool_out_ref[k] = ext[first + k * nb:first + (k + 1) * nb]

    def sink_conv(ext):
        first = ext.shape[0] - CONV_HIST * nb
        for k in range(CONV_HIST):
            conv_out_ref[k] = ext[first + k * nb:first + (k + 1) * nb]

    wait(fetch["x"])
    wait(fetch["mod"])
    x = _row_cat([x_ref[t] for t in range(nt)])
    mods = _fold_mods(mod_ref[...], g_refs, nt)
    h = _front_head(x, mods)
    mixed = {}
    front = _front_stages(
        x, h, mods, w, mixed, nb=nb,
        cnt_fn=lambda win: float(min(win, PAST_LEN + 1)),
        hist_pool=lambda: _row_cat([sp_ref[k] for k in range(POOL_BUF)]),
        hist_conv=lambda: _row_cat([sc_ref[k] for k in range(CONV_HIST)]),
        sink_pool=sink_pool, sink_conv=sink_conv)
    wait(fetch["w_in"])
    next(front)
    wait(fetch["pool"])
    wait(fetch["conv"])
    next(front)
    start(send["pool"])
    start(send["conv"])
    wait(fetch["w_out"])
    next(front)

    wait(fetch["ffn"])
    wait(fetch["ffn_w0"])
    ups = _UpValues()
    _ffn_head(front, mixed, w, h2_ref, ups)
    ffn = _ffn_stages(w, h2_ref, f_ref, act_ref, ups, mixed, nb=nb, hist_ffn=hist_ffn,
                      sink_ffn=sink_ffn)
    groups_sent = 0
    for c in range(N_FF_CHUNKS):
        ahead = c + FFN_UPS_AHEAD
        if ahead < N_FF_CHUNKS and ahead % FF_DOWN_GROUP == 0:
            wait(fetch[f"ffn_w{ahead // FF_DOWN_GROUP}"])
            while (groups_sent + 1) * FF_DOWN_GROUP <= c:
                start(send[f"ffn{groups_sent}"])
                groups_sent += 1
        next(ffn)
    y = _finish(mixed, mods[5])
    for t in range(nt):
        y_ref[t] = y[t * nb:(t + 1) * nb]
    for gi in range(groups_sent, n_groups):
        start(send[f"ffn{gi}"])
    start(send["y"])
    for copies in send.values():
        wait(copies)


def _sample(x, sp, sc, sf, mod, g, w_in, pool_w, pool_scale, conv_w, w_out, w_up, ffn_conv_w, w_down):
    nb, nt, _ = x.shape
    hbm = pl.BlockSpec(memory_space=pl.ANY)
    vmem = pl.BlockSpec(memory_space=pltpu.VMEM)
    out_shape = [
        jax.ShapeDtypeStruct(x.shape, _f32),
        jax.ShapeDtypeStruct(sp.shape, _f32),
        jax.ShapeDtypeStruct(sc.shape, _f32),
        jax.ShapeDtypeStruct(sf.shape, _f32),
    ]
    step_major = [pltpu.VMEM((nt, nb, D_MODEL), _f32),
                  pltpu.VMEM(sp.shape, _f32),
                  pltpu.VMEM((CONV_HIST, nb, D_CONV), _f32)]
    n_fetches = nt + 1 + 1 + 1 + CONV_HIST + 1 + CONV_HIST + 3 * -(-N_FF_CHUNKS // FF_DOWN_GROUP)
    n_sends = nt + 1 + CONV_HIST + 2 * CONV_HIST * -(-N_FF_CHUNKS // FF_DOWN_GROUP)
    return pl.pallas_call(
        functools.partial(_sample_kernel, nb=nb, nt=nt),
        in_specs=[hbm, hbm, hbm, hbm, hbm, vmem, vmem, vmem, vmem, hbm, vmem, vmem, vmem, hbm, hbm,
                  vmem, hbm],
        out_specs=[hbm] * len(out_shape),
        out_shape=out_shape,
        scratch_shapes=step_major
        + [pltpu.VMEM(a.shape, a.dtype) for a in (mod, w_in, w_out, w_up, w_down)]
        + [pltpu.VMEM((CONV_HIST, nb, 2 * D_FF), _f32)]
        + step_major
        + [pltpu.VMEM((CONV_HIST, nb, 2 * D_FF), _f32)]
        + [
            pltpu.VMEM((nt * nb, D_MODEL), _bf16),
            pltpu.VMEM((nt * nb, D_MODEL), _f32),
            pltpu.VMEM((nt * nb, D_FF), _bf16),
            pltpu.SemaphoreType.DMA((n_fetches,)),
            pltpu.SemaphoreType.DMA((n_sends,)),
        ],
        compiler_params=pltpu.CompilerParams(vmem_limit_bytes=V7X_VMEM_LIMIT_BYTES),
        name="sample_trunk",
    )(x, sp, sc, sf, mod, *g, w_in, pool_w, pool_scale, conv_w, w_out, w_up, ffn_conv_w, w_down)


def kernel(x_prompt, x_sample, state_pool, state_conv, state_ffn, c_prompt, c_sample, w_ada, b_ada,
           g_pre_mix, g_post_mix, g_pre_ffn, g_post_ffn, w_in, pool_w, pool_scale, conv_w, w_out,
           ffn_w_up, ffn_conv_w, ffn_w_down):
    assert w_ada.shape[0] == 1, "single trunk layer"
    mod_p, mod_s = _ada(c_prompt, c_sample, w_ada[0], b_ada)
    g = (g_pre_mix, g_post_mix, g_pre_ffn, g_post_ffn)
    conv_taps = conv_w.reshape(1, CONV_W * D_CONV)
    ffn_taps = ffn_conv_w.reshape(1, CONV_W * 2 * D_FF)
    y_p, pool_p, conv_p, ffn_p, w_in_bf, w_out_bf, w_up_bf, w_down_bf = _prompt(
        x_prompt, mod_p, g, w_in, pool_w[0], pool_scale, conv_taps, w_out, ffn_w_up, ffn_taps,
        ffn_w_down)
    y_s, pool_s, conv_s, ffn_s = _sample(
        x_sample, state_pool[0].transpose(1, 0, 2), state_conv, state_ffn, mod_s, g, w_in_bf,
        pool_w[0], pool_scale, conv_taps, w_out_bf, w_up_bf, ffn_taps, w_down_bf)
    return (y_p, y_s, pool_p.transpose(1, 0, 2)[None], conv_p, ffn_p,
            pool_s.transpose(1, 0, 2)[None], conv_s, ffn_s)
```

```python
import functools

import jax
import jax.numpy as jnp
from jax import lax
from jax.experimental import pallas as pl
from jax.experimental.pallas import tpu as pltpu

D_MODEL = 1024
D_POOL = 512
D_CONV = 512
POOL_WINDOWS = (2, 4, 8, 16)
POOL_GC = 128
POOL_BUF = 15
CONV_W = 3
CONV_HIST = CONV_W - 1
D_FF = 2816
D_IN_PROJ = D_POOL + 3 * D_CONV
RMS_EPS = 1e-6
PAST_LEN = 16384

V7X_VMEM_LIMIT_BYTES = 58 * 1024 * 1024
SUBLANES = 8
LANES = 128
FF_CHUNK = 256
N_FF_CHUNKS = D_FF // FF_CHUNK
FFN_UPS_AHEAD = 2
FF_DOWN_GROUP = 3
ROW_BLOCKS = 2
PROMPT_TT = 64
WEIGHT_STAGE_ROWS = {2048: 128, 1024: 256, 5632: 64}
WEIGHT_STAGE_SLOTS = 4
ADA_ROWS = 256
POOL_HALO = 16

_bf16 = jnp.bfloat16
_f32 = jnp.float32


def _dot(a, b):
    return jnp.dot(a, b, preferred_element_type=_f32)


def _rms_scaled(x, scale):
    ms = jnp.mean(x * x, axis=-1, keepdims=True)
    return x * lax.rsqrt(ms + RMS_EPS) * scale


def _silu(a):
    return a * jax.nn.sigmoid(a)


def _lane_cat(parts):
    return jnp.concatenate(parts, axis=1)


def _row_cat(parts):
    return jnp.concatenate(parts, axis=0)


def _ada_kernel(cp_hbm, cs_hbm, w_hbm, b_hbm, mp_hbm, ms_hbm, c_buf, b_buf, mod_buf, w_buf, sems,
                io_sems):
    n_blocks, rows = w_buf.shape[0], w_buf.shape[1]
    bp = cp_hbm.shape[0]
    prompt_rows, sample_rows = pl.ds(0, bp), pl.ds(bp, cs_hbm.shape[0])

    def fetch(k):
        return pltpu.make_async_copy(w_hbm.at[pl.ds(k * rows, rows), :], w_buf.at[k], sems.at[k])

    small = [pltpu.make_async_copy(cp_hbm, c_buf.at[prompt_rows], io_sems.at[0]),
             pltpu.make_async_copy(cs_hbm, c_buf.at[sample_rows], io_sems.at[1]),
             pltpu.make_async_copy(b_hbm, b_buf, io_sems.at[2])]
    for copy in small:
        copy.start()
    for k in range(n_blocks):
        fetch(k).start()
    for copy in small:
        copy.wait()
    c = _silu(c_buf[...]).astype(_bf16)
    mod = b_buf[...]
    for k in range(n_blocks):
        fetch(k).wait()
        mod = mod + _dot(c[:, k * rows:(k + 1) * rows], w_buf[k].astype(_bf16))
    mod_buf[...] = mod
    results = [pltpu.make_async_copy(mod_buf.at[prompt_rows], mp_hbm, io_sems.at[0]),
               pltpu.make_async_copy(mod_buf.at[sample_rows], ms_hbm, io_sems.at[1])]
    for copy in results:
        copy.start()
    for copy in results:
        copy.wait()


def _ada(c_prompt, c_sample, w_ada, b_ada):
    n = w_ada.shape[1]
    bp, bs = c_prompt.shape[0], c_sample.shape[0]
    hbm = pl.BlockSpec(memory_space=pl.ANY)
    return pl.pallas_call(
        _ada_kernel,
        in_specs=[hbm, hbm, hbm, hbm],
        out_specs=[hbm, hbm],
        out_shape=[
            jax.ShapeDtypeStruct((bp, n), _f32),
            jax.ShapeDtypeStruct((bs, n), _f32),
        ],
        scratch_shapes=[
            pltpu.VMEM((bp + bs, D_MODEL), _f32),
            pltpu.VMEM(b_ada.shape, _f32),
            pltpu.VMEM((bp + bs, n), _f32),
            pltpu.VMEM((D_MODEL // ADA_ROWS, ADA_ROWS, n), _f32),
            pltpu.SemaphoreType.DMA((D_MODEL // ADA_ROWS,)),
            pltpu.SemaphoreType.DMA((3,)),
        ],
        compiler_params=pltpu.CompilerParams(vmem_limit_bytes=V7X_VMEM_LIMIT_BYTES),
        name="ada_mod",
    )(c_prompt, c_sample, w_ada, b_ada)


def _taps(w_ref, width, cols):
    return [w_ref[:, k * width + cols.start:k * width + cols.stop] for k in range(CONV_W)]


def _row_blocks(m):
    n = m // ROW_BLOCKS
    return [slice(k * n, (k + 1) * n) for k in range(ROW_BLOCKS)]


def _front_head(x, mods):
    return (_rms_scaled(x, mods[1]) + mods[0]).astype(_bf16)


def _front_stages(x, h, mods, w, out, *, nb, cnt_fn, hist_pool, hist_conv, sink_pool, sink_conv):
    (_, w_in_ref, pool_w_ref, pool_scale_ref, conv_w_ref, w_out_ref) = w[:6]
    _, _, gate1, sh2, scale2, _ = mods
    m = x.shape[0]

    proj = _dot(h[...], w_in_ref[...])
    yield
    v_pool = proj[:, :D_POOL]
    x_conv = proj[:, D_POOL:D_POOL + D_CONV]
    gate_b = proj[:, D_POOL + D_CONV:D_POOL + 2 * D_CONV]
    gate_c = proj[:, D_POOL + 2 * D_CONV:]

    pool_ext = _row_cat([hist_pool(), v_pool])
    halo = (pool_ext.shape[0] - m) // nb
    y_pool = []
    for gi, win in enumerate(POOL_WINDOWS):
        sl = slice(gi * POOL_GC, (gi + 1) * POOL_GC)
        s = pool_ext[:, sl]
        step = 1
        while step < win:
            n = s.shape[0]
            s = s[step * nb:] + s[:n - step * nb]
            step *= 2
        first = (halo - (win - 1)) * nb
        d = s[first:first + m] / cnt_fn(win) - v_pool[:, sl]
        y_pool.append(_dot(d.astype(_bf16), pool_w_ref[gi].astype(_bf16)))
    y_pool = _lane_cat(y_pool) * pool_scale_ref[...]
    sink_pool(pool_ext)

    cx = gate_c * x_conv
    conv_ext = _row_cat([hist_conv(), cx])
    cw = _taps(conv_w_ref, D_CONV, slice(0, D_CONV))
    conv = conv_ext[0:m] * cw[0] + conv_ext[nb:nb + m] * cw[1] + cx * cw[2]
    y_conv = gate_b * conv
    sink_conv(conv_ext)
    mixed = _lane_cat([y_pool, y_conv]).astype(_bf16)
    yield
    blocks = _row_blocks(m)
    mix = [_dot(mixed[rows], w_out_ref[...]) for rows in blocks]
    yield
    out["x1"], out["h2"] = [], []
    for rows, mix_rows in zip(blocks, mix):
        x1 = x[rows] + _rms_scaled(mix_rows, gate1[rows])
        out["x1"].append(x1)
        out["h2"].append((_rms_scaled(x1, scale2[rows]) + sh2[rows]).astype(_bf16))
        yield


class _UpValues:
    def __init__(self):
        self._v = {}

    def put(self, c, half, val):
        self._v.setdefault((c, half), []).append(val)

    def get(self, c, half):
        return _row_cat(self._v.pop((c, half)))


def _ffn_cols(c):
    return [slice(base + c * FF_CHUNK, base + (c + 1) * FF_CHUNK) for base in (0, D_FF)]


def _ffn_issue(w, h2_ref, ups, c, rows=slice(None)):
    for half, cols in enumerate(_ffn_cols(c)):
        ups.put(c, half, _dot(h2_ref[rows, :], w[6][:, cols]))


def _ffn_stages(w, h2_ref, f_ref, act_ref, ups, out, *, nb, hist_ffn, sink_ffn):
    _, ffn_conv_w_ref, w_down_ref = w[6:]
    m = h2_ref.shape[0]
    last = N_FF_CHUNKS - 1
    groups = [range(first, min(first + FF_DOWN_GROUP, N_FF_CHUNKS))
              for first in range(0, N_FF_CHUNKS, FF_DOWN_GROUP)]
    assert len(groups) >= 2, "the final norm adds to earlier groups' partial sums in f_ref"

    def group_rows(group):
        return slice(group.start * FF_CHUNK, group.stop * FF_CHUNK)

    def down(group):
        rows = group_rows(group)
        contrib = _dot(act_ref[:, rows], w_down_ref[rows, :])
        f_ref[...] = contrib if group.start == 0 else f_ref[...] + contrib

    kept = groups[-2] if groups[-2].stop - 1 < last else None
    for c in range(N_FF_CHUNKS):
        if c + FFN_UPS_AHEAD < N_FF_CHUNKS:
            _ffn_issue(w, h2_ref, ups, c + FFN_UPS_AHEAD)
        if c == last and kept is not None:
            down(kept)
        halves = []
        for half, cols in enumerate(_ffn_cols(c)):
            up = ups.get(c, half)
            up_ext = _row_cat([hist_ffn(cols), up])
            fw = _taps(ffn_conv_w_ref, 2 * D_FF, cols)
            halves.append(up_ext[0:m] * fw[0] + up_ext[nb:nb + m] * fw[1] + up * fw[2])
            sink_ffn(cols, up_ext)
        act_ref[:, c * FF_CHUNK:(c + 1) * FF_CHUNK] = (_silu(halves[0]) * halves[1]).astype(_bf16)
        for group in groups:
            if c == group.stop - 1 and c < last and group is not kept:
                down(group)
        if c == last:
            rows = group_rows(groups[-1])
            contrib = [_dot(act_ref[blk, rows], w_down_ref[rows, :]) for blk in _row_blocks(m)]
            out["f"] = [f_ref[blk, :] + part for blk, part in zip(_row_blocks(m), contrib)]
        yield


def _ffn_head(front, out, w, h2_ref, ups):
    for blk, rows in enumerate(_row_blocks(h2_ref.shape[0])):
        next(front)
        h2_ref[rows, :] = out["h2"][blk]
        for c in range(FFN_UPS_AHEAD):
            _ffn_issue(w, h2_ref, ups, c, rows)


def _finish(out, gate2):
    blocks = _row_blocks(gate2.shape[0])
    return _row_cat([x1 + _rms_scaled(f, gate2[rows])
                     for rows, x1, f in zip(blocks, out["x1"], out["f"])])


def _fold_mods(mod, g_refs, n_steps):
    g = [g_ref[...] for g_ref in g_refs]
    sh1, sc1, gt1, sh2, sc2, gt2 = [mod[:, i * D_MODEL:(i + 1) * D_MODEL] for i in range(6)]
    folded = [sh1, g[0] * (1.0 + sc1), gt1 * g[1], sh2, g[2] * (1.0 + sc2), gt2 * g[3]]
    return [_row_cat([a] * n_steps) for a in folded]


def _trunk(x, mods, w, h2_ref, f_ref, act_ref, *, nb, cnt_fn, hist_pool, hist_conv, hist_ffn,
           sink_pool, sink_conv, sink_ffn):
    out = {}
    front = _front_stages(x, _front_head(x, mods), mods, w, out, nb=nb, cnt_fn=cnt_fn,
                          hist_pool=hist_pool, hist_conv=hist_conv, sink_pool=sink_pool,
                          sink_conv=sink_conv)
    for _ in range(3):
        next(front)
    ups = _UpValues()
    _ffn_head(front, out, w, h2_ref, ups)
    for _ in _ffn_stages(w, h2_ref, f_ref, act_ref, ups, out, nb=nb, hist_ffn=hist_ffn,
                         sink_ffn=sink_ffn):
        pass
    return _finish(out, mods[5])


def _stream_cast(src, dst, stage, sems, sem_row):
    slots, rows = stage.shape[0], stage.shape[1]
    n = src.shape[0] // rows
    ahead = slots - 1

    def chunk(i, slot):
        return pltpu.make_async_copy(src.at[pl.ds(i * rows, rows), :], stage.at[slot],
                                     sems.at[sem_row, slot])

    for i in range(min(ahead, n)):
        chunk(i, i).start()

    def body(i, carry):
        slot = lax.rem(i, slots)

        @pl.when(i + ahead < n)
        def _():
            chunk(i + ahead, lax.rem(i + ahead, slots)).start()

        chunk(i, slot).wait()
        dst[pl.ds(pl.multiple_of(i * rows, rows), rows), :] = stage[slot].astype(_bf16)
        return carry

    lax.fori_loop(0, n, body, 0)


def _prompt_kernel(x_hbm, mod_ref, *rest, nb, tt, n_steps):
    g_refs, rest = rest[:4], rest[4:]
    (w_in_hbm, pool_w_ref, pool_scale_ref, conv_w_ref, w_out_hbm, w_up_hbm, ffn_conv_w_ref,
     w_down_hbm) = rest[:8]
    y_hbm, pool_out_ref, conv_out_ref, ffn_out_ref = rest[8:12]
    bf_out = rest[12:16]
    (x_buf, y_buf, pool_carry, conv_carry, ffn_carry, h2_ref, f_ref, act_ref,
     w_in_bf, w_out_bf, w_up_bf, w_down_bf, stage_in, stage_sq, stage_up,
     x_sems, y_sems, stage_sems, out_sems) = rest[16:]
    bf_vmem = (w_in_bf, w_out_bf, w_up_bf, w_down_bf)
    w = (g_refs, w_in_bf, pool_w_ref, pool_scale_ref, conv_w_ref, w_out_bf, w_up_bf, ffn_conv_w_ref,
         w_down_bf)
    j = pl.program_id(0)

    def bf_writeback(k):
        return pltpu.make_async_copy(bf_vmem[k], bf_out[k], out_sems.at[k])

    m = tt * nb
    slot = lax.rem(j, 2)

    def tile_copies(tile, buf_slot, to_hbm):
        copies = []
        for b in range(nb):
            hbm = (y_hbm if to_hbm else x_hbm).at[b, pl.ds(tile * tt, tt), :]
            if to_hbm:
                copies.append(pltpu.make_async_copy(y_buf.at[buf_slot, :, b, :], hbm,
                                                    y_sems.at[buf_slot, b]))
            else:
                copies.append(pltpu.make_async_copy(hbm, x_buf.at[buf_slot, :, b, :],
                                                    x_sems.at[buf_slot, b]))
        return copies

    def start(copies):
        for copy in copies:
            copy.start()

    def wait(copies):
        for copy in copies:
            copy.wait()

    @pl.when(j == 0)
    def _():
        start(tile_copies(0, 0, False))
        pool_carry[...] = jnp.zeros_like(pool_carry)
        conv_carry[...] = jnp.zeros_like(conv_carry)
        ffn_carry[...] = jnp.zeros_like(ffn_carry)
        _stream_cast(w_in_hbm.at[0], w_in_bf, stage_in, stage_sems, 0)
        _stream_cast(w_out_hbm.at[0], w_out_bf, stage_sq, stage_sems, 1)
        _stream_cast(w_up_hbm.at[0], w_up_bf, stage_up, stage_sems, 2)
        _stream_cast(w_down_hbm.at[0], w_down_bf, stage_sq, stage_sems, 1)
        for k in range(len(bf_vmem)):
            bf_writeback(k).start()

    @pl.when(j + 1 < n_steps)
    def _():
        start(tile_copies(j + 1, 1 - slot, False))

    @pl.when(j >= 2)
    def _():
        wait(tile_copies(j - 2, slot, True))

    wait(tile_copies(j, slot, False))
    x = x_buf[slot].reshape(m, D_MODEL)

    mods = _fold_mods(mod_ref[...], g_refs, tt)

    t_idx = j * tt + lax.shift_right_logical(
        lax.broadcasted_iota(jnp.int32, (m, POOL_GC), 0), nb.bit_length() - 1)

    def cnt_fn(win):
        return jnp.minimum(win, t_idx + 1).astype(_f32)

    def carry_rows(ref, first, n):
        return _lane_cat([ref[first + q] for q in range(n)])

    def keep_rows(ref, first, ext):
        rows = ref.shape[1]
        for q in range(ext.shape[1] // LANES):
            ref[first + q] = ext[ext.shape[0] - rows:, q * LANES:(q + 1) * LANES]

    y = _trunk(
        x, mods, w, h2_ref, f_ref, act_ref, nb=nb, cnt_fn=cnt_fn,
        hist_pool=lambda: carry_rows(pool_carry, 0, D_POOL // LANES),
        hist_conv=lambda: carry_rows(conv_carry, 0, D_CONV // LANES),
        hist_ffn=lambda cols: carry_rows(ffn_carry, cols.start // LANES, FF_CHUNK // LANES),
        sink_pool=lambda ext: keep_rows(pool_carry, 0, ext),
        sink_conv=lambda ext: keep_rows(conv_carry, 0, ext),
        sink_ffn=lambda cols, ext: keep_rows(ffn_carry, cols.start // LANES, ext))

    y_buf[slot] = y.reshape(tt, nb, D_MODEL)
    start(tile_copies(j, slot, True))

    @pl.when(j == n_steps - 1)
    def _():
        def emit(out_ref, carry, n):
            first = carry.shape[1] // nb - n
            for b in range(nb):
                for q in range(carry.shape[0]):
                    out_ref[0, b, :, q * LANES:(q + 1) * LANES] = (
                        carry[q, pl.ds(first * nb + b, n, stride=nb), :])
        pool_rows = carry_rows(pool_carry, 0, D_POOL // LANES)
        for k in range(POOL_BUF):
            first = (POOL_HALO - POOL_BUF + k) * nb
            pool_out_ref[k] = pool_rows[first:first + nb]
        emit(conv_out_ref, conv_carry, CONV_HIST)
        emit(ffn_out_ref, ffn_carry, CONV_HIST)
        for k in range(len(bf_vmem)):
            bf_writeback(k).wait()
        if n_steps >= 2:
            wait(tile_copies(j - 1, 1 - slot, True))
        wait(tile_copies(j, slot, True))


def _prompt(x, mod, g, w_in, pool_w, pool_scale, conv_w, w_out, w_up, ffn_conv_w, w_down):
    nb, seq, _ = x.shape
    assert nb == SUBLANES, "one vreg row per time step"
    tt = PROMPT_TT
    n_steps = seq // tt
    m = tt * nb
    big = (w_in, w_out, w_up, w_down)
    stage_widths = (D_IN_PROJ, D_MODEL, 2 * D_FF)
    in_hbm = pl.BlockSpec(memory_space=pl.ANY)

    def const_spec(a):
        nd = a.ndim
        return pl.BlockSpec(a.shape, lambda j: (0,) * nd, pipeline_mode=pl.Buffered(1))

    return pl.pallas_call(
        functools.partial(_prompt_kernel, nb=nb, tt=tt, n_steps=n_steps),
        grid=(n_steps,),
        in_specs=[in_hbm, const_spec(mod),
                  *[const_spec(a) for a in g], in_hbm, const_spec(pool_w), const_spec(pool_scale),
                  const_spec(conv_w), in_hbm, in_hbm, const_spec(ffn_conv_w), in_hbm],
        out_specs=[
            in_hbm,
            pl.BlockSpec((POOL_BUF, nb, D_POOL), lambda j: (0, 0, 0)),
            pl.BlockSpec((1, nb, CONV_HIST, D_CONV), lambda j: (0, 0, 0, 0)),
            pl.BlockSpec((1, nb, CONV_HIST, 2 * D_FF), lambda j: (0, 0, 0, 0)),
        ] + [pl.BlockSpec(memory_space=pl.ANY) for _ in big],
        out_shape=[
            jax.ShapeDtypeStruct((nb, seq, D_MODEL), _f32),
            jax.ShapeDtypeStruct((POOL_BUF, nb, D_POOL), _f32),
            jax.ShapeDtypeStruct((1, nb, CONV_HIST, D_CONV), _f32),
            jax.ShapeDtypeStruct((1, nb, CONV_HIST, 2 * D_FF), _f32),
        ] + [jax.ShapeDtypeStruct(a.shape[1:], _bf16) for a in big],
        scratch_shapes=[
            pltpu.VMEM((2, tt, nb, D_MODEL), _f32),
            pltpu.VMEM((2, tt, nb, D_MODEL), _f32),
            pltpu.VMEM((D_POOL // LANES, POOL_HALO * nb, LANES), _f32),
            pltpu.VMEM((D_CONV // LANES, CONV_HIST * nb, LANES), _f32),
            pltpu.VMEM((2 * D_FF // LANES, CONV_HIST * nb, LANES), _f32),
            pltpu.VMEM((m, D_MODEL), _bf16),
            pltpu.VMEM((m, D_MODEL), _f32),
            pltpu.VMEM((m, D_FF), _bf16),
        ] + [pltpu.VMEM(a.shape[1:], _bf16) for a in big]
        + [pltpu.VMEM((WEIGHT_STAGE_SLOTS, WEIGHT_STAGE_ROWS[c], c), _f32) for c in stage_widths]
        + [pltpu.SemaphoreType.DMA((2, nb)), pltpu.SemaphoreType.DMA((2, nb)),
           pltpu.SemaphoreType.DMA((len(stage_widths), WEIGHT_STAGE_SLOTS)),
           pltpu.SemaphoreType.DMA((len(big),))],
        compiler_params=pltpu.CompilerParams(
            dimension_semantics=("arbitrary",),
            vmem_limit_bytes=V7X_VMEM_LIMIT_BYTES),
        name="prompt_trunk",
    )(x, mod, *g, w_in, pool_w, pool_scale, conv_w, w_out, w_up, ffn_conv_w, w_down)


def _sample_kernel(*refs, nb, nt):
    refs = iter(refs)

    def take(n):
        return [next(refs) for _ in range(n)]

    x_hbm, sp_hbm, sc_hbm, sf_hbm, mod_hbm = take(5)
    g_refs = take(4)
    (w_in_hbm, pool_w_ref, pool_scale_ref, conv_w_ref, w_out_hbm, w_up_hbm, ffn_conv_w_ref,
     w_down_hbm) = take(8)
    y_hbm, pool_hbm, conv_hbm, ffn_hbm = take(4)
    (x_ref, sp_ref, sc_ref, mod_ref, w_in_ref, w_out_ref, w_up_ref, w_down_ref, sf_ref) = take(9)
    y_ref, pool_out_ref, conv_out_ref, ffn_out_ref = take(4)
    h2_ref, f_ref, act_ref, in_sems, out_sems = take(5)
    w = (g_refs, w_in_ref, pool_w_ref, pool_scale_ref, conv_w_ref, w_out_ref, w_up_ref,
         ffn_conv_w_ref, w_down_ref)

    n_groups = -(-N_FF_CHUNKS // FF_DOWN_GROUP)

    def group_cols(gi, base):
        first = gi * FF_DOWN_GROUP
        n = min(FF_DOWN_GROUP, N_FF_CHUNKS - first)
        return pl.ds(base + first * FF_CHUNK, n * FF_CHUNK)

    in_order = [("x", [(x_hbm.at[:, t, :], x_ref.at[t]) for t in range(nt)]),
                ("mod", [(mod_hbm, mod_ref)]), ("w_in", [(w_in_hbm, w_in_ref)]),
                ("pool", [(sp_hbm, sp_ref)]),
                ("conv", [(sc_hbm.at[0, :, k, :], sc_ref.at[k]) for k in range(CONV_HIST)]),
                ("w_out", [(w_out_hbm, w_out_ref)]),
                ("ffn", [(sf_hbm.at[0, :, k, :], sf_ref.at[k]) for k in range(CONV_HIST)])]
    for gi in range(n_groups):
        cols = [group_cols(gi, base) for base in (0, D_FF)]
        rows = group_cols(gi, 0)
        in_order.append((f"ffn_w{gi}", [(w_up_hbm.at[:, c], w_up_ref.at[:, c]) for c in cols]
                         + [(w_down_hbm.at[rows, :], w_down_ref.at[rows, :])]))
    fetch, k = {}, 0
    for name, pairs in in_order:
        fetch[name] = [pltpu.make_async_copy(src, dst, in_sems.at[k + i])
                       for i, (src, dst) in enumerate(pairs)]
        k += len(pairs)

    out_order = [("y", [(y_ref.at[t], y_hbm.at[:, t, :]) for t in range(nt)]),
                 ("pool", [(pool_out_ref, pool_hbm)]),
                 ("conv", [(conv_out_ref.at[k], conv_hbm.at[0, :, k, :]) for k in range(CONV_HIST)])]
    for gi in range(n_groups):
        out_order.append((f"ffn{gi}", [
            (ffn_out_ref.at[k, :, group_cols(gi, base)], ffn_hbm.at[0, :, k, group_cols(gi, base)])
            for base in (0, D_FF) for k in range(CONV_HIST)]))
    send, k = {}, 0
    for name, pairs in out_order:
        send[name] = [pltpu.make_async_copy(src, dst, out_sems.at[k + i])
                      for i, (src, dst) in enumerate(pairs)]
        k += len(pairs)

    def start(copies):
        for copy in copies:
            copy.start()

    def wait(copies):
        for copy in copies:
            copy.wait()

    for name, _ in in_order:
        start(fetch[name])

    def hist_ffn(cols):
        return _row_cat([sf_ref[k, :, cols] for k in range(CONV_HIST)])

    def sink_ffn(cols, ext):
        first = ext.shape[0] - CONV_HIST * nb
        for k in range(CONV_HIST):
            ffn_out_ref[k, :, cols] = ext[first + k * nb:first + (k + 1) * nb]

    def sink_pool(ext):
        first = ext.shape[0] - POOL_BUF * nb
        for k in range(POOL_BUF):
            pool_out_ref[k] = ext[first + k * nb:first + (k + 1) * nb]

    def sink_conv(ext):
        first = ext.shape[0] - CONV_HIST * nb
        for k in range(CONV_HIST):
            conv_out_ref[k] = ext[first + k * nb:first + (k + 1) * nb]

    wait(fetch["x"])
    wait(fetch["mod"])
    x = _row_cat([x_ref[t] for t in range(nt)])
    mods = _fold_mods(mod_ref[...], g_refs, nt)
    h = _front_head(x, mods)
    mixed = {}
    front = _front_stages(
        x, h, mods, w, mixed, nb=nb,
        cnt_fn=lambda win: float(min(win, PAST_LEN + 1)),
        hist_pool=lambda: _row_cat([sp_ref[k] for k in range(POOL_BUF)]),
        hist_conv=lambda: _row_cat([sc_ref[k] for k in range(CONV_HIST)]),
        sink_pool=sink_pool, sink_conv=sink_conv)
    wait(fetch["w_in"])
    next(front)
    wait(fetch["pool"])
    wait(fetch["conv"])
    next(front)
    start(send["pool"])
    start(send["conv"])
    wait(fetch["w_out"])
    next(front)

    wait(fetch["ffn"])
    wait(fetch["ffn_w0"])
    ups = _UpValues()
    _ffn_head(front, mixed, w, h2_ref, ups)
    ffn = _ffn_stages(w, h2_ref, f_ref, act_ref, ups, mixed, nb=nb, hist_ffn=hist_ffn,
                      sink_ffn=sink_ffn)
    groups_sent = 0
    for c in range(N_FF_CHUNKS):
        ahead = c + FFN_UPS_AHEAD
        if ahead < N_FF_CHUNKS and ahead % FF_DOWN_GROUP == 0:
            wait(fetch[f"ffn_w{ahead // FF_DOWN_GROUP}"])
            while (groups_sent + 1) * FF_DOWN_GROUP <= c:
                start(send[f"ffn{groups_sent}"])
                groups_sent += 1
        next(ffn)
    y = _finish(mixed, mods[5])
    for t in range(nt):
        y_ref[t] = y[t * nb:(t + 1) * nb]
    for gi in range(groups_sent, n_groups):
        start(send[f"ffn{gi}"])
    start(send["y"])
    for copies in send.values():
        wait(copies)


def _sample(x, sp, sc, sf, mod, g, w_in, pool_w, pool_scale, conv_w, w_out, w_up, ffn_conv_w, w_down):
    nb, nt, _ = x.shape
    hbm = pl.BlockSpec(memory_space=pl.ANY)
    vmem = pl.BlockSpec(memory_space=pltpu.VMEM)
    out_shape = [
        jax.ShapeDtypeStruct(x.shape, _f32),
        jax.ShapeDtypeStruct(sp.shape, _f32),
        jax.ShapeDtypeStruct(sc.shape, _f32),
        jax.ShapeDtypeStruct(sf.shape, _f32),
    ]
    step_major = [pltpu.VMEM((nt, nb, D_MODEL), _f32),
                  pltpu.VMEM(sp.shape, _f32),
                  pltpu.VMEM((CONV_HIST, nb, D_CONV), _f32)]
    n_fetches = nt + 1 + 1 + 1 + CONV_HIST + 1 + CONV_HIST + 3 * -(-N_FF_CHUNKS // FF_DOWN_GROUP)
    n_sends = nt + 1 + CONV_HIST + 2 * CONV_HIST * -(-N_FF_CHUNKS // FF_DOWN_GROUP)
    return pl.pallas_call(
        functools.partial(_sample_kernel, nb=nb, nt=nt),
        in_specs=[hbm, hbm, hbm, hbm, hbm, vmem, vmem, vmem, vmem, hbm, vmem, vmem, vmem, hbm, hbm,
                  vmem, hbm],
        out_specs=[hbm] * len(out_shape),
        out_shape=out_shape,
        scratch_shapes=step_major
        + [pltpu.VMEM(a.shape, a.dtype) for a in (mod, w_in, w_out, w_up, w_down)]
        + [pltpu.VMEM((CONV_HIST, nb, 2 * D_FF), _f32)]
        + step_major
        + [pltpu.VMEM((CONV_HIST, nb, 2 * D_FF), _f32)]
        + [
            pltpu.VMEM((nt * nb, D_MODEL), _bf16),
            pltpu.VMEM((nt * nb, D_MODEL), _f32),
            pltpu.VMEM((nt * nb, D_FF), _bf16),
            pltpu.SemaphoreType.DMA((n_fetches,)),
            pltpu.SemaphoreType.DMA((n_sends,)),
        ],
        compiler_params=pltpu.CompilerParams(vmem_limit_bytes=V7X_VMEM_LIMIT_BYTES),
        name="sample_trunk",
    )(x, sp, sc, sf, mod, *g, w_in, pool_w, pool_scale, conv_w, w_out, w_up, ffn_conv_w, w_down)


def kernel(x_prompt, x_sample, state_pool, state_conv, state_ffn, c_prompt, c_sample, w_ada, b_ada,
           g_pre_mix, g_post_mix, g_pre_ffn, g_post_ffn, w_in, pool_w, pool_scale, conv_w, w_out,
           ffn_w_up, ffn_conv_w, ffn_w_down):
    assert w_ada.shape[0] == 1, "single trunk layer"
    mod_p, mod_s = _ada(c_prompt, c_sample, w_ada[0], b_ada)
    g = (g_pre_mix, g_post_mix, g_pre_ffn, g_post_ffn)
    conv_taps = conv_w.reshape(1, CONV_W * D_CONV)
    ffn_taps = ffn_conv_w.reshape(1, CONV_W * 2 * D_FF)
    y_p, pool_p, conv_p, ffn_p, w_in_bf, w_out_bf, w_up_bf, w_down_bf = _prompt(
        x_prompt, mod_p, g, w_in, pool_w[0], pool_scale, conv_taps, w_out, ffn_w_up, ffn_taps,
        ffn_w_down)
    y_s, pool_s, conv_s, ffn_s = _sample(
        x_sample, state_pool[0].transpose(1, 0, 2), state_conv, state_ffn, mod_s, g, w_in_bf,
        pool_w[0], pool_scale, conv_taps, w_out_bf, w_up_bf, ffn_taps, w_down_bf)
    return (y_p, y_s, pool_p.transpose(1, 0, 2)[None], conv_p, ffn_p,
            pool_s.transpose(1, 0, 2)[None], conv_s, ffn_s)
```
